```python
import math
import jax, jax.numpy as jnp
from jax import lax
import numpy as np

D_MODEL = 1024
BATCH = 16
SEQ = 256
DEPTH = 2
DEC_BATCH = 4
DEC_SEQ = 2048
PAST_LEN = 512

GRID_W = 64
HEAD_DIM = 64
N_MIXERS = 4
MIX_WIDTH = D_MODEL
GROUP_WIDTH = MIX_WIDTH // N_MIXERS
N_GROUP_HEADS = GROUP_WIDTH // HEAD_DIM
CHUNK = 128
LRU_WIDTH = GROUP_WIDTH
LRU_BLOCKS = N_GROUP_HEADS
LRU_BLOCK = LRU_WIDTH // LRU_BLOCKS
LRU_C = 8.0
CONV_W = 4
DIFF_HEADS = N_GROUP_HEADS
DIFF_V_DIM = GROUP_WIDTH // DIFF_HEADS
DIFF_QK_DIM = DIFF_V_DIM // 2
SWA_HEADS = N_GROUP_HEADS
SWA_KV_HEADS = 2
SWA_GROUPS = SWA_HEADS // SWA_KV_HEADS
WINDOW = 128
Q_BLOCK = 128
N_EXPERTS = 32
TOP_K = 4
D_FF_EXPERT = D_MODEL
SWIGLU_LIMIT = 7.0
SWIGLU_ALPHA = 1.702
ROPE_BASE = 10000.0
EPS = 1e-6
IN_SPLITS = (GROUP_WIDTH, GROUP_WIDTH,
             LRU_WIDTH, LRU_WIDTH,
             DIFF_HEADS * 2 * DIFF_QK_DIM, DIFF_HEADS * 2 * DIFF_QK_DIM, DIFF_HEADS * DIFF_V_DIM,
             SWA_HEADS * HEAD_DIM, SWA_KV_HEADS * HEAD_DIM, SWA_KV_HEADS * HEAD_DIM)
IN_WIDTH = sum(IN_SPLITS)

kernel_name = 'hybrid_diffusion_prefix_step'


def split_points():
    return [int(p) for p in np.cumsum(IN_SPLITS)[:-1]]


def rms_norm(x, g=None):
    xf = x.astype(jnp.float32)
    y = (xf * lax.rsqrt(jnp.mean(xf * xf, axis=-1, keepdims=True) + EPS)).astype(x.dtype)
    return y if g is None else y * g


def axial_rope(T, rot_dim):
    rows = T // GRID_W
    nf = rot_dim // 4
    inv = 1.0 / (ROPE_BASE ** (jnp.arange(nf, dtype=jnp.float32) / nf))
    row = jnp.repeat(jnp.arange(rows, dtype=jnp.float32), GRID_W)
    col = jnp.tile(jnp.arange(GRID_W, dtype=jnp.float32), rows)
    ang = jnp.stack([row[:, None] * inv, col[:, None] * inv], axis=1)
    return jnp.cos(ang), jnp.sin(ang)


def apply_axial_rope(x, cos, sin):
    nf = x.shape[-1] // 4
    shp = (x.shape[1],) + (1,) * (x.ndim - 3) + (2, nf)
    c = cos.reshape(shp).astype(x.dtype)
    s = sin.reshape(shp).astype(x.dtype)
    xr = x.reshape(x.shape[:-1] + (2, 2, nf))
    x1, x2 = xr[..., 0, :], xr[..., 1, :]
    return jnp.stack([x1 * c - x2 * s, x2 * c + x1 * s], axis=-2).reshape(x.shape)


def softmax_with_sink(s, sink):
    m = jnp.maximum(jnp.max(s, axis=-1, keepdims=True), sink)
    e = jnp.exp(s - m)
    return e / (jnp.sum(e, axis=-1, keepdims=True) + jnp.exp(sink - m))


def chunk_mlp(u, v, vnorm_g, ws, bs):
    B, T, _ = u.shape
    n = T // CHUNK
    vh = rms_norm(v.reshape(B, n, CHUNK, N_GROUP_HEADS, HEAD_DIM), vnorm_g)
    mixed = jnp.einsum('hpq,bnqhd->bnphd', ws, vh) + bs.T[:, :, None]
    return u * mixed.reshape(B, T, GROUP_WIDTH)


def centred_conv(x, w, b):
    T = x.shape[1]
    left = CONV_W // 2
    xp = jnp.pad(x, ((0, 0), (left, CONV_W - 1 - left), (0, 0)))
    y = b + xp[:, 0:T] * w[0]
    for j in range(1, CONV_W):
        y = y + xp[:, j:j + T] * w[j]
    return y


def rglru_scan(x, h0, wa, ba, wx, bx, lam, reverse):
    B, T, W = x.shape
    xh = x.reshape(B, T, LRU_BLOCKS, LRU_BLOCK)
    r = jax.nn.sigmoid(jnp.einsum('btni,nij->btnj', xh, wa) + ba).reshape(B, T, W)
    i = jax.nn.sigmoid(jnp.einsum('btni,nij->btnj', xh, wx) + bx).reshape(B, T, W)
    log_a = (-LRU_C * r.astype(jnp.float32)) * jax.nn.softplus(-lam.astype(jnp.float32))
    a = jnp.exp(log_a)
    b = jnp.sqrt(-jnp.expm1(2.0 * log_a)) * (i * x).astype(jnp.float32)
    edge = T - 1 if reverse else 0
    b = b.at[:, edge].add(a[:, edge] * h0.astype(jnp.float32))

    def combine(lhs, rhs):
        a1, b1 = lhs
        a2, b2 = rhs
        return a1 * a2, a2 * b1 + b2

    _, h = lax.associative_scan(combine, (a, b), axis=1, reverse=reverse)
    h_final = h[:, 0] if reverse else h[:, -1]
    return h.astype(x.dtype), h_final.astype(x.dtype)


def rglru_mixer(xb, gb, h0, lp):
    xc = centred_conv(xb, lp['lru_conv_w'], lp['lru_conv_b'])
    h_f, s_f = rglru_scan(xc, h0[:, 0], lp['lru_wa'][0], lp['lru_ba'][0], lp['lru_wx'][0],
                          lp['lru_bx'][0], lp['lru_lambda'][0], False)
    h_b, s_b = rglru_scan(xc, h0[:, 1], lp['lru_wa'][1], lp['lru_ba'][1], lp['lru_wx'][1],
                          lp['lru_bx'][1], lp['lru_lambda'][1], True)
    return (h_f + h_b) * jax.nn.gelu(gb), jnp.stack([s_f, s_b], axis=1)


def diff_attention(q, k, v, lam):
    B, Tq = q.shape[:2]
    nb = Tq // Q_BLOCK
    qb = jnp.moveaxis(q.reshape((B, nb, Q_BLOCK) + q.shape[2:]), 1, 0)
    scale = DIFF_QK_DIM ** -0.5

    def block(qblk):
        s = jnp.einsum('bqhid,bkhid->bhiqk', qblk, k).astype(jnp.float32) * scale
        p = jax.nn.softmax(s, axis=-1)
        p = p[:, :, 0] - lam * p[:, :, 1]
        return jnp.einsum('bhqk,bkhd->bqhd', p.astype(v.dtype), v)

    o = lax.map(block, qb)
    return jnp.moveaxis(o, 0, 1).reshape(B, Tq, DIFF_HEADS, DIFF_V_DIM)


def sink_attention_dense(q, k, v, sink):
    B, T = q.shape[:2]
    nb = T // Q_BLOCK
    qb = jnp.moveaxis(q.reshape(B, nb, Q_BLOCK, SWA_KV_HEADS, SWA_GROUPS, HEAD_DIM), 1, 0)
    sink_l = sink.astype(jnp.float32).reshape(SWA_KV_HEADS, SWA_GROUPS)[:, :, None, None]
    scale = HEAD_DIM ** -0.5

    def block(qblk):
        s = jnp.einsum('bqkgd,btkd->bkgqt', qblk, k).astype(jnp.float32) * scale
        p = softmax_with_sink(s, sink_l)
        return jnp.einsum('bkgqt,btkd->bqkgd', p.astype(v.dtype), v)

    o = lax.map(block, qb)
    return jnp.moveaxis(o, 0, 1).reshape(B, T, SWA_HEADS * HEAD_DIM)


def sink_attention_window(q, k, v, ck, cv, sink):
    B, T = q.shape[:2]
    P = ck.shape[1]
    nb = T // Q_BLOCK
    span = Q_BLOCK + 2 * WINDOW
    qb = q.reshape(B, nb, Q_BLOCK, SWA_KV_HEADS, SWA_GROUPS, HEAD_DIM)
    pad = ((0, 0), (WINDOW, WINDOW), (0, 0), (0, 0))
    idx = (jnp.arange(nb) * Q_BLOCK)[:, None] + jnp.arange(span)[None]
    kb = jnp.pad(k, pad)[:, idx]
    vb = jnp.pad(v, pad)[:, idx]
    qpos = (jnp.arange(nb) * Q_BLOCK)[:, None] + jnp.arange(Q_BLOCK)[None]
    kpos = idx - WINDOW
    valid = ((jnp.abs(qpos[:, :, None] - kpos[:, None, :]) <= WINDOW)
             & (kpos[:, None, :] >= 0) & (kpos[:, None, :] < T))
    scale = HEAD_DIM ** -0.5
    s_loc = jnp.einsum('bnqkgd,bnjkd->bnkgqj', qb, kb).astype(jnp.float32) * scale
    s_loc = jnp.where(valid[None, :, None, None], s_loc, -jnp.inf)
    s_ctx = jnp.einsum('bnqkgd,bpkd->bnkgqp', qb, ck).astype(jnp.float32) * scale
    sink_l = sink.astype(jnp.float32).reshape(SWA_KV_HEADS, SWA_GROUPS)[:, :, None, None]
    p = softmax_with_sink(jnp.concatenate([s_ctx, s_loc], axis=-1), sink_l)
    o = (jnp.einsum('bnkgqp,bpkd->bnqkgd', p[..., :P].astype(v.dtype), cv)
         + jnp.einsum('bnkgqj,bnjkd->bnqkgd', p[..., P:].astype(v.dtype), vb))
    return o.reshape(B, T, SWA_HEADS * HEAD_DIM)


def mixer_sublayer(h, lp, lam_init, ctx):
    B, T, _ = h.shape
    a_u, a_v, b_x, b_g, c_q, c_k, c_v, d_q, d_k, d_v = jnp.split(h @ lp['w_in'], split_points(), axis=-1)
    y_a = chunk_mlp(jax.nn.gelu(a_u), jax.nn.gelu(a_v), lp['mlp_vnorm_g'], lp['mlp_ws'], lp['mlp_bs'])
    h0 = jnp.zeros((B, 2, LRU_WIDTH), h.dtype) if ctx is None else ctx['lru']
    y_b, lru_state = rglru_mixer(b_x, b_g, h0, lp)
    qc = rms_norm(c_q.reshape(B, T, DIFF_HEADS, 2, DIFF_QK_DIM), lp['diff_qnorm_g'])
    kc = rms_norm(c_k.reshape(B, T, DIFF_HEADS, 2, DIFF_QK_DIM), lp['diff_knorm_g'])
    vc = c_v.reshape(B, T, DIFF_HEADS, DIFF_V_DIM)
    ld = lp['diff_lambda'].astype(jnp.float32)
    lam = jnp.exp(jnp.sum(ld[0] * ld[1])) - jnp.exp(jnp.sum(ld[2] * ld[3])) + lam_init
    qd = rms_norm(d_q.reshape(B, T, SWA_HEADS, HEAD_DIM), lp['swa_qnorm_g'])
    kd = rms_norm(d_k.reshape(B, T, SWA_KV_HEADS, HEAD_DIM), lp['swa_knorm_g'])
    vd = d_v.reshape(B, T, SWA_KV_HEADS, HEAD_DIM)
    if ctx is None:
        o_c = diff_attention(qc, kc, vc, lam)
        y_d = sink_attention_dense(qd, kd, vd, lp['swa_sink'])
        state = (kc, vc, kd, vd, lru_state)
    else:
        cos_c, sin_c = axial_rope(T, DIFF_QK_DIM)
        qc_r = apply_axial_rope(qc, cos_c, sin_c)
        kc_r = apply_axial_rope(kc, cos_c, sin_c)
        o_c = diff_attention(qc_r, jnp.concatenate([ctx['diff_k'], kc_r], axis=1),
                             jnp.concatenate([ctx['diff_v'], vc], axis=1), lam)
        cos_d, sin_d = axial_rope(T, HEAD_DIM)
        y_d = sink_attention_window(apply_axial_rope(qd, cos_d, sin_d), apply_axial_rope(kd, cos_d, sin_d),
                                    vd, ctx['swa_k'], ctx['swa_v'], lp['swa_sink'])
        state = None
    y_c = (rms_norm(o_c, lp['diff_subln_g']) * (1.0 - lam_init)).reshape(B, T, GROUP_WIDTH)
    out = jnp.concatenate([y_a, y_b, y_c, y_d], axis=-1) @ lp['w_out']
    return out, state


def moe(h, lp):
    B, T, D = h.shape
    xt = h.reshape(B * T, D)
    logits = (xt @ lp['router_w'] + lp['router_b']).astype(jnp.float32)
    top_v, top_i = lax.top_k(logits, TOP_K)
    top_w = jax.nn.softmax(top_v, axis=-1)
    combine = jnp.sum(jax.nn.one_hot(top_i, N_EXPERTS, dtype=jnp.float32) * top_w[..., None], axis=1).astype(h.dtype)
    out = jnp.zeros_like(xt)
    for e in range(N_EXPERTS):
        gu = xt @ lp['moe_w_gu'][e] + lp['moe_b_gu'][e]
        gate = jnp.minimum(gu[:, :D_FF_EXPERT], SWIGLU_LIMIT)
        up = jnp.clip(gu[:, D_FF_EXPERT:], -SWIGLU_LIMIT, SWIGLU_LIMIT)
        y = ((up + 1.0) * gate * jax.nn.sigmoid(SWIGLU_ALPHA * gate)) @ lp['moe_w_down'][e] + lp['moe_b_down'][e]
        out = out + combine[:, e:e + 1] * y
    return out.reshape(B, T, D)


def trunk_layer(x, cond, lp, lam_init, ctx):
    N = cond.shape[0]
    mod = (jax.nn.silu(cond) @ lp['mod_w'] + lp['mod_b']).reshape(N, 1, 6, D_MODEL)
    sh1, sc1, g1, sh2, sc2, g2 = (mod[:, :, j] for j in range(6))
    h = rms_norm(x, lp['norm1_g']) * (1.0 + sc1) + sh1
    mix, state = mixer_sublayer(h, lp, lam_init, ctx)
    x = x + g1 * mix
    h = rms_norm(x, lp['norm2_g']) * (1.0 + sc2) + sh2
    x = x + g2 * moe(h, lp)
    return x, state


def setup_inputs(seed: int = 0) -> dict:
    key = jax.random.key(seed)
    ks = jax.random.split(key, 40)

    def nrm(i, shape, scale=1.0):
        return jax.random.normal(ks[i], shape, jnp.float32) * scale

    a0 = jax.random.uniform(ks[24], (DEPTH, 2, LRU_WIDTH), jnp.float32, 0.9, 0.999)
    return {
        'x_prompt': nrm(0, (BATCH, SEQ, D_MODEL)),
        'x_sample': nrm(1, (DEC_BATCH, DEC_SEQ, D_MODEL)),
        'c': nrm(2, (DEC_BATCH, D_MODEL)),
        'cache_diff_k': nrm(3, (DEC_BATCH, DEPTH, PAST_LEN, DIFF_HEADS, 2, DIFF_QK_DIM)),
        'cache_diff_v': nrm(4, (DEC_BATCH, DEPTH, PAST_LEN, DIFF_HEADS, DIFF_V_DIM)),
        'cache_swa_k': nrm(5, (DEC_BATCH, DEPTH, PAST_LEN, SWA_KV_HEADS, HEAD_DIM)),
        'cache_swa_v': nrm(6, (DEC_BATCH, DEPTH, PAST_LEN, SWA_KV_HEADS, HEAD_DIM)),
        'state_lru': nrm(7, (DEC_BATCH, DEPTH, 2, LRU_WIDTH), 0.5),
        'c_ctx': nrm(8, (D_MODEL,)),
        'mod_w': nrm(9, (DEPTH, D_MODEL, 6 * D_MODEL), D_MODEL ** -0.5),
        'mod_b': nrm(10, (DEPTH, 6 * D_MODEL), 0.02),
        'norm1_g': 1.0 + nrm(11, (DEPTH, D_MODEL), 0.02),
        'norm2_g': 1.0 + nrm(12, (DEPTH, D_MODEL), 0.02),
        'w_in': nrm(13, (DEPTH, D_MODEL, IN_WIDTH), D_MODEL ** -0.5),
        'w_out': nrm(14, (DEPTH, MIX_WIDTH, D_MODEL), MIX_WIDTH ** -0.5),
        'mlp_vnorm_g': 1.0 + nrm(15, (DEPTH, N_GROUP_HEADS, HEAD_DIM), 0.02),
        'mlp_ws': nrm(16, (DEPTH, N_GROUP_HEADS, CHUNK, CHUNK), CHUNK ** -0.5),
        'mlp_bs': 1.0 + nrm(17, (DEPTH, N_GROUP_HEADS, CHUNK), 0.02),
        'lru_conv_w': nrm(18, (DEPTH, CONV_W, LRU_WIDTH), CONV_W ** -0.5),
        'lru_conv_b': nrm(19, (DEPTH, LRU_WIDTH), 0.02),
        'lru_wa': nrm(20, (DEPTH, 2, LRU_BLOCKS, LRU_BLOCK, LRU_BLOCK), LRU_BLOCK ** -0.5),
        'lru_ba': nrm(21, (DEPTH, 2, LRU_BLOCKS, LRU_BLOCK), 0.02),
        'lru_wx': nrm(22, (DEPTH, 2, LRU_BLOCKS, LRU_BLOCK, LRU_BLOCK), LRU_BLOCK ** -0.5),
        'lru_bx': nrm(23, (DEPTH, 2, LRU_BLOCKS, LRU_BLOCK), 0.02),
        'lru_lambda': jnp.log(a0) - jnp.log1p(-a0),
        'diff_qnorm_g': 1.0 + nrm(25, (DEPTH, 2, DIFF_QK_DIM), 0.02),
        'diff_knorm_g': 1.0 + nrm(26, (DEPTH, 2, DIFF_QK_DIM), 0.02),
        'diff_lambda': nrm(27, (DEPTH, 4, DIFF_QK_DIM), 0.1),
        'diff_subln_g': 1.0 + nrm(28, (DEPTH, DIFF_V_DIM), 0.02),
        'swa_qnorm_g': 1.0 + nrm(29, (DEPTH, HEAD_DIM), 0.02),
        'swa_knorm_g': 1.0 + nrm(30, (DEPTH, HEAD_DIM), 0.02),
        'swa_sink': nrm(31, (DEPTH, SWA_HEADS), 0.5),
        'router_w': nrm(32, (DEPTH, D_MODEL, N_EXPERTS), D_MODEL ** -0.5),
        'router_b': nrm(33, (DEPTH, N_EXPERTS), 0.01),
        'moe_w_gu': nrm(34, (DEPTH, N_EXPERTS, D_MODEL, 2 * D_FF_EXPERT), D_MODEL ** -0.5),
        'moe_b_gu': nrm(35, (DEPTH, N_EXPERTS, 2 * D_FF_EXPERT), 0.02),
        'moe_w_down': nrm(36, (DEPTH, N_EXPERTS, D_FF_EXPERT, D_MODEL), D_FF_EXPERT ** -0.5),
        'moe_b_down': nrm(37, (DEPTH, N_EXPERTS, D_MODEL), 0.02),
    }


def reference(x_prompt, x_sample, c, cache_diff_k, cache_diff_v, cache_swa_k, cache_swa_v, state_lru,
              c_ctx, mod_w, mod_b, norm1_g, norm2_g, w_in, w_out, mlp_vnorm_g, mlp_ws, mlp_bs,
              lru_conv_w, lru_conv_b, lru_wa, lru_ba, lru_wx, lru_bx, lru_lambda,
              diff_qnorm_g, diff_knorm_g, diff_lambda, diff_subln_g, swa_qnorm_g, swa_knorm_g, swa_sink,
              router_w, router_b, moe_w_gu, moe_b_gu, moe_w_down, moe_b_down):
    y_p = x_prompt
    y_s = x_sample
    cond_ctx = c_ctx[None]
    dk_list, dv_list, sk_list, sv_list, lru_list = [], [], [], [], []
    for l in range(DEPTH):
        lp = {
            'mod_w': mod_w[l], 'mod_b': mod_b[l], 'norm1_g': norm1_g[l], 'norm2_g': norm2_g[l],
            'w_in': w_in[l], 'w_out': w_out[l],
            'mlp_vnorm_g': mlp_vnorm_g[l], 'mlp_ws': mlp_ws[l], 'mlp_bs': mlp_bs[l],
            'lru_conv_w': lru_conv_w[l], 'lru_conv_b': lru_conv_b[l], 'lru_wa': lru_wa[l], 'lru_ba': lru_ba[l],
            'lru_wx': lru_wx[l], 'lru_bx': lru_bx[l], 'lru_lambda': lru_lambda[l],
            'diff_qnorm_g': diff_qnorm_g[l], 'diff_knorm_g': diff_knorm_g[l], 'diff_lambda': diff_lambda[l],
            'diff_subln_g': diff_subln_g[l],
            'swa_qnorm_g': swa_qnorm_g[l], 'swa_knorm_g': swa_knorm_g[l], 'swa_sink': swa_sink[l],
            'router_w': router_w[l], 'router_b': router_b[l], 'moe_w_gu': moe_w_gu[l], 'moe_b_gu': moe_b_gu[l],
            'moe_w_down': moe_w_down[l], 'moe_b_down': moe_b_down[l],
        }
        lam_init = 0.8 - 0.6 * math.exp(-0.3 * l)
        y_p, st = trunk_layer(y_p, cond_ctx, lp, lam_init, None)
        dk_list.append(st[0])
        dv_list.append(st[1])
        sk_list.append(st[2])
        sv_list.append(st[3])
        lru_list.append(st[4])
        ctx = {'diff_k': cache_diff_k[:, l], 'diff_v': cache_diff_v[:, l],
               'swa_k': cache_swa_k[:, l], 'swa_v': cache_swa_v[:, l], 'lru': state_lru[:, l]}
        y_s, _ = trunk_layer(y_s, c, lp, lam_init, ctx)
    new_diff_k = jnp.stack(dk_list, axis=1)
    new_diff_v = jnp.stack(dv_list, axis=1)
    new_swa_k = jnp.stack(sk_list, axis=1)
    new_swa_v = jnp.stack(sv_list, axis=1)
    new_state_lru = jnp.stack(lru_list, axis=1)
    return (y_p, y_s, new_diff_k, new_diff_v, new_swa_k, new_swa_v, new_state_lru)
```

```python
import functools
import math

import jax
import jax.numpy as jnp
import numpy as np
from jax import lax
from jax.experimental import pallas as pl
from jax.experimental.pallas import tpu as pltpu

F32 = jnp.float32
BF16 = jnp.bfloat16

D_MODEL = 1024
BATCH = 16
SEQ = 256
DEPTH = 2
DEC_BATCH = 4
DEC_SEQ = 2048
PAST_LEN = 512
GRID_W = 64
HEAD_DIM = 64
GROUP_WIDTH = 256
N_GROUP_HEADS = 4
CHUNK = 128
LRU_C = 8.0
CONV_W = 4
DIFF_QK_DIM = 32
SWA_KV_HEADS = 2
SWA_GROUPS = 2
WINDOW = 128
N_EXPERTS = 32
TOP_K = 4
D_FF = 1024
SWIGLU_LIMIT = 7.0
SWIGLU_ALPHA = 1.702
ROPE_BASE = 10000.0
EPS = 1e-6

N_PROMPT = BATCH * SEQ
N_SAMPLE = DEC_BATCH * DEC_SEQ
N_TOK = N_PROMPT + N_SAMPLE
N_COND = 8
TM = 256
N_TILES = N_TOK // TM
PROMPT_TILES = N_PROMPT // TM
SAMPLE_TILES_PER_SEQ = DEC_SEQ // TM
IN_WIDTH = 2304
OFF_AU, OFF_AV, OFF_BX, OFF_BG, OFF_CQ, OFF_CK, OFF_CV, OFF_DQ, OFF_DK, OFF_DV = (
    0, 256, 512, 768, 1024, 1280, 1536, 1792, 2048, 2176)
MOE_TM = 1024
LRU_CHUNK = 256
VMEM_LIMIT = 56 * 1024 * 1024


def _cparams(sem):
    return pltpu.CompilerParams(dimension_semantics=sem, vmem_limit_bytes=VMEM_LIMIT)


def _dot(a, b):
    return jnp.dot(a, b, preferred_element_type=F32)


def _dot_nt(a, b):
    return lax.dot_general(a, b, (((1,), (1,)), ((), ())), preferred_element_type=F32)


def _split_bf16(x):
    hi = x.astype(BF16)
    lo = (x - hi.astype(F32)).astype(BF16)
    return hi, lo


def _seg_rms_norm(x, seg_mat, g):
    hi, lo = _split_bf16(x * x)
    ms = _dot(hi, seg_mat) + _dot(lo, seg_mat)
    return x * lax.rsqrt(ms + EPS) * g


def _rope(x, cos_t, sin_t, nf):
    n = x.shape[-1]
    lane = lax.broadcasted_iota(jnp.int32, x.shape, 1)
    first = (lane & (2 * nf - 1)) < nf
    partner = jnp.where(first, pltpu.roll(x, n - nf, axis=1), pltpu.roll(x, nf, axis=1))
    return x * cos_t + partner * sin_t


def _softplus(x):
    return jnp.maximum(x, 0.0) + jnp.log1p(jnp.exp(-jnp.abs(x)))


def _mod_kernel(cond_ref, w_ref, b_ref, o_ref):
    c = cond_ref[...]
    s = c * jax.nn.sigmoid(c)
    o_ref[...] = _dot(s.astype(BF16), w_ref[...].astype(BF16)) + b_ref[...]


def _mod_call(cond, w, b):
    nb = 6
    return pl.pallas_call(
        _mod_kernel,
        grid=(nb,),
        in_specs=[pl.BlockSpec((N_COND, D_MODEL), lambda j: (0, 0)),
                  pl.BlockSpec((D_MODEL, D_MODEL), lambda j: (0, j)),
                  pl.BlockSpec((1, D_MODEL), lambda j: (0, j))],
        out_specs=pl.BlockSpec((N_COND, D_MODEL), lambda j: (0, j)),
        out_shape=jax.ShapeDtypeStruct((N_COND, 6 * D_MODEL), F32),
        compiler_params=_cparams(("arbitrary",)),
        name="mod",
    )(cond, w, b.reshape(1, 6 * D_MODEL))


def _inproj_kernel(seq_ref, rope_ref, x_ref, mod_ref, g1_ref, w_ref, s32_ref, s64_ref,
                   gqc_ref, gkc_ref, gqd_ref, gkd_ref, cosc_ref, sinc_ref, cosd_ref, sind_ref, o_ref):
    del seq_ref, rope_ref
    x = x_ref[...]
    xn = x * lax.rsqrt(jnp.mean(x * x, axis=-1, keepdims=True) + EPS) * g1_ref[...]
    mod = mod_ref[0]
    sh1 = mod[:, 0:D_MODEL]
    sc1 = mod[:, D_MODEL:2 * D_MODEL]
    h = xn * (1.0 + sc1) + sh1
    p = _dot(h.astype(BF16), w_ref[...])
    o_ref[:, OFF_AU:OFF_BX] = jax.nn.gelu(p[:, OFF_AU:OFF_BX])
    o_ref[:, OFF_BX:OFF_BG] = p[:, OFF_BX:OFF_BG]
    o_ref[:, OFF_BG:OFF_CQ] = jax.nn.gelu(p[:, OFF_BG:OFF_CQ])
    s32 = s32_ref[...]
    cosc = cosc_ref[...]
    sinc = sinc_ref[...]
    cq = _seg_rms_norm(p[:, OFF_CQ:OFF_CK], s32, gqc_ref[...])
    ck = _seg_rms_norm(p[:, OFF_CK:OFF_CV], s32, gkc_ref[...])
    o_ref[:, OFF_CQ:OFF_CK] = _rope(cq, cosc, sinc, DIFF_QK_DIM // 4)
    o_ref[:, OFF_CK:OFF_CV] = _rope(ck, cosc, sinc, DIFF_QK_DIM // 4)
    o_ref[:, OFF_CV:OFF_DQ] = p[:, OFF_CV:OFF_DQ]
    s64 = s64_ref[...]
    cosd = cosd_ref[...]
    sind = sind_ref[...]
    dq = _seg_rms_norm(p[:, OFF_DQ:OFF_DK], s64, gqd_ref[...])
    dk = _seg_rms_norm(p[:, OFF_DK:OFF_DV], s64[0:128, 0:128], gkd_ref[...])
    o_ref[:, OFF_DQ:OFF_DK] = _rope(dq, cosd, sind, HEAD_DIM // 4)
    o_ref[:, OFF_DK:OFF_DV] = _rope(dk, cosd[:, 0:128], sind[:, 0:128], HEAD_DIM // 4)
    o_ref[:, OFF_DV:IN_WIDTH] = p[:, OFF_DV:IN_WIDTH]


def _inproj_call(seq_of_tile, rope_of_tile, x, mod3, g1, w_in, s32, s64, gqc, gkc, gqd, gkd, cosc, sinc, cosd, sind):
    full = lambda shape: pl.BlockSpec(shape, lambda i, s, r: (0,) * len(shape))
    rope_spec = pl.BlockSpec((TM, GROUP_WIDTH), lambda i, s, r: (r[i], 0))
    grid_spec = pltpu.PrefetchScalarGridSpec(
        num_scalar_prefetch=2,
        grid=(N_TILES,),
        in_specs=[pl.BlockSpec((TM, D_MODEL), lambda i, s, r: (i, 0)),
                  pl.BlockSpec((1, 1, 6 * D_MODEL), lambda i, s, r: (s[i], 0, 0)),
                  full((1, D_MODEL)),
                  full((D_MODEL, IN_WIDTH)),
                  full((GROUP_WIDTH, GROUP_WIDTH)),
                  full((GROUP_WIDTH, GROUP_WIDTH)),
                  full((1, GROUP_WIDTH)), full((1, GROUP_WIDTH)), full((1, GROUP_WIDTH)), full((1, 128)),
                  rope_spec, rope_spec, rope_spec, rope_spec],
        out_specs=pl.BlockSpec((TM, IN_WIDTH), lambda i, s, r: (i, 0)),
    )
    return pl.pallas_call(
        _inproj_kernel,
        grid_spec=grid_spec,
        out_shape=jax.ShapeDtypeStruct((N_TOK, IN_WIDTH), F32),
        compiler_params=_cparams(("arbitrary",)),
        name="inproj",
    )(seq_of_tile, rope_of_tile, x, mod3, g1, w_in, s32, s64, gqc, gkc, gqd, gkd, cosc, sinc, cosd, sind)


def _gmlp_kernel(u_ref, v_ref, s64_ref, g_ref, ws_ref, bias_ref, o_ref):
    vh = _seg_rms_norm(v_ref[...], s64_ref[...], g_ref[...]).astype(BF16)
    head = lax.broadcasted_iota(jnp.int32, (CHUNK, GROUP_WIDTH), 1) // HEAD_DIM
    for ch in range(TM // CHUNK):
        rows = slice(ch * CHUNK, (ch + 1) * CHUNK)
        vc = vh[rows]
        mixed = bias_ref[...]
        for h in range(N_GROUP_HEADS):
            mixed = mixed + jnp.where(head == h, _dot(ws_ref[h], vc), 0.0)
        o_ref[rows, :] = u_ref[rows, :] * mixed


def _gmlp_call(proj, s64, g, ws, bias):
    return pl.pallas_call(
        _gmlp_kernel,
        grid=(N_TILES,),
        in_specs=[pl.BlockSpec((TM, GROUP_WIDTH), lambda i: (i, OFF_AU // GROUP_WIDTH)),
                  pl.BlockSpec((TM, GROUP_WIDTH), lambda i: (i, OFF_AV // GROUP_WIDTH)),
                  pl.BlockSpec((GROUP_WIDTH, GROUP_WIDTH), lambda i: (0, 0)),
                  pl.BlockSpec((1, GROUP_WIDTH), lambda i: (0, 0)),
                  pl.BlockSpec((N_GROUP_HEADS, CHUNK, CHUNK), lambda i: (0, 0, 0)),
                  pl.BlockSpec((CHUNK, GROUP_WIDTH), lambda i: (0, 0))],
        out_specs=pl.BlockSpec((TM, GROUP_WIDTH), lambda i: (i, 0)),
        out_shape=jax.ShapeDtypeStruct((N_TOK, GROUP_WIDTH), F32),
        compiler_params=_cparams(("arbitrary",)),
        name="gmlp",
    )(proj, proj, s64, g, ws, bias)


def _scan_chunk(a, b, reverse):
    n = a.shape[0]
    row = lax.broadcasted_iota(jnp.int32, a.shape, 0)
    s = 1
    while s < n:
        if reverse:
            keep = row < n - s
            shift = n - s
        else:
            keep = row >= s
            shift = s
        a_prev = jnp.where(keep, pltpu.roll(a, shift, axis=0), 1.0)
        b_prev = jnp.where(keep, pltpu.roll(b, shift, axis=0), 0.0)
        b = a * b_prev + b
        a = a * a_prev
        s *= 2
    return a, b


def _lru_kernel(x_ref, g_ref, h0_ref, cw_ref, cb_ref, wcat_ref, bcat_ref, lam_ref, y_ref, st_ref,
                xpad, a_f, b_f, a_b, b_b, h_f, *, seq_len):
    nc = seq_len // LRU_CHUNK
    w = GROUP_WIDTH
    zeros8 = jnp.zeros((8, w), F32)
    xpad[0:8, :] = zeros8
    xpad[seq_len + 8:seq_len + 16, :] = zeros8
    xpad[8:seq_len + 8, :] = x_ref[...]
    sp = _softplus(-lam_ref[...])
    cw = cw_ref[...]
    cb = cb_ref[...]
    win_rows = LRU_CHUNK + 16

    def gates(c, carry):
        r0 = pl.multiple_of(c * LRU_CHUNK, LRU_CHUNK)
        win = xpad[pl.ds(r0, win_rows), :]
        inner = slice(8, 8 + LRU_CHUNK)
        xc = cb + pltpu.roll(win, 2, axis=0)[inner] * cw[0:1]
        xc = xc + pltpu.roll(win, 1, axis=0)[inner] * cw[1:2]
        xc = xc + win[inner] * cw[2:3]
        xc = xc + pltpu.roll(win, win_rows - 1, axis=0)[inner] * cw[3:4]
        sg = jax.nn.sigmoid(_dot(xc.astype(BF16), wcat_ref[...]) + bcat_ref[...])
        for d, (a_ref, b_ref) in enumerate(((a_f, b_f), (a_b, b_b))):
            r = sg[:, (2 * d) * w:(2 * d + 1) * w]
            i = sg[:, (2 * d + 1) * w:(2 * d + 2) * w]
            log_a = (-LRU_C * r) * sp[d:d + 1]
            a = jnp.exp(log_a)
            a_ref[pl.ds(r0, LRU_CHUNK), :] = a
            b_ref[pl.ds(r0, LRU_CHUNK), :] = jnp.sqrt(-jnp.tanh(log_a) * (a * a + 1.0)) * (i * xc)
        return carry

    lax.fori_loop(0, nc, gates, 0)

    def fwd(c, carry):
        r0 = pl.multiple_of(c * LRU_CHUNK, LRU_CHUNK)
        a_cum, h_loc = _scan_chunk(a_f[pl.ds(r0, LRU_CHUNK), :], b_f[pl.ds(r0, LRU_CHUNK), :], False)
        h = h_loc + a_cum * carry
        h_f[pl.ds(r0, LRU_CHUNK), :] = h
        return h[LRU_CHUNK - 1:LRU_CHUNK, :]

    s_f = lax.fori_loop(0, nc, fwd, h0_ref[0, 0:1, :])

    def bwd(k, carry):
        c = nc - 1 - k
        r0 = pl.multiple_of(c * LRU_CHUNK, LRU_CHUNK)
        a_cum, h_loc = _scan_chunk(a_b[pl.ds(r0, LRU_CHUNK), :], b_b[pl.ds(r0, LRU_CHUNK), :], True)
        h = h_loc + a_cum * carry
        y_ref[pl.ds(r0, LRU_CHUNK), :] = (h_f[pl.ds(r0, LRU_CHUNK), :] + h) * g_ref[pl.ds(r0, LRU_CHUNK), :]
        return h[0:1, :]

    s_b = lax.fori_loop(0, nc, bwd, h0_ref[0, 1:2, :])
    st_ref[0, 0:1, :] = s_f
    st_ref[0, 1:2, :] = s_b


def _lru_call(proj, h0, cw, cb, wcat, bcat, lam, *, seq_len, n_seq, row_block0):
    w = GROUP_WIDTH
    full = lambda shape: pl.BlockSpec(shape, lambda b: (0,) * len(shape))
    seq_block = lambda col: pl.BlockSpec((seq_len, w), lambda b: (b + row_block0, col))
    return pl.pallas_call(
        functools.partial(_lru_kernel, seq_len=seq_len),
        grid=(n_seq,),
        in_specs=[seq_block(OFF_BX // w), seq_block(OFF_BG // w),
                  pl.BlockSpec((1, 2, w), lambda b: (b, 0, 0)),
                  full((CONV_W, w)), full((1, w)), full((w, 4 * w)), full((1, 4 * w)), full((2, w))],
        out_specs=[pl.BlockSpec((seq_len, w), lambda b: (b, 0)),
                   pl.BlockSpec((1, 2, w), lambda b: (b, 0, 0))],
        out_shape=[jax.ShapeDtypeStruct((n_seq * seq_len, w), F32),
                   jax.ShapeDtypeStruct((n_seq, 2, w), F32)],
        scratch_shapes=[pltpu.VMEM((seq_len + 16, w), F32)] + [pltpu.VMEM((seq_len, w), F32)] * 5,
        compiler_params=_cparams(("arbitrary",)),
        name="lru_%d" % seq_len,
    )(proj, proj, h0, cw, cb, wcat, bcat, lam)


def _diff_lambda(ld_ref, lam_init):
    ld = ld_ref[...]
    l1 = jnp.sum(ld[0:1] * ld[1:2], axis=-1, keepdims=True)
    l2 = jnp.sum(ld[2:3] * ld[3:4], axis=-1, keepdims=True)
    return jnp.exp(l1) - jnp.exp(l2) + lam_init


def _diff_attn_body(q, keys, values, lam, s64, g, lam_init):
    tq = q.shape[0]
    lane = lax.broadcasted_iota(jnp.int32, (1, GROUP_WIDTH), 1)
    scale = DIFF_QK_DIM ** -0.5
    o = jnp.zeros((tq, GROUP_WIDTH), F32)
    for h in range(N_GROUP_HEADS):
        probs = []
        for i in range(2):
            seg = h * 2 + i
            qm = jnp.where(lane // DIFF_QK_DIM == seg, q, 0.0).astype(BF16)
            s = [_dot_nt(qm, k) * scale for k in keys]
            m = functools.reduce(jnp.maximum, [jnp.max(x, axis=-1, keepdims=True) for x in s])
            e = [jnp.exp(x - m) for x in s]
            den = functools.reduce(jnp.add, [jnp.sum(x, axis=-1, keepdims=True) for x in e])
            inv = 1.0 / den
            probs.append([x * inv for x in e])
        oh = None
        for p0, p1, v in zip(probs[0], probs[1], values):
            part = _dot((p0 - lam * p1).astype(BF16), v)
            oh = part if oh is None else oh + part
        o = o + jnp.where(lane // HEAD_DIM == h, oh, 0.0)
    return _seg_rms_norm(o, s64, g) * (1.0 - lam_init)


def _diff_prompt_kernel(q_ref, k_ref, v_ref, ld_ref, s64_ref, g_ref, o_ref, *, lam_init):
    lam = _diff_lambda(ld_ref, lam_init)
    o_ref[...] = _diff_attn_body(q_ref[...], [k_ref[...].astype(BF16)], [v_ref[...].astype(BF16)],
                                 lam, s64_ref[...], g_ref[...], lam_init)


def _diff_sample_kernel(q_ref, k_ref, v_ref, ck_ref, cv_ref, ld_ref, s64_ref, g_ref, o_ref, *, lam_init):
    lam = _diff_lambda(ld_ref, lam_init)
    keys = [ck_ref[0].astype(BF16), k_ref[...].astype(BF16)]
    values = [cv_ref[0].astype(BF16), v_ref[...].astype(BF16)]
    o_ref[...] = _diff_attn_body(q_ref[...], keys, values, lam, s64_ref[...], g_ref[...], lam_init)


def _diff_prompt_call(proj, ld, s64, g, lam_init):
    w = GROUP_WIDTH
    full = lambda shape: pl.BlockSpec(shape, lambda b: (0,) * len(shape))
    return pl.pallas_call(
        functools.partial(_diff_prompt_kernel, lam_init=lam_init),
        grid=(BATCH,),
        in_specs=[pl.BlockSpec((SEQ, w), lambda b: (b, OFF_CQ // w)),
                  pl.BlockSpec((SEQ, w), lambda b: (b, OFF_CK // w)),
                  pl.BlockSpec((SEQ, w), lambda b: (b, OFF_CV // w)),
                  full((4, DIFF_QK_DIM)), full((w, w)), full((1, w))],
        out_specs=pl.BlockSpec((SEQ, w), lambda b: (b, 0)),
        out_shape=jax.ShapeDtypeStruct((N_PROMPT, w), F32),
        compiler_params=_cparams(("arbitrary",)),
        name="diff_prompt",
    )(proj, proj, proj, ld, s64, g)


def _diff_sample_call(proj, ctx_k, ctx_v, ld, s64, g, lam_init):
    w = GROUP_WIDTH
    tq = 256
    nq = DEC_SEQ // tq
    full = lambda shape: pl.BlockSpec(shape, lambda b, i: (0,) * len(shape))
    seq_block0 = N_PROMPT // DEC_SEQ
    return pl.pallas_call(
        functools.partial(_diff_sample_kernel, lam_init=lam_init),
        grid=(DEC_BATCH, nq),
        in_specs=[pl.BlockSpec((tq, w), lambda b, i: (N_PROMPT // tq + b * nq + i, OFF_CQ // w)),
                  pl.BlockSpec((DEC_SEQ, w), lambda b, i: (seq_block0 + b, OFF_CK // w)),
                  pl.BlockSpec((DEC_SEQ, w), lambda b, i: (seq_block0 + b, OFF_CV // w)),
                  pl.BlockSpec((1, PAST_LEN, w), lambda b, i: (b, 0, 0)),
                  pl.BlockSpec((1, PAST_LEN, w), lambda b, i: (b, 0, 0)),
                  full((4, DIFF_QK_DIM)), full((w, w)), full((1, w))],
        out_specs=pl.BlockSpec((tq, w), lambda b, i: (b * nq + i, 0)),
        out_shape=jax.ShapeDtypeStruct((N_SAMPLE, w), F32),
        compiler_params=_cparams(("arbitrary", "arbitrary")),
        name="diff_sample",
    )(proj, proj, proj, ctx_k, ctx_v, ld, s64, g)


def _sink_attn_body(q, keys, values, masks, sink_ref):
    tq = q.shape[0]
    lane = lax.broadcasted_iota(jnp.int32, (1, 128), 1)
    scale = HEAD_DIM ** -0.5
    outs = []
    for grp in range(SWA_GROUPS):
        qg = q[:, grp * 128:(grp + 1) * 128]
        og = jnp.zeros((tq, 128), F32)
        for kh in range(SWA_KV_HEADS):
            sink = sink_ref[kh * SWA_GROUPS + grp]
            qm = jnp.where(lane // HEAD_DIM == kh, qg, 0.0).astype(BF16)
            s = []
            for k, msk in zip(keys, masks):
                x = _dot_nt(qm, k) * scale
                s.append(x if msk is None else jnp.where(msk, x, -jnp.inf))
            m = functools.reduce(jnp.maximum, [jnp.max(x, axis=-1, keepdims=True) for x in s])
            m = jnp.maximum(m, sink)
            e = [jnp.exp(x - m) for x in s]
            den = functools.reduce(jnp.add, [jnp.sum(x, axis=-1, keepdims=True) for x in e]) + jnp.exp(sink - m)
            inv = 1.0 / den
            oh = None
            for x, v in zip(e, values):
                part = _dot((x * inv).astype(BF16), v)
                oh = part if oh is None else oh + part
            og = og + jnp.where(lane // HEAD_DIM == kh, oh, 0.0)
        outs.append(og)
    return outs


def _swa_prompt_kernel(sink_ref, q_ref, k_ref, v_ref, o_ref):
    outs = _sink_attn_body(q_ref[...], [k_ref[...].astype(BF16)], [v_ref[...].astype(BF16)], [None], sink_ref)
    for grp in range(SWA_GROUPS):
        o_ref[:, grp * 128:(grp + 1) * 128] = outs[grp]


def _swa_sample_kernel(sink_ref, q_ref, kp_ref, kc_ref, kn_ref, vp_ref, vc_ref, vn_ref, ck_ref, cv_ref, o_ref):
    n = pl.program_id(1)
    nb = pl.num_programs(1)
    r = lax.broadcasted_iota(jnp.int32, (WINDOW, WINDOW), 0)
    c = lax.broadcasted_iota(jnp.int32, (WINDOW, WINDOW), 1)
    mask_prev = c >= r + jnp.where(n > 0, 0, WINDOW)
    mask_next = c <= r - jnp.where(n < nb - 1, 0, WINDOW)
    keys = [ck_ref[0].astype(BF16), kp_ref[...].astype(BF16), kc_ref[...].astype(BF16), kn_ref[...].astype(BF16)]
    values = [cv_ref[0].astype(BF16), vp_ref[...].astype(BF16), vc_ref[...].astype(BF16), vn_ref[...].astype(BF16)]
    outs = _sink_attn_body(q_ref[...], keys, values, [None, mask_prev, None, mask_next], sink_ref)
    for grp in range(SWA_GROUPS):
        o_ref[:, grp * 128:(grp + 1) * 128] = outs[grp]


def _swa_prompt_call(sink, proj):
    w = GROUP_WIDTH
    return pl.pallas_call(
        _swa_prompt_kernel,
        grid=(BATCH,),
        in_specs=[pl.BlockSpec(memory_space=pltpu.SMEM),
                  pl.BlockSpec((SEQ, w), lambda b: (b, OFF_DQ // w)),
                  pl.BlockSpec((SEQ, 128), lambda b: (b, OFF_DK // 128)),
                  pl.BlockSpec((SEQ, 128), lambda b: (b, OFF_DV // 128))],
        out_specs=pl.BlockSpec((SEQ, w), lambda b: (b, 0)),
        out_shape=jax.ShapeDtypeStruct((N_PROMPT, w), F32),
        compiler_params=_cparams(("arbitrary",)),
        name="swa_prompt",
    )(sink, proj, proj, proj)


def _swa_sample_call(sink, proj, ctx_k, ctx_v):
    w = GROUP_WIDTH
    tq = WINDOW
    nq = DEC_SEQ // tq
    row0 = N_PROMPT // tq

    def kv_spec(col, delta):
        def index(b, i):
            j = jnp.clip(i + delta, 0, nq - 1)
            return (row0 + b * nq + j, col)
        return pl.BlockSpec((tq, 128), index)

    ctx_spec = pl.BlockSpec((1, PAST_LEN, 128), lambda b, i: (b, 0, 0))
    return pl.pallas_call(
        _swa_sample_kernel,
        grid=(DEC_BATCH, nq),
        in_specs=[pl.BlockSpec(memory_space=pltpu.SMEM),
                  pl.BlockSpec((tq, w), lambda b, i: (row0 + b * nq + i, OFF_DQ // w)),
                  kv_spec(OFF_DK // 128, -1), kv_spec(OFF_DK // 128, 0), kv_spec(OFF_DK // 128, 1),
                  kv_spec(OFF_DV // 128, -1), kv_spec(OFF_DV // 128, 0), kv_spec(OFF_DV // 128, 1),
                  ctx_spec, ctx_spec],
        out_specs=pl.BlockSpec((tq, w), lambda b, i: (b * nq + i, 0)),
        out_shape=jax.ShapeDtypeStruct((N_SAMPLE, w), F32),
        compiler_params=_cparams(("arbitrary", "arbitrary")),
        name="swa_sample",
    )(sink, proj, proj, proj, proj, proj, proj, proj, ctx_k, ctx_v)


def _outproj_kernel(seq_ref, ya_ref, yb_ref, yc_ref, yd_ref, x_ref, mod_ref, w_ref, g2_ref, rw_ref, rb_ref,
                    x1_ref, h2_ref, comb_ref):
    del seq_ref
    w = GROUP_WIDTH
    mix = _dot(ya_ref[...].astype(BF16), w_ref[0:w, :])
    mix = mix + _dot(yb_ref[...].astype(BF16), w_ref[w:2 * w, :])
    mix = mix + _dot(yc_ref[...].astype(BF16), w_ref[2 * w:3 * w, :])
    mix = mix + _dot(yd_ref[...].astype(BF16), w_ref[3 * w:4 * w, :])
    mod = mod_ref[0]
    g1 = mod[:, 2 * D_MODEL:3 * D_MODEL]
    sh2 = mod[:, 3 * D_MODEL:4 * D_MODEL]
    sc2 = mod[:, 4 * D_MODEL:5 * D_MODEL]
    x1 = x_ref[...] + g1 * mix
    x1_ref[...] = x1
    xn = x1 * lax.rsqrt(jnp.mean(x1 * x1, axis=-1, keepdims=True) + EPS) * g2_ref[...]
    h2 = xn * (1.0 + sc2) + sh2
    h2_ref[...] = h2.astype(BF16)
    h_hi, h_lo = _split_bf16(h2)
    r_hi, r_lo = _split_bf16(rw_ref[...])
    logits = _dot(h_hi, r_hi) + (_dot(h_hi, r_lo) + _dot(h_lo, r_hi)) + rb_ref[...]
    lane = lax.broadcasted_iota(jnp.int32, logits.shape, 1)
    work = logits
    picks = []
    for _ in range(TOP_K):
        m = jnp.max(work, axis=-1, keepdims=True)
        idx = jnp.min(jnp.where(work == m, lane, N_EXPERTS), axis=-1, keepdims=True)
        hit = lane == idx
        picks.append((m, hit))
        work = jnp.where(hit, -jnp.inf, work)
    top = picks[0][0]
    ex = [jnp.exp(m - top) for m, _ in picks]
    inv = 1.0 / functools.reduce(jnp.add, ex)
    comb = jnp.zeros(logits.shape, F32)
    for e, (_, hit) in zip(ex, picks):
        comb = comb + jnp.where(hit, e * inv, 0.0)
    comb_ref[...] = comb


def _outproj_call(seq_of_tile, ya, yb, yc, yd, x, mod3, w_out, g2, rw, rb):
    w = GROUP_WIDTH
    full = lambda shape: pl.BlockSpec(shape, lambda i, s: (0,) * len(shape))
    tile = lambda width: pl.BlockSpec((TM, width), lambda i, s: (i, 0))
    return pl.pallas_call(
        _outproj_kernel,
        grid_spec=pltpu.PrefetchScalarGridSpec(
            num_scalar_prefetch=1,
            grid=(N_TILES,),
            in_specs=[tile(w), tile(w), tile(w), tile(w), tile(D_MODEL),
                      pl.BlockSpec((1, 1, 6 * D_MODEL), lambda i, s: (s[i], 0, 0)),
                      full((D_MODEL, D_MODEL)), full((1, D_MODEL)),
                      full((D_MODEL, N_EXPERTS)), full((1, N_EXPERTS))],
            out_specs=[tile(D_MODEL), tile(D_MODEL), tile(N_EXPERTS)],
        ),
        out_shape=[jax.ShapeDtypeStruct((N_TOK, D_MODEL), F32),
                   jax.ShapeDtypeStruct((N_TOK, D_MODEL), BF16),
                   jax.ShapeDtypeStruct((N_TOK, N_EXPERTS), F32)],
        compiler_params=_cparams(("arbitrary",)),
        name="outproj",
    )(seq_of_tile, ya, yb, yc, yd, x, mod3, w_out, g2, rw, rb)


def _moe_kernel(seq_ref, h_ref, comb_ref, wgu_ref, bgu_ref, wd_ref, bd_ref, x1_ref, mod_ref, o_ref, acc_ref):
    del seq_ref
    e = pl.program_id(1)

    @pl.when(e == 0)
    def _():
        acc_ref[...] = jnp.zeros_like(acc_ref)

    gu = _dot(h_ref[...], wgu_ref[0]) + bgu_ref[0]
    gate = jnp.minimum(gu[:, :D_FF], SWIGLU_LIMIT)
    up = jnp.clip(gu[:, D_FF:], -SWIGLU_LIMIT, SWIGLU_LIMIT)
    act = (up + 1.0) * gate * jax.nn.sigmoid(SWIGLU_ALPHA * gate)
    y = _dot(act.astype(BF16), wd_ref[0]) + bd_ref[0]
    lane = lax.broadcasted_iota(jnp.int32, (1, N_EXPERTS), 1)
    cw = jnp.sum(jnp.where(lane == e, comb_ref[...], 0.0), axis=-1, keepdims=True)
    acc_ref[...] += cw * y

    @pl.when(e == N_EXPERTS - 1)
    def _():
        g2 = mod_ref[0][:, 5 * D_MODEL:6 * D_MODEL]
        o_ref[...] = x1_ref[...] + g2 * acc_ref[...]


def _moe_call(seq_of_moe_tile, h2, comb, wgu, bgu, wd, bd, x1, mod3):
    tm = MOE_TM
    return pl.pallas_call(
        _moe_kernel,
        grid_spec=pltpu.PrefetchScalarGridSpec(
            num_scalar_prefetch=1,
            grid=(N_TOK // tm, N_EXPERTS),
            in_specs=[pl.BlockSpec((tm, D_MODEL), lambda i, e, s: (i, 0)),
                      pl.BlockSpec((tm, N_EXPERTS), lambda i, e, s: (i, 0)),
                      pl.BlockSpec((1, D_MODEL, 2 * D_FF), lambda i, e, s: (e, 0, 0)),
                      pl.BlockSpec((1, 1, 2 * D_FF), lambda i, e, s: (e, 0, 0)),
                      pl.BlockSpec((1, D_FF, D_MODEL), lambda i, e, s: (e, 0, 0)),
                      pl.BlockSpec((1, 1, D_MODEL), lambda i, e, s: (e, 0, 0)),
                      pl.BlockSpec((tm, D_MODEL), lambda i, e, s: (i, 0)),
                      pl.BlockSpec((1, 1, 6 * D_MODEL), lambda i, e, s: (s[i], 0, 0))],
            out_specs=pl.BlockSpec((tm, D_MODEL), lambda i, e, s: (i, 0)),
            scratch_shapes=[pltpu.VMEM((tm, D_MODEL), F32)],
        ),
        out_shape=jax.ShapeDtypeStruct((N_TOK, D_MODEL), F32),
        compiler_params=_cparams(("arbitrary", "arbitrary")),
        name="moe",
    )(seq_of_moe_tile, h2, comb, wgu, bgu, wd, bd, x1, mod3)


def _segment_matrix(width, seg):
    idx = np.arange(width) // seg
    return jnp.asarray((idx[:, None] == idx[None, :]).astype(np.float32) / seg, BF16)


def _rope_tables(rot_dim, width):
    rows = DEC_SEQ // GRID_W
    nf = rot_dim // 4
    inv = 1.0 / (ROPE_BASE ** (jnp.arange(nf, dtype=F32) / nf))
    row = jnp.repeat(jnp.arange(rows, dtype=F32), GRID_W)
    col = jnp.tile(jnp.arange(GRID_W, dtype=F32), rows)
    ang = jnp.stack([row[:, None] * inv, col[:, None] * inv], axis=1)
    cos, sin = jnp.cos(ang), jnp.sin(ang)
    cos_r = jnp.concatenate([cos[:, 0], cos[:, 0], cos[:, 1], cos[:, 1]], axis=-1)
    sin_r = jnp.concatenate([-sin[:, 0], sin[:, 0], -sin[:, 1], sin[:, 1]], axis=-1)
    reps = width // rot_dim
    cos_t = jnp.concatenate([jnp.ones((TM, width), F32), jnp.tile(cos_r, (1, reps))], axis=0)
    sin_t = jnp.concatenate([jnp.zeros((TM, width), F32), jnp.tile(sin_r, (1, reps))], axis=0)
    return cos_t, sin_t


def _block_diag(wb):
    nb, n, _ = wb.shape
    out = jnp.zeros((nb * n, nb * n), wb.dtype)
    for i in range(nb):
        out = out.at[i * n:(i + 1) * n, i * n:(i + 1) * n].set(wb[i])
    return out


_DQ_HEADS = [kh * SWA_GROUPS + g for g in range(SWA_GROUPS) for kh in range(SWA_KV_HEADS)]


def kernel(x_prompt, x_sample, c, cache_diff_k, cache_diff_v, cache_swa_k, cache_swa_v, state_lru, c_ctx, mod_w, mod_b, norm1_g, norm2_g, w_in, w_out, mlp_vnorm_g, mlp_ws, mlp_bs, lru_conv_w, lru_conv_b, lru_wa, lru_ba, lru_wx, lru_bx, lru_lambda, diff_qnorm_g, diff_knorm_g, diff_lambda, diff_subln_g, swa_qnorm_g, swa_knorm_g, swa_sink, router_w, router_b, moe_w_gu, moe_b_gu, moe_w_down, moe_b_down):
    params = dict(locals())
    x = jnp.concatenate([x_prompt.reshape(N_PROMPT, D_MODEL), x_sample.reshape(N_SAMPLE, D_MODEL)], axis=0)
    cond = jnp.concatenate([c_ctx[None], c, jnp.zeros((N_COND - 1 - DEC_BATCH, D_MODEL), F32)], axis=0)
    consts = _constants()
    dk_l, dv_l, sk_l, sv_l, lru_l = [], [], [], [], []
    for l in range(DEPTH):
        st = _mixer_stage(x, cond, params, l, consts)
        x = _ffn_stage(x, st, params, l, consts)
        pp = st["proj"][:N_PROMPT]
        dk_l.append(pp[:, OFF_CK:OFF_CV].reshape(BATCH, SEQ, N_GROUP_HEADS, 2, DIFF_QK_DIM))
        dv_l.append(pp[:, OFF_CV:OFF_DQ].reshape(BATCH, SEQ, N_GROUP_HEADS, HEAD_DIM))
        sk_l.append(pp[:, OFF_DK:OFF_DV].reshape(BATCH, SEQ, SWA_KV_HEADS, HEAD_DIM))
        sv_l.append(pp[:, OFF_DV:IN_WIDTH].reshape(BATCH, SEQ, SWA_KV_HEADS, HEAD_DIM))
        lru_l.append(st["st_p"])

    y_p = x[:N_PROMPT].reshape(BATCH, SEQ, D_MODEL)
    y_s = x[N_PROMPT:].reshape(DEC_BATCH, DEC_SEQ, D_MODEL)
    return (y_p, y_s, jnp.stack(dk_l, axis=1), jnp.stack(dv_l, axis=1), jnp.stack(sk_l, axis=1),
            jnp.stack(sv_l, axis=1), jnp.stack(lru_l, axis=1))


def _constants():
    w = GROUP_WIDTH
    tiles = np.arange(N_TILES)
    seq_np = np.where(tiles < PROMPT_TILES, 0, 1 + (tiles - PROMPT_TILES) // SAMPLE_TILES_PER_SEQ)
    rope_np = np.where(tiles < PROMPT_TILES, 0, 1 + (tiles - PROMPT_TILES) % SAMPLE_TILES_PER_SEQ)
    seq_of_tile = jnp.asarray(seq_np, jnp.int32)
    rope_of_tile = jnp.asarray(rope_np, jnp.int32)
    seq_of_moe_tile = jnp.asarray(seq_np[::MOE_TM // TM], jnp.int32)

    s32 = _segment_matrix(w, DIFF_QK_DIM)
    s64 = _segment_matrix(w, HEAD_DIM)
    cosc, sinc = _rope_tables(DIFF_QK_DIM, w)
    cosd, sind = _rope_tables(HEAD_DIM, w)
    dq_cols = np.concatenate([np.arange(h * HEAD_DIM, (h + 1) * HEAD_DIM) for h in _DQ_HEADS])
    return dict(seq_of_tile=seq_of_tile, rope_of_tile=rope_of_tile, seq_of_moe_tile=seq_of_moe_tile, s32=s32, s64=s64,
                cosc=cosc, sinc=sinc, cosd=cosd, sind=sind, dq_cols=dq_cols)


def _mixer_stage(x, cond, params, l, consts):
    w = GROUP_WIDTH
    (mod_w, mod_b, norm1_g, w_in, mlp_vnorm_g, mlp_ws, mlp_bs, lru_conv_w, lru_conv_b, lru_wa, lru_ba, lru_wx, lru_bx,
     lru_lambda, diff_qnorm_g, diff_knorm_g, diff_lambda, diff_subln_g, swa_qnorm_g, swa_knorm_g, swa_sink,
     cache_diff_k, cache_diff_v, cache_swa_k, cache_swa_v, state_lru) = (params[k] for k in (
         "mod_w", "mod_b", "norm1_g", "w_in", "mlp_vnorm_g", "mlp_ws", "mlp_bs", "lru_conv_w", "lru_conv_b", "lru_wa",
         "lru_ba", "lru_wx", "lru_bx", "lru_lambda", "diff_qnorm_g", "diff_knorm_g", "diff_lambda", "diff_subln_g",
         "swa_qnorm_g", "swa_knorm_g", "swa_sink", "cache_diff_k", "cache_diff_v", "cache_swa_k", "cache_swa_v",
         "state_lru"))
    seq_of_tile, rope_of_tile, s32, s64, cosc, sinc, cosd, sind, dq_cols = (consts[k] for k in (
        "seq_of_tile", "rope_of_tile", "s32", "s64", "cosc", "sinc", "cosd", "sind", "dq_cols"))
    if True:
        lam_init = 0.8 - 0.6 * math.exp(-0.3 * l)
        w_in_l = w_in[l]
        w_in_l = jnp.concatenate([w_in_l[:, :OFF_DQ], w_in_l[:, OFF_DQ:OFF_DK][:, dq_cols], w_in_l[:, OFF_DK:]], axis=1)
        w_in_l = w_in_l.astype(BF16)
        gqc = jnp.tile(diff_qnorm_g[l].reshape(1, 2 * DIFF_QK_DIM), (1, N_GROUP_HEADS))
        gkc = jnp.tile(diff_knorm_g[l].reshape(1, 2 * DIFF_QK_DIM), (1, N_GROUP_HEADS))
        gqd = jnp.tile(swa_qnorm_g[l].reshape(1, HEAD_DIM), (1, 4))
        gkd = jnp.tile(swa_knorm_g[l].reshape(1, HEAD_DIM), (1, SWA_KV_HEADS))
        g_sub = jnp.tile(diff_subln_g[l].reshape(1, HEAD_DIM), (1, N_GROUP_HEADS))
        g_mlp = mlp_vnorm_g[l].reshape(1, w)
        mlp_bias = jnp.repeat(mlp_bs[l].T, HEAD_DIM, axis=1)
        wcat = jnp.concatenate([_block_diag(lru_wa[l, 0]), _block_diag(lru_wx[l, 0]),
                                _block_diag(lru_wa[l, 1]), _block_diag(lru_wx[l, 1])], axis=1).astype(BF16)
        bcat = jnp.concatenate([lru_ba[l, 0].reshape(1, w), lru_bx[l, 0].reshape(1, w),
                                lru_ba[l, 1].reshape(1, w), lru_bx[l, 1].reshape(1, w)], axis=1)
        sink = swa_sink[l].astype(F32)

        mod3 = _mod_call(cond, mod_w[l], mod_b[l]).reshape(N_COND, 1, 6 * D_MODEL)
        proj = _inproj_call(seq_of_tile, rope_of_tile, x, mod3, norm1_g[l].reshape(1, D_MODEL), w_in_l, s32, s64,
                            gqc, gkc, gqd, gkd, cosc, sinc, cosd, sind)

        ya = _gmlp_call(proj, s64, g_mlp, mlp_ws[l].astype(BF16), mlp_bias)

        lru_args = (lru_conv_w[l], lru_conv_b[l].reshape(1, w), wcat, bcat, lru_lambda[l])
        yb_p, st_p = _lru_call(proj, jnp.zeros((BATCH, 2, w), F32), *lru_args, seq_len=SEQ, n_seq=BATCH, row_block0=0)
        yb_s, _ = _lru_call(proj, state_lru[:, l], *lru_args, seq_len=DEC_SEQ, n_seq=DEC_BATCH,
                            row_block0=N_PROMPT // DEC_SEQ)

        yc_p = _diff_prompt_call(proj, diff_lambda[l], s64, g_sub, lam_init)
        yc_s = _diff_sample_call(proj, cache_diff_k[:, l].reshape(DEC_BATCH, PAST_LEN, w),
                                 cache_diff_v[:, l].reshape(DEC_BATCH, PAST_LEN, w), diff_lambda[l], s64, g_sub, lam_init)

        yd_p = _swa_prompt_call(sink, proj)
        yd_s = _swa_sample_call(sink, proj, cache_swa_k[:, l].reshape(DEC_BATCH, PAST_LEN, 128),
                                cache_swa_v[:, l].reshape(DEC_BATCH, PAST_LEN, 128))

        yb = jnp.concatenate([yb_p, yb_s], axis=0)
        yc = jnp.concatenate([yc_p, yc_s], axis=0)
        yd = jnp.concatenate([yd_p, yd_s], axis=0)
    return dict(mod3=mod3, proj=proj, ya=ya, yb=yb, yc=yc, yd=yd, st_p=st_p)


def _ffn_stage(x, st, params, l, consts):
    w = GROUP_WIDTH
    w_out, norm2_g, router_w, router_b, moe_w_gu, moe_b_gu, moe_w_down, moe_b_down = (params[k] for k in (
        "w_out", "norm2_g", "router_w", "router_b", "moe_w_gu", "moe_b_gu", "moe_w_down", "moe_b_down"))
    w_out_l = w_out[l]
    w_out_l = jnp.concatenate([w_out_l[:3 * w], w_out_l[3 * w:][consts["dq_cols"]]], axis=0).astype(BF16)
    x1, h2, comb = _outproj_call(consts["seq_of_tile"], st["ya"], st["yb"], st["yc"], st["yd"], x, st["mod3"], w_out_l,
                                 norm2_g[l].reshape(1, D_MODEL), router_w[l], router_b[l].reshape(1, N_EXPERTS))
    return _moe_call(consts["seq_of_moe_tile"], h2, comb, moe_w_gu[l].astype(BF16),
                     moe_b_gu[l].reshape(N_EXPERTS, 1, 2 * D_FF), moe_w_down[l].astype(BF16),
                     moe_b_down[l].reshape(N_EXPERTS, 1, D_MODEL), x1, st["mod3"])
```

```python
import functools
import math

import jax
import jax.numpy as jnp
import numpy as np
from jax import lax
from jax.experimental import pallas as pl
from jax.experimental.pallas import tpu as pltpu

F32 = jnp.float32
BF16 = jnp.bfloat16

D_MODEL = 1024
BATCH = 16
SEQ = 256
DEPTH = 2
DEC_BATCH = 4
DEC_SEQ = 2048
PAST_LEN = 512
GRID_W = 64
HEAD_DIM = 64
GROUP_WIDTH = 256
N_GROUP_HEADS = 4
CHUNK = 128
LRU_C = 8.0
CONV_W = 4
DIFF_QK_DIM = 32
SWA_KV_HEADS = 2
SWA_GROUPS = 2
WINDOW = 128
N_EXPERTS = 32
TOP_K = 4
D_FF = 1024
SWIGLU_LIMIT = 7.0
SWIGLU_ALPHA = 1.702
ROPE_BASE = 10000.0
EPS = 1e-6

N_PROMPT = BATCH * SEQ
N_SAMPLE = DEC_BATCH * DEC_SEQ
N_TOK = N_PROMPT + N_SAMPLE
N_COND = 8
TM = 256
N_TILES = N_TOK // TM
PROMPT_TILES = N_PROMPT // TM
SAMPLE_TILES_PER_SEQ = DEC_SEQ // TM
IN_WIDTH = 2304
OFF_AU, OFF_AV, OFF_BX, OFF_BG, OFF_CQ, OFF_CK, OFF_CV, OFF_DQ, OFF_DK, OFF_DV = (
    0, 256, 512, 768, 1024, 1280, 1536, 1792, 2048, 2176)
TOPK_PAD = 8
MOE_TILE = 256
MOE_TB = 4096
MOE_MAX_TILES = (N_TOK // MOE_TB) * (MOE_TB * TOP_K // MOE_TILE + N_EXPERTS)
MOE_XT_STRIDE = MOE_TILE + 8
MOE_RMW_BATCH = 4
LRU_CHUNK = 256
VMEM_LIMIT = 56 * 1024 * 1024


def _cparams(sem):
    return pltpu.CompilerParams(dimension_semantics=sem, vmem_limit_bytes=VMEM_LIMIT)


def _dot(a, b):
    return jnp.dot(a, b, preferred_element_type=F32)


def _dot_nt(a, b):
    return lax.dot_general(a, b, (((1,), (1,)), ((), ())), preferred_element_type=F32)


def _split_bf16(x):
    hi = x.astype(BF16)
    lo = (x - hi.astype(F32)).astype(BF16)
    return hi, lo


def _seg_rms_norm(x, seg_mat, g):
    hi, lo = _split_bf16(x * x)
    ms = _dot(hi, seg_mat) + _dot(lo, seg_mat)
    return x * lax.rsqrt(ms + EPS) * g


def _rope(x, cos_t, sin_t, nf):
    n = x.shape[-1]
    lane = lax.broadcasted_iota(jnp.int32, x.shape, 1)
    first = (lane & (2 * nf - 1)) < nf
    partner = jnp.where(first, pltpu.roll(x, n - nf, axis=1), pltpu.roll(x, nf, axis=1))
    return x * cos_t + partner * sin_t


def _softplus(x):
    return jnp.maximum(x, 0.0) + jnp.log1p(jnp.exp(-jnp.abs(x)))


def _mod_kernel(cond_ref, w_ref, b_ref, o_ref):
    c = cond_ref[...]
    s = c * jax.nn.sigmoid(c)
    o_ref[...] = _dot(s.astype(BF16), w_ref[...].astype(BF16)) + b_ref[...]


def _mod_call(cond, w, b):
    nb = 6
    return pl.pallas_call(
        _mod_kernel,
        grid=(nb,),
        in_specs=[pl.BlockSpec((N_COND, D_MODEL), lambda j: (0, 0)),
                  pl.BlockSpec((D_MODEL, D_MODEL), lambda j: (0, j)),
                  pl.BlockSpec((1, D_MODEL), lambda j: (0, j))],
        out_specs=pl.BlockSpec((N_COND, D_MODEL), lambda j: (0, j)),
        out_shape=jax.ShapeDtypeStruct((N_COND, 6 * D_MODEL), F32),
        compiler_params=_cparams(("arbitrary",)),
        name="mod",
    )(cond, w, b.reshape(1, 6 * D_MODEL))


def _inproj_kernel(seq_ref, rope_ref, x_ref, mod_ref, g1_ref, w_ref, s32_ref, s64_ref,
                   gqc_ref, gkc_ref, gqd_ref, gkd_ref, cosc_ref, sinc_ref, cosd_ref, sind_ref, o_ref):
    del seq_ref, rope_ref
    x = x_ref[...]
    xn = x * lax.rsqrt(jnp.mean(x * x, axis=-1, keepdims=True) + EPS) * g1_ref[...]
    mod = mod_ref[0]
    sh1 = mod[:, 0:D_MODEL]
    sc1 = mod[:, D_MODEL:2 * D_MODEL]
    h = xn * (1.0 + sc1) + sh1
    p = _dot(h.astype(BF16), w_ref[...])
    o_ref[:, OFF_AU:OFF_BX] = jax.nn.gelu(p[:, OFF_AU:OFF_BX])
    o_ref[:, OFF_BX:OFF_BG] = p[:, OFF_BX:OFF_BG]
    o_ref[:, OFF_BG:OFF_CQ] = jax.nn.gelu(p[:, OFF_BG:OFF_CQ])
    s32 = s32_ref[...]
    cosc = cosc_ref[...]
    sinc = sinc_ref[...]
    cq = _seg_rms_norm(p[:, OFF_CQ:OFF_CK], s32, gqc_ref[...])
    ck = _seg_rms_norm(p[:, OFF_CK:OFF_CV], s32, gkc_ref[...])
    o_ref[:, OFF_CQ:OFF_CK] = _rope(cq, cosc, sinc, DIFF_QK_DIM // 4)
    o_ref[:, OFF_CK:OFF_CV] = _rope(ck, cosc, sinc, DIFF_QK_DIM // 4)
    o_ref[:, OFF_CV:OFF_DQ] = p[:, OFF_CV:OFF_DQ]
    s64 = s64_ref[...]
    cosd = cosd_ref[...]
    sind = sind_ref[...]
    dq = _seg_rms_norm(p[:, OFF_DQ:OFF_DK], s64, gqd_ref[...])
    dk = _seg_rms_norm(p[:, OFF_DK:OFF_DV], s64[0:128, 0:128], gkd_ref[...])
    o_ref[:, OFF_DQ:OFF_DK] = _rope(dq, cosd, sind, HEAD_DIM // 4)
    o_ref[:, OFF_DK:OFF_DV] = _rope(dk, cosd[:, 0:128], sind[:, 0:128], HEAD_DIM // 4)
    o_ref[:, OFF_DV:IN_WIDTH] = p[:, OFF_DV:IN_WIDTH]


def _inproj_call(seq_of_tile, rope_of_tile, x, mod3, g1, w_in, s32, s64, gqc, gkc, gqd, gkd, cosc, sinc, cosd, sind):
    full = lambda shape: pl.BlockSpec(shape, lambda i, s, r: (0,) * len(shape))
    rope_spec = pl.BlockSpec((TM, GROUP_WIDTH), lambda i, s, r: (r[i], 0))
    grid_spec = pltpu.PrefetchScalarGridSpec(
        num_scalar_prefetch=2,
        grid=(N_TILES,),
        in_specs=[pl.BlockSpec((TM, D_MODEL), lambda i, s, r: (i, 0)),
                  pl.BlockSpec((1, 1, 6 * D_MODEL), lambda i, s, r: (s[i], 0, 0)),
                  full((1, D_MODEL)),
                  full((D_MODEL, IN_WIDTH)),
                  full((GROUP_WIDTH, GROUP_WIDTH)),
                  full((GROUP_WIDTH, GROUP_WIDTH)),
                  full((1, GROUP_WIDTH)), full((1, GROUP_WIDTH)), full((1, GROUP_WIDTH)), full((1, 128)),
                  rope_spec, rope_spec, rope_spec, rope_spec],
        out_specs=pl.BlockSpec((TM, IN_WIDTH), lambda i, s, r: (i, 0)),
    )
    return pl.pallas_call(
        _inproj_kernel,
        grid_spec=grid_spec,
        out_shape=jax.ShapeDtypeStruct((N_TOK, IN_WIDTH), F32),
        compiler_params=_cparams(("arbitrary",)),
        name="inproj",
    )(seq_of_tile, rope_of_tile, x, mod3, g1, w_in, s32, s64, gqc, gkc, gqd, gkd, cosc, sinc, cosd, sind)


def _gmlp_kernel(u_ref, v_ref, s64_ref, g_ref, ws_ref, bias_ref, o_ref):
    vh = _seg_rms_norm(v_ref[...], s64_ref[...], g_ref[...]).astype(BF16)
    head = lax.broadcasted_iota(jnp.int32, (CHUNK, GROUP_WIDTH), 1) // HEAD_DIM
    for ch in range(TM // CHUNK):
        rows = slice(ch * CHUNK, (ch + 1) * CHUNK)
        vc = vh[rows]
        mixed = bias_ref[...]
        for h in range(N_GROUP_HEADS):
            mixed = mixed + jnp.where(head == h, _dot(ws_ref[h], vc), 0.0)
        o_ref[rows, :] = u_ref[rows, :] * mixed


def _gmlp_call(proj, s64, g, ws, bias):
    return pl.pallas_call(
        _gmlp_kernel,
        grid=(N_TILES,),
        in_specs=[pl.BlockSpec((TM, GROUP_WIDTH), lambda i: (i, OFF_AU // GROUP_WIDTH)),
                  pl.BlockSpec((TM, GROUP_WIDTH), lambda i: (i, OFF_AV // GROUP_WIDTH)),
                  pl.BlockSpec((GROUP_WIDTH, GROUP_WIDTH), lambda i: (0, 0)),
                  pl.BlockSpec((1, GROUP_WIDTH), lambda i: (0, 0)),
                  pl.BlockSpec((N_GROUP_HEADS, CHUNK, CHUNK), lambda i: (0, 0, 0)),
                  pl.BlockSpec((CHUNK, GROUP_WIDTH), lambda i: (0, 0))],
        out_specs=pl.BlockSpec((TM, GROUP_WIDTH), lambda i: (i, 0)),
        out_shape=jax.ShapeDtypeStruct((N_TOK, GROUP_WIDTH), F32),
        compiler_params=_cparams(("arbitrary",)),
        name="gmlp",
    )(proj, proj, s64, g, ws, bias)


def _scan_chunk(a, b, reverse):
    n = a.shape[0]
    row = lax.broadcasted_iota(jnp.int32, a.shape, 0)
    s = 1
    while s < n:
        if reverse:
            keep = row < n - s
            shift = n - s
        else:
            keep = row >= s
            shift = s
        a_prev = jnp.where(keep, pltpu.roll(a, shift, axis=0), 1.0)
        b_prev = jnp.where(keep, pltpu.roll(b, shift, axis=0), 0.0)
        b = a * b_prev + b
        a = a * a_prev
        s *= 2
    return a, b


def _lru_kernel(x_ref, g_ref, h0_ref, cw_ref, cb_ref, wcat_ref, bcat_ref, lam_ref, y_ref, st_ref,
                xpad, a_f, b_f, a_b, b_b, h_f, *, seq_len):
    nc = seq_len // LRU_CHUNK
    w = GROUP_WIDTH
    zeros8 = jnp.zeros((8, w), F32)
    xpad[0:8, :] = zeros8
    xpad[seq_len + 8:seq_len + 16, :] = zeros8
    xpad[8:seq_len + 8, :] = x_ref[...]
    sp = _softplus(-lam_ref[...])
    cw = cw_ref[...]
    cb = cb_ref[...]
    win_rows = LRU_CHUNK + 16

    def gates(c, carry):
        r0 = pl.multiple_of(c * LRU_CHUNK, LRU_CHUNK)
        win = xpad[pl.ds(r0, win_rows), :]
        inner = slice(8, 8 + LRU_CHUNK)
        xc = cb + pltpu.roll(win, 2, axis=0)[inner] * cw[0:1]
        xc = xc + pltpu.roll(win, 1, axis=0)[inner] * cw[1:2]
        xc = xc + win[inner] * cw[2:3]
        xc = xc + pltpu.roll(win, win_rows - 1, axis=0)[inner] * cw[3:4]
        sg = jax.nn.sigmoid(_dot(xc.astype(BF16), wcat_ref[...]) + bcat_ref[...])
        for d, (a_ref, b_ref) in enumerate(((a_f, b_f), (a_b, b_b))):
            r = sg[:, (2 * d) * w:(2 * d + 1) * w]
            i = sg[:, (2 * d + 1) * w:(2 * d + 2) * w]
            log_a = (-LRU_C * r) * sp[d:d + 1]
            a = jnp.exp(log_a)
            a_ref[pl.ds(r0, LRU_CHUNK), :] = a
            b_ref[pl.ds(r0, LRU_CHUNK), :] = jnp.sqrt(-jnp.tanh(log_a) * (a * a + 1.0)) * (i * xc)
        return carry

    lax.fori_loop(0, nc, gates, 0)

    def fwd(c, carry):
        r0 = pl.multiple_of(c * LRU_CHUNK, LRU_CHUNK)
        a_cum, h_loc = _scan_chunk(a_f[pl.ds(r0, LRU_CHUNK), :], b_f[pl.ds(r0, LRU_CHUNK), :], False)
        h = h_loc + a_cum * carry
        h_f[pl.ds(r0, LRU_CHUNK), :] = h
        return h[LRU_CHUNK - 1:LRU_CHUNK, :]

    s_f = lax.fori_loop(0, nc, fwd, h0_ref[0, 0:1, :])

    def bwd(k, carry):
        c = nc - 1 - k
        r0 = pl.multiple_of(c * LRU_CHUNK, LRU_CHUNK)
        a_cum, h_loc = _scan_chunk(a_b[pl.ds(r0, LRU_CHUNK), :], b_b[pl.ds(r0, LRU_CHUNK), :], True)
        h = h_loc + a_cum * carry
        y_ref[pl.ds(r0, LRU_CHUNK), :] = (h_f[pl.ds(r0, LRU_CHUNK), :] + h) * g_ref[pl.ds(r0, LRU_CHUNK), :]
        return h[0:1, :]

    s_b = lax.fori_loop(0, nc, bwd, h0_ref[0, 1:2, :])
    st_ref[0, 0:1, :] = s_f
    st_ref[0, 1:2, :] = s_b


def _lru_call(proj, h0, cw, cb, wcat, bcat, lam, *, seq_len, n_seq, row_block0):
    w = GROUP_WIDTH
    full = lambda shape: pl.BlockSpec(shape, lambda b: (0,) * len(shape))
    seq_block = lambda col: pl.BlockSpec((seq_len, w), lambda b: (b + row_block0, col))
    return pl.pallas_call(
        functools.partial(_lru_kernel, seq_len=seq_len),
        grid=(n_seq,),
        in_specs=[seq_block(OFF_BX // w), seq_block(OFF_BG // w),
                  pl.BlockSpec((1, 2, w), lambda b: (b, 0, 0)),
                  full((CONV_W, w)), full((1, w)), full((w, 4 * w)), full((1, 4 * w)), full((2, w))],
        out_specs=[pl.BlockSpec((seq_len, w), lambda b: (b, 0)),
                   pl.BlockSpec((1, 2, w), lambda b: (b, 0, 0))],
        out_shape=[jax.ShapeDtypeStruct((n_seq * seq_len, w), F32),
                   jax.ShapeDtypeStruct((n_seq, 2, w), F32)],
        scratch_shapes=[pltpu.VMEM((seq_len + 16, w), F32)] + [pltpu.VMEM((seq_len, w), F32)] * 5,
        compiler_params=_cparams(("arbitrary",)),
        name="lru_%d" % seq_len,
    )(proj, proj, h0, cw, cb, wcat, bcat, lam)


def _diff_lambda(ld_ref, lam_init):
    ld = ld_ref[...]
    l1 = jnp.sum(ld[0:1] * ld[1:2], axis=-1, keepdims=True)
    l2 = jnp.sum(ld[2:3] * ld[3:4], axis=-1, keepdims=True)
    return jnp.exp(l1) - jnp.exp(l2) + lam_init


def _diff_attn_body(q, keys, values, lam, s64, g, lam_init):
    tq = q.shape[0]
    lane = lax.broadcasted_iota(jnp.int32, (1, GROUP_WIDTH), 1)
    scale = DIFF_QK_DIM ** -0.5
    o = jnp.zeros((tq, GROUP_WIDTH), F32)
    for h in range(N_GROUP_HEADS):
        probs = []
        for i in range(2):
            seg = h * 2 + i
            qm = jnp.where(lane // DIFF_QK_DIM == seg, q, 0.0).astype(BF16)
            s = [_dot_nt(qm, k) * scale for k in keys]
            m = functools.reduce(jnp.maximum, [jnp.max(x, axis=-1, keepdims=True) for x in s])
            e = [jnp.exp(x - m) for x in s]
            den = functools.reduce(jnp.add, [jnp.sum(x, axis=-1, keepdims=True) for x in e])
            inv = 1.0 / den
            probs.append([x * inv for x in e])
        oh = None
        for p0, p1, v in zip(probs[0], probs[1], values):
            part = _dot((p0 - lam * p1).astype(BF16), v)
            oh = part if oh is None else oh + part
        o = o + jnp.where(lane // HEAD_DIM == h, oh, 0.0)
    return _seg_rms_norm(o, s64, g) * (1.0 - lam_init)


def _diff_prompt_kernel(q_ref, k_ref, v_ref, ld_ref, s64_ref, g_ref, o_ref, *, lam_init):
    lam = _diff_lambda(ld_ref, lam_init)
    o_ref[...] = _diff_attn_body(q_ref[...], [k_ref[...].astype(BF16)], [v_ref[...].astype(BF16)],
                                 lam, s64_ref[...], g_ref[...], lam_init)


def _diff_sample_kernel(q_ref, k_ref, v_ref, ck_ref, cv_ref, ld_ref, s64_ref, g_ref, o_ref, *, lam_init):
    lam = _diff_lambda(ld_ref, lam_init)
    keys = [ck_ref[0].astype(BF16), k_ref[...].astype(BF16)]
    values = [cv_ref[0].astype(BF16), v_ref[...].astype(BF16)]
    o_ref[...] = _diff_attn_body(q_ref[...], keys, values, lam, s64_ref[...], g_ref[...], lam_init)


def _diff_prompt_call(proj, ld, s64, g, lam_init):
    w = GROUP_WIDTH
    full = lambda shape: pl.BlockSpec(shape, lambda b: (0,) * len(shape))
    return pl.pallas_call(
        functools.partial(_diff_prompt_kernel, lam_init=lam_init),
        grid=(BATCH,),
        in_specs=[pl.BlockSpec((SEQ, w), lambda b: (b, OFF_CQ // w)),
                  pl.BlockSpec((SEQ, w), lambda b: (b, OFF_CK // w)),
                  pl.BlockSpec((SEQ, w), lambda b: (b, OFF_CV // w)),
                  full((4, DIFF_QK_DIM)), full((w, w)), full((1, w))],
        out_specs=pl.BlockSpec((SEQ, w), lambda b: (b, 0)),
        out_shape=jax.ShapeDtypeStruct((N_PROMPT, w), F32),
        compiler_params=_cparams(("arbitrary",)),
        name="diff_prompt",
    )(proj, proj, proj, ld, s64, g)


def _diff_sample_call(proj, ctx_k, ctx_v, ld, s64, g, lam_init):
    w = GROUP_WIDTH
    tq = 256
    nq = DEC_SEQ // tq
    full = lambda shape: pl.BlockSpec(shape, lambda b, i: (0,) * len(shape))
    seq_block0 = N_PROMPT // DEC_SEQ
    return pl.pallas_call(
        functools.partial(_diff_sample_kernel, lam_init=lam_init),
        grid=(DEC_BATCH, nq),
        in_specs=[pl.BlockSpec((tq, w), lambda b, i: (N_PROMPT // tq + b * nq + i, OFF_CQ // w)),
                  pl.BlockSpec((DEC_SEQ, w), lambda b, i: (seq_block0 + b, OFF_CK // w)),
                  pl.BlockSpec((DEC_SEQ, w), lambda b, i: (seq_block0 + b, OFF_CV // w)),
                  pl.BlockSpec((1, PAST_LEN, w), lambda b, i: (b, 0, 0)),
                  pl.BlockSpec((1, PAST_LEN, w), lambda b, i: (b, 0, 0)),
                  full((4, DIFF_QK_DIM)), full((w, w)), full((1, w))],
        out_specs=pl.BlockSpec((tq, w), lambda b, i: (b * nq + i, 0)),
        out_shape=jax.ShapeDtypeStruct((N_SAMPLE, w), F32),
        compiler_params=_cparams(("arbitrary", "arbitrary")),
        name="diff_sample",
    )(proj, proj, proj, ctx_k, ctx_v, ld, s64, g)


def _sink_attn_body(q, keys, values, masks, sink_ref):
    tq = q.shape[0]
    lane = lax.broadcasted_iota(jnp.int32, (1, 128), 1)
    scale = HEAD_DIM ** -0.5
    outs = []
    for grp in range(SWA_GROUPS):
        qg = q[:, grp * 128:(grp + 1) * 128]
        og = jnp.zeros((tq, 128), F32)
        for kh in range(SWA_KV_HEADS):
            sink = sink_ref[kh * SWA_GROUPS + grp]
            qm = jnp.where(lane // HEAD_DIM == kh, qg, 0.0).astype(BF16)
            s = []
            for k, msk in zip(keys, masks):
                x = _dot_nt(qm, k) * scale
                s.append(x if msk is None else jnp.where(msk, x, -jnp.inf))
            m = functools.reduce(jnp.maximum, [jnp.max(x, axis=-1, keepdims=True) for x in s])
            m = jnp.maximum(m, sink)
            e = [jnp.exp(x - m) for x in s]
            den = functools.reduce(jnp.add, [jnp.sum(x, axis=-1, keepdims=True) for x in e]) + jnp.exp(sink - m)
            inv = 1.0 / den
            oh = None
            for x, v in zip(e, values):
                part = _dot((x * inv).astype(BF16), v)
                oh = part if oh is None else oh + part
            og = og + jnp.where(lane // HEAD_DIM == kh, oh, 0.0)
        outs.append(og)
    return outs


def _swa_prompt_kernel(sink_ref, q_ref, k_ref, v_ref, o_ref):
    outs = _sink_attn_body(q_ref[...], [k_ref[...].astype(BF16)], [v_ref[...].astype(BF16)], [None], sink_ref)
    for grp in range(SWA_GROUPS):
        o_ref[:, grp * 128:(grp + 1) * 128] = outs[grp]


def _swa_sample_kernel(sink_ref, q_ref, kp_ref, kc_ref, kn_ref, vp_ref, vc_ref, vn_ref, ck_ref, cv_ref, o_ref):
    n = pl.program_id(1)
    nb = pl.num_programs(1)
    r = lax.broadcasted_iota(jnp.int32, (WINDOW, WINDOW), 0)
    c = lax.broadcasted_iota(jnp.int32, (WINDOW, WINDOW), 1)
    mask_prev = c >= r + jnp.where(n > 0, 0, WINDOW)
    mask_next = c <= r - jnp.where(n < nb - 1, 0, WINDOW)
    keys = [ck_ref[0].astype(BF16), kp_ref[...].astype(BF16), kc_ref[...].astype(BF16), kn_ref[...].astype(BF16)]
    values = [cv_ref[0].astype(BF16), vp_ref[...].astype(BF16), vc_ref[...].astype(BF16), vn_ref[...].astype(BF16)]
    outs = _sink_attn_body(q_ref[...], keys, values, [None, mask_prev, None, mask_next], sink_ref)
    for grp in range(SWA_GROUPS):
        o_ref[:, grp * 128:(grp + 1) * 128] = outs[grp]


def _swa_prompt_call(sink, proj):
    w = GROUP_WIDTH
    return pl.pallas_call(
        _swa_prompt_kernel,
        grid=(BATCH,),
        in_specs=[pl.BlockSpec(memory_space=pltpu.SMEM),
                  pl.BlockSpec((SEQ, w), lambda b: (b, OFF_DQ // w)),
                  pl.BlockSpec((SEQ, 128), lambda b: (b, OFF_DK // 128)),
                  pl.BlockSpec((SEQ, 128), lambda b: (b, OFF_DV // 128))],
        out_specs=pl.BlockSpec((SEQ, w), lambda b: (b, 0)),
        out_shape=jax.ShapeDtypeStruct((N_PROMPT, w), F32),
        compiler_params=_cparams(("arbitrary",)),
        name="swa_prompt",
    )(sink, proj, proj, proj)


def _swa_sample_call(sink, proj, ctx_k, ctx_v):
    w = GROUP_WIDTH
    tq = WINDOW
    nq = DEC_SEQ // tq
    row0 = N_PROMPT // tq

    def kv_spec(col, delta):
        def index(b, i):
            j = jnp.clip(i + delta, 0, nq - 1)
            return (row0 + b * nq + j, col)
        return pl.BlockSpec((tq, 128), index)

    ctx_spec = pl.BlockSpec((1, PAST_LEN, 128), lambda b, i: (b, 0, 0))
    return pl.pallas_call(
        _swa_sample_kernel,
        grid=(DEC_BATCH, nq),
        in_specs=[pl.BlockSpec(memory_space=pltpu.SMEM),
                  pl.BlockSpec((tq, w), lambda b, i: (row0 + b * nq + i, OFF_DQ // w)),
                  kv_spec(OFF_DK // 128, -1), kv_spec(OFF_DK // 128, 0), kv_spec(OFF_DK // 128, 1),
                  kv_spec(OFF_DV // 128, -1), kv_spec(OFF_DV // 128, 0), kv_spec(OFF_DV // 128, 1),
                  ctx_spec, ctx_spec],
        out_specs=pl.BlockSpec((tq, w), lambda b, i: (b * nq + i, 0)),
        out_shape=jax.ShapeDtypeStruct((N_SAMPLE, w), F32),
        compiler_params=_cparams(("arbitrary", "arbitrary")),
        name="swa_sample",
    )(sink, proj, proj, proj, proj, proj, proj, proj, ctx_k, ctx_v)


def _outproj_kernel(seq_ref, ya_ref, yb_ref, yc_ref, yd_ref, x_ref, mod_ref, w_ref, g2_ref, rw_ref, rb_ref,
                    x1_ref, h2p_ref, topi_ref, topw_ref):
    del seq_ref
    w = GROUP_WIDTH
    mix = _dot(ya_ref[...].astype(BF16), w_ref[0:w, :])
    mix = mix + _dot(yb_ref[...].astype(BF16), w_ref[w:2 * w, :])
    mix = mix + _dot(yc_ref[...].astype(BF16), w_ref[2 * w:3 * w, :])
    mix = mix + _dot(yd_ref[...].astype(BF16), w_ref[3 * w:4 * w, :])
    mod = mod_ref[0]
    g1 = mod[:, 2 * D_MODEL:3 * D_MODEL]
    sh2 = mod[:, 3 * D_MODEL:4 * D_MODEL]
    sc2 = mod[:, 4 * D_MODEL:5 * D_MODEL]
    x1 = x_ref[...] + g1 * mix
    x1_ref[...] = x1
    xn = x1 * lax.rsqrt(jnp.mean(x1 * x1, axis=-1, keepdims=True) + EPS) * g2_ref[...]
    h2 = xn * (1.0 + sc2) + sh2
    half = D_MODEL // 2
    h2_bits = pltpu.bitcast(h2.astype(BF16).astype(F32), jnp.uint32)
    h2p_ref[...] = (h2_bits[:, :half] >> 16) | (h2_bits[:, half:] & jnp.uint32(0xFFFF0000))
    h_hi, h_lo = _split_bf16(h2)
    r_hi, r_lo = _split_bf16(rw_ref[...])
    logits = _dot(h_hi, r_hi) + (_dot(h_hi, r_lo) + _dot(h_lo, r_hi)) + rb_ref[...]
    lane = lax.broadcasted_iota(jnp.int32, logits.shape, 1)
    work = logits
    picks = []
    for _ in range(TOP_K):
        m = jnp.max(work, axis=-1, keepdims=True)
        idx = jnp.min(jnp.where(work == m, lane, N_EXPERTS), axis=-1, keepdims=True)
        picks.append((m, idx))
        work = jnp.where(lane == idx, -jnp.inf, work)
    top = picks[0][0]
    ex = [jnp.exp(m - top) for m, _ in picks]
    inv = 1.0 / functools.reduce(jnp.add, ex)
    slot = lax.broadcasted_iota(jnp.int32, topi_ref.shape, 1)
    top_i = jnp.zeros(topi_ref.shape, jnp.int32)
    top_w = jnp.zeros(topw_ref.shape, F32)
    for k, (e, (_, idx)) in enumerate(zip(ex, picks)):
        top_i = jnp.where(slot == k, idx, top_i)
        top_w = jnp.where(slot == k, e * inv, top_w)
    topi_ref[...] = top_i
    topw_ref[...] = top_w


def _outproj_call(seq_of_tile, ya, yb, yc, yd, x, mod3, w_out, g2, rw, rb):
    w = GROUP_WIDTH
    full = lambda shape: pl.BlockSpec(shape, lambda i, s: (0,) * len(shape))
    tile = lambda width: pl.BlockSpec((TM, width), lambda i, s: (i, 0))
    return pl.pallas_call(
        _outproj_kernel,
        grid_spec=pltpu.PrefetchScalarGridSpec(
            num_scalar_prefetch=1,
            grid=(N_TILES,),
            in_specs=[tile(w), tile(w), tile(w), tile(w), tile(D_MODEL),
                      pl.BlockSpec((1, 1, 6 * D_MODEL), lambda i, s: (s[i], 0, 0)),
                      full((D_MODEL, D_MODEL)), full((1, D_MODEL)),
                      full((D_MODEL, N_EXPERTS)), full((1, N_EXPERTS))],
            out_specs=[tile(D_MODEL), tile(D_MODEL // 2), tile(TOPK_PAD), tile(TOPK_PAD)],
        ),
        out_shape=[jax.ShapeDtypeStruct((N_TOK, D_MODEL), F32),
                   jax.ShapeDtypeStruct((N_TOK, D_MODEL // 2), jnp.uint32),
                   jax.ShapeDtypeStruct((N_TOK, TOPK_PAD), jnp.int32),
                   jax.ShapeDtypeStruct((N_TOK, TOPK_PAD), F32)],
        compiler_params=_cparams(("arbitrary",)),
        name="outproj",
    )(seq_of_tile, ya, yb, yc, yd, x, mod3, w_out, g2, rw, rb)


def _moe_kernel(expert_ref, block_ref, flag_ref, tok_ref, roww_ref, src_ref, wgu_ref, bgu_ref, wd_ref, bd_ref,
                acc_ref, xt_ref, y3_ref):
    del expert_ref, block_ref
    i = pl.program_id(0)
    flags = flag_ref[i]

    @pl.when((flags & 2) != 0)
    def _():
        def zero(c, carry):
            acc_ref[pl.ds(pl.multiple_of(c * MOE_TILE, MOE_TILE), MOE_TILE)] = jnp.zeros((MOE_TILE, 8, 128), F32)
            return carry
        lax.fori_loop(0, MOE_TB // MOE_TILE, zero, 0)

    @pl.when((flags & 1) != 0)
    def _():
        pieces = D_MODEL // 2 // 128
        for m in range(MOE_TILE):
            t = tok_ref[0, 0, m]
            slab = src_ref[pl.ds(pl.multiple_of(t * pieces, pieces), pieces), :]
            xt_ref[pl.ds(m, pieces, stride=MOE_XT_STRIDE), :] = slab
        lo, hi = [], []
        for p in range(pieces):
            u = xt_ref[p * MOE_XT_STRIDE:p * MOE_XT_STRIDE + MOE_TILE, :]
            lo.append(pltpu.bitcast(u << 16, F32))
            hi.append(pltpu.bitcast(u & jnp.uint32(0xFFFF0000), F32))
        x = jnp.concatenate(lo + hi, axis=1).astype(BF16)
        gu = _dot(x, wgu_ref[0]) + bgu_ref[0]
        gate = jnp.minimum(gu[:, :D_FF], SWIGLU_LIMIT)
        up = jnp.clip(gu[:, D_FF:], -SWIGLU_LIMIT, SWIGLU_LIMIT)
        act = (up + 1.0) * gate * jax.nn.sigmoid(SWIGLU_ALPHA * gate)
        y = (_dot(act.astype(BF16), wd_ref[0]) + bd_ref[0]) * roww_ref[...]
        for j in range(D_MODEL // 128):
            y3_ref[:, j * 8:(j + 1) * 8, :] = y[:, j * 128:(j + 1) * 128].reshape(MOE_TILE // 8, 8, 128)
        for m0 in range(0, MOE_TILE, MOE_RMW_BATCH):
            rows = range(m0, m0 + MOE_RMW_BATCH)
            toks = [tok_ref[0, 0, m] for m in rows]
            vals = [acc_ref[t] + y3_ref[m // 8, pl.ds(m % 8, 8, stride=8), :] for t, m in zip(toks, rows)]
            for t, v in reversed(list(zip(toks, vals))):
                acc_ref[t] = v


def _moe_call(tile_expert, tile_block, tile_flags, row_tok, row_w, h2p_rows, wgu, bgu, wd, bd):
    pieces = D_MODEL // 2 // 128
    once = pl.Buffered(1)
    return pl.pallas_call(
        _moe_kernel,
        grid_spec=pltpu.PrefetchScalarGridSpec(
            num_scalar_prefetch=3,
            grid=(MOE_MAX_TILES,),
            in_specs=[pl.BlockSpec((1, 1, MOE_TILE), lambda i, e, b, f: (i, 0, 0), memory_space=pltpu.SMEM),
                      pl.BlockSpec((MOE_TILE, 1), lambda i, e, b, f: (i, 0)),
                      pl.BlockSpec((MOE_TB * pieces, 128), lambda i, e, b, f: (b[i], 0), pipeline_mode=once),
                      pl.BlockSpec((1, D_MODEL, 2 * D_FF), lambda i, e, b, f: (e[i], 0, 0)),
                      pl.BlockSpec((1, 1, 2 * D_FF), lambda i, e, b, f: (e[i], 0, 0)),
                      pl.BlockSpec((1, D_FF, D_MODEL), lambda i, e, b, f: (e[i], 0, 0)),
                      pl.BlockSpec((1, 1, D_MODEL), lambda i, e, b, f: (e[i], 0, 0))],
            out_specs=pl.BlockSpec((MOE_TB, 8, 128), lambda i, e, b, f: (b[i], 0, 0), pipeline_mode=once),
            scratch_shapes=[pltpu.VMEM((pieces * MOE_XT_STRIDE, 128), jnp.uint32),
                            pltpu.VMEM((MOE_TILE // 8, 64, 128), F32)],
        ),
        out_shape=jax.ShapeDtypeStruct((N_TOK, 8, 128), F32),
        compiler_params=_cparams(("arbitrary",)),
        name="moe",
    )(tile_expert, tile_block, tile_flags, row_tok, row_w, h2p_rows, wgu, bgu, wd, bd)


def _moe_routing(top_i, top_w):
    nb = N_TOK // MOE_TB
    experts = jnp.arange(N_EXPERTS, dtype=jnp.int32)
    sel = jnp.sum((top_i[:, :, None] == experts[None, None, :]).astype(jnp.int32), axis=1)
    sel_b = sel.reshape(nb, MOE_TB, N_EXPERTS)
    csum = jnp.cumsum(sel_b, axis=1)
    counts = csum[:, -1, :]
    rank = (csum - sel_b).reshape(N_TOK, N_EXPERTS)
    padded = ((counts + MOE_TILE - 1) // MOE_TILE) * MOE_TILE
    group_end = jnp.cumsum(padded.reshape(-1))
    group_start = group_end - padded.reshape(-1)
    tok = jnp.arange(N_TOK, dtype=jnp.int32)
    group = (tok // MOE_TB)[:, None] * N_EXPERTS + top_i
    dest = group_start[group] + jnp.take_along_axis(rank, top_i, axis=1)
    local = jnp.broadcast_to((tok % MOE_TB)[:, None], dest.shape)
    n_rows = MOE_MAX_TILES * MOE_TILE
    row_tok = jnp.zeros((n_rows,), jnp.int32).at[dest.reshape(-1)].set(local.reshape(-1), unique_indices=True)
    row_w = jnp.zeros((n_rows,), F32).at[dest.reshape(-1)].set(top_w.reshape(-1), unique_indices=True)
    tile_start = jnp.arange(MOE_MAX_TILES, dtype=jnp.int32) * MOE_TILE
    tile_group = jnp.minimum(jnp.searchsorted(group_end, tile_start, side="right"), nb * N_EXPERTS - 1).astype(jnp.int32)
    valid = tile_start < group_end[-1]
    tile_block = tile_group // N_EXPERTS
    first = jnp.concatenate([jnp.ones((1,), bool), tile_block[1:] != tile_block[:-1]])
    flags = valid.astype(jnp.int32) + 2 * first.astype(jnp.int32)
    return (tile_group % N_EXPERTS, tile_block, flags, row_tok.reshape(MOE_MAX_TILES, 1, MOE_TILE),
            row_w.reshape(n_rows, 1))


def _residual_kernel(seq_ref, x1_ref, moe_ref, mod_ref, o_ref):
    del seq_ref
    g2 = mod_ref[0][:, 5 * D_MODEL:6 * D_MODEL]
    o_ref[...] = x1_ref[...] + g2 * moe_ref[...]


def _residual_call(seq_of_tile, x1, moe, mod3):
    tile = pl.BlockSpec((TM, D_MODEL), lambda i, s: (i, 0))
    return pl.pallas_call(
        _residual_kernel,
        grid_spec=pltpu.PrefetchScalarGridSpec(
            num_scalar_prefetch=1,
            grid=(N_TILES,),
            in_specs=[tile, tile, pl.BlockSpec((1, 1, 6 * D_MODEL), lambda i, s: (s[i], 0, 0))],
            out_specs=tile,
        ),
        out_shape=jax.ShapeDtypeStruct((N_TOK, D_MODEL), F32),
        compiler_params=_cparams(("arbitrary",)),
        name="residual",
    )(seq_of_tile, x1, moe, mod3)


def _segment_matrix(width, seg):
    idx = np.arange(width) // seg
    return jnp.asarray((idx[:, None] == idx[None, :]).astype(np.float32) / seg, BF16)


def _rope_tables(rot_dim, width):
    rows = DEC_SEQ // GRID_W
    nf = rot_dim // 4
    inv = 1.0 / (ROPE_BASE ** (jnp.arange(nf, dtype=F32) / nf))
    row = jnp.repeat(jnp.arange(rows, dtype=F32), GRID_W)
    col = jnp.tile(jnp.arange(GRID_W, dtype=F32), rows)
    ang = jnp.stack([row[:, None] * inv, col[:, None] * inv], axis=1)
    cos, sin = jnp.cos(ang), jnp.sin(ang)
    cos_r = jnp.concatenate([cos[:, 0], cos[:, 0], cos[:, 1], cos[:, 1]], axis=-1)
    sin_r = jnp.concatenate([-sin[:, 0], sin[:, 0], -sin[:, 1], sin[:, 1]], axis=-1)
    reps = width // rot_dim
    cos_t = jnp.concatenate([jnp.ones((TM, width), F32), jnp.tile(cos_r, (1, reps))], axis=0)
    sin_t = jnp.concatenate([jnp.zeros((TM, width), F32), jnp.tile(sin_r, (1, reps))], axis=0)
    return cos_t, sin_t


def _block_diag(wb):
    nb, n, _ = wb.shape
    out = jnp.zeros((nb * n, nb * n), wb.dtype)
    for i in range(nb):
        out = out.at[i * n:(i + 1) * n, i * n:(i + 1) * n].set(wb[i])
    return out


_DQ_HEADS = [kh * SWA_GROUPS + g for g in range(SWA_GROUPS) for kh in range(SWA_KV_HEADS)]


def kernel(x_prompt, x_sample, c, cache_diff_k, cache_diff_v, cache_swa_k, cache_swa_v, state_lru, c_ctx, mod_w, mod_b, norm1_g, norm2_g, w_in, w_out, mlp_vnorm_g, mlp_ws, mlp_bs, lru_conv_w, lru_conv_b, lru_wa, lru_ba, lru_wx, lru_bx, lru_lambda, diff_qnorm_g, diff_knorm_g, diff_lambda, diff_subln_g, swa_qnorm_g, swa_knorm_g, swa_sink, router_w, router_b, moe_w_gu, moe_b_gu, moe_w_down, moe_b_down):
    params = dict(locals())
    x = jnp.concatenate([x_prompt.reshape(N_PROMPT, D_MODEL), x_sample.reshape(N_SAMPLE, D_MODEL)], axis=0)
    cond = jnp.concatenate([c_ctx[None], c, jnp.zeros((N_COND - 1 - DEC_BATCH, D_MODEL), F32)], axis=0)
    consts = _constants()
    dk_l, dv_l, sk_l, sv_l, lru_l = [], [], [], [], []
    for l in range(DEPTH):
        st = _mixer_stage(x, cond, params, l, consts)
        x = _ffn_stage(x, st, params, l, consts)
        pp = st["proj"][:N_PROMPT]
        dk_l.append(pp[:, OFF_CK:OFF_CV].reshape(BATCH, SEQ, N_GROUP_HEADS, 2, DIFF_QK_DIM))
        dv_l.append(pp[:, OFF_CV:OFF_DQ].reshape(BATCH, SEQ, N_GROUP_HEADS, HEAD_DIM))
        sk_l.append(pp[:, OFF_DK:OFF_DV].reshape(BATCH, SEQ, SWA_KV_HEADS, HEAD_DIM))
        sv_l.append(pp[:, OFF_DV:IN_WIDTH].reshape(BATCH, SEQ, SWA_KV_HEADS, HEAD_DIM))
        lru_l.append(st["st_p"])

    y_p = x[:N_PROMPT].reshape(BATCH, SEQ, D_MODEL)
    y_s = x[N_PROMPT:].reshape(DEC_BATCH, DEC_SEQ, D_MODEL)
    return (y_p, y_s, jnp.stack(dk_l, axis=1), jnp.stack(dv_l, axis=1), jnp.stack(sk_l, axis=1),
            jnp.stack(sv_l, axis=1), jnp.stack(lru_l, axis=1))


def _constants():
    w = GROUP_WIDTH
    tiles = np.arange(N_TILES)
    seq_np = np.where(tiles < PROMPT_TILES, 0, 1 + (tiles - PROMPT_TILES) // SAMPLE_TILES_PER_SEQ)
    rope_np = np.where(tiles < PROMPT_TILES, 0, 1 + (tiles - PROMPT_TILES) % SAMPLE_TILES_PER_SEQ)
    seq_of_tile = jnp.asarray(seq_np, jnp.int32)
    rope_of_tile = jnp.asarray(rope_np, jnp.int32)

    s32 = _segment_matrix(w, DIFF_QK_DIM)
    s64 = _segment_matrix(w, HEAD_DIM)
    cosc, sinc = _rope_tables(DIFF_QK_DIM, w)
    cosd, sind = _rope_tables(HEAD_DIM, w)
    dq_cols = np.concatenate([np.arange(h * HEAD_DIM, (h + 1) * HEAD_DIM) for h in _DQ_HEADS])
    return dict(seq_of_tile=seq_of_tile, rope_of_tile=rope_of_tile, s32=s32, s64=s64,
                cosc=cosc, sinc=sinc, cosd=cosd, sind=sind, dq_cols=dq_cols)


def _mixer_stage(x, cond, params, l, consts):
    w = GROUP_WIDTH
    (mod_w, mod_b, norm1_g, w_in, mlp_vnorm_g, mlp_ws, mlp_bs, lru_conv_w, lru_conv_b, lru_wa, lru_ba, lru_wx, lru_bx,
     lru_lambda, diff_qnorm_g, diff_knorm_g, diff_lambda, diff_subln_g, swa_qnorm_g, swa_knorm_g, swa_sink,
     cache_diff_k, cache_diff_v, cache_swa_k, cache_swa_v, state_lru) = (params[k] for k in (
         "mod_w", "mod_b", "norm1_g", "w_in", "mlp_vnorm_g", "mlp_ws", "mlp_bs", "lru_conv_w", "lru_conv_b", "lru_wa",
         "lru_ba", "lru_wx", "lru_bx", "lru_lambda", "diff_qnorm_g", "diff_knorm_g", "diff_lambda", "diff_subln_g",
         "swa_qnorm_g", "swa_knorm_g", "swa_sink", "cache_diff_k", "cache_diff_v", "cache_swa_k", "cache_swa_v",
         "state_lru"))
    seq_of_tile, rope_of_tile, s32, s64, cosc, sinc, cosd, sind, dq_cols = (consts[k] for k in (
        "seq_of_tile", "rope_of_tile", "s32", "s64", "cosc", "sinc", "cosd", "sind", "dq_cols"))
    if True:
        lam_init = 0.8 - 0.6 * math.exp(-0.3 * l)
        w_in_l = w_in[l]
        w_in_l = jnp.concatenate([w_in_l[:, :OFF_DQ], w_in_l[:, OFF_DQ:OFF_DK][:, dq_cols], w_in_l[:, OFF_DK:]], axis=1)
        w_in_l = w_in_l.astype(BF16)
        gqc = jnp.tile(diff_qnorm_g[l].reshape(1, 2 * DIFF_QK_DIM), (1, N_GROUP_HEADS))
        gkc = jnp.tile(diff_knorm_g[l].reshape(1, 2 * DIFF_QK_DIM), (1, N_GROUP_HEADS))
        gqd = jnp.tile(swa_qnorm_g[l].reshape(1, HEAD_DIM), (1, 4))
        gkd = jnp.tile(swa_knorm_g[l].reshape(1, HEAD_DIM), (1, SWA_KV_HEADS))
        g_sub = jnp.tile(diff_subln_g[l].reshape(1, HEAD_DIM), (1, N_GROUP_HEADS))
        g_mlp = mlp_vnorm_g[l].reshape(1, w)
        mlp_bias = jnp.repeat(mlp_bs[l].T, HEAD_DIM, axis=1)
        wcat = jnp.concatenate([_block_diag(lru_wa[l, 0]), _block_diag(lru_wx[l, 0]),
                                _block_diag(lru_wa[l, 1]), _block_diag(lru_wx[l, 1])], axis=1).astype(BF16)
        bcat = jnp.concatenate([lru_ba[l, 0].reshape(1, w), lru_bx[l, 0].reshape(1, w),
                                lru_ba[l, 1].reshape(1, w), lru_bx[l, 1].reshape(1, w)], axis=1)
        sink = swa_sink[l].astype(F32)

        mod3 = _mod_call(cond, mod_w[l], mod_b[l]).reshape(N_COND, 1, 6 * D_MODEL)
        proj = _inproj_call(seq_of_tile, rope_of_tile, x, mod3, norm1_g[l].reshape(1, D_MODEL), w_in_l, s32, s64,
                            gqc, gkc, gqd, gkd, cosc, sinc, cosd, sind)

        ya = _gmlp_call(proj, s64, g_mlp, mlp_ws[l].astype(BF16), mlp_bias)

        lru_args = (lru_conv_w[l], lru_conv_b[l].reshape(1, w), wcat, bcat, lru_lambda[l])
        yb_p, st_p = _lru_call(proj, jnp.zeros((BATCH, 2, w), F32), *lru_args, seq_len=SEQ, n_seq=BATCH, row_block0=0)
        yb_s, _ = _lru_call(proj, state_lru[:, l], *lru_args, seq_len=DEC_SEQ, n_seq=DEC_BATCH,
                            row_block0=N_PROMPT // DEC_SEQ)

        yc_p = _diff_prompt_call(proj, diff_lambda[l], s64, g_sub, lam_init)
        yc_s = _diff_sample_call(proj, cache_diff_k[:, l].reshape(DEC_BATCH, PAST_LEN, w),
                                 cache_diff_v[:, l].reshape(DEC_BATCH, PAST_LEN, w), diff_lambda[l], s64, g_sub, lam_init)

        yd_p = _swa_prompt_call(sink, proj)
        yd_s = _swa_sample_call(sink, proj, cache_swa_k[:, l].reshape(DEC_BATCH, PAST_LEN, 128),
                                cache_swa_v[:, l].reshape(DEC_BATCH, PAST_LEN, 128))

        yb = jnp.concatenate([yb_p, yb_s], axis=0)
        yc = jnp.concatenate([yc_p, yc_s], axis=0)
        yd = jnp.concatenate([yd_p, yd_s], axis=0)
    return dict(mod3=mod3, proj=proj, ya=ya, yb=yb, yc=yc, yd=yd, st_p=st_p)


def _ffn_stage(x, st, params, l, consts):
    w = GROUP_WIDTH
    w_out, norm2_g, router_w, router_b, moe_w_gu, moe_b_gu, moe_w_down, moe_b_down = (params[k] for k in (
        "w_out", "norm2_g", "router_w", "router_b", "moe_w_gu", "moe_b_gu", "moe_w_down", "moe_b_down"))
    w_out_l = w_out[l]
    w_out_l = jnp.concatenate([w_out_l[:3 * w], w_out_l[3 * w:][consts["dq_cols"]]], axis=0).astype(BF16)
    x1, h2p, top_i, top_w = _outproj_call(consts["seq_of_tile"], st["ya"], st["yb"], st["yc"], st["yd"], x, st["mod3"],
                                          w_out_l, norm2_g[l].reshape(1, D_MODEL), router_w[l],
                                          router_b[l].reshape(1, N_EXPERTS))
    routing = _moe_routing(top_i[:, :TOP_K], top_w[:, :TOP_K])
    moe = _moe_call(*routing, h2p.reshape(N_TOK * (D_MODEL // 2 // 128), 128), moe_w_gu[l].astype(BF16),
                    moe_b_gu[l].reshape(N_EXPERTS, 1, 2 * D_FF), moe_w_down[l].astype(BF16),
                    moe_b_down[l].reshape(N_EXPERTS, 1, D_MODEL))
    return _residual_call(consts["seq_of_tile"], x1, moe.reshape(N_TOK, D_MODEL), st["mod3"])
```

```python
import functools
import math

import jax
import jax.numpy as jnp
import numpy as np
from jax import lax
from jax.experimental import pallas as pl
from jax.experimental.pallas import tpu as pltpu

F32 = jnp.float32
BF16 = jnp.bfloat16

D_MODEL = 1024
BATCH = 16
SEQ = 256
DEPTH = 2
DEC_BATCH = 4
DEC_SEQ = 2048
PAST_LEN = 512
GRID_W = 64
HEAD_DIM = 64
GROUP_WIDTH = 256
N_GROUP_HEADS = 4
CHUNK = 128
LRU_C = 8.0
CONV_W = 4
DIFF_QK_DIM = 32
SWA_KV_HEADS = 2
SWA_GROUPS = 2
WINDOW = 128
N_EXPERTS = 32
TOP_K = 4
D_FF = 1024
SWIGLU_LIMIT = 7.0
SWIGLU_ALPHA = 1.702
ROPE_BASE = 10000.0
EPS = 1e-6

N_PROMPT = BATCH * SEQ
N_SAMPLE = DEC_BATCH * DEC_SEQ
N_TOK = N_PROMPT + N_SAMPLE
N_COND = 8
TM = 256
N_TILES = N_TOK // TM
PROMPT_TILES = N_PROMPT // TM
SAMPLE_TILES_PER_SEQ = DEC_SEQ // TM
IN_WIDTH = 2304
OFF_AU, OFF_AV, OFF_BX, OFF_BG, OFF_CQ, OFF_CK, OFF_CV, OFF_DQ, OFF_DK, OFF_DV = (
    0, 256, 512, 768, 1024, 1280, 1536, 1792, 2048, 2176)
TOPK_PAD = 8
MOE_TILE = 256
MOE_TB = 4096
MOE_MAX_TILES = (N_TOK // MOE_TB) * (MOE_TB * TOP_K // MOE_TILE + N_EXPERTS)
MOE_XT_STRIDE = MOE_TILE + 8
MOE_RMW_BATCH = 4
MOE_ROWS_PER_BLOCK = MOE_TB * TOP_K + N_EXPERTS * MOE_TILE
MOE_PAD_PAIR = MOE_TB * TOP_K
MOE_PAIR_SHIFT = TOP_K.bit_length() - 1
assert 1 << MOE_PAIR_SHIFT == TOP_K
LRU_CHUNK = 256
VMEM_LIMIT = 56 * 1024 * 1024


def _cparams(sem):
    return pltpu.CompilerParams(dimension_semantics=sem, vmem_limit_bytes=VMEM_LIMIT)


def _dot(a, b):
    return jnp.dot(a, b, preferred_element_type=F32)


def _dot_nt(a, b):
    return lax.dot_general(a, b, (((1,), (1,)), ((), ())), preferred_element_type=F32)


def _split_bf16(x):
    hi = x.astype(BF16)
    lo = (x - hi.astype(F32)).astype(BF16)
    return hi, lo


def _seg_rms_norm(x, seg_mat, g):
    hi, lo = _split_bf16(x * x)
    ms = _dot(hi, seg_mat) + _dot(lo, seg_mat)
    return x * lax.rsqrt(ms + EPS) * g


def _rope(x, cos_t, sin_t, nf):
    n = x.shape[-1]
    lane = lax.broadcasted_iota(jnp.int32, x.shape, 1)
    first = (lane & (2 * nf - 1)) < nf
    partner = jnp.where(first, pltpu.roll(x, n - nf, axis=1), pltpu.roll(x, nf, axis=1))
    return x * cos_t + partner * sin_t


def _softplus(x):
    return jnp.maximum(x, 0.0) + jnp.log1p(jnp.exp(-jnp.abs(x)))


def _mod_kernel(cond_ref, w_ref, b_ref, o_ref):
    c = cond_ref[...]
    s = c * jax.nn.sigmoid(c)
    o_ref[...] = _dot(s.astype(BF16), w_ref[...].astype(BF16)) + b_ref[...]


def _mod_call(cond, w, b):
    nb = 6
    return pl.pallas_call(
        _mod_kernel,
        grid=(nb,),
        in_specs=[pl.BlockSpec((N_COND, D_MODEL), lambda j: (0, 0)),
                  pl.BlockSpec((D_MODEL, D_MODEL), lambda j: (0, j)),
                  pl.BlockSpec((1, D_MODEL), lambda j: (0, j))],
        out_specs=pl.BlockSpec((N_COND, D_MODEL), lambda j: (0, j)),
        out_shape=jax.ShapeDtypeStruct((N_COND, 6 * D_MODEL), F32),
        compiler_params=_cparams(("arbitrary",)),
        name="mod",
    )(cond, w, b.reshape(1, 6 * D_MODEL))


def _inproj_kernel(seq_ref, rope_ref, x_ref, mod_ref, g1_ref, w_ref, s32_ref, s64_ref,
                   gqc_ref, gkc_ref, gqd_ref, gkd_ref, cosc_ref, sinc_ref, cosd_ref, sind_ref, o_ref):
    del seq_ref, rope_ref
    x = x_ref[...]
    xn = x * lax.rsqrt(jnp.mean(x * x, axis=-1, keepdims=True) + EPS) * g1_ref[...]
    mod = mod_ref[0]
    sh1 = mod[:, 0:D_MODEL]
    sc1 = mod[:, D_MODEL:2 * D_MODEL]
    h = xn * (1.0 + sc1) + sh1
    p = _dot(h.astype(BF16), w_ref[...])
    o_ref[:, OFF_AU:OFF_BX] = jax.nn.gelu(p[:, OFF_AU:OFF_BX])
    o_ref[:, OFF_BX:OFF_BG] = p[:, OFF_BX:OFF_BG]
    o_ref[:, OFF_BG:OFF_CQ] = jax.nn.gelu(p[:, OFF_BG:OFF_CQ])
    s32 = s32_ref[...]
    cosc = cosc_ref[...]
    sinc = sinc_ref[...]
    cq = _seg_rms_norm(p[:, OFF_CQ:OFF_CK], s32, gqc_ref[...])
    ck = _seg_rms_norm(p[:, OFF_CK:OFF_CV], s32, gkc_ref[...])
    o_ref[:, OFF_CQ:OFF_CK] = _rope(cq, cosc, sinc, DIFF_QK_DIM // 4)
    o_ref[:, OFF_CK:OFF_CV] = _rope(ck, cosc, sinc, DIFF_QK_DIM // 4)
    o_ref[:, OFF_CV:OFF_DQ] = p[:, OFF_CV:OFF_DQ]
    s64 = s64_ref[...]
    cosd = cosd_ref[...]
    sind = sind_ref[...]
    dq = _seg_rms_norm(p[:, OFF_DQ:OFF_DK], s64, gqd_ref[...])
    dk = _seg_rms_norm(p[:, OFF_DK:OFF_DV], s64[0:128, 0:128], gkd_ref[...])
    o_ref[:, OFF_DQ:OFF_DK] = _rope(dq, cosd, sind, HEAD_DIM // 4)
    o_ref[:, OFF_DK:OFF_DV] = _rope(dk, cosd[:, 0:128], sind[:, 0:128], HEAD_DIM // 4)
    o_ref[:, OFF_DV:IN_WIDTH] = p[:, OFF_DV:IN_WIDTH]


def _inproj_call(seq_of_tile, rope_of_tile, x, mod3, g1, w_in, s32, s64, gqc, gkc, gqd, gkd, cosc, sinc, cosd, sind):
    full = lambda shape: pl.BlockSpec(shape, lambda i, s, r: (0,) * len(shape))
    rope_spec = pl.BlockSpec((TM, GROUP_WIDTH), lambda i, s, r: (r[i], 0))
    grid_spec = pltpu.PrefetchScalarGridSpec(
        num_scalar_prefetch=2,
        grid=(N_TILES,),
        in_specs=[pl.BlockSpec((TM, D_MODEL), lambda i, s, r: (i, 0)),
                  pl.BlockSpec((1, 1, 6 * D_MODEL), lambda i, s, r: (s[i], 0, 0)),
                  full((1, D_MODEL)),
                  full((D_MODEL, IN_WIDTH)),
                  full((GROUP_WIDTH, GROUP_WIDTH)),
                  full((GROUP_WIDTH, GROUP_WIDTH)),
                  full((1, GROUP_WIDTH)), full((1, GROUP_WIDTH)), full((1, GROUP_WIDTH)), full((1, 128)),
                  rope_spec, rope_spec, rope_spec, rope_spec],
        out_specs=pl.BlockSpec((TM, IN_WIDTH), lambda i, s, r: (i, 0)),
    )
    return pl.pallas_call(
        _inproj_kernel,
        grid_spec=grid_spec,
        out_shape=jax.ShapeDtypeStruct((N_TOK, IN_WIDTH), F32),
        compiler_params=_cparams(("arbitrary",)),
        name="inproj",
    )(seq_of_tile, rope_of_tile, x, mod3, g1, w_in, s32, s64, gqc, gkc, gqd, gkd, cosc, sinc, cosd, sind)


def _gmlp_kernel(u_ref, v_ref, s64_ref, g_ref, ws_ref, bias_ref, o_ref):
    vh = _seg_rms_norm(v_ref[...], s64_ref[...], g_ref[...]).astype(BF16)
    head = lax.broadcasted_iota(jnp.int32, (CHUNK, GROUP_WIDTH), 1) // HEAD_DIM
    for ch in range(TM // CHUNK):
        rows = slice(ch * CHUNK, (ch + 1) * CHUNK)
        vc = vh[rows]
        mixed = bias_ref[...]
        for h in range(N_GROUP_HEADS):
            mixed = mixed + jnp.where(head == h, _dot(ws_ref[h], vc), 0.0)
        o_ref[rows, :] = u_ref[rows, :] * mixed


def _gmlp_call(proj, s64, g, ws, bias):
    return pl.pallas_call(
        _gmlp_kernel,
        grid=(N_TILES,),
        in_specs=[pl.BlockSpec((TM, GROUP_WIDTH), lambda i: (i, OFF_AU // GROUP_WIDTH)),
                  pl.BlockSpec((TM, GROUP_WIDTH), lambda i: (i, OFF_AV // GROUP_WIDTH)),
                  pl.BlockSpec((GROUP_WIDTH, GROUP_WIDTH), lambda i: (0, 0)),
                  pl.BlockSpec((1, GROUP_WIDTH), lambda i: (0, 0)),
                  pl.BlockSpec((N_GROUP_HEADS, CHUNK, CHUNK), lambda i: (0, 0, 0)),
                  pl.BlockSpec((CHUNK, GROUP_WIDTH), lambda i: (0, 0))],
        out_specs=pl.BlockSpec((TM, GROUP_WIDTH), lambda i: (i, 0)),
        out_shape=jax.ShapeDtypeStruct((N_TOK, GROUP_WIDTH), F32),
        compiler_params=_cparams(("arbitrary",)),
        name="gmlp",
    )(proj, proj, s64, g, ws, bias)


def _scan_chunk(a, b, reverse):
    n = a.shape[0]
    row = lax.broadcasted_iota(jnp.int32, a.shape, 0)
    s = 1
    while s < n:
        if reverse:
            keep = row < n - s
            shift = n - s
        else:
            keep = row >= s
            shift = s
        a_prev = jnp.where(keep, pltpu.roll(a, shift, axis=0), 1.0)
        b_prev = jnp.where(keep, pltpu.roll(b, shift, axis=0), 0.0)
        b = a * b_prev + b
        a = a * a_prev
        s *= 2
    return a, b


def _lru_kernel(x_ref, g_ref, h0_ref, cw_ref, cb_ref, wcat_ref, bcat_ref, lam_ref, y_ref, st_ref,
                xpad, a_f, b_f, a_b, b_b, h_f, *, seq_len):
    nc = seq_len // LRU_CHUNK
    w = GROUP_WIDTH
    zeros8 = jnp.zeros((8, w), F32)
    xpad[0:8, :] = zeros8
    xpad[seq_len + 8:seq_len + 16, :] = zeros8
    xpad[8:seq_len + 8, :] = x_ref[...]
    sp = _softplus(-lam_ref[...])
    cw = cw_ref[...]
    cb = cb_ref[...]
    win_rows = LRU_CHUNK + 16

    def gates(c, carry):
        r0 = pl.multiple_of(c * LRU_CHUNK, LRU_CHUNK)
        win = xpad[pl.ds(r0, win_rows), :]
        inner = slice(8, 8 + LRU_CHUNK)
        xc = cb + pltpu.roll(win, 2, axis=0)[inner] * cw[0:1]
        xc = xc + pltpu.roll(win, 1, axis=0)[inner] * cw[1:2]
        xc = xc + win[inner] * cw[2:3]
        xc = xc + pltpu.roll(win, win_rows - 1, axis=0)[inner] * cw[3:4]
        sg = jax.nn.sigmoid(_dot(xc.astype(BF16), wcat_ref[...]) + bcat_ref[...])
        for d, (a_ref, b_ref) in enumerate(((a_f, b_f), (a_b, b_b))):
            r = sg[:, (2 * d) * w:(2 * d + 1) * w]
            i = sg[:, (2 * d + 1) * w:(2 * d + 2) * w]
            log_a = (-LRU_C * r) * sp[d:d + 1]
            a = jnp.exp(log_a)
            a_ref[pl.ds(r0, LRU_CHUNK), :] = a
            b_ref[pl.ds(r0, LRU_CHUNK), :] = jnp.sqrt(-jnp.tanh(log_a) * (a * a + 1.0)) * (i * xc)
        return carry

    lax.fori_loop(0, nc, gates, 0)

    def fwd(c, carry):
        r0 = pl.multiple_of(c * LRU_CHUNK, LRU_CHUNK)
        a_cum, h_loc = _scan_chunk(a_f[pl.ds(r0, LRU_CHUNK), :], b_f[pl.ds(r0, LRU_CHUNK), :], False)
        h = h_loc + a_cum * carry
        h_f[pl.ds(r0, LRU_CHUNK), :] = h
        return h[LRU_CHUNK - 1:LRU_CHUNK, :]

    s_f = lax.fori_loop(0, nc, fwd, h0_ref[0, 0:1, :])

    def bwd(k, carry):
        c = nc - 1 - k
        r0 = pl.multiple_of(c * LRU_CHUNK, LRU_CHUNK)
        a_cum, h_loc = _scan_chunk(a_b[pl.ds(r0, LRU_CHUNK), :], b_b[pl.ds(r0, LRU_CHUNK), :], True)
        h = h_loc + a_cum * carry
        y_ref[pl.ds(r0, LRU_CHUNK), :] = (h_f[pl.ds(r0, LRU_CHUNK), :] + h) * g_ref[pl.ds(r0, LRU_CHUNK), :]
        return h[0:1, :]

    s_b = lax.fori_loop(0, nc, bwd, h0_ref[0, 1:2, :])
    st_ref[0, 0:1, :] = s_f
    st_ref[0, 1:2, :] = s_b


def _lru_call(proj, h0, cw, cb, wcat, bcat, lam, *, seq_len, n_seq, row_block0):
    w = GROUP_WIDTH
    full = lambda shape: pl.BlockSpec(shape, lambda b: (0,) * len(shape))
    seq_block = lambda col: pl.BlockSpec((seq_len, w), lambda b: (b + row_block0, col))
    return pl.pallas_call(
        functools.partial(_lru_kernel, seq_len=seq_len),
        grid=(n_seq,),
        in_specs=[seq_block(OFF_BX // w), seq_block(OFF_BG // w),
                  pl.BlockSpec((1, 2, w), lambda b: (b, 0, 0)),
                  full((CONV_W, w)), full((1, w)), full((w, 4 * w)), full((1, 4 * w)), full((2, w))],
        out_specs=[pl.BlockSpec((seq_len, w), lambda b: (b, 0)),
                   pl.BlockSpec((1, 2, w), lambda b: (b, 0, 0))],
        out_shape=[jax.ShapeDtypeStruct((n_seq * seq_len, w), F32),
                   jax.ShapeDtypeStruct((n_seq, 2, w), F32)],
        scratch_shapes=[pltpu.VMEM((seq_len + 16, w), F32)] + [pltpu.VMEM((seq_len, w), F32)] * 5,
        compiler_params=_cparams(("arbitrary",)),
        name="lru_%d" % seq_len,
    )(proj, proj, h0, cw, cb, wcat, bcat, lam)


def _diff_lambda(ld_ref, lam_init):
    ld = ld_ref[...]
    l1 = jnp.sum(ld[0:1] * ld[1:2], axis=-1, keepdims=True)
    l2 = jnp.sum(ld[2:3] * ld[3:4], axis=-1, keepdims=True)
    return jnp.exp(l1) - jnp.exp(l2) + lam_init


def _diff_attn_body(q, keys, values, lam, s64, g, lam_init):
    tq = q.shape[0]
    lane = lax.broadcasted_iota(jnp.int32, (1, GROUP_WIDTH), 1)
    scale = DIFF_QK_DIM ** -0.5
    o = jnp.zeros((tq, GROUP_WIDTH), F32)
    for h in range(N_GROUP_HEADS):
        probs = []
        for i in range(2):
            seg = h * 2 + i
            qm = jnp.where(lane // DIFF_QK_DIM == seg, q, 0.0).astype(BF16)
            s = [_dot_nt(qm, k) * scale for k in keys]
            m = functools.reduce(jnp.maximum, [jnp.max(x, axis=-1, keepdims=True) for x in s])
            e = [jnp.exp(x - m) for x in s]
            den = functools.reduce(jnp.add, [jnp.sum(x, axis=-1, keepdims=True) for x in e])
            inv = 1.0 / den
            probs.append([x * inv for x in e])
        oh = None
        for p0, p1, v in zip(probs[0], probs[1], values):
            part = _dot((p0 - lam * p1).astype(BF16), v)
            oh = part if oh is None else oh + part
        o = o + jnp.where(lane // HEAD_DIM == h, oh, 0.0)
    return _seg_rms_norm(o, s64, g) * (1.0 - lam_init)


def _diff_prompt_kernel(q_ref, k_ref, v_ref, ld_ref, s64_ref, g_ref, o_ref, *, lam_init):
    lam = _diff_lambda(ld_ref, lam_init)
    o_ref[...] = _diff_attn_body(q_ref[...], [k_ref[...].astype(BF16)], [v_ref[...].astype(BF16)],
                                 lam, s64_ref[...], g_ref[...], lam_init)


def _diff_sample_kernel(q_ref, k_ref, v_ref, ck_ref, cv_ref, ld_ref, s64_ref, g_ref, o_ref, *, lam_init):
    lam = _diff_lambda(ld_ref, lam_init)
    keys = [ck_ref[0].astype(BF16), k_ref[...].astype(BF16)]
    values = [cv_ref[0].astype(BF16), v_ref[...].astype(BF16)]
    o_ref[...] = _diff_attn_body(q_ref[...], keys, values, lam, s64_ref[...], g_ref[...], lam_init)


def _diff_prompt_call(proj, ld, s64, g, lam_init):
    w = GROUP_WIDTH
    full = lambda shape: pl.BlockSpec(shape, lambda b: (0,) * len(shape))
    return pl.pallas_call(
        functools.partial(_diff_prompt_kernel, lam_init=lam_init),
        grid=(BATCH,),
        in_specs=[pl.BlockSpec((SEQ, w), lambda b: (b, OFF_CQ // w)),
                  pl.BlockSpec((SEQ, w), lambda b: (b, OFF_CK // w)),
                  pl.BlockSpec((SEQ, w), lambda b: (b, OFF_CV // w)),
                  full((4, DIFF_QK_DIM)), full((w, w)), full((1, w))],
        out_specs=pl.BlockSpec((SEQ, w), lambda b: (b, 0)),
        out_shape=jax.ShapeDtypeStruct((N_PROMPT, w), F32),
        compiler_params=_cparams(("arbitrary",)),
        name="diff_prompt",
    )(proj, proj, proj, ld, s64, g)


def _diff_sample_call(proj, ctx_k, ctx_v, ld, s64, g, lam_init):
    w = GROUP_WIDTH
    tq = 256
    nq = DEC_SEQ // tq
    full = lambda shape: pl.BlockSpec(shape, lambda b, i: (0,) * len(shape))
    seq_block0 = N_PROMPT // DEC_SEQ
    return pl.pallas_call(
        functools.partial(_diff_sample_kernel, lam_init=lam_init),
        grid=(DEC_BATCH, nq),
        in_specs=[pl.BlockSpec((tq, w), lambda b, i: (N_PROMPT // tq + b * nq + i, OFF_CQ // w)),
                  pl.BlockSpec((DEC_SEQ, w), lambda b, i: (seq_block0 + b, OFF_CK // w)),
                  pl.BlockSpec((DEC_SEQ, w), lambda b, i: (seq_block0 + b, OFF_CV // w)),
                  pl.BlockSpec((1, PAST_LEN, w), lambda b, i: (b, 0, 0)),
                  pl.BlockSpec((1, PAST_LEN, w), lambda b, i: (b, 0, 0)),
                  full((4, DIFF_QK_DIM)), full((w, w)), full((1, w))],
        out_specs=pl.BlockSpec((tq, w), lambda b, i: (b * nq + i, 0)),
        out_shape=jax.ShapeDtypeStruct((N_SAMPLE, w), F32),
        compiler_params=_cparams(("arbitrary", "arbitrary")),
        name="diff_sample",
    )(proj, proj, proj, ctx_k, ctx_v, ld, s64, g)


def _sink_attn_body(q, keys, values, masks, sink_ref):
    tq = q.shape[0]
    lane = lax.broadcasted_iota(jnp.int32, (1, 128), 1)
    scale = HEAD_DIM ** -0.5
    outs = []
    for grp in range(SWA_GROUPS):
        qg = q[:, grp * 128:(grp + 1) * 128]
        og = jnp.zeros((tq, 128), F32)
        for kh in range(SWA_KV_HEADS):
            sink = sink_ref[kh * SWA_GROUPS + grp]
            qm = jnp.where(lane // HEAD_DIM == kh, qg, 0.0).astype(BF16)
            s = []
            for k, msk in zip(keys, masks):
                x = _dot_nt(qm, k) * scale
                s.append(x if msk is None else jnp.where(msk, x, -jnp.inf))
            m = functools.reduce(jnp.maximum, [jnp.max(x, axis=-1, keepdims=True) for x in s])
            m = jnp.maximum(m, sink)
            e = [jnp.exp(x - m) for x in s]
            den = functools.reduce(jnp.add, [jnp.sum(x, axis=-1, keepdims=True) for x in e]) + jnp.exp(sink - m)
            inv = 1.0 / den
            oh = None
            for x, v in zip(e, values):
                part = _dot((x * inv).astype(BF16), v)
                oh = part if oh is None else oh + part
            og = og + jnp.where(lane // HEAD_DIM == kh, oh, 0.0)
        outs.append(og)
    return outs


def _swa_prompt_kernel(sink_ref, q_ref, k_ref, v_ref, o_ref):
    outs = _sink_attn_body(q_ref[...], [k_ref[...].astype(BF16)], [v_ref[...].astype(BF16)], [None], sink_ref)
    for grp in range(SWA_GROUPS):
        o_ref[:, grp * 128:(grp + 1) * 128] = outs[grp]


def _swa_sample_kernel(sink_ref, q_ref, kp_ref, kc_ref, kn_ref, vp_ref, vc_ref, vn_ref, ck_ref, cv_ref, o_ref):
    n = pl.program_id(1)
    nb = pl.num_programs(1)
    r = lax.broadcasted_iota(jnp.int32, (WINDOW, WINDOW), 0)
    c = lax.broadcasted_iota(jnp.int32, (WINDOW, WINDOW), 1)
    mask_prev = c >= r + jnp.where(n > 0, 0, WINDOW)
    mask_next = c <= r - jnp.where(n < nb - 1, 0, WINDOW)
    keys = [ck_ref[0].astype(BF16), kp_ref[...].astype(BF16), kc_ref[...].astype(BF16), kn_ref[...].astype(BF16)]
    values = [cv_ref[0].astype(BF16), vp_ref[...].astype(BF16), vc_ref[...].astype(BF16), vn_ref[...].astype(BF16)]
    outs = _sink_attn_body(q_ref[...], keys, values, [None, mask_prev, None, mask_next], sink_ref)
    for grp in range(SWA_GROUPS):
        o_ref[:, grp * 128:(grp + 1) * 128] = outs[grp]


def _swa_prompt_call(sink, proj):
    w = GROUP_WIDTH
    return pl.pallas_call(
        _swa_prompt_kernel,
        grid=(BATCH,),
        in_specs=[pl.BlockSpec(memory_space=pltpu.SMEM),
                  pl.BlockSpec((SEQ, w), lambda b: (b, OFF_DQ // w)),
                  pl.BlockSpec((SEQ, 128), lambda b: (b, OFF_DK // 128)),
                  pl.BlockSpec((SEQ, 128), lambda b: (b, OFF_DV // 128))],
        out_specs=pl.BlockSpec((SEQ, w), lambda b: (b, 0)),
        out_shape=jax.ShapeDtypeStruct((N_PROMPT, w), F32),
        compiler_params=_cparams(("arbitrary",)),
        name="swa_prompt",
    )(sink, proj, proj, proj)


def _swa_sample_call(sink, proj, ctx_k, ctx_v):
    w = GROUP_WIDTH
    tq = WINDOW
    nq = DEC_SEQ // tq
    row0 = N_PROMPT // tq

    def kv_spec(col, delta):
        def index(b, i):
            j = jnp.clip(i + delta, 0, nq - 1)
            return (row0 + b * nq + j, col)
        return pl.BlockSpec((tq, 128), index)

    ctx_spec = pl.BlockSpec((1, PAST_LEN, 128), lambda b, i: (b, 0, 0))
    return pl.pallas_call(
        _swa_sample_kernel,
        grid=(DEC_BATCH, nq),
        in_specs=[pl.BlockSpec(memory_space=pltpu.SMEM),
                  pl.BlockSpec((tq, w), lambda b, i: (row0 + b * nq + i, OFF_DQ // w)),
                  kv_spec(OFF_DK // 128, -1), kv_spec(OFF_DK // 128, 0), kv_spec(OFF_DK // 128, 1),
                  kv_spec(OFF_DV // 128, -1), kv_spec(OFF_DV // 128, 0), kv_spec(OFF_DV // 128, 1),
                  ctx_spec, ctx_spec],
        out_specs=pl.BlockSpec((tq, w), lambda b, i: (b * nq + i, 0)),
        out_shape=jax.ShapeDtypeStruct((N_SAMPLE, w), F32),
        compiler_params=_cparams(("arbitrary", "arbitrary")),
        name="swa_sample",
    )(sink, proj, proj, proj, proj, proj, proj, proj, ctx_k, ctx_v)


def _outproj_kernel(seq_ref, ya_ref, yb_ref, yc_ref, yd_ref, x_ref, mod_ref, w_ref, g2_ref, rw_ref, rb_ref, tri_ref,
                    x1_ref, h2p_ref, topi_ref, topw_ref, rank_ref, count_ref, cnt_ref):
    del seq_ref
    w = GROUP_WIDTH
    mix = _dot(ya_ref[...].astype(BF16), w_ref[0:w, :])
    mix = mix + _dot(yb_ref[...].astype(BF16), w_ref[w:2 * w, :])
    mix = mix + _dot(yc_ref[...].astype(BF16), w_ref[2 * w:3 * w, :])
    mix = mix + _dot(yd_ref[...].astype(BF16), w_ref[3 * w:4 * w, :])
    mod = mod_ref[0]
    g1 = mod[:, 2 * D_MODEL:3 * D_MODEL]
    sh2 = mod[:, 3 * D_MODEL:4 * D_MODEL]
    sc2 = mod[:, 4 * D_MODEL:5 * D_MODEL]
    x1 = x_ref[...] + g1 * mix
    x1_ref[...] = x1
    xn = x1 * lax.rsqrt(jnp.mean(x1 * x1, axis=-1, keepdims=True) + EPS) * g2_ref[...]
    h2 = xn * (1.0 + sc2) + sh2
    half = D_MODEL // 2
    h2_bits = pltpu.bitcast(h2.astype(BF16).astype(F32), jnp.uint32)
    h2p_ref[...] = (h2_bits[:, :half] >> 16) | (h2_bits[:, half:] & jnp.uint32(0xFFFF0000))
    h_hi, h_lo = _split_bf16(h2)
    r_hi, r_lo = _split_bf16(rw_ref[...])
    logits = _dot(h_hi, r_hi) + (_dot(h_hi, r_lo) + _dot(h_lo, r_hi)) + rb_ref[...]
    lane = lax.broadcasted_iota(jnp.int32, logits.shape, 1)
    work = logits
    picks = []
    for _ in range(TOP_K):
        m = jnp.max(work, axis=-1, keepdims=True)
        idx = jnp.min(jnp.where(work == m, lane, N_EXPERTS), axis=-1, keepdims=True)
        picks.append((m, idx))
        work = jnp.where(lane == idx, -jnp.inf, work)
    top = picks[0][0]
    ex = [jnp.exp(m - top) for m, _ in picks]
    inv = 1.0 / functools.reduce(jnp.add, ex)
    @pl.when(pl.program_id(0) % (MOE_TB // TM) == 0)
    def _():
        cnt_ref[...] = jnp.zeros_like(cnt_ref)

    sel = jnp.zeros(logits.shape, F32)
    for _, idx in picks:
        sel = sel + jnp.where(lane == idx, 1.0, 0.0)
    csum = _dot(tri_ref[...], sel.astype(BF16))
    before = cnt_ref[...] + csum - sel
    cnt_ref[...] += csum[TM - 1:TM, :]
    count_ref[0] = cnt_ref[...].astype(jnp.int32)
    slot = lax.broadcasted_iota(jnp.int32, topi_ref.shape, 1)
    top_i = jnp.zeros(topi_ref.shape, jnp.int32)
    top_w = jnp.zeros(topw_ref.shape, F32)
    rank = jnp.zeros(rank_ref.shape, jnp.int32)
    for k, (e, (_, idx)) in enumerate(zip(ex, picks)):
        rank_k = jnp.sum(jnp.where(lane == idx, before, 0.0), axis=-1, keepdims=True).astype(jnp.int32)
        top_i = jnp.where(slot == k, idx, top_i)
        top_w = jnp.where(slot == k, e * inv, top_w)
        rank = jnp.where(slot == k, rank_k, rank)
    topi_ref[...] = top_i
    topw_ref[...] = top_w
    rank_ref[...] = rank


def _outproj_call(seq_of_tile, ya, yb, yc, yd, x, mod3, w_out, g2, rw, rb, tri):
    w = GROUP_WIDTH
    full = lambda shape: pl.BlockSpec(shape, lambda i, s: (0,) * len(shape))
    tile = lambda width: pl.BlockSpec((TM, width), lambda i, s: (i, 0))
    tiles_per_block = MOE_TB // TM
    return pl.pallas_call(
        _outproj_kernel,
        grid_spec=pltpu.PrefetchScalarGridSpec(
            num_scalar_prefetch=1,
            grid=(N_TILES,),
            in_specs=[tile(w), tile(w), tile(w), tile(w), tile(D_MODEL),
                      pl.BlockSpec((1, 1, 6 * D_MODEL), lambda i, s: (s[i], 0, 0)),
                      full((D_MODEL, D_MODEL)), full((1, D_MODEL)),
                      full((D_MODEL, N_EXPERTS)), full((1, N_EXPERTS)), full((TM, TM))],
            out_specs=[tile(D_MODEL), tile(D_MODEL // 2), tile(TOPK_PAD), tile(TOPK_PAD), tile(TOPK_PAD),
                       pl.BlockSpec((1, 1, N_EXPERTS), lambda i, s: (i // tiles_per_block, 0, 0))],
            scratch_shapes=[pltpu.VMEM((1, N_EXPERTS), F32)],
        ),
        out_shape=[jax.ShapeDtypeStruct((N_TOK, D_MODEL), F32),
                   jax.ShapeDtypeStruct((N_TOK, D_MODEL // 2), jnp.uint32),
                   jax.ShapeDtypeStruct((N_TOK, TOPK_PAD), jnp.int32),
                   jax.ShapeDtypeStruct((N_TOK, TOPK_PAD), F32),
                   jax.ShapeDtypeStruct((N_TOK, TOPK_PAD), jnp.int32),
                   jax.ShapeDtypeStruct((N_TOK // MOE_TB, 1, N_EXPERTS), jnp.int32)],
        compiler_params=_cparams(("arbitrary",)),
        name="outproj",
    )(seq_of_tile, ya, yb, yc, yd, x, mod3, w_out, g2, rw, rb, tri)


def _moe_kernel(expert_ref, block_ref, flag_ref, local_ref, dest_ref, topw_ref, gstart_ref, count_ref,
                src_ref, wgu_ref, bgu_ref, wd_ref, bd_ref, acc_ref, pair_ref, xt_ref, y3_ref):
    del expert_ref, block_ref
    i = pl.program_id(0)
    flags = flag_ref[i]

    @pl.when((flags & 2) != 0)
    def _():
        def zero(c, carry):
            acc_ref[pl.ds(pl.multiple_of(c * MOE_TILE, MOE_TILE), MOE_TILE)] = jnp.zeros((MOE_TILE, 8, 128), F32)
            return carry
        lax.fori_loop(0, MOE_TB // MOE_TILE, zero, 0)

        def pad_group(e, carry):
            cnt = count_ref[0, 0, e]
            first_pad = gstart_ref[0, 0, e] + cnt
            n_pad = (-cnt) & (MOE_TILE - 1)

            def pad_row(r, c2):
                pair_ref[first_pad + r] = MOE_PAD_PAIR
                return c2
            lax.fori_loop(0, n_pad, pad_row, 0)
            return carry
        lax.fori_loop(0, N_EXPERTS, pad_group, 0)

        def place(t, carry):
            for k in range(TOP_K):
                pair = t * TOP_K + k
                pair_ref[dest_ref[0, 0, pair]] = pair
            return carry
        lax.fori_loop(0, MOE_TB, place, 0)

    @pl.when((flags & 1) != 0)
    def _():
        pieces = D_MODEL // 2 // 128
        row0 = local_ref[i]

        def token_of(pair):
            return (pair >> MOE_PAIR_SHIFT) & (MOE_TB - 1)

        for m in range(MOE_TILE):
            t = token_of(pair_ref[row0 + m])
            slab = src_ref[pl.ds(pl.multiple_of(t * pieces, pieces), pieces), :]
            xt_ref[pl.ds(m, pieces, stride=MOE_XT_STRIDE), :] = slab
        lo, hi = [], []
        for p in range(pieces):
            u = xt_ref[p * MOE_XT_STRIDE:p * MOE_XT_STRIDE + MOE_TILE, :]
            lo.append(pltpu.bitcast(u << 16, F32))
            hi.append(pltpu.bitcast(u & jnp.uint32(0xFFFF0000), F32))
        x = jnp.concatenate(lo + hi, axis=1).astype(BF16)
        gu = _dot(x, wgu_ref[0, 0]) + bgu_ref[0, 0]
        gate = jnp.minimum(gu[:, :D_FF], SWIGLU_LIMIT)
        up = jnp.clip(gu[:, D_FF:], -SWIGLU_LIMIT, SWIGLU_LIMIT)
        act = (up + 1.0) * gate * jax.nn.sigmoid(SWIGLU_ALPHA * gate)
        y = _dot(act.astype(BF16), wd_ref[0, 0]) + bd_ref[0, 0]
        for j in range(D_MODEL // 128):
            y3_ref[:, j * 8:(j + 1) * 8, :] = y[:, j * 128:(j + 1) * 128].reshape(MOE_TILE // 8, 8, 128)
        for m0 in range(0, MOE_TILE, MOE_RMW_BATCH):
            rows = range(m0, m0 + MOE_RMW_BATCH)
            pairs = [pair_ref[row0 + m] for m in rows]
            toks = [token_of(p) for p in pairs]
            vals = [acc_ref[t] + topw_ref[0, 0, p] * y3_ref[m // 8, pl.ds(m % 8, 8, stride=8), :]
                    for t, p, m in zip(toks, pairs, rows)]
            for t, v in reversed(list(zip(toks, vals))):
                acc_ref[t] = v


def _moe_call(layer, tile_expert, tile_block, tile_flags, tile_local, dest, topw, gstart, counts, h2p_rows,
              wgu, bgu, wd, bd):
    pieces = D_MODEL // 2 // 128
    once = pl.Buffered(1)
    n_pairs = MOE_TB * TOP_K

    def smem_block(width):
        return pl.BlockSpec((1, 1, width), lambda i, e, b, f, lo: (b[i], 0, 0), memory_space=pltpu.SMEM,
                            pipeline_mode=once)

    def expert_block(rows, cols):
        return pl.BlockSpec((1, 1, rows, cols), lambda i, e, b, f, lo: (layer, e[i], 0, 0))

    return pl.pallas_call(
        _moe_kernel,
        grid_spec=pltpu.PrefetchScalarGridSpec(
            num_scalar_prefetch=4,
            grid=(MOE_MAX_TILES,),
            in_specs=[smem_block(n_pairs), smem_block(n_pairs + 128), smem_block(N_EXPERTS), smem_block(N_EXPERTS),
                      pl.BlockSpec((MOE_TB * pieces, 128), lambda i, e, b, f, lo: (b[i], 0), pipeline_mode=once),
                      expert_block(D_MODEL, 2 * D_FF), expert_block(1, 2 * D_FF),
                      expert_block(D_FF, D_MODEL), expert_block(1, D_MODEL)],
            out_specs=pl.BlockSpec((MOE_TB, 8, 128), lambda i, e, b, f, lo: (b[i], 0, 0), pipeline_mode=once),
            scratch_shapes=[pltpu.SMEM((MOE_ROWS_PER_BLOCK,), jnp.int32),
                            pltpu.VMEM((pieces * MOE_XT_STRIDE, 128), jnp.uint32),
                            pltpu.VMEM((MOE_TILE // 8, 64, 128), F32)],
        ),
        out_shape=jax.ShapeDtypeStruct((N_TOK, 8, 128), F32),
        compiler_params=_cparams(("arbitrary",)),
        name="moe",
    )(tile_expert, tile_block, tile_flags, tile_local, dest, topw, gstart, counts, h2p_rows, wgu, bgu, wd, bd)


def _moe_routing(top_i, top_w, rank, counts):
    nb = N_TOK // MOE_TB
    n_groups = nb * N_EXPERTS
    counts = counts.reshape(nb, N_EXPERTS)
    padded = ((counts + MOE_TILE - 1) // MOE_TILE) * MOE_TILE
    group_end = jnp.cumsum(padded.reshape(-1))
    group_start = (group_end - padded.reshape(-1)).reshape(nb, N_EXPERTS)
    block_row0 = group_start[:, 0]
    gstart_local = group_start - block_row0[:, None]
    experts = jnp.arange(N_EXPERTS, dtype=jnp.int32)
    picked = top_i.reshape(nb, MOE_TB, TOP_K, 1) == experts
    dest = jnp.sum(jnp.where(picked, gstart_local[:, None, None, :], 0), axis=-1) + rank.reshape(nb, MOE_TB, TOP_K)
    dest = dest.reshape(nb, 1, MOE_TB * TOP_K).astype(jnp.int32)
    topw = jnp.concatenate([top_w.reshape(nb, 1, MOE_TB * TOP_K), jnp.zeros((nb, 1, 128), F32)], axis=-1)
    tile_start = jnp.arange(MOE_MAX_TILES, dtype=jnp.int32) * MOE_TILE
    tile_group = jnp.sum((group_end[None, :] <= tile_start[:, None]).astype(jnp.int32), axis=1)
    tile_group = jnp.minimum(tile_group, n_groups - 1)
    valid = tile_start < group_end[-1]
    tile_block = tile_group // N_EXPERTS
    first = jnp.concatenate([jnp.ones((1,), bool), tile_block[1:] != tile_block[:-1]])
    flags = valid.astype(jnp.int32) + 2 * first.astype(jnp.int32)
    blocks = jnp.arange(nb, dtype=jnp.int32)
    tile_row0 = jnp.sum(jnp.where(tile_block[:, None] == blocks[None, :], block_row0[None, :], 0), axis=1)
    tile_local = (tile_start - tile_row0).astype(jnp.int32)
    return (tile_group % N_EXPERTS, tile_block, flags, tile_local, dest, topw,
            gstart_local.reshape(nb, 1, N_EXPERTS).astype(jnp.int32), counts.reshape(nb, 1, N_EXPERTS))


def _residual_kernel(seq_ref, x1_ref, moe_ref, mod_ref, o_ref):
    del seq_ref
    g2 = mod_ref[0][:, 5 * D_MODEL:6 * D_MODEL]
    o_ref[...] = x1_ref[...] + g2 * moe_ref[...]


def _residual_call(seq_of_tile, x1, moe, mod3):
    tile = pl.BlockSpec((TM, D_MODEL), lambda i, s: (i, 0))
    return pl.pallas_call(
        _residual_kernel,
        grid_spec=pltpu.PrefetchScalarGridSpec(
            num_scalar_prefetch=1,
            grid=(N_TILES,),
            in_specs=[tile, tile, pl.BlockSpec((1, 1, 6 * D_MODEL), lambda i, s: (s[i], 0, 0))],
            out_specs=tile,
        ),
        out_shape=jax.ShapeDtypeStruct((N_TOK, D_MODEL), F32),
        compiler_params=_cparams(("arbitrary",)),
        name="residual",
    )(seq_of_tile, x1, moe, mod3)


def _segment_matrix(width, seg):
    idx = np.arange(width) // seg
    return jnp.asarray((idx[:, None] == idx[None, :]).astype(np.float32) / seg, BF16)


def _rope_tables(rot_dim, width):
    rows = DEC_SEQ // GRID_W
    nf = rot_dim // 4
    inv = 1.0 / (ROPE_BASE ** (jnp.arange(nf, dtype=F32) / nf))
    row = jnp.repeat(jnp.arange(rows, dtype=F32), GRID_W)
    col = jnp.tile(jnp.arange(GRID_W, dtype=F32), rows)
    ang = jnp.stack([row[:, None] * inv, col[:, None] * inv], axis=1)
    cos, sin = jnp.cos(ang), jnp.sin(ang)
    cos_r = jnp.concatenate([cos[:, 0], cos[:, 0], cos[:, 1], cos[:, 1]], axis=-1)
    sin_r = jnp.concatenate([-sin[:, 0], sin[:, 0], -sin[:, 1], sin[:, 1]], axis=-1)
    reps = width // rot_dim
    cos_t = jnp.concatenate([jnp.ones((TM, width), F32), jnp.tile(cos_r, (1, reps))], axis=0)
    sin_t = jnp.concatenate([jnp.zeros((TM, width), F32), jnp.tile(sin_r, (1, reps))], axis=0)
    return cos_t, sin_t


def _block_diag(wb):
    nb, n, _ = wb.shape
    out = jnp.zeros((nb * n, nb * n), wb.dtype)
    for i in range(nb):
        out = out.at[i * n:(i + 1) * n, i * n:(i + 1) * n].set(wb[i])
    return out


_DQ_HEADS = [kh * SWA_GROUPS + g for g in range(SWA_GROUPS) for kh in range(SWA_KV_HEADS)]


def kernel(x_prompt, x_sample, c, cache_diff_k, cache_diff_v, cache_swa_k, cache_swa_v, state_lru, c_ctx, mod_w, mod_b, norm1_g, norm2_g, w_in, w_out, mlp_vnorm_g, mlp_ws, mlp_bs, lru_conv_w, lru_conv_b, lru_wa, lru_ba, lru_wx, lru_bx, lru_lambda, diff_qnorm_g, diff_knorm_g, diff_lambda, diff_subln_g, swa_qnorm_g, swa_knorm_g, swa_sink, router_w, router_b, moe_w_gu, moe_b_gu, moe_w_down, moe_b_down):
    params = dict(locals())
    params["moe_w_gu_bf16"] = moe_w_gu.astype(BF16)
    params["moe_w_down_bf16"] = moe_w_down.astype(BF16)
    x = jnp.concatenate([x_prompt.reshape(N_PROMPT, D_MODEL), x_sample.reshape(N_SAMPLE, D_MODEL)], axis=0)
    cond = jnp.concatenate([c_ctx[None], c, jnp.zeros((N_COND - 1 - DEC_BATCH, D_MODEL), F32)], axis=0)
    consts = _constants()
    dk_l, dv_l, sk_l, sv_l, lru_l = [], [], [], [], []
    for l in range(DEPTH):
        st = _mixer_stage(x, cond, params, l, consts)
        x = _ffn_stage(x, st, params, l, consts)
        pp = st["proj"][:N_PROMPT]
        dk_l.append(pp[:, OFF_CK:OFF_CV].reshape(BATCH, SEQ, N_GROUP_HEADS, 2, DIFF_QK_DIM))
        dv_l.append(pp[:, OFF_CV:OFF_DQ].reshape(BATCH, SEQ, N_GROUP_HEADS, HEAD_DIM))
        sk_l.append(pp[:, OFF_DK:OFF_DV].reshape(BATCH, SEQ, SWA_KV_HEADS, HEAD_DIM))
        sv_l.append(pp[:, OFF_DV:IN_WIDTH].reshape(BATCH, SEQ, SWA_KV_HEADS, HEAD_DIM))
        lru_l.append(st["st_p"])

    y_p = x[:N_PROMPT].reshape(BATCH, SEQ, D_MODEL)
    y_s = x[N_PROMPT:].reshape(DEC_BATCH, DEC_SEQ, D_MODEL)
    return (y_p, y_s, jnp.stack(dk_l, axis=1), jnp.stack(dv_l, axis=1), jnp.stack(sk_l, axis=1),
            jnp.stack(sv_l, axis=1), jnp.stack(lru_l, axis=1))


def _constants():
    w = GROUP_WIDTH
    tiles = np.arange(N_TILES)
    seq_np = np.where(tiles < PROMPT_TILES, 0, 1 + (tiles - PROMPT_TILES) // SAMPLE_TILES_PER_SEQ)
    rope_np = np.where(tiles < PROMPT_TILES, 0, 1 + (tiles - PROMPT_TILES) % SAMPLE_TILES_PER_SEQ)
    seq_of_tile = jnp.asarray(seq_np, jnp.int32)
    rope_of_tile = jnp.asarray(rope_np, jnp.int32)

    s32 = _segment_matrix(w, DIFF_QK_DIM)
    s64 = _segment_matrix(w, HEAD_DIM)
    cosc, sinc = _rope_tables(DIFF_QK_DIM, w)
    cosd, sind = _rope_tables(HEAD_DIM, w)
    dq_cols = np.concatenate([np.arange(h * HEAD_DIM, (h + 1) * HEAD_DIM) for h in _DQ_HEADS])
    tri = jnp.asarray(np.tril(np.ones((TM, TM), np.float32)), BF16)
    return dict(seq_of_tile=seq_of_tile, rope_of_tile=rope_of_tile, tri=tri, s32=s32, s64=s64,
                cosc=cosc, sinc=sinc, cosd=cosd, sind=sind, dq_cols=dq_cols)


def _mixer_stage(x, cond, params, l, consts):
    w = GROUP_WIDTH
    (mod_w, mod_b, norm1_g, w_in, mlp_vnorm_g, mlp_ws, mlp_bs, lru_conv_w, lru_conv_b, lru_wa, lru_ba, lru_wx, lru_bx,
     lru_lambda, diff_qnorm_g, diff_knorm_g, diff_lambda, diff_subln_g, swa_qnorm_g, swa_knorm_g, swa_sink,
     cache_diff_k, cache_diff_v, cache_swa_k, cache_swa_v, state_lru) = (params[k] for k in (
         "mod_w", "mod_b", "norm1_g", "w_in", "mlp_vnorm_g", "mlp_ws", "mlp_bs", "lru_conv_w", "lru_conv_b", "lru_wa",
         "lru_ba", "lru_wx", "lru_bx", "lru_lambda", "diff_qnorm_g", "diff_knorm_g", "diff_lambda", "diff_subln_g",
         "swa_qnorm_g", "swa_knorm_g", "swa_sink", "cache_diff_k", "cache_diff_v", "cache_swa_k", "cache_swa_v",
         "state_lru"))
    seq_of_tile, rope_of_tile, s32, s64, cosc, sinc, cosd, sind, dq_cols = (consts[k] for k in (
        "seq_of_tile", "rope_of_tile", "s32", "s64", "cosc", "sinc", "cosd", "sind", "dq_cols"))
    if True:
        lam_init = 0.8 - 0.6 * math.exp(-0.3 * l)
        w_in_l = w_in[l]
        w_in_l = jnp.concatenate([w_in_l[:, :OFF_DQ], w_in_l[:, OFF_DQ:OFF_DK][:, dq_cols], w_in_l[:, OFF_DK:]], axis=1)
        w_in_l = w_in_l.astype(BF16)
        gqc = jnp.tile(diff_qnorm_g[l].reshape(1, 2 * DIFF_QK_DIM), (1, N_GROUP_HEADS))
        gkc = jnp.tile(diff_knorm_g[l].reshape(1, 2 * DIFF_QK_DIM), (1, N_GROUP_HEADS))
        gqd = jnp.tile(swa_qnorm_g[l].reshape(1, HEAD_DIM), (1, 4))
        gkd = jnp.tile(swa_knorm_g[l].reshape(1, HEAD_DIM), (1, SWA_KV_HEADS))
        g_sub = jnp.tile(diff_subln_g[l].reshape(1, HEAD_DIM), (1, N_GROUP_HEADS))
        g_mlp = mlp_vnorm_g[l].reshape(1, w)
        mlp_bias = jnp.repeat(mlp_bs[l].T, HEAD_DIM, axis=1)
        wcat = jnp.concatenate([_block_diag(lru_wa[l, 0]), _block_diag(lru_wx[l, 0]),
                                _block_diag(lru_wa[l, 1]), _block_diag(lru_wx[l, 1])], axis=1).astype(BF16)
        bcat = jnp.concatenate([lru_ba[l, 0].reshape(1, w), lru_bx[l, 0].reshape(1, w),
                                lru_ba[l, 1].reshape(1, w), lru_bx[l, 1].reshape(1, w)], axis=1)
        sink = swa_sink[l].astype(F32)

        mod3 = _mod_call(cond, mod_w[l], mod_b[l]).reshape(N_COND, 1, 6 * D_MODEL)
        proj = _inproj_call(seq_of_tile, rope_of_tile, x, mod3, norm1_g[l].reshape(1, D_MODEL), w_in_l, s32, s64,
                            gqc, gkc, gqd, gkd, cosc, sinc, cosd, sind)

        ya = _gmlp_call(proj, s64, g_mlp, mlp_ws[l].astype(BF16), mlp_bias)

        lru_args = (lru_conv_w[l], lru_conv_b[l].reshape(1, w), wcat, bcat, lru_lambda[l])
        yb_p, st_p = _lru_call(proj, jnp.zeros((BATCH, 2, w), F32), *lru_args, seq_len=SEQ, n_seq=BATCH, row_block0=0)
        yb_s, _ = _lru_call(proj, state_lru[:, l], *lru_args, seq_len=DEC_SEQ, n_seq=DEC_BATCH,
                            row_block0=N_PROMPT // DEC_SEQ)

        yc_p = _diff_prompt_call(proj, diff_lambda[l], s64, g_sub, lam_init)
        yc_s = _diff_sample_call(proj, cache_diff_k[:, l].reshape(DEC_BATCH, PAST_LEN, w),
                                 cache_diff_v[:, l].reshape(DEC_BATCH, PAST_LEN, w), diff_lambda[l], s64, g_sub, lam_init)

        yd_p = _swa_prompt_call(sink, proj)
        yd_s = _swa_sample_call(sink, proj, cache_swa_k[:, l].reshape(DEC_BATCH, PAST_LEN, 128),
                                cache_swa_v[:, l].reshape(DEC_BATCH, PAST_LEN, 128))

        yb = jnp.concatenate([yb_p, yb_s], axis=0)
        yc = jnp.concatenate([yc_p, yc_s], axis=0)
        yd = jnp.concatenate([yd_p, yd_s], axis=0)
    return dict(mod3=mod3, proj=proj, ya=ya, yb=yb, yc=yc, yd=yd, st_p=st_p)


def _ffn_stage(x, st, params, l, consts):
    w = GROUP_WIDTH
    w_out, norm2_g, router_w, router_b, moe_w_gu, moe_b_gu, moe_w_down, moe_b_down = (params[k] for k in (
        "w_out", "norm2_g", "router_w", "router_b", "moe_w_gu", "moe_b_gu", "moe_w_down", "moe_b_down"))
    w_out_l = w_out[l]
    w_out_l = jnp.concatenate([w_out_l[:3 * w], w_out_l[3 * w:][consts["dq_cols"]]], axis=0).astype(BF16)
    x1, h2p, top_i, top_w, rank, counts = _outproj_call(
        consts["seq_of_tile"], st["ya"], st["yb"], st["yc"], st["yd"], x, st["mod3"], w_out_l,
        norm2_g[l].reshape(1, D_MODEL), router_w[l], router_b[l].reshape(1, N_EXPERTS), consts["tri"])
    routing = _moe_routing(top_i[:, :TOP_K], top_w[:, :TOP_K], rank[:, :TOP_K], counts)
    moe = _moe_call(l, *routing, h2p.reshape(N_TOK * (D_MODEL // 2 // 128), 128), params["moe_w_gu_bf16"],
                    moe_b_gu.reshape(DEPTH, N_EXPERTS, 1, 2 * D_FF), params["moe_w_down_bf16"],
                    moe_b_down.reshape(DEPTH, N_EXPERTS, 1, D_MODEL))
    return _residual_call(consts["seq_of_tile"], x1, moe.reshape(N_TOK, D_MODEL), st["mod3"])
```

```python
import functools
import math

import jax
import jax.numpy as jnp
import numpy as np
from jax import lax
from jax.experimental import pallas as pl
from jax.experimental.pallas import tpu as pltpu

F32 = jnp.float32
BF16 = jnp.bfloat16

D_MODEL = 1024
BATCH = 16
SEQ = 256
DEPTH = 2
DEC_BATCH = 4
DEC_SEQ = 2048
PAST_LEN = 512
GRID_W = 64
HEAD_DIM = 64
GROUP_WIDTH = 256
N_GROUP_HEADS = 4
CHUNK = 128
LRU_C = 8.0
CONV_W = 4
DIFF_QK_DIM = 32
SWA_KV_HEADS = 2
SWA_GROUPS = 2
WINDOW = 128
N_EXPERTS = 32
TOP_K = 4
D_FF = 1024
SWIGLU_LIMIT = 7.0
SWIGLU_ALPHA = 1.702
ROPE_BASE = 10000.0
EPS = 1e-6

N_PROMPT = BATCH * SEQ
N_SAMPLE = DEC_BATCH * DEC_SEQ
N_TOK = N_PROMPT + N_SAMPLE
N_COND = 8
TM = 256
N_TILES = N_TOK // TM
PROMPT_TILES = N_PROMPT // TM
SAMPLE_TILES_PER_SEQ = DEC_SEQ // TM
IN_WIDTH = 2304
OFF_AU, OFF_AV, OFF_BX, OFF_BG, OFF_CQ, OFF_CK, OFF_CV, OFF_DQ, OFF_DK, OFF_DV = (
    0, 256, 512, 768, 1024, 1280, 1536, 1792, 2048, 2176)
TOPK_PAD = 8
ROW_PIECES = D_MODEL // 128
MOE_TILE = 256
MOE_TB = 4096
MOE_MAX_TILES = (N_TOK // MOE_TB) * (MOE_TB * TOP_K // MOE_TILE + N_EXPERTS)
MOE_XT_STRIDE = MOE_TILE + 8
MOE_RMW_BATCH = 4
MOE_ROWS_PER_BLOCK = MOE_TB * TOP_K + N_EXPERTS * MOE_TILE
MOE_PAD_PAIR = MOE_TB * TOP_K
MOE_PAIR_SHIFT = TOP_K.bit_length() - 1
assert 1 << MOE_PAIR_SHIFT == TOP_K
LRU_CHUNK = 256
VMEM_LIMIT = 56 * 1024 * 1024


def _cparams(sem):
    return pltpu.CompilerParams(dimension_semantics=sem, vmem_limit_bytes=VMEM_LIMIT)


def _dot(a, b):
    return jnp.dot(a, b, preferred_element_type=F32)


def _dot_nt(a, b):
    return lax.dot_general(a, b, (((1,), (1,)), ((), ())), preferred_element_type=F32)


def _split_bf16(x):
    hi = x.astype(BF16)
    lo = (x - hi.astype(F32)).astype(BF16)
    return hi, lo


def _seg_rms_norm(x, seg_mat, g):
    hi, lo = _split_bf16(x * x)
    ms = _dot(hi, seg_mat) + _dot(lo, seg_mat)
    return x * lax.rsqrt(ms + EPS) * g


def _rope(x, cos_t, sin_t, nf):
    n = x.shape[-1]
    lane = lax.broadcasted_iota(jnp.int32, x.shape, 1)
    first = (lane & (2 * nf - 1)) < nf
    partner = jnp.where(first, pltpu.roll(x, n - nf, axis=1), pltpu.roll(x, nf, axis=1))
    return x * cos_t + partner * sin_t


def _softplus(x):
    return jnp.maximum(x, 0.0) + jnp.log1p(jnp.exp(-jnp.abs(x)))


def _mod_kernel(cond_ref, w_ref, b_ref, o_ref):
    c = cond_ref[...]
    s = c * jax.nn.sigmoid(c)
    o_ref[...] = _dot(s.astype(BF16), w_ref[...].astype(BF16)) + b_ref[...]


def _mod_call(cond, w, b):
    nb = 6
    return pl.pallas_call(
        _mod_kernel,
        grid=(nb,),
        in_specs=[pl.BlockSpec((N_COND, D_MODEL), lambda j: (0, 0)),
                  pl.BlockSpec((D_MODEL, D_MODEL), lambda j: (0, j)),
                  pl.BlockSpec((1, D_MODEL), lambda j: (0, j))],
        out_specs=pl.BlockSpec((N_COND, D_MODEL), lambda j: (0, j)),
        out_shape=jax.ShapeDtypeStruct((N_COND, 6 * D_MODEL), F32),
        compiler_params=_cparams(("arbitrary",)),
        name="mod",
    )(cond, w, b.reshape(1, 6 * D_MODEL))


def _inproj_kernel(seq_ref, rope_ref, x_ref, mod_ref, g1_ref, w_ref, s32_ref, s64_ref,
                   gqc_ref, gkc_ref, gqd_ref, gkd_ref, cosc_ref, sinc_ref, cosd_ref, sind_ref, o_ref):
    del seq_ref, rope_ref
    x = x_ref[...]
    xn = x * lax.rsqrt(jnp.mean(x * x, axis=-1, keepdims=True) + EPS) * g1_ref[...]
    mod = mod_ref[0]
    sh1 = mod[:, 0:D_MODEL]
    sc1 = mod[:, D_MODEL:2 * D_MODEL]
    h = xn * (1.0 + sc1) + sh1
    p = _dot(h.astype(BF16), w_ref[...])
    o_ref[:, OFF_AU:OFF_BX] = jax.nn.gelu(p[:, OFF_AU:OFF_BX])
    o_ref[:, OFF_BX:OFF_BG] = p[:, OFF_BX:OFF_BG]
    o_ref[:, OFF_BG:OFF_CQ] = jax.nn.gelu(p[:, OFF_BG:OFF_CQ])
    s32 = s32_ref[...]
    cosc = cosc_ref[...]
    sinc = sinc_ref[...]
    cq = _seg_rms_norm(p[:, OFF_CQ:OFF_CK], s32, gqc_ref[...])
    ck = _seg_rms_norm(p[:, OFF_CK:OFF_CV], s32, gkc_ref[...])
    o_ref[:, OFF_CQ:OFF_CK] = _rope(cq, cosc, sinc, DIFF_QK_DIM // 4)
    o_ref[:, OFF_CK:OFF_CV] = _rope(ck, cosc, sinc, DIFF_QK_DIM // 4)
    o_ref[:, OFF_CV:OFF_DQ] = p[:, OFF_CV:OFF_DQ]
    s64 = s64_ref[...]
    cosd = cosd_ref[...]
    sind = sind_ref[...]
    dq = _seg_rms_norm(p[:, OFF_DQ:OFF_DK], s64, gqd_ref[...])
    dk = _seg_rms_norm(p[:, OFF_DK:OFF_DV], s64[0:128, 0:128], gkd_ref[...])
    o_ref[:, OFF_DQ:OFF_DK] = _rope(dq, cosd, sind, HEAD_DIM // 4)
    o_ref[:, OFF_DK:OFF_DV] = _rope(dk, cosd[:, 0:128], sind[:, 0:128], HEAD_DIM // 4)
    o_ref[:, OFF_DV:IN_WIDTH] = p[:, OFF_DV:IN_WIDTH]


def _inproj_call(seq_of_tile, rope_of_tile, x, mod3, g1, w_in, s32, s64, gqc, gkc, gqd, gkd, cosc, sinc, cosd, sind):
    full = lambda shape: pl.BlockSpec(shape, lambda i, s, r: (0,) * len(shape))
    rope_spec = pl.BlockSpec((TM, GROUP_WIDTH), lambda i, s, r: (r[i], 0))
    grid_spec = pltpu.PrefetchScalarGridSpec(
        num_scalar_prefetch=2,
        grid=(N_TILES,),
        in_specs=[pl.BlockSpec((TM, D_MODEL), lambda i, s, r: (i, 0)),
                  pl.BlockSpec((1, 1, 6 * D_MODEL), lambda i, s, r: (s[i], 0, 0)),
                  full((1, D_MODEL)),
                  full((D_MODEL, IN_WIDTH)),
                  full((GROUP_WIDTH, GROUP_WIDTH)),
                  full((GROUP_WIDTH, GROUP_WIDTH)),
                  full((1, GROUP_WIDTH)), full((1, GROUP_WIDTH)), full((1, GROUP_WIDTH)), full((1, 128)),
                  rope_spec, rope_spec, rope_spec, rope_spec],
        out_specs=pl.BlockSpec((TM, IN_WIDTH), lambda i, s, r: (i, 0)),
    )
    return pl.pallas_call(
        _inproj_kernel,
        grid_spec=grid_spec,
        out_shape=jax.ShapeDtypeStruct((N_TOK, IN_WIDTH), F32),
        compiler_params=_cparams(("arbitrary",)),
        name="inproj",
    )(seq_of_tile, rope_of_tile, x, mod3, g1, w_in, s32, s64, gqc, gkc, gqd, gkd, cosc, sinc, cosd, sind)


def _gmlp_kernel(u_ref, v_ref, s64_ref, g_ref, ws_ref, bias_ref, o_ref):
    vh = _seg_rms_norm(v_ref[...], s64_ref[...], g_ref[...]).astype(BF16)
    head = lax.broadcasted_iota(jnp.int32, (CHUNK, GROUP_WIDTH), 1) // HEAD_DIM
    for ch in range(TM // CHUNK):
        rows = slice(ch * CHUNK, (ch + 1) * CHUNK)
        vc = vh[rows]
        mixed = bias_ref[...]
        for h in range(N_GROUP_HEADS):
            mixed = mixed + jnp.where(head == h, _dot(ws_ref[h], vc), 0.0)
        o_ref[rows, :] = u_ref[rows, :] * mixed


def _gmlp_call(proj, s64, g, ws, bias):
    return pl.pallas_call(
        _gmlp_kernel,
        grid=(N_TILES,),
        in_specs=[pl.BlockSpec((TM, GROUP_WIDTH), lambda i: (i, OFF_AU // GROUP_WIDTH)),
                  pl.BlockSpec((TM, GROUP_WIDTH), lambda i: (i, OFF_AV // GROUP_WIDTH)),
                  pl.BlockSpec((GROUP_WIDTH, GROUP_WIDTH), lambda i: (0, 0)),
                  pl.BlockSpec((1, GROUP_WIDTH), lambda i: (0, 0)),
                  pl.BlockSpec((N_GROUP_HEADS, CHUNK, CHUNK), lambda i: (0, 0, 0)),
                  pl.BlockSpec((CHUNK, GROUP_WIDTH), lambda i: (0, 0))],
        out_specs=pl.BlockSpec((TM, GROUP_WIDTH), lambda i: (i, 0)),
        out_shape=jax.ShapeDtypeStruct((N_TOK, GROUP_WIDTH), F32),
        compiler_params=_cparams(("arbitrary",)),
        name="gmlp",
    )(proj, proj, s64, g, ws, bias)


def _scan_chunk(a, b, reverse):
    n = a.shape[0]
    row = lax.broadcasted_iota(jnp.int32, a.shape, 0)
    s = 1
    while s < n:
        if reverse:
            keep = row < n - s
            shift = n - s
        else:
            keep = row >= s
            shift = s
        a_prev = jnp.where(keep, pltpu.roll(a, shift, axis=0), 1.0)
        b_prev = jnp.where(keep, pltpu.roll(b, shift, axis=0), 0.0)
        b = a * b_prev + b
        a = a * a_prev
        s *= 2
    return a, b


def _lru_kernel(x_ref, g_ref, h0_ref, cw_ref, cb_ref, wcat_ref, bcat_ref, lam_ref, y_ref, st_ref,
                xpad, a_f, b_f, a_b, b_b, h_f, *, seq_len):
    nc = seq_len // LRU_CHUNK
    w = GROUP_WIDTH
    zeros8 = jnp.zeros((8, w), F32)
    xpad[0:8, :] = zeros8
    xpad[seq_len + 8:seq_len + 16, :] = zeros8
    xpad[8:seq_len + 8, :] = x_ref[...]
    sp = _softplus(-lam_ref[...])
    cw = cw_ref[...]
    cb = cb_ref[...]
    win_rows = LRU_CHUNK + 16

    def gates(c, carry):
        r0 = pl.multiple_of(c * LRU_CHUNK, LRU_CHUNK)
        win = xpad[pl.ds(r0, win_rows), :]
        inner = slice(8, 8 + LRU_CHUNK)
        xc = cb + pltpu.roll(win, 2, axis=0)[inner] * cw[0:1]
        xc = xc + pltpu.roll(win, 1, axis=0)[inner] * cw[1:2]
        xc = xc + win[inner] * cw[2:3]
        xc = xc + pltpu.roll(win, win_rows - 1, axis=0)[inner] * cw[3:4]
        sg = jax.nn.sigmoid(_dot(xc.astype(BF16), wcat_ref[...]) + bcat_ref[...])
        for d, (a_ref, b_ref) in enumerate(((a_f, b_f), (a_b, b_b))):
            r = sg[:, (2 * d) * w:(2 * d + 1) * w]
            i = sg[:, (2 * d + 1) * w:(2 * d + 2) * w]
            log_a = (-LRU_C * r) * sp[d:d + 1]
            a = jnp.exp(log_a)
            a_ref[pl.ds(r0, LRU_CHUNK), :] = a
            b_ref[pl.ds(r0, LRU_CHUNK), :] = jnp.sqrt(-jnp.tanh(log_a) * (a * a + 1.0)) * (i * xc)
        return carry

    lax.fori_loop(0, nc, gates, 0)

    def fwd(c, carry):
        r0 = pl.multiple_of(c * LRU_CHUNK, LRU_CHUNK)
        a_cum, h_loc = _scan_chunk(a_f[pl.ds(r0, LRU_CHUNK), :], b_f[pl.ds(r0, LRU_CHUNK), :], False)
        h = h_loc + a_cum * carry
        h_f[pl.ds(r0, LRU_CHUNK), :] = h
        return h[LRU_CHUNK - 1:LRU_CHUNK, :]

    s_f = lax.fori_loop(0, nc, fwd, h0_ref[0, 0:1, :])

    def bwd(k, carry):
        c = nc - 1 - k
        r0 = pl.multiple_of(c * LRU_CHUNK, LRU_CHUNK)
        a_cum, h_loc = _scan_chunk(a_b[pl.ds(r0, LRU_CHUNK), :], b_b[pl.ds(r0, LRU_CHUNK), :], True)
        h = h_loc + a_cum * carry
        y_ref[pl.ds(r0, LRU_CHUNK), :] = (h_f[pl.ds(r0, LRU_CHUNK), :] + h) * g_ref[pl.ds(r0, LRU_CHUNK), :]
        return h[0:1, :]

    s_b = lax.fori_loop(0, nc, bwd, h0_ref[0, 1:2, :])
    st_ref[0, 0:1, :] = s_f
    st_ref[0, 1:2, :] = s_b


def _lru_call(proj, h0, cw, cb, wcat, bcat, lam, *, seq_len, n_seq, row_block0):
    w = GROUP_WIDTH
    full = lambda shape: pl.BlockSpec(shape, lambda b: (0,) * len(shape))
    seq_block = lambda col: pl.BlockSpec((seq_len, w), lambda b: (b + row_block0, col))
    return pl.pallas_call(
        functools.partial(_lru_kernel, seq_len=seq_len),
        grid=(n_seq,),
        in_specs=[seq_block(OFF_BX // w), seq_block(OFF_BG // w),
                  pl.BlockSpec((1, 2, w), lambda b: (b, 0, 0)),
                  full((CONV_W, w)), full((1, w)), full((w, 4 * w)), full((1, 4 * w)), full((2, w))],
        out_specs=[pl.BlockSpec((seq_len, w), lambda b: (b, 0)),
                   pl.BlockSpec((1, 2, w), lambda b: (b, 0, 0))],
        out_shape=[jax.ShapeDtypeStruct((n_seq * seq_len, w), F32),
                   jax.ShapeDtypeStruct((n_seq, 2, w), F32)],
        scratch_shapes=[pltpu.VMEM((seq_len + 16, w), F32)] + [pltpu.VMEM((seq_len, w), F32)] * 5,
        compiler_params=_cparams(("arbitrary",)),
        name="lru_%d" % seq_len,
    )(proj, proj, h0, cw, cb, wcat, bcat, lam)


def _diff_lambda(ld_ref, lam_init):
    ld = ld_ref[...]
    l1 = jnp.sum(ld[0:1] * ld[1:2], axis=-1, keepdims=True)
    l2 = jnp.sum(ld[2:3] * ld[3:4], axis=-1, keepdims=True)
    return jnp.exp(l1) - jnp.exp(l2) + lam_init


def _diff_attn_body(q, keys, values, lam, s64, g, lam_init):
    tq = q.shape[0]
    lane = lax.broadcasted_iota(jnp.int32, (1, GROUP_WIDTH), 1)
    scale = DIFF_QK_DIM ** -0.5
    o = jnp.zeros((tq, GROUP_WIDTH), F32)
    for h in range(N_GROUP_HEADS):
        probs = []
        for i in range(2):
            seg = h * 2 + i
            qm = jnp.where(lane // DIFF_QK_DIM == seg, q, 0.0).astype(BF16)
            s = [_dot_nt(qm, k) * scale for k in keys]
            m = functools.reduce(jnp.maximum, [jnp.max(x, axis=-1, keepdims=True) for x in s])
            e = [jnp.exp(x - m) for x in s]
            den = functools.reduce(jnp.add, [jnp.sum(x, axis=-1, keepdims=True) for x in e])
            inv = 1.0 / den
            probs.append([x * inv for x in e])
        oh = None
        for p0, p1, v in zip(probs[0], probs[1], values):
            part = _dot((p0 - lam * p1).astype(BF16), v)
            oh = part if oh is None else oh + part
        o = o + jnp.where(lane // HEAD_DIM == h, oh, 0.0)
    return _seg_rms_norm(o, s64, g) * (1.0 - lam_init)


def _diff_prompt_kernel(q_ref, k_ref, v_ref, ld_ref, s64_ref, g_ref, o_ref, *, lam_init):
    lam = _diff_lambda(ld_ref, lam_init)
    o_ref[...] = _diff_attn_body(q_ref[...], [k_ref[...].astype(BF16)], [v_ref[...].astype(BF16)],
                                 lam, s64_ref[...], g_ref[...], lam_init)


def _diff_sample_kernel(q_ref, k_ref, v_ref, ck_ref, cv_ref, ld_ref, s64_ref, g_ref, o_ref, *, lam_init):
    lam = _diff_lambda(ld_ref, lam_init)
    keys = [ck_ref[0].astype(BF16), k_ref[...].astype(BF16)]
    values = [cv_ref[0].astype(BF16), v_ref[...].astype(BF16)]
    o_ref[...] = _diff_attn_body(q_ref[...], keys, values, lam, s64_ref[...], g_ref[...], lam_init)


def _diff_prompt_call(proj, ld, s64, g, lam_init):
    w = GROUP_WIDTH
    full = lambda shape: pl.BlockSpec(shape, lambda b: (0,) * len(shape))
    return pl.pallas_call(
        functools.partial(_diff_prompt_kernel, lam_init=lam_init),
        grid=(BATCH,),
        in_specs=[pl.BlockSpec((SEQ, w), lambda b: (b, OFF_CQ // w)),
                  pl.BlockSpec((SEQ, w), lambda b: (b, OFF_CK // w)),
                  pl.BlockSpec((SEQ, w), lambda b: (b, OFF_CV // w)),
                  full((4, DIFF_QK_DIM)), full((w, w)), full((1, w))],
        out_specs=pl.BlockSpec((SEQ, w), lambda b: (b, 0)),
        out_shape=jax.ShapeDtypeStruct((N_PROMPT, w), F32),
        compiler_params=_cparams(("arbitrary",)),
        name="diff_prompt",
    )(proj, proj, proj, ld, s64, g)


def _diff_sample_call(proj, ctx_k, ctx_v, ld, s64, g, lam_init):
    w = GROUP_WIDTH
    tq = 256
    nq = DEC_SEQ // tq
    full = lambda shape: pl.BlockSpec(shape, lambda b, i: (0,) * len(shape))
    seq_block0 = N_PROMPT // DEC_SEQ
    return pl.pallas_call(
        functools.partial(_diff_sample_kernel, lam_init=lam_init),
        grid=(DEC_BATCH, nq),
        in_specs=[pl.BlockSpec((tq, w), lambda b, i: (N_PROMPT // tq + b * nq + i, OFF_CQ // w)),
                  pl.BlockSpec((DEC_SEQ, w), lambda b, i: (seq_block0 + b, OFF_CK // w)),
                  pl.BlockSpec((DEC_SEQ, w), lambda b, i: (seq_block0 + b, OFF_CV // w)),
                  pl.BlockSpec((1, PAST_LEN, w), lambda b, i: (b, 0, 0)),
                  pl.BlockSpec((1, PAST_LEN, w), lambda b, i: (b, 0, 0)),
                  full((4, DIFF_QK_DIM)), full((w, w)), full((1, w))],
        out_specs=pl.BlockSpec((tq, w), lambda b, i: (b * nq + i, 0)),
        out_shape=jax.ShapeDtypeStruct((N_SAMPLE, w), F32),
        compiler_params=_cparams(("arbitrary", "arbitrary")),
        name="diff_sample",
    )(proj, proj, proj, ctx_k, ctx_v, ld, s64, g)


def _sink_attn_body(q, keys, values, masks, sink_ref):
    tq = q.shape[0]
    lane = lax.broadcasted_iota(jnp.int32, (1, 128), 1)
    scale = HEAD_DIM ** -0.5
    outs = []
    for grp in range(SWA_GROUPS):
        qg = q[:, grp * 128:(grp + 1) * 128]
        og = jnp.zeros((tq, 128), F32)
        for kh in range(SWA_KV_HEADS):
            sink = sink_ref[kh * SWA_GROUPS + grp]
            qm = jnp.where(lane // HEAD_DIM == kh, qg, 0.0).astype(BF16)
            s = []
            for k, msk in zip(keys, masks):
                x = _dot_nt(qm, k) * scale
                s.append(x if msk is None else jnp.where(msk, x, -jnp.inf))
            m = functools.reduce(jnp.maximum, [jnp.max(x, axis=-1, keepdims=True) for x in s])
            m = jnp.maximum(m, sink)
            e = [jnp.exp(x - m) for x in s]
            den = functools.reduce(jnp.add, [jnp.sum(x, axis=-1, keepdims=True) for x in e]) + jnp.exp(sink - m)
            inv = 1.0 / den
            oh = None
            for x, v in zip(e, values):
                part = _dot((x * inv).astype(BF16), v)
                oh = part if oh is None else oh + part
            og = og + jnp.where(lane // HEAD_DIM == kh, oh, 0.0)
        outs.append(og)
    return outs


def _swa_prompt_kernel(sink_ref, q_ref, k_ref, v_ref, o_ref):
    outs = _sink_attn_body(q_ref[...], [k_ref[...].astype(BF16)], [v_ref[...].astype(BF16)], [None], sink_ref)
    for grp in range(SWA_GROUPS):
        o_ref[:, grp * 128:(grp + 1) * 128] = outs[grp]


def _swa_sample_kernel(sink_ref, q_ref, kp_ref, kc_ref, kn_ref, vp_ref, vc_ref, vn_ref, ck_ref, cv_ref, o_ref):
    n = pl.program_id(1)
    nb = pl.num_programs(1)
    r = lax.broadcasted_iota(jnp.int32, (WINDOW, WINDOW), 0)
    c = lax.broadcasted_iota(jnp.int32, (WINDOW, WINDOW), 1)
    mask_prev = c >= r + jnp.where(n > 0, 0, WINDOW)
    mask_next = c <= r - jnp.where(n < nb - 1, 0, WINDOW)
    keys = [ck_ref[0].astype(BF16), kp_ref[...].astype(BF16), kc_ref[...].astype(BF16), kn_ref[...].astype(BF16)]
    values = [cv_ref[0].astype(BF16), vp_ref[...].astype(BF16), vc_ref[...].astype(BF16), vn_ref[...].astype(BF16)]
    outs = _sink_attn_body(q_ref[...], keys, values, [None, mask_prev, None, mask_next], sink_ref)
    for grp in range(SWA_GROUPS):
        o_ref[:, grp * 128:(grp + 1) * 128] = outs[grp]


def _swa_prompt_call(sink, proj):
    w = GROUP_WIDTH
    return pl.pallas_call(
        _swa_prompt_kernel,
        grid=(BATCH,),
        in_specs=[pl.BlockSpec(memory_space=pltpu.SMEM),
                  pl.BlockSpec((SEQ, w), lambda b: (b, OFF_DQ // w)),
                  pl.BlockSpec((SEQ, 128), lambda b: (b, OFF_DK // 128)),
                  pl.BlockSpec((SEQ, 128), lambda b: (b, OFF_DV // 128))],
        out_specs=pl.BlockSpec((SEQ, w), lambda b: (b, 0)),
        out_shape=jax.ShapeDtypeStruct((N_PROMPT, w), F32),
        compiler_params=_cparams(("arbitrary",)),
        name="swa_prompt",
    )(sink, proj, proj, proj)


def _swa_sample_call(sink, proj, ctx_k, ctx_v):
    w = GROUP_WIDTH
    tq = WINDOW
    nq = DEC_SEQ // tq
    row0 = N_PROMPT // tq

    def kv_spec(col, delta):
        def index(b, i):
            j = jnp.clip(i + delta, 0, nq - 1)
            return (row0 + b * nq + j, col)
        return pl.BlockSpec((tq, 128), index)

    ctx_spec = pl.BlockSpec((1, PAST_LEN, 128), lambda b, i: (b, 0, 0))
    return pl.pallas_call(
        _swa_sample_kernel,
        grid=(DEC_BATCH, nq),
        in_specs=[pl.BlockSpec(memory_space=pltpu.SMEM),
                  pl.BlockSpec((tq, w), lambda b, i: (row0 + b * nq + i, OFF_DQ // w)),
                  kv_spec(OFF_DK // 128, -1), kv_spec(OFF_DK // 128, 0), kv_spec(OFF_DK // 128, 1),
                  kv_spec(OFF_DV // 128, -1), kv_spec(OFF_DV // 128, 0), kv_spec(OFF_DV // 128, 1),
                  ctx_spec, ctx_spec],
        out_specs=pl.BlockSpec((tq, w), lambda b, i: (b * nq + i, 0)),
        out_shape=jax.ShapeDtypeStruct((N_SAMPLE, w), F32),
        compiler_params=_cparams(("arbitrary", "arbitrary")),
        name="swa_sample",
    )(sink, proj, proj, proj, proj, proj, proj, proj, ctx_k, ctx_v)


def _outproj_kernel(seq_ref, ya_ref, yb_ref, yc_ref, yd_ref, x_ref, mod_ref, w_ref, g2_ref, rw_ref, rb_ref, tri_ref,
                    x1_ref, h2r_ref, topi_ref, topw_ref, rank_ref, count_ref, cnt_ref):
    del seq_ref
    w = GROUP_WIDTH
    mix = _dot(ya_ref[...].astype(BF16), w_ref[0:w, :])
    mix = mix + _dot(yb_ref[...].astype(BF16), w_ref[w:2 * w, :])
    mix = mix + _dot(yc_ref[...].astype(BF16), w_ref[2 * w:3 * w, :])
    mix = mix + _dot(yd_ref[...].astype(BF16), w_ref[3 * w:4 * w, :])
    mod = mod_ref[0]
    g1 = mod[:, 2 * D_MODEL:3 * D_MODEL]
    sh2 = mod[:, 3 * D_MODEL:4 * D_MODEL]
    sc2 = mod[:, 4 * D_MODEL:5 * D_MODEL]
    x1 = x_ref[...] + g1 * mix
    x1_ref[...] = x1
    xn = x1 * lax.rsqrt(jnp.mean(x1 * x1, axis=-1, keepdims=True) + EPS) * g2_ref[...]
    h2 = xn * (1.0 + sc2) + sh2
    for j in range(ROW_PIECES):
        h2r_ref[pl.ds(j, TM, stride=ROW_PIECES), :] = h2[:, j * 128:(j + 1) * 128]
    h_hi, h_lo = _split_bf16(h2)
    r_hi, r_lo = _split_bf16(rw_ref[...])
    logits = _dot(h_hi, r_hi) + (_dot(h_hi, r_lo) + _dot(h_lo, r_hi)) + rb_ref[...]
    lane = lax.broadcasted_iota(jnp.int32, logits.shape, 1)
    work = logits
    picks = []
    for _ in range(TOP_K):
        m = jnp.max(work, axis=-1, keepdims=True)
        idx = jnp.min(jnp.where(work == m, lane, N_EXPERTS), axis=-1, keepdims=True)
        picks.append((m, idx))
        work = jnp.where(lane == idx, -jnp.inf, work)
    top = picks[0][0]
    ex = [jnp.exp(m - top) for m, _ in picks]
    inv = 1.0 / functools.reduce(jnp.add, ex)
    @pl.when(pl.program_id(0) % (MOE_TB // TM) == 0)
    def _():
        cnt_ref[...] = jnp.zeros_like(cnt_ref)

    sel = jnp.zeros(logits.shape, F32)
    for _, idx in picks:
        sel = sel + jnp.where(lane == idx, 1.0, 0.0)
    csum = _dot(tri_ref[...], sel.astype(BF16))
    before = cnt_ref[...] + csum - sel
    cnt_ref[...] += csum[TM - 1:TM, :]
    count_ref[0] = cnt_ref[...].astype(jnp.int32)
    slot = lax.broadcasted_iota(jnp.int32, topi_ref.shape, 1)
    top_i = jnp.zeros(topi_ref.shape, jnp.int32)
    top_w = jnp.zeros(topw_ref.shape, F32)
    rank = jnp.zeros(rank_ref.shape, jnp.int32)
    for k, (e, (_, idx)) in enumerate(zip(ex, picks)):
        rank_k = jnp.sum(jnp.where(lane == idx, before, 0.0), axis=-1, keepdims=True).astype(jnp.int32)
        top_i = jnp.where(slot == k, idx, top_i)
        top_w = jnp.where(slot == k, e * inv, top_w)
        rank = jnp.where(slot == k, rank_k, rank)
    topi_ref[...] = top_i
    topw_ref[...] = top_w
    rank_ref[...] = rank


def _outproj_call(seq_of_tile, ya, yb, yc, yd, x, mod3, w_out, g2, rw, rb, tri):
    w = GROUP_WIDTH
    full = lambda shape: pl.BlockSpec(shape, lambda i, s: (0,) * len(shape))
    tile = lambda width: pl.BlockSpec((TM, width), lambda i, s: (i, 0))
    tiles_per_block = MOE_TB // TM
    return pl.pallas_call(
        _outproj_kernel,
        grid_spec=pltpu.PrefetchScalarGridSpec(
            num_scalar_prefetch=1,
            grid=(N_TILES,),
            in_specs=[tile(w), tile(w), tile(w), tile(w), tile(D_MODEL),
                      pl.BlockSpec((1, 1, 6 * D_MODEL), lambda i, s: (s[i], 0, 0)),
                      full((D_MODEL, D_MODEL)), full((1, D_MODEL)),
                      full((D_MODEL, N_EXPERTS)), full((1, N_EXPERTS)), full((TM, TM))],
            out_specs=[tile(D_MODEL), pl.BlockSpec((TM * ROW_PIECES, 128), lambda i, s: (i, 0)),
                       tile(TOPK_PAD), tile(TOPK_PAD), tile(TOPK_PAD),
                       pl.BlockSpec((1, 1, N_EXPERTS), lambda i, s: (i // tiles_per_block, 0, 0))],
            scratch_shapes=[pltpu.VMEM((1, N_EXPERTS), F32)],
        ),
        out_shape=[jax.ShapeDtypeStruct((N_TOK, D_MODEL), F32),
                   jax.ShapeDtypeStruct((N_TOK * ROW_PIECES, 128), F32),
                   jax.ShapeDtypeStruct((N_TOK, TOPK_PAD), jnp.int32),
                   jax.ShapeDtypeStruct((N_TOK, TOPK_PAD), F32),
                   jax.ShapeDtypeStruct((N_TOK, TOPK_PAD), jnp.int32),
                   jax.ShapeDtypeStruct((N_TOK // MOE_TB, 1, N_EXPERTS), jnp.int32)],
        compiler_params=_cparams(("arbitrary",)),
        name="outproj",
    )(seq_of_tile, ya, yb, yc, yd, x, mod3, w_out, g2, rw, rb, tri)


def _moe_kernel(expert_ref, block_ref, flag_ref, local_ref, dest_ref, topw_ref, gstart_ref, count_ref,
                src_ref, wgu_ref, bgu_ref, wd_ref, bd_ref, acc_ref, pair_ref, xt_ref, y3_ref):
    del expert_ref, block_ref
    i = pl.program_id(0)
    flags = flag_ref[i]

    @pl.when((flags & 2) != 0)
    def _():
        def zero(c, carry):
            acc_ref[pl.ds(pl.multiple_of(c * MOE_TILE, MOE_TILE), MOE_TILE)] = jnp.zeros((MOE_TILE, 8, 128), F32)
            return carry
        lax.fori_loop(0, MOE_TB // MOE_TILE, zero, 0)

        def pad_group(e, carry):
            cnt = count_ref[0, 0, e]
            first_pad = gstart_ref[0, 0, e] + cnt
            n_pad = (-cnt) & (MOE_TILE - 1)

            def pad_row(r, c2):
                pair_ref[first_pad + r] = MOE_PAD_PAIR
                return c2
            lax.fori_loop(0, n_pad, pad_row, 0)
            return carry
        lax.fori_loop(0, N_EXPERTS, pad_group, 0)

        def place(t, carry):
            for k in range(TOP_K):
                pair = t * TOP_K + k
                pair_ref[dest_ref[0, 0, pair]] = pair
            return carry
        lax.fori_loop(0, MOE_TB, place, 0)

    @pl.when((flags & 1) != 0)
    def _():
        row0 = local_ref[i]

        def token_of(pair):
            return (pair >> MOE_PAIR_SHIFT) & (MOE_TB - 1)

        for m in range(MOE_TILE):
            t = token_of(pair_ref[row0 + m])
            slab = src_ref[pl.ds(pl.multiple_of(t * ROW_PIECES, ROW_PIECES), ROW_PIECES), :]
            xt_ref[pl.ds(m, ROW_PIECES, stride=MOE_XT_STRIDE), :] = slab
        x = jnp.concatenate([xt_ref[j * MOE_XT_STRIDE:j * MOE_XT_STRIDE + MOE_TILE, :] for j in range(ROW_PIECES)],
                            axis=1).astype(BF16)
        gu = _dot(x, wgu_ref[0, 0]) + bgu_ref[0, 0]
        gate = jnp.minimum(gu[:, :D_FF], SWIGLU_LIMIT)
        up = jnp.clip(gu[:, D_FF:], -SWIGLU_LIMIT, SWIGLU_LIMIT)
        act = (up + 1.0) * gate * jax.nn.sigmoid(SWIGLU_ALPHA * gate)
        y = _dot(act.astype(BF16), wd_ref[0, 0]) + bd_ref[0, 0]
        for j in range(D_MODEL // 128):
            y3_ref[:, j * 8:(j + 1) * 8, :] = y[:, j * 128:(j + 1) * 128].reshape(MOE_TILE // 8, 8, 128)
        for m0 in range(0, MOE_TILE, MOE_RMW_BATCH):
            rows = range(m0, m0 + MOE_RMW_BATCH)
            pairs = [pair_ref[row0 + m] for m in rows]
            toks = [token_of(p) for p in pairs]
            vals = [acc_ref[t] + topw_ref[0, 0, p] * y3_ref[m // 8, pl.ds(m % 8, 8, stride=8), :]
                    for t, p, m in zip(toks, pairs, rows)]
            for t, v in reversed(list(zip(toks, vals))):
                acc_ref[t] = v


def _moe_call(layer, tile_expert, tile_block, tile_flags, tile_local, dest, topw, gstart, counts, h2_rows,
              wgu, bgu, wd, bd):
    pieces = ROW_PIECES
    once = pl.Buffered(1)
    n_pairs = MOE_TB * TOP_K

    def smem_block(width):
        return pl.BlockSpec((1, 1, width), lambda i, e, b, f, lo: (b[i], 0, 0), memory_space=pltpu.SMEM,
                            pipeline_mode=once)

    def expert_block(rows, cols):
        return pl.BlockSpec((1, 1, rows, cols), lambda i, e, b, f, lo: (layer, e[i], 0, 0))

    return pl.pallas_call(
        _moe_kernel,
        grid_spec=pltpu.PrefetchScalarGridSpec(
            num_scalar_prefetch=4,
            grid=(MOE_MAX_TILES,),
            in_specs=[smem_block(n_pairs), smem_block(n_pairs + 128), smem_block(N_EXPERTS), smem_block(N_EXPERTS),
                      pl.BlockSpec((MOE_TB * pieces, 128), lambda i, e, b, f, lo: (b[i], 0), pipeline_mode=once),
                      expert_block(D_MODEL, 2 * D_FF), expert_block(1, 2 * D_FF),
                      expert_block(D_FF, D_MODEL), expert_block(1, D_MODEL)],
            out_specs=pl.BlockSpec((MOE_TB, 8, 128), lambda i, e, b, f, lo: (b[i], 0, 0), pipeline_mode=once),
            scratch_shapes=[pltpu.SMEM((MOE_ROWS_PER_BLOCK,), jnp.int32),
                            pltpu.VMEM((pieces * MOE_XT_STRIDE, 128), F32),
                            pltpu.VMEM((MOE_TILE // 8, 64, 128), F32)],
        ),
        out_shape=jax.ShapeDtypeStruct((N_TOK, 8, 128), F32),
        compiler_params=_cparams(("arbitrary",)),
        name="moe",
    )(tile_expert, tile_block, tile_flags, tile_local, dest, topw, gstart, counts, h2_rows, wgu, bgu, wd, bd)


def _moe_routing(top_i, top_w, rank, counts):
    nb = N_TOK // MOE_TB
    n_groups = nb * N_EXPERTS
    counts = counts.reshape(nb, N_EXPERTS)
    padded = ((counts + MOE_TILE - 1) // MOE_TILE) * MOE_TILE
    group_end = jnp.cumsum(padded.reshape(-1))
    group_start = (group_end - padded.reshape(-1)).reshape(nb, N_EXPERTS)
    block_row0 = group_start[:, 0]
    gstart_local = group_start - block_row0[:, None]
    experts = jnp.arange(N_EXPERTS, dtype=jnp.int32)
    picked = top_i.reshape(nb, MOE_TB, TOP_K, 1) == experts
    dest = jnp.sum(jnp.where(picked, gstart_local[:, None, None, :], 0), axis=-1) + rank.reshape(nb, MOE_TB, TOP_K)
    dest = dest.reshape(nb, 1, MOE_TB * TOP_K).astype(jnp.int32)
    topw = jnp.concatenate([top_w.reshape(nb, 1, MOE_TB * TOP_K), jnp.zeros((nb, 1, 128), F32)], axis=-1)
    tile_start = jnp.arange(MOE_MAX_TILES, dtype=jnp.int32) * MOE_TILE
    tile_group = jnp.sum((group_end[None, :] <= tile_start[:, None]).astype(jnp.int32), axis=1)
    tile_group = jnp.minimum(tile_group, n_groups - 1)
    valid = tile_start < group_end[-1]
    tile_block = tile_group // N_EXPERTS
    first = jnp.concatenate([jnp.ones((1,), bool), tile_block[1:] != tile_block[:-1]])
    flags = valid.astype(jnp.int32) + 2 * first.astype(jnp.int32)
    blocks = jnp.arange(nb, dtype=jnp.int32)
    tile_row0 = jnp.sum(jnp.where(tile_block[:, None] == blocks[None, :], block_row0[None, :], 0), axis=1)
    tile_local = (tile_start - tile_row0).astype(jnp.int32)
    return (tile_group % N_EXPERTS, tile_block, flags, tile_local, dest, topw,
            gstart_local.reshape(nb, 1, N_EXPERTS).astype(jnp.int32), counts.reshape(nb, 1, N_EXPERTS))


def _residual_kernel(seq_ref, x1_ref, moe_ref, mod_ref, o_ref):
    del seq_ref
    g2 = mod_ref[0][:, 5 * D_MODEL:6 * D_MODEL]
    moe = jnp.concatenate([moe_ref[pl.ds(j, TM, stride=ROW_PIECES), :] for j in range(ROW_PIECES)], axis=1)
    o_ref[...] = x1_ref[...] + g2 * moe


def _residual_call(seq_of_tile, x1, moe_rows, mod3):
    tile = pl.BlockSpec((TM, D_MODEL), lambda i, s: (i, 0))
    return pl.pallas_call(
        _residual_kernel,
        grid_spec=pltpu.PrefetchScalarGridSpec(
            num_scalar_prefetch=1,
            grid=(N_TILES,),
            in_specs=[tile, pl.BlockSpec((TM * ROW_PIECES, 128), lambda i, s: (i, 0)),
                      pl.BlockSpec((1, 1, 6 * D_MODEL), lambda i, s: (s[i], 0, 0))],
            out_specs=tile,
        ),
        out_shape=jax.ShapeDtypeStruct((N_TOK, D_MODEL), F32),
        compiler_params=_cparams(("arbitrary",)),
        name="residual",
    )(seq_of_tile, x1, moe_rows, mod3)


def _segment_matrix(width, seg):
    idx = np.arange(width) // seg
    return jnp.asarray((idx[:, None] == idx[None, :]).astype(np.float32) / seg, BF16)


def _rope_tables(rot_dim, width):
    rows = DEC_SEQ // GRID_W
    nf = rot_dim // 4
    inv = 1.0 / (ROPE_BASE ** (jnp.arange(nf, dtype=F32) / nf))
    row = jnp.repeat(jnp.arange(rows, dtype=F32), GRID_W)
    col = jnp.tile(jnp.arange(GRID_W, dtype=F32), rows)
    ang = jnp.stack([row[:, None] * inv, col[:, None] * inv], axis=1)
    cos, sin = jnp.cos(ang), jnp.sin(ang)
    cos_r = jnp.concatenate([cos[:, 0], cos[:, 0], cos[:, 1], cos[:, 1]], axis=-1)
    sin_r = jnp.concatenate([-sin[:, 0], sin[:, 0], -sin[:, 1], sin[:, 1]], axis=-1)
    reps = width // rot_dim
    cos_t = jnp.concatenate([jnp.ones((TM, width), F32), jnp.tile(cos_r, (1, reps))], axis=0)
    sin_t = jnp.concatenate([jnp.zeros((TM, width), F32), jnp.tile(sin_r, (1, reps))], axis=0)
    return cos_t, sin_t


def _block_diag(wb):
    nb, n, _ = wb.shape
    out = jnp.zeros((nb * n, nb * n), wb.dtype)
    for i in range(nb):
        out = out.at[i * n:(i + 1) * n, i * n:(i + 1) * n].set(wb[i])
    return out


_DQ_HEADS = [kh * SWA_GROUPS + g for g in range(SWA_GROUPS) for kh in range(SWA_KV_HEADS)]


def kernel(x_prompt, x_sample, c, cache_diff_k, cache_diff_v, cache_swa_k, cache_swa_v, state_lru, c_ctx, mod_w, mod_b, norm1_g, norm2_g, w_in, w_out, mlp_vnorm_g, mlp_ws, mlp_bs, lru_conv_w, lru_conv_b, lru_wa, lru_ba, lru_wx, lru_bx, lru_lambda, diff_qnorm_g, diff_knorm_g, diff_lambda, diff_subln_g, swa_qnorm_g, swa_knorm_g, swa_sink, router_w, router_b, moe_w_gu, moe_b_gu, moe_w_down, moe_b_down):
    params = dict(locals())
    params["moe_w_gu_bf16"] = moe_w_gu.astype(BF16)
    params["moe_w_down_bf16"] = moe_w_down.astype(BF16)
    x = jnp.concatenate([x_prompt.reshape(N_PROMPT, D_MODEL), x_sample.reshape(N_SAMPLE, D_MODEL)], axis=0)
    cond = jnp.concatenate([c_ctx[None], c, jnp.zeros((N_COND - 1 - DEC_BATCH, D_MODEL), F32)], axis=0)
    consts = _constants()
    dk_l, dv_l, sk_l, sv_l, lru_l = [], [], [], [], []
    for l in range(DEPTH):
        st = _mixer_stage(x, cond, params, l, consts)
        x = _ffn_stage(x, st, params, l, consts)
        pp = st["proj"][:N_PROMPT]
        dk_l.append(pp[:, OFF_CK:OFF_CV].reshape(BATCH, SEQ, N_GROUP_HEADS, 2, DIFF_QK_DIM))
        dv_l.append(pp[:, OFF_CV:OFF_DQ].reshape(BATCH, SEQ, N_GROUP_HEADS, HEAD_DIM))
        sk_l.append(pp[:, OFF_DK:OFF_DV].reshape(BATCH, SEQ, SWA_KV_HEADS, HEAD_DIM))
        sv_l.append(pp[:, OFF_DV:IN_WIDTH].reshape(BATCH, SEQ, SWA_KV_HEADS, HEAD_DIM))
        lru_l.append(st["st_p"])

    y_p = x[:N_PROMPT].reshape(BATCH, SEQ, D_MODEL)
    y_s = x[N_PROMPT:].reshape(DEC_BATCH, DEC_SEQ, D_MODEL)
    return (y_p, y_s, jnp.stack(dk_l, axis=1), jnp.stack(dv_l, axis=1), jnp.stack(sk_l, axis=1),
            jnp.stack(sv_l, axis=1), jnp.stack(lru_l, axis=1))


def _constants():
    w = GROUP_WIDTH
    tiles = np.arange(N_TILES)
    seq_np = np.where(tiles < PROMPT_TILES, 0, 1 + (tiles - PROMPT_TILES) // SAMPLE_TILES_PER_SEQ)
    rope_np = np.where(tiles < PROMPT_TILES, 0, 1 + (tiles - PROMPT_TILES) % SAMPLE_TILES_PER_SEQ)
    seq_of_tile = jnp.asarray(seq_np, jnp.int32)
    rope_of_tile = jnp.asarray(rope_np, jnp.int32)

    s32 = _segment_matrix(w, DIFF_QK_DIM)
    s64 = _segment_matrix(w, HEAD_DIM)
    cosc, sinc = _rope_tables(DIFF_QK_DIM, w)
    cosd, sind = _rope_tables(HEAD_DIM, w)
    dq_cols = np.concatenate([np.arange(h * HEAD_DIM, (h + 1) * HEAD_DIM) for h in _DQ_HEADS])
    tri = jnp.asarray(np.tril(np.ones((TM, TM), np.float32)), BF16)
    return dict(seq_of_tile=seq_of_tile, rope_of_tile=rope_of_tile, tri=tri, s32=s32, s64=s64,
                cosc=cosc, sinc=sinc, cosd=cosd, sind=sind, dq_cols=dq_cols)


def _mixer_stage(x, cond, params, l, consts):
    w = GROUP_WIDTH
    (mod_w, mod_b, norm1_g, w_in, mlp_vnorm_g, mlp_ws, mlp_bs, lru_conv_w, lru_conv_b, lru_wa, lru_ba, lru_wx, lru_bx,
     lru_lambda, diff_qnorm_g, diff_knorm_g, diff_lambda, diff_subln_g, swa_qnorm_g, swa_knorm_g, swa_sink,
     cache_diff_k, cache_diff_v, cache_swa_k, cache_swa_v, state_lru) = (params[k] for k in (
         "mod_w", "mod_b", "norm1_g", "w_in", "mlp_vnorm_g", "mlp_ws", "mlp_bs", "lru_conv_w", "lru_conv_b", "lru_wa",
         "lru_ba", "lru_wx", "lru_bx", "lru_lambda", "diff_qnorm_g", "diff_knorm_g", "diff_lambda", "diff_subln_g",
         "swa_qnorm_g", "swa_knorm_g", "swa_sink", "cache_diff_k", "cache_diff_v", "cache_swa_k", "cache_swa_v",
         "state_lru"))
    seq_of_tile, rope_of_tile, s32, s64, cosc, sinc, cosd, sind, dq_cols = (consts[k] for k in (
        "seq_of_tile", "rope_of_tile", "s32", "s64", "cosc", "sinc", "cosd", "sind", "dq_cols"))
    if True:
        lam_init = 0.8 - 0.6 * math.exp(-0.3 * l)
        w_in_l = w_in[l]
        w_in_l = jnp.concatenate([w_in_l[:, :OFF_DQ], w_in_l[:, OFF_DQ:OFF_DK][:, dq_cols], w_in_l[:, OFF_DK:]], axis=1)
        w_in_l = w_in_l.astype(BF16)
        gqc = jnp.tile(diff_qnorm_g[l].reshape(1, 2 * DIFF_QK_DIM), (1, N_GROUP_HEADS))
        gkc = jnp.tile(diff_knorm_g[l].reshape(1, 2 * DIFF_QK_DIM), (1, N_GROUP_HEADS))
        gqd = jnp.tile(swa_qnorm_g[l].reshape(1, HEAD_DIM), (1, 4))
        gkd = jnp.tile(swa_knorm_g[l].reshape(1, HEAD_DIM), (1, SWA_KV_HEADS))
        g_sub = jnp.tile(diff_subln_g[l].reshape(1, HEAD_DIM), (1, N_GROUP_HEADS))
        g_mlp = mlp_vnorm_g[l].reshape(1, w)
        mlp_bias = jnp.repeat(mlp_bs[l].T, HEAD_DIM, axis=1)
        wcat = jnp.concatenate([_block_diag(lru_wa[l, 0]), _block_diag(lru_wx[l, 0]),
                                _block_diag(lru_wa[l, 1]), _block_diag(lru_wx[l, 1])], axis=1).astype(BF16)
        bcat = jnp.concatenate([lru_ba[l, 0].reshape(1, w), lru_bx[l, 0].reshape(1, w),
                                lru_ba[l, 1].reshape(1, w), lru_bx[l, 1].reshape(1, w)], axis=1)
        sink = swa_sink[l].astype(F32)

        mod3 = _mod_call(cond, mod_w[l], mod_b[l]).reshape(N_COND, 1, 6 * D_MODEL)
        proj = _inproj_call(seq_of_tile, rope_of_tile, x, mod3, norm1_g[l].reshape(1, D_MODEL), w_in_l, s32, s64,
                            gqc, gkc, gqd, gkd, cosc, sinc, cosd, sind)

        ya = _gmlp_call(proj, s64, g_mlp, mlp_ws[l].astype(BF16), mlp_bias)

        lru_args = (lru_conv_w[l], lru_conv_b[l].reshape(1, w), wcat, bcat, lru_lambda[l])
        yb_p, st_p = _lru_call(proj, jnp.zeros((BATCH, 2, w), F32), *lru_args, seq_len=SEQ, n_seq=BATCH, row_block0=0)
        yb_s, _ = _lru_call(proj, state_lru[:, l], *lru_args, seq_len=DEC_SEQ, n_seq=DEC_BATCH,
                            row_block0=N_PROMPT // DEC_SEQ)

        yc_p = _diff_prompt_call(proj, diff_lambda[l], s64, g_sub, lam_init)
        yc_s = _diff_sample_call(proj, cache_diff_k[:, l].reshape(DEC_BATCH, PAST_LEN, w),
                                 cache_diff_v[:, l].reshape(DEC_BATCH, PAST_LEN, w), diff_lambda[l], s64, g_sub, lam_init)

        yd_p = _swa_prompt_call(sink, proj)
        yd_s = _swa_sample_call(sink, proj, cache_swa_k[:, l].reshape(DEC_BATCH, PAST_LEN, 128),
                                cache_swa_v[:, l].reshape(DEC_BATCH, PAST_LEN, 128))

        yb = jnp.concatenate([yb_p, yb_s], axis=0)
        yc = jnp.concatenate([yc_p, yc_s], axis=0)
        yd = jnp.concatenate([yd_p, yd_s], axis=0)
    return dict(mod3=mod3, proj=proj, ya=ya, yb=yb, yc=yc, yd=yd, st_p=st_p)


def _ffn_stage(x, st, params, l, consts):
    w = GROUP_WIDTH
    w_out, norm2_g, router_w, router_b, moe_w_gu, moe_b_gu, moe_w_down, moe_b_down = (params[k] for k in (
        "w_out", "norm2_g", "router_w", "router_b", "moe_w_gu", "moe_b_gu", "moe_w_down", "moe_b_down"))
    w_out_l = w_out[l]
    w_out_l = jnp.concatenate([w_out_l[:3 * w], w_out_l[3 * w:][consts["dq_cols"]]], axis=0).astype(BF16)
    x1, h2_rows, top_i, top_w, rank, counts = _outproj_call(
        consts["seq_of_tile"], st["ya"], st["yb"], st["yc"], st["yd"], x, st["mod3"], w_out_l,
        norm2_g[l].reshape(1, D_MODEL), router_w[l], router_b[l].reshape(1, N_EXPERTS), consts["tri"])
    routing = _moe_routing(top_i[:, :TOP_K], top_w[:, :TOP_K], rank[:, :TOP_K], counts)
    moe = _moe_call(l, *routing, h2_rows, params["moe_w_gu_bf16"],
                    moe_b_gu.reshape(DEPTH, N_EXPERTS, 1, 2 * D_FF), params["moe_w_down_bf16"],
                    moe_b_down.reshape(DEPTH, N_EXPERTS, 1, D_MODEL))
    return _residual_call(consts["seq_of_tile"], x1, moe.reshape(N_TOK * ROW_PIECES, 128), st["mod3"])
```

```python
import functools
import math

import jax
import jax.numpy as jnp
import numpy as np
from jax import lax
from jax.experimental import pallas as pl
from jax.experimental.pallas import tpu as pltpu

F32 = jnp.float32
BF16 = jnp.bfloat16

D_MODEL = 1024
BATCH = 16
SEQ = 256
DEPTH = 2
DEC_BATCH = 4
DEC_SEQ = 2048
PAST_LEN = 512
GRID_W = 64
HEAD_DIM = 64
GROUP_WIDTH = 256
N_GROUP_HEADS = 4
CHUNK = 128
LRU_C = 8.0
CONV_W = 4
DIFF_QK_DIM = 32
SWA_KV_HEADS = 2
SWA_GROUPS = 2
WINDOW = 128
N_EXPERTS = 32
TOP_K = 4
D_FF = 1024
SWIGLU_LIMIT = 7.0
SWIGLU_ALPHA = 1.702
ROPE_BASE = 10000.0
EPS = 1e-6

N_PROMPT = BATCH * SEQ
N_SAMPLE = DEC_BATCH * DEC_SEQ
N_TOK = N_PROMPT + N_SAMPLE
N_COND = 8
TM = 256
N_TILES = N_TOK // TM
PROMPT_TILES = N_PROMPT // TM
SAMPLE_TILES_PER_SEQ = DEC_SEQ // TM
IN_WIDTH = 2304
OFF_AU, OFF_AV, OFF_BX, OFF_BG, OFF_CQ, OFF_CK, OFF_CV, OFF_DQ, OFF_DK, OFF_DV = (
    0, 256, 512, 768, 1024, 1280, 1536, 1792, 2048, 2176)
TOPK_PAD = 8
ROW_PIECES = D_MODEL // 128
MOE_TILE = 256
MOE_TB = 4096
MOE_MAX_TILES = (N_TOK // MOE_TB) * (MOE_TB * TOP_K // MOE_TILE + N_EXPERTS)
MOE_XT_STRIDE = MOE_TILE + 8
MOE_RMW_BATCH = 4
MOE_ROWS_PER_BLOCK = MOE_TB * TOP_K + N_EXPERTS * MOE_TILE
MOE_PAD_PAIR = MOE_TB * TOP_K
MOE_PAIR_SHIFT = TOP_K.bit_length() - 1
assert 1 << MOE_PAIR_SHIFT == TOP_K
MOE_KIND_SKIP, MOE_KIND_FIRST, MOE_KIND_MIDDLE, MOE_KIND_LAST = 0, 1, 2, 3
LRU_CHUNK = 256
VMEM_LIMIT = 56 * 1024 * 1024


def _cparams(sem):
    return pltpu.CompilerParams(dimension_semantics=sem, vmem_limit_bytes=VMEM_LIMIT)


def _dot(a, b):
    return jnp.dot(a, b, preferred_element_type=F32)


def _dot_nt(a, b):
    return lax.dot_general(a, b, (((1,), (1,)), ((), ())), preferred_element_type=F32)


def _split_bf16(x):
    hi = x.astype(BF16)
    lo = (x - hi.astype(F32)).astype(BF16)
    return hi, lo


def _seg_rms_norm(x, seg_mat, g):
    hi, lo = _split_bf16(x * x)
    ms = _dot(hi, seg_mat) + _dot(lo, seg_mat)
    return x * lax.rsqrt(ms + EPS) * g


def _rope(x, cos_t, sin_t, nf):
    n = x.shape[-1]
    lane = lax.broadcasted_iota(jnp.int32, x.shape, 1)
    first = (lane & (2 * nf - 1)) < nf
    partner = jnp.where(first, pltpu.roll(x, n - nf, axis=1), pltpu.roll(x, nf, axis=1))
    return x * cos_t + partner * sin_t


def _softplus(x):
    return jnp.maximum(x, 0.0) + jnp.log1p(jnp.exp(-jnp.abs(x)))


def _mod_kernel(cond_ref, w_ref, b_ref, o_ref):
    c = cond_ref[...]
    s = c * jax.nn.sigmoid(c)
    o_ref[...] = _dot(s.astype(BF16), w_ref[...].astype(BF16)) + b_ref[...]


def _mod_call(cond, w, b):
    nb = 6
    return pl.pallas_call(
        _mod_kernel,
        grid=(nb,),
        in_specs=[pl.BlockSpec((N_COND, D_MODEL), lambda j: (0, 0)),
                  pl.BlockSpec((D_MODEL, D_MODEL), lambda j: (0, j)),
                  pl.BlockSpec((1, D_MODEL), lambda j: (0, j))],
        out_specs=pl.BlockSpec((N_COND, D_MODEL), lambda j: (0, j)),
        out_shape=jax.ShapeDtypeStruct((N_COND, 6 * D_MODEL), F32),
        compiler_params=_cparams(("arbitrary",)),
        name="mod",
    )(cond, w, b.reshape(1, 6 * D_MODEL))


def _inproj_kernel(seq_ref, rope_ref, x_ref, mod_ref, g1_ref, w_ref, s32_ref, s64_ref,
                   gqc_ref, gkc_ref, gqd_ref, gkd_ref, cosc_ref, sinc_ref, cosd_ref, sind_ref, o_ref):
    del seq_ref, rope_ref
    x = x_ref[...]
    xn = x * lax.rsqrt(jnp.mean(x * x, axis=-1, keepdims=True) + EPS) * g1_ref[...]
    mod = mod_ref[0]
    sh1 = mod[:, 0:D_MODEL]
    sc1 = mod[:, D_MODEL:2 * D_MODEL]
    h = xn * (1.0 + sc1) + sh1
    p = _dot(h.astype(BF16), w_ref[...])
    o_ref[:, OFF_AU:OFF_BX] = jax.nn.gelu(p[:, OFF_AU:OFF_BX])
    o_ref[:, OFF_BX:OFF_BG] = p[:, OFF_BX:OFF_BG]
    o_ref[:, OFF_BG:OFF_CQ] = jax.nn.gelu(p[:, OFF_BG:OFF_CQ])
    s32 = s32_ref[...]
    cosc = cosc_ref[...]
    sinc = sinc_ref[...]
    cq = _seg_rms_norm(p[:, OFF_CQ:OFF_CK], s32, gqc_ref[...])
    ck = _seg_rms_norm(p[:, OFF_CK:OFF_CV], s32, gkc_ref[...])
    o_ref[:, OFF_CQ:OFF_CK] = _rope(cq, cosc, sinc, DIFF_QK_DIM // 4)
    o_ref[:, OFF_CK:OFF_CV] = _rope(ck, cosc, sinc, DIFF_QK_DIM // 4)
    o_ref[:, OFF_CV:OFF_DQ] = p[:, OFF_CV:OFF_DQ]
    s64 = s64_ref[...]
    cosd = cosd_ref[...]
    sind = sind_ref[...]
    dq = _seg_rms_norm(p[:, OFF_DQ:OFF_DK], s64, gqd_ref[...])
    dk = _seg_rms_norm(p[:, OFF_DK:OFF_DV], s64[0:128, 0:128], gkd_ref[...])
    o_ref[:, OFF_DQ:OFF_DK] = _rope(dq, cosd, sind, HEAD_DIM // 4)
    o_ref[:, OFF_DK:OFF_DV] = _rope(dk, cosd[:, 0:128], sind[:, 0:128], HEAD_DIM // 4)
    o_ref[:, OFF_DV:IN_WIDTH] = p[:, OFF_DV:IN_WIDTH]


def _inproj_call(seq_of_tile, rope_of_tile, x, mod3, g1, w_in, s32, s64, gqc, gkc, gqd, gkd, cosc, sinc, cosd, sind):
    full = lambda shape: pl.BlockSpec(shape, lambda i, s, r: (0,) * len(shape))
    rope_spec = pl.BlockSpec((TM, GROUP_WIDTH), lambda i, s, r: (r[i], 0))
    grid_spec = pltpu.PrefetchScalarGridSpec(
        num_scalar_prefetch=2,
        grid=(N_TILES,),
        in_specs=[pl.BlockSpec((TM, D_MODEL), lambda i, s, r: (i, 0)),
                  pl.BlockSpec((1, 1, 6 * D_MODEL), lambda i, s, r: (s[i], 0, 0)),
                  full((1, D_MODEL)),
                  full((D_MODEL, IN_WIDTH)),
                  full((GROUP_WIDTH, GROUP_WIDTH)),
                  full((GROUP_WIDTH, GROUP_WIDTH)),
                  full((1, GROUP_WIDTH)), full((1, GROUP_WIDTH)), full((1, GROUP_WIDTH)), full((1, 128)),
                  rope_spec, rope_spec, rope_spec, rope_spec],
        out_specs=pl.BlockSpec((TM, IN_WIDTH), lambda i, s, r: (i, 0)),
    )
    return pl.pallas_call(
        _inproj_kernel,
        grid_spec=grid_spec,
        out_shape=jax.ShapeDtypeStruct((N_TOK, IN_WIDTH), F32),
        compiler_params=_cparams(("arbitrary",)),
        name="inproj",
    )(seq_of_tile, rope_of_tile, x, mod3, g1, w_in, s32, s64, gqc, gkc, gqd, gkd, cosc, sinc, cosd, sind)


def _gmlp_kernel(u_ref, v_ref, s64_ref, g_ref, ws_ref, bias_ref, o_ref):
    vh = _seg_rms_norm(v_ref[...], s64_ref[...], g_ref[...]).astype(BF16)
    head = lax.broadcasted_iota(jnp.int32, (CHUNK, GROUP_WIDTH), 1) // HEAD_DIM
    for ch in range(TM // CHUNK):
        rows = slice(ch * CHUNK, (ch + 1) * CHUNK)
        vc = vh[rows]
        mixed = bias_ref[...]
        for h in range(N_GROUP_HEADS):
            mixed = mixed + jnp.where(head == h, _dot(ws_ref[h], vc), 0.0)
        o_ref[rows, :] = u_ref[rows, :] * mixed


def _gmlp_call(proj, s64, g, ws, bias):
    return pl.pallas_call(
        _gmlp_kernel,
        grid=(N_TILES,),
        in_specs=[pl.BlockSpec((TM, GROUP_WIDTH), lambda i: (i, OFF_AU // GROUP_WIDTH)),
                  pl.BlockSpec((TM, GROUP_WIDTH), lambda i: (i, OFF_AV // GROUP_WIDTH)),
                  pl.BlockSpec((GROUP_WIDTH, GROUP_WIDTH), lambda i: (0, 0)),
                  pl.BlockSpec((1, GROUP_WIDTH), lambda i: (0, 0)),
                  pl.BlockSpec((N_GROUP_HEADS, CHUNK, CHUNK), lambda i: (0, 0, 0)),
                  pl.BlockSpec((CHUNK, GROUP_WIDTH), lambda i: (0, 0))],
        out_specs=pl.BlockSpec((TM, GROUP_WIDTH), lambda i: (i, 0)),
        out_shape=jax.ShapeDtypeStruct((N_TOK, GROUP_WIDTH), F32),
        compiler_params=_cparams(("arbitrary",)),
        name="gmlp",
    )(proj, proj, s64, g, ws, bias)


def _scan_chunk(a, b, reverse):
    n = a.shape[0]
    row = lax.broadcasted_iota(jnp.int32, a.shape, 0)
    s = 1
    while s < n:
        if reverse:
            keep = row < n - s
            shift = n - s
        else:
            keep = row >= s
            shift = s
        a_prev = jnp.where(keep, pltpu.roll(a, shift, axis=0), 1.0)
        b_prev = jnp.where(keep, pltpu.roll(b, shift, axis=0), 0.0)
        b = a * b_prev + b
        a = a * a_prev
        s *= 2
    return a, b


def _lru_kernel(x_ref, g_ref, h0_ref, cw_ref, cb_ref, wcat_ref, bcat_ref, lam_ref, y_ref, st_ref,
                xpad, a_f, b_f, a_b, b_b, h_f, *, seq_len):
    nc = seq_len // LRU_CHUNK
    w = GROUP_WIDTH
    zeros8 = jnp.zeros((8, w), F32)
    xpad[0:8, :] = zeros8
    xpad[seq_len + 8:seq_len + 16, :] = zeros8
    xpad[8:seq_len + 8, :] = x_ref[...]
    sp = _softplus(-lam_ref[...])
    cw = cw_ref[...]
    cb = cb_ref[...]
    win_rows = LRU_CHUNK + 16

    def gates(c, carry):
        r0 = pl.multiple_of(c * LRU_CHUNK, LRU_CHUNK)
        win = xpad[pl.ds(r0, win_rows), :]
        inner = slice(8, 8 + LRU_CHUNK)
        xc = cb + pltpu.roll(win, 2, axis=0)[inner] * cw[0:1]
        xc = xc + pltpu.roll(win, 1, axis=0)[inner] * cw[1:2]
        xc = xc + win[inner] * cw[2:3]
        xc = xc + pltpu.roll(win, win_rows - 1, axis=0)[inner] * cw[3:4]
        sg = jax.nn.sigmoid(_dot(xc.astype(BF16), wcat_ref[...]) + bcat_ref[...])
        for d, (a_ref, b_ref) in enumerate(((a_f, b_f), (a_b, b_b))):
            r = sg[:, (2 * d) * w:(2 * d + 1) * w]
            i = sg[:, (2 * d + 1) * w:(2 * d + 2) * w]
            log_a = (-LRU_C * r) * sp[d:d + 1]
            a = jnp.exp(log_a)
            a_ref[pl.ds(r0, LRU_CHUNK), :] = a
            b_ref[pl.ds(r0, LRU_CHUNK), :] = jnp.sqrt(-jnp.tanh(log_a) * (a * a + 1.0)) * (i * xc)
        return carry

    lax.fori_loop(0, nc, gates, 0)

    def fwd(c, carry):
        r0 = pl.multiple_of(c * LRU_CHUNK, LRU_CHUNK)
        a_cum, h_loc = _scan_chunk(a_f[pl.ds(r0, LRU_CHUNK), :], b_f[pl.ds(r0, LRU_CHUNK), :], False)
        h = h_loc + a_cum * carry
        h_f[pl.ds(r0, LRU_CHUNK), :] = h
        return h[LRU_CHUNK - 1:LRU_CHUNK, :]

    s_f = lax.fori_loop(0, nc, fwd, h0_ref[0, 0:1, :])

    def bwd(k, carry):
        c = nc - 1 - k
        r0 = pl.multiple_of(c * LRU_CHUNK, LRU_CHUNK)
        a_cum, h_loc = _scan_chunk(a_b[pl.ds(r0, LRU_CHUNK), :], b_b[pl.ds(r0, LRU_CHUNK), :], True)
        h = h_loc + a_cum * carry
        y_ref[pl.ds(r0, LRU_CHUNK), :] = (h_f[pl.ds(r0, LRU_CHUNK), :] + h) * g_ref[pl.ds(r0, LRU_CHUNK), :]
        return h[0:1, :]

    s_b = lax.fori_loop(0, nc, bwd, h0_ref[0, 1:2, :])
    st_ref[0, 0:1, :] = s_f
    st_ref[0, 1:2, :] = s_b


def _lru_call(proj, h0, cw, cb, wcat, bcat, lam, *, seq_len, n_seq, row_block0):
    w = GROUP_WIDTH
    full = lambda shape: pl.BlockSpec(shape, lambda b: (0,) * len(shape))
    seq_block = lambda col: pl.BlockSpec((seq_len, w), lambda b: (b + row_block0, col))
    return pl.pallas_call(
        functools.partial(_lru_kernel, seq_len=seq_len),
        grid=(n_seq,),
        in_specs=[seq_block(OFF_BX // w), seq_block(OFF_BG // w),
                  pl.BlockSpec((1, 2, w), lambda b: (b, 0, 0)),
                  full((CONV_W, w)), full((1, w)), full((w, 4 * w)), full((1, 4 * w)), full((2, w))],
        out_specs=[pl.BlockSpec((seq_len, w), lambda b: (b, 0)),
                   pl.BlockSpec((1, 2, w), lambda b: (b, 0, 0))],
        out_shape=[jax.ShapeDtypeStruct((n_seq * seq_len, w), F32),
                   jax.ShapeDtypeStruct((n_seq, 2, w), F32)],
        scratch_shapes=[pltpu.VMEM((seq_len + 16, w), F32)] + [pltpu.VMEM((seq_len, w), F32)] * 5,
        compiler_params=_cparams(("arbitrary",)),
        name="lru_%d" % seq_len,
    )(proj, proj, h0, cw, cb, wcat, bcat, lam)


def _diff_lambda(ld_ref, lam_init):
    ld = ld_ref[...]
    l1 = jnp.sum(ld[0:1] * ld[1:2], axis=-1, keepdims=True)
    l2 = jnp.sum(ld[2:3] * ld[3:4], axis=-1, keepdims=True)
    return jnp.exp(l1) - jnp.exp(l2) + lam_init


def _diff_attn_body(q, keys, values, lam, s64, g, lam_init):
    tq = q.shape[0]
    lane = lax.broadcasted_iota(jnp.int32, (1, GROUP_WIDTH), 1)
    scale = DIFF_QK_DIM ** -0.5
    o = jnp.zeros((tq, GROUP_WIDTH), F32)
    for h in range(N_GROUP_HEADS):
        probs = []
        for i in range(2):
            seg = h * 2 + i
            qm = jnp.where(lane // DIFF_QK_DIM == seg, q, 0.0).astype(BF16)
            s = [_dot_nt(qm, k) * scale for k in keys]
            m = functools.reduce(jnp.maximum, [jnp.max(x, axis=-1, keepdims=True) for x in s])
            e = [jnp.exp(x - m) for x in s]
            den = functools.reduce(jnp.add, [jnp.sum(x, axis=-1, keepdims=True) for x in e])
            inv = 1.0 / den
            probs.append([x * inv for x in e])
        oh = None
        for p0, p1, v in zip(probs[0], probs[1], values):
            part = _dot((p0 - lam * p1).astype(BF16), v)
            oh = part if oh is None else oh + part
        o = o + jnp.where(lane // HEAD_DIM == h, oh, 0.0)
    return _seg_rms_norm(o, s64, g) * (1.0 - lam_init)


def _diff_prompt_kernel(q_ref, k_ref, v_ref, ld_ref, s64_ref, g_ref, o_ref, *, lam_init):
    lam = _diff_lambda(ld_ref, lam_init)
    o_ref[...] = _diff_attn_body(q_ref[...], [k_ref[...].astype(BF16)], [v_ref[...].astype(BF16)],
                                 lam, s64_ref[...], g_ref[...], lam_init)


def _diff_sample_kernel(q_ref, k_ref, v_ref, ck_ref, cv_ref, ld_ref, s64_ref, g_ref, o_ref, *, lam_init):
    lam = _diff_lambda(ld_ref, lam_init)
    keys = [ck_ref[0].astype(BF16), k_ref[...].astype(BF16)]
    values = [cv_ref[0].astype(BF16), v_ref[...].astype(BF16)]
    o_ref[...] = _diff_attn_body(q_ref[...], keys, values, lam, s64_ref[...], g_ref[...], lam_init)


def _diff_prompt_call(proj, ld, s64, g, lam_init):
    w = GROUP_WIDTH
    full = lambda shape: pl.BlockSpec(shape, lambda b: (0,) * len(shape))
    return pl.pallas_call(
        functools.partial(_diff_prompt_kernel, lam_init=lam_init),
        grid=(BATCH,),
        in_specs=[pl.BlockSpec((SEQ, w), lambda b: (b, OFF_CQ // w)),
                  pl.BlockSpec((SEQ, w), lambda b: (b, OFF_CK // w)),
                  pl.BlockSpec((SEQ, w), lambda b: (b, OFF_CV // w)),
                  full((4, DIFF_QK_DIM)), full((w, w)), full((1, w))],
        out_specs=pl.BlockSpec((SEQ, w), lambda b: (b, 0)),
        out_shape=jax.ShapeDtypeStruct((N_PROMPT, w), F32),
        compiler_params=_cparams(("arbitrary",)),
        name="diff_prompt",
    )(proj, proj, proj, ld, s64, g)


def _diff_sample_call(proj, ctx_k, ctx_v, ld, s64, g, lam_init):
    w = GROUP_WIDTH
    tq = 256
    nq = DEC_SEQ // tq
    full = lambda shape: pl.BlockSpec(shape, lambda b, i: (0,) * len(shape))
    seq_block0 = N_PROMPT // DEC_SEQ
    return pl.pallas_call(
        functools.partial(_diff_sample_kernel, lam_init=lam_init),
        grid=(DEC_BATCH, nq),
        in_specs=[pl.BlockSpec((tq, w), lambda b, i: (N_PROMPT // tq + b * nq + i, OFF_CQ // w)),
                  pl.BlockSpec((DEC_SEQ, w), lambda b, i: (seq_block0 + b, OFF_CK // w)),
                  pl.BlockSpec((DEC_SEQ, w), lambda b, i: (seq_block0 + b, OFF_CV // w)),
                  pl.BlockSpec((1, PAST_LEN, w), lambda b, i: (b, 0, 0)),
                  pl.BlockSpec((1, PAST_LEN, w), lambda b, i: (b, 0, 0)),
                  full((4, DIFF_QK_DIM)), full((w, w)), full((1, w))],
        out_specs=pl.BlockSpec((tq, w), lambda b, i: (b * nq + i, 0)),
        out_shape=jax.ShapeDtypeStruct((N_SAMPLE, w), F32),
        compiler_params=_cparams(("arbitrary", "arbitrary")),
        name="diff_sample",
    )(proj, proj, proj, ctx_k, ctx_v, ld, s64, g)


def _sink_attn_body(q, keys, values, masks, sink_ref):
    tq = q.shape[0]
    lane = lax.broadcasted_iota(jnp.int32, (1, 128), 1)
    scale = HEAD_DIM ** -0.5
    outs = []
    for grp in range(SWA_GROUPS):
        qg = q[:, grp * 128:(grp + 1) * 128]
        og = jnp.zeros((tq, 128), F32)
        for kh in range(SWA_KV_HEADS):
            sink = sink_ref[kh * SWA_GROUPS + grp]
            qm = jnp.where(lane // HEAD_DIM == kh, qg, 0.0).astype(BF16)
            s = []
            for k, msk in zip(keys, masks):
                x = _dot_nt(qm, k) * scale
                s.append(x if msk is None else jnp.where(msk, x, -jnp.inf))
            m = functools.reduce(jnp.maximum, [jnp.max(x, axis=-1, keepdims=True) for x in s])
            m = jnp.maximum(m, sink)
            e = [jnp.exp(x - m) for x in s]
            den = functools.reduce(jnp.add, [jnp.sum(x, axis=-1, keepdims=True) for x in e]) + jnp.exp(sink - m)
            inv = 1.0 / den
            oh = None
            for x, v in zip(e, values):
                part = _dot((x * inv).astype(BF16), v)
                oh = part if oh is None else oh + part
            og = og + jnp.where(lane // HEAD_DIM == kh, oh, 0.0)
        outs.append(og)
    return outs


def _swa_prompt_kernel(sink_ref, q_ref, k_ref, v_ref, o_ref):
    outs = _sink_attn_body(q_ref[...], [k_ref[...].astype(BF16)], [v_ref[...].astype(BF16)], [None], sink_ref)
    for grp in range(SWA_GROUPS):
        o_ref[:, grp * 128:(grp + 1) * 128] = outs[grp]


def _swa_sample_kernel(sink_ref, q_ref, kp_ref, kc_ref, kn_ref, vp_ref, vc_ref, vn_ref, ck_ref, cv_ref, o_ref):
    n = pl.program_id(1)
    nb = pl.num_programs(1)
    r = lax.broadcasted_iota(jnp.int32, (WINDOW, WINDOW), 0)
    c = lax.broadcasted_iota(jnp.int32, (WINDOW, WINDOW), 1)
    mask_prev = c >= r + jnp.where(n > 0, 0, WINDOW)
    mask_next = c <= r - jnp.where(n < nb - 1, 0, WINDOW)
    keys = [ck_ref[0].astype(BF16), kp_ref[...].astype(BF16), kc_ref[...].astype(BF16), kn_ref[...].astype(BF16)]
    values = [cv_ref[0].astype(BF16), vp_ref[...].astype(BF16), vc_ref[...].astype(BF16), vn_ref[...].astype(BF16)]
    outs = _sink_attn_body(q_ref[...], keys, values, [None, mask_prev, None, mask_next], sink_ref)
    for grp in range(SWA_GROUPS):
        o_ref[:, grp * 128:(grp + 1) * 128] = outs[grp]


def _swa_prompt_call(sink, proj):
    w = GROUP_WIDTH
    return pl.pallas_call(
        _swa_prompt_kernel,
        grid=(BATCH,),
        in_specs=[pl.BlockSpec(memory_space=pltpu.SMEM),
                  pl.BlockSpec((SEQ, w), lambda b: (b, OFF_DQ // w)),
                  pl.BlockSpec((SEQ, 128), lambda b: (b, OFF_DK // 128)),
                  pl.BlockSpec((SEQ, 128), lambda b: (b, OFF_DV // 128))],
        out_specs=pl.BlockSpec((SEQ, w), lambda b: (b, 0)),
        out_shape=jax.ShapeDtypeStruct((N_PROMPT, w), F32),
        compiler_params=_cparams(("arbitrary",)),
        name="swa_prompt",
    )(sink, proj, proj, proj)


def _swa_sample_call(sink, proj, ctx_k, ctx_v):
    w = GROUP_WIDTH
    tq = WINDOW
    nq = DEC_SEQ // tq
    row0 = N_PROMPT // tq

    def kv_spec(col, delta):
        def index(b, i):
            j = jnp.clip(i + delta, 0, nq - 1)
            return (row0 + b * nq + j, col)
        return pl.BlockSpec((tq, 128), index)

    ctx_spec = pl.BlockSpec((1, PAST_LEN, 128), lambda b, i: (b, 0, 0))
    return pl.pallas_call(
        _swa_sample_kernel,
        grid=(DEC_BATCH, nq),
        in_specs=[pl.BlockSpec(memory_space=pltpu.SMEM),
                  pl.BlockSpec((tq, w), lambda b, i: (row0 + b * nq + i, OFF_DQ // w)),
                  kv_spec(OFF_DK // 128, -1), kv_spec(OFF_DK // 128, 0), kv_spec(OFF_DK // 128, 1),
                  kv_spec(OFF_DV // 128, -1), kv_spec(OFF_DV // 128, 0), kv_spec(OFF_DV // 128, 1),
                  ctx_spec, ctx_spec],
        out_specs=pl.BlockSpec((tq, w), lambda b, i: (b * nq + i, 0)),
        out_shape=jax.ShapeDtypeStruct((N_SAMPLE, w), F32),
        compiler_params=_cparams(("arbitrary", "arbitrary")),
        name="swa_sample",
    )(sink, proj, proj, proj, proj, proj, proj, proj, ctx_k, ctx_v)


def _outproj_kernel(seq_ref, ya_ref, yb_ref, yc_ref, yd_ref, x_ref, mod_ref, w_ref, g2_ref, rw_ref, rb_ref, tri_ref,
                    x1_ref, h2r_ref, topi_ref, topw_ref, rank_ref, count_ref, cnt_ref):
    del seq_ref
    w = GROUP_WIDTH
    mix = _dot(ya_ref[...].astype(BF16), w_ref[0:w, :])
    mix = mix + _dot(yb_ref[...].astype(BF16), w_ref[w:2 * w, :])
    mix = mix + _dot(yc_ref[...].astype(BF16), w_ref[2 * w:3 * w, :])
    mix = mix + _dot(yd_ref[...].astype(BF16), w_ref[3 * w:4 * w, :])
    mod = mod_ref[0]
    g1 = mod[:, 2 * D_MODEL:3 * D_MODEL]
    sh2 = mod[:, 3 * D_MODEL:4 * D_MODEL]
    sc2 = mod[:, 4 * D_MODEL:5 * D_MODEL]
    x1 = x_ref[...] + g1 * mix
    x1_ref[...] = x1
    xn = x1 * lax.rsqrt(jnp.mean(x1 * x1, axis=-1, keepdims=True) + EPS) * g2_ref[...]
    h2 = xn * (1.0 + sc2) + sh2
    for j in range(ROW_PIECES):
        h2r_ref[pl.ds(j, TM, stride=ROW_PIECES), :] = h2[:, j * 128:(j + 1) * 128]
    h_hi, h_lo = _split_bf16(h2)
    r_hi, r_lo = _split_bf16(rw_ref[...])
    logits = _dot(h_hi, r_hi) + (_dot(h_hi, r_lo) + _dot(h_lo, r_hi)) + rb_ref[...]
    lane = lax.broadcasted_iota(jnp.int32, logits.shape, 1)
    work = logits
    picks = []
    for _ in range(TOP_K):
        m = jnp.max(work, axis=-1, keepdims=True)
        idx = jnp.min(jnp.where(work == m, lane, N_EXPERTS), axis=-1, keepdims=True)
        picks.append((m, idx))
        work = jnp.where(lane == idx, -jnp.inf, work)
    top = picks[0][0]
    ex = [jnp.exp(m - top) for m, _ in picks]
    inv = 1.0 / functools.reduce(jnp.add, ex)
    @pl.when(pl.program_id(0) % (MOE_TB // TM) == 0)
    def _():
        cnt_ref[...] = jnp.zeros_like(cnt_ref)

    sel = jnp.zeros(logits.shape, F32)
    for _, idx in picks:
        sel = sel + jnp.where(lane == idx, 1.0, 0.0)
    csum = _dot(tri_ref[...], sel.astype(BF16))
    before = cnt_ref[...] + csum - sel
    cnt_ref[...] += csum[TM - 1:TM, :]
    count_ref[0] = cnt_ref[...].astype(jnp.int32)
    slot = lax.broadcasted_iota(jnp.int32, topi_ref.shape, 1)
    top_i = jnp.zeros(topi_ref.shape, jnp.int32)
    top_w = jnp.zeros(topw_ref.shape, F32)
    rank = jnp.zeros(rank_ref.shape, jnp.int32)
    for k, (e, (_, idx)) in enumerate(zip(ex, picks)):
        rank_k = jnp.sum(jnp.where(lane == idx, before, 0.0), axis=-1, keepdims=True).astype(jnp.int32)
        top_i = jnp.where(slot == k, idx, top_i)
        top_w = jnp.where(slot == k, e * inv, top_w)
        rank = jnp.where(slot == k, rank_k, rank)
    topi_ref[...] = top_i
    topw_ref[...] = top_w
    rank_ref[...] = rank


def _outproj_call(seq_of_tile, ya, yb, yc, yd, x, mod3, w_out, g2, rw, rb, tri):
    w = GROUP_WIDTH
    full = lambda shape: pl.BlockSpec(shape, lambda i, s: (0,) * len(shape))
    tile = lambda width: pl.BlockSpec((TM, width), lambda i, s: (i, 0))
    tiles_per_block = MOE_TB // TM
    return pl.pallas_call(
        _outproj_kernel,
        grid_spec=pltpu.PrefetchScalarGridSpec(
            num_scalar_prefetch=1,
            grid=(N_TILES,),
            in_specs=[tile(w), tile(w), tile(w), tile(w), tile(D_MODEL),
                      pl.BlockSpec((1, 1, 6 * D_MODEL), lambda i, s: (s[i], 0, 0)),
                      full((D_MODEL, D_MODEL)), full((1, D_MODEL)),
                      full((D_MODEL, N_EXPERTS)), full((1, N_EXPERTS)), full((TM, TM))],
            out_specs=[tile(D_MODEL), pl.BlockSpec((TM * ROW_PIECES, 128), lambda i, s: (i, 0)),
                       tile(TOPK_PAD), tile(TOPK_PAD), tile(TOPK_PAD),
                       pl.BlockSpec((1, 1, N_EXPERTS), lambda i, s: (i // tiles_per_block, 0, 0))],
            scratch_shapes=[pltpu.VMEM((1, N_EXPERTS), F32)],
        ),
        out_shape=[jax.ShapeDtypeStruct((N_TOK, D_MODEL), F32),
                   jax.ShapeDtypeStruct((N_TOK * ROW_PIECES, 128), F32),
                   jax.ShapeDtypeStruct((N_TOK, TOPK_PAD), jnp.int32),
                   jax.ShapeDtypeStruct((N_TOK, TOPK_PAD), F32),
                   jax.ShapeDtypeStruct((N_TOK, TOPK_PAD), jnp.int32),
                   jax.ShapeDtypeStruct((N_TOK // MOE_TB, 1, N_EXPERTS), jnp.int32)],
        compiler_params=_cparams(("arbitrary",)),
        name="outproj",
    )(seq_of_tile, ya, yb, yc, yd, x, mod3, w_out, g2, rw, rb, tri)


def _moe_kernel(expert_ref, block_ref, kind_ref, local_ref, dest_ref, topw_ref, gstart_ref, count_ref,
                src_ref, wgu_ref, bgu_ref, wd_ref, bd_ref, acc_ref, pair_ref, xt_even, xt_odd, y3_even, y3_odd):
    del expert_ref, block_ref
    i = pl.program_id(0)
    kind = kind_ref[i]

    def token_of(pair):
        return (pair >> MOE_PAIR_SHIFT) & (MOE_TB - 1)

    def gather(tile, xt_ref, lo=0, hi=MOE_TILE):
        row0 = local_ref[tile]
        for m in range(lo, hi):
            t = token_of(pair_ref[row0 + m])
            slab = src_ref[pl.ds(pl.multiple_of(t * ROW_PIECES, ROW_PIECES), ROW_PIECES), :]
            xt_ref[pl.ds(m, ROW_PIECES, stride=MOE_XT_STRIDE), :] = slab

    def scatter(tile, y3_ref, lo=0, hi=MOE_TILE):
        row0 = local_ref[tile]
        for m0 in range(lo, hi, MOE_RMW_BATCH):
            rows = range(m0, m0 + MOE_RMW_BATCH)
            pairs = [pair_ref[row0 + m] for m in rows]
            toks = [token_of(p) for p in pairs]
            vals = [acc_ref[t] + topw_ref[0, 0, p] * y3_ref[m // 8, pl.ds(m % 8, 8, stride=8), :]
                    for t, p, m in zip(toks, pairs, rows)]
            for t, v in reversed(list(zip(toks, vals))):
                acc_ref[t] = v

    def compute(xt_ref, y3_ref):
        x = jnp.concatenate([xt_ref[j * MOE_XT_STRIDE:j * MOE_XT_STRIDE + MOE_TILE, :] for j in range(ROW_PIECES)],
                            axis=1).astype(BF16)
        gu = _dot(x, wgu_ref[0, 0]) + bgu_ref[0, 0]
        gate = jnp.minimum(gu[:, :D_FF], SWIGLU_LIMIT)
        up = jnp.clip(gu[:, D_FF:], -SWIGLU_LIMIT, SWIGLU_LIMIT)
        act = (up + 1.0) * gate * jax.nn.sigmoid(SWIGLU_ALPHA * gate)
        y = _dot(act.astype(BF16), wd_ref[0, 0]) + bd_ref[0, 0]
        for j in range(ROW_PIECES):
            y3_ref[:, j * 8:(j + 1) * 8, :] = y[:, j * 128:(j + 1) * 128].reshape(MOE_TILE // 8, 8, 128)

    @pl.when(kind == MOE_KIND_FIRST)
    def _():
        def zero(c, carry):
            acc_ref[pl.ds(pl.multiple_of(c * MOE_TILE, MOE_TILE), MOE_TILE)] = jnp.zeros((MOE_TILE, 8, 128), F32)
            return carry
        lax.fori_loop(0, MOE_TB // MOE_TILE, zero, 0)

        def pad_group(e, carry):
            cnt = count_ref[0, 0, e]
            first_pad = gstart_ref[0, 0, e] + cnt
            n_pad = (-cnt) & (MOE_TILE - 1)

            def pad_row(r, c2):
                pair_ref[first_pad + r] = MOE_PAD_PAIR
                return c2
            lax.fori_loop(0, n_pad, pad_row, 0)
            return carry
        lax.fori_loop(0, N_EXPERTS, pad_group, 0)

        def place(t, carry):
            for k in range(TOP_K):
                pair = t * TOP_K + k
                pair_ref[dest_ref[0, 0, pair]] = pair
            return carry
        lax.fori_loop(0, MOE_TB, place, 0)

    step = kind * 2 + (i & 1)
    for parity, (xt_cur, xt_nxt, y3_cur, y3_prv) in enumerate(((xt_even, xt_odd, y3_even, y3_odd),
                                                               (xt_odd, xt_even, y3_odd, y3_even))):
        @pl.when(step == MOE_KIND_FIRST * 2 + parity)
        def _():
            gather(i, xt_cur)
            gather(i + 1, xt_nxt)
            compute(xt_cur, y3_cur)

        @pl.when(step == MOE_KIND_MIDDLE * 2 + parity)
        def _():
            gather(i + 1, xt_nxt)
            compute(xt_cur, y3_cur)
            scatter(i - 1, y3_prv)

        @pl.when(step == MOE_KIND_LAST * 2 + parity)
        def _():
            compute(xt_cur, y3_cur)
            scatter(i - 1, y3_prv)
            scatter(i, y3_cur)


def _moe_call(layer, tile_expert, tile_block, tile_flags, tile_local, dest, topw, gstart, counts, h2_rows,
              wgu, bgu, wd, bd):
    pieces = ROW_PIECES
    once = pl.Buffered(1)
    n_pairs = MOE_TB * TOP_K

    def smem_block(width):
        return pl.BlockSpec((1, 1, width), lambda i, e, b, f, lo: (b[i], 0, 0), memory_space=pltpu.SMEM,
                            pipeline_mode=once)

    def expert_block(rows, cols):
        return pl.BlockSpec((1, 1, rows, cols), lambda i, e, b, f, lo: (layer, e[i], 0, 0))

    return pl.pallas_call(
        _moe_kernel,
        grid_spec=pltpu.PrefetchScalarGridSpec(
            num_scalar_prefetch=4,
            grid=(MOE_MAX_TILES,),
            in_specs=[smem_block(n_pairs), smem_block(n_pairs + 128), smem_block(N_EXPERTS), smem_block(N_EXPERTS),
                      pl.BlockSpec((MOE_TB * pieces, 128), lambda i, e, b, f, lo: (b[i], 0), pipeline_mode=once),
                      expert_block(D_MODEL, 2 * D_FF), expert_block(1, 2 * D_FF),
                      expert_block(D_FF, D_MODEL), expert_block(1, D_MODEL)],
            out_specs=pl.BlockSpec((MOE_TB, 8, 128), lambda i, e, b, f, lo: (b[i], 0, 0), pipeline_mode=once),
            scratch_shapes=[pltpu.SMEM((MOE_ROWS_PER_BLOCK,), jnp.int32),
                            pltpu.VMEM((pieces * MOE_XT_STRIDE, 128), F32),
                            pltpu.VMEM((pieces * MOE_XT_STRIDE, 128), F32),
                            pltpu.VMEM((MOE_TILE // 8, 8 * pieces, 128), F32),
                            pltpu.VMEM((MOE_TILE // 8, 8 * pieces, 128), F32)],
        ),
        out_shape=jax.ShapeDtypeStruct((N_TOK, 8, 128), F32),
        compiler_params=_cparams(("arbitrary",)),
        name="moe",
    )(tile_expert, tile_block, tile_flags, tile_local, dest, topw, gstart, counts, h2_rows, wgu, bgu, wd, bd)


def _moe_routing(top_i, top_w, rank, counts):
    nb = N_TOK // MOE_TB
    n_groups = nb * N_EXPERTS
    counts = counts.reshape(nb, N_EXPERTS)
    padded = ((counts + MOE_TILE - 1) // MOE_TILE) * MOE_TILE
    group_end = jnp.cumsum(padded.reshape(-1))
    group_start = (group_end - padded.reshape(-1)).reshape(nb, N_EXPERTS)
    block_row0 = group_start[:, 0]
    gstart_local = group_start - block_row0[:, None]
    experts = jnp.arange(N_EXPERTS, dtype=jnp.int32)
    picked = top_i.reshape(nb, MOE_TB, TOP_K, 1) == experts
    dest = jnp.sum(jnp.where(picked, gstart_local[:, None, None, :], 0), axis=-1) + rank.reshape(nb, MOE_TB, TOP_K)
    dest = dest.reshape(nb, 1, MOE_TB * TOP_K).astype(jnp.int32)
    topw = jnp.concatenate([top_w.reshape(nb, 1, MOE_TB * TOP_K), jnp.zeros((nb, 1, 128), F32)], axis=-1)
    tile_start = jnp.arange(MOE_MAX_TILES, dtype=jnp.int32) * MOE_TILE
    tile_group = jnp.sum((group_end[None, :] <= tile_start[:, None]).astype(jnp.int32), axis=1)
    tile_group = jnp.minimum(tile_group, n_groups - 1)
    valid = tile_start < group_end[-1]
    tile_block = tile_group // N_EXPERTS
    new_block = tile_block[1:] != tile_block[:-1]
    first = jnp.concatenate([jnp.ones((1,), bool), new_block])
    last = jnp.concatenate([new_block | ~valid[1:], jnp.ones((1,), bool)])
    kinds = jnp.where(first, MOE_KIND_FIRST, jnp.where(last, MOE_KIND_LAST, MOE_KIND_MIDDLE))
    flags = jnp.where(valid, kinds, MOE_KIND_SKIP).astype(jnp.int32)
    blocks = jnp.arange(nb, dtype=jnp.int32)
    tile_row0 = jnp.sum(jnp.where(tile_block[:, None] == blocks[None, :], block_row0[None, :], 0), axis=1)
    tile_local = (tile_start - tile_row0).astype(jnp.int32)
    return (tile_group % N_EXPERTS, tile_block, flags, tile_local, dest, topw,
            gstart_local.reshape(nb, 1, N_EXPERTS).astype(jnp.int32), counts.reshape(nb, 1, N_EXPERTS))


def _residual_kernel(seq_ref, x1_ref, moe_ref, mod_ref, o_ref):
    del seq_ref
    g2 = mod_ref[0][:, 5 * D_MODEL:6 * D_MODEL]
    moe = jnp.concatenate([moe_ref[pl.ds(j, TM, stride=ROW_PIECES), :] for j in range(ROW_PIECES)], axis=1)
    o_ref[...] = x1_ref[...] + g2 * moe


def _residual_call(seq_of_tile, x1, moe_rows, mod3):
    tile = pl.BlockSpec((TM, D_MODEL), lambda i, s: (i, 0))
    return pl.pallas_call(
        _residual_kernel,
        grid_spec=pltpu.PrefetchScalarGridSpec(
            num_scalar_prefetch=1,
            grid=(N_TILES,),
            in_specs=[tile, pl.BlockSpec((TM * ROW_PIECES, 128), lambda i, s: (i, 0)),
                      pl.BlockSpec((1, 1, 6 * D_MODEL), lambda i, s: (s[i], 0, 0))],
            out_specs=tile,
        ),
        out_shape=jax.ShapeDtypeStruct((N_TOK, D_MODEL), F32),
        compiler_params=_cparams(("arbitrary",)),
        name="residual",
    )(seq_of_tile, x1, moe_rows, mod3)


def _segment_matrix(width, seg):
    idx = np.arange(width) // seg
    return jnp.asarray((idx[:, None] == idx[None, :]).astype(np.float32) / seg, BF16)


def _rope_tables(rot_dim, width):
    rows = DEC_SEQ // GRID_W
    nf = rot_dim // 4
    inv = 1.0 / (ROPE_BASE ** (jnp.arange(nf, dtype=F32) / nf))
    row = jnp.repeat(jnp.arange(rows, dtype=F32), GRID_W)
    col = jnp.tile(jnp.arange(GRID_W, dtype=F32), rows)
    ang = jnp.stack([row[:, None] * inv, col[:, None] * inv], axis=1)
    cos, sin = jnp.cos(ang), jnp.sin(ang)
    cos_r = jnp.concatenate([cos[:, 0], cos[:, 0], cos[:, 1], cos[:, 1]], axis=-1)
    sin_r = jnp.concatenate([-sin[:, 0], sin[:, 0], -sin[:, 1], sin[:, 1]], axis=-1)
    reps = width // rot_dim
    cos_t = jnp.concatenate([jnp.ones((TM, width), F32), jnp.tile(cos_r, (1, reps))], axis=0)
    sin_t = jnp.concatenate([jnp.zeros((TM, width), F32), jnp.tile(sin_r, (1, reps))], axis=0)
    return cos_t, sin_t


def _block_diag(wb):
    nb, n, _ = wb.shape
    out = jnp.zeros((nb * n, nb * n), wb.dtype)
    for i in range(nb):
        out = out.at[i * n:(i + 1) * n, i * n:(i + 1) * n].set(wb[i])
    return out


_DQ_HEADS = [kh * SWA_GROUPS + g for g in range(SWA_GROUPS) for kh in range(SWA_KV_HEADS)]


def kernel(x_prompt, x_sample, c, cache_diff_k, cache_diff_v, cache_swa_k, cache_swa_v, state_lru, c_ctx, mod_w, mod_b, norm1_g, norm2_g, w_in, w_out, mlp_vnorm_g, mlp_ws, mlp_bs, lru_conv_w, lru_conv_b, lru_wa, lru_ba, lru_wx, lru_bx, lru_lambda, diff_qnorm_g, diff_knorm_g, diff_lambda, diff_subln_g, swa_qnorm_g, swa_knorm_g, swa_sink, router_w, router_b, moe_w_gu, moe_b_gu, moe_w_down, moe_b_down):
    params = dict(locals())
    params["moe_w_gu_bf16"] = moe_w_gu.astype(BF16)
    params["moe_w_down_bf16"] = moe_w_down.astype(BF16)
    x = jnp.concatenate([x_prompt.reshape(N_PROMPT, D_MODEL), x_sample.reshape(N_SAMPLE, D_MODEL)], axis=0)
    cond = jnp.concatenate([c_ctx[None], c, jnp.zeros((N_COND - 1 - DEC_BATCH, D_MODEL), F32)], axis=0)
    consts = _constants()
    dk_l, dv_l, sk_l, sv_l, lru_l = [], [], [], [], []
    for l in range(DEPTH):
        st = _mixer_stage(x, cond, params, l, consts)
        x = _ffn_stage(x, st, params, l, consts)
        pp = st["proj"][:N_PROMPT]
        dk_l.append(pp[:, OFF_CK:OFF_CV].reshape(BATCH, SEQ, N_GROUP_HEADS, 2, DIFF_QK_DIM))
        dv_l.append(pp[:, OFF_CV:OFF_DQ].reshape(BATCH, SEQ, N_GROUP_HEADS, HEAD_DIM))
        sk_l.append(pp[:, OFF_DK:OFF_DV].reshape(BATCH, SEQ, SWA_KV_HEADS, HEAD_DIM))
        sv_l.append(pp[:, OFF_DV:IN_WIDTH].reshape(BATCH, SEQ, SWA_KV_HEADS, HEAD_DIM))
        lru_l.append(st["st_p"])

    y_p = x[:N_PROMPT].reshape(BATCH, SEQ, D_MODEL)
    y_s = x[N_PROMPT:].reshape(DEC_BATCH, DEC_SEQ, D_MODEL)
    return (y_p, y_s, jnp.stack(dk_l, axis=1), jnp.stack(dv_l, axis=1), jnp.stack(sk_l, axis=1),
            jnp.stack(sv_l, axis=1), jnp.stack(lru_l, axis=1))


def _constants():
    w = GROUP_WIDTH
    tiles = np.arange(N_TILES)
    seq_np = np.where(tiles < PROMPT_TILES, 0, 1 + (tiles - PROMPT_TILES) // SAMPLE_TILES_PER_SEQ)
    rope_np = np.where(tiles < PROMPT_TILES, 0, 1 + (tiles - PROMPT_TILES) % SAMPLE_TILES_PER_SEQ)
    seq_of_tile = jnp.asarray(seq_np, jnp.int32)
    rope_of_tile = jnp.asarray(rope_np, jnp.int32)

    s32 = _segment_matrix(w, DIFF_QK_DIM)
    s64 = _segment_matrix(w, HEAD_DIM)
    cosc, sinc = _rope_tables(DIFF_QK_DIM, w)
    cosd, sind = _rope_tables(HEAD_DIM, w)
    dq_cols = np.concatenate([np.arange(h * HEAD_DIM, (h + 1) * HEAD_DIM) for h in _DQ_HEADS])
    tri = jnp.asarray(np.tril(np.ones((TM, TM), np.float32)), BF16)
    return dict(seq_of_tile=seq_of_tile, rope_of_tile=rope_of_tile, tri=tri, s32=s32, s64=s64,
                cosc=cosc, sinc=sinc, cosd=cosd, sind=sind, dq_cols=dq_cols)


def _mixer_stage(x, cond, params, l, consts):
    w = GROUP_WIDTH
    (mod_w, mod_b, norm1_g, w_in, mlp_vnorm_g, mlp_ws, mlp_bs, lru_conv_w, lru_conv_b, lru_wa, lru_ba, lru_wx, lru_bx,
     lru_lambda, diff_qnorm_g, diff_knorm_g, diff_lambda, diff_subln_g, swa_qnorm_g, swa_knorm_g, swa_sink,
     cache_diff_k, cache_diff_v, cache_swa_k, cache_swa_v, state_lru) = (params[k] for k in (
         "mod_w", "mod_b", "norm1_g", "w_in", "mlp_vnorm_g", "mlp_ws", "mlp_bs", "lru_conv_w", "lru_conv_b", "lru_wa",
         "lru_ba", "lru_wx", "lru_bx", "lru_lambda", "diff_qnorm_g", "diff_knorm_g", "diff_lambda", "diff_subln_g",
         "swa_qnorm_g", "swa_knorm_g", "swa_sink", "cache_diff_k", "cache_diff_v", "cache_swa_k", "cache_swa_v",
         "state_lru"))
    seq_of_tile, rope_of_tile, s32, s64, cosc, sinc, cosd, sind, dq_cols = (consts[k] for k in (
        "seq_of_tile", "rope_of_tile", "s32", "s64", "cosc", "sinc", "cosd", "sind", "dq_cols"))
    if True:
        lam_init = 0.8 - 0.6 * math.exp(-0.3 * l)
        w_in_l = w_in[l]
        w_in_l = jnp.concatenate([w_in_l[:, :OFF_DQ], w_in_l[:, OFF_DQ:OFF_DK][:, dq_cols], w_in_l[:, OFF_DK:]], axis=1)
        w_in_l = w_in_l.astype(BF16)
        gqc = jnp.tile(diff_qnorm_g[l].reshape(1, 2 * DIFF_QK_DIM), (1, N_GROUP_HEADS))
        gkc = jnp.tile(diff_knorm_g[l].reshape(1, 2 * DIFF_QK_DIM), (1, N_GROUP_HEADS))
        gqd = jnp.tile(swa_qnorm_g[l].reshape(1, HEAD_DIM), (1, 4))
        gkd = jnp.tile(swa_knorm_g[l].reshape(1, HEAD_DIM), (1, SWA_KV_HEADS))
        g_sub = jnp.tile(diff_subln_g[l].reshape(1, HEAD_DIM), (1, N_GROUP_HEADS))
        g_mlp = mlp_vnorm_g[l].reshape(1, w)
        mlp_bias = jnp.repeat(mlp_bs[l].T, HEAD_DIM, axis=1)
        wcat = jnp.concatenate([_block_diag(lru_wa[l, 0]), _block_diag(lru_wx[l, 0]),
                                _block_diag(lru_wa[l, 1]), _block_diag(lru_wx[l, 1])], axis=1).astype(BF16)
        bcat = jnp.concatenate([lru_ba[l, 0].reshape(1, w), lru_bx[l, 0].reshape(1, w),
                                lru_ba[l, 1].reshape(1, w), lru_bx[l, 1].reshape(1, w)], axis=1)
        sink = swa_sink[l].astype(F32)

        mod3 = _mod_call(cond, mod_w[l], mod_b[l]).reshape(N_COND, 1, 6 * D_MODEL)
        proj = _inproj_call(seq_of_tile, rope_of_tile, x, mod3, norm1_g[l].reshape(1, D_MODEL), w_in_l, s32, s64,
                            gqc, gkc, gqd, gkd, cosc, sinc, cosd, sind)

        ya = _gmlp_call(proj, s64, g_mlp, mlp_ws[l].astype(BF16), mlp_bias)

        lru_args = (lru_conv_w[l], lru_conv_b[l].reshape(1, w), wcat, bcat, lru_lambda[l])
        yb_p, st_p = _lru_call(proj, jnp.zeros((BATCH, 2, w), F32), *lru_args, seq_len=SEQ, n_seq=BATCH, row_block0=0)
        yb_s, _ = _lru_call(proj, state_lru[:, l], *lru_args, seq_len=DEC_SEQ, n_seq=DEC_BATCH,
                            row_block0=N_PROMPT // DEC_SEQ)

        yc_p = _diff_prompt_call(proj, diff_lambda[l], s64, g_sub, lam_init)
        yc_s = _diff_sample_call(proj, cache_diff_k[:, l].reshape(DEC_BATCH, PAST_LEN, w),
                                 cache_diff_v[:, l].reshape(DEC_BATCH, PAST_LEN, w), diff_lambda[l], s64, g_sub, lam_init)

        yd_p = _swa_prompt_call(sink, proj)
        yd_s = _swa_sample_call(sink, proj, cache_swa_k[:, l].reshape(DEC_BATCH, PAST_LEN, 128),
                                cache_swa_v[:, l].reshape(DEC_BATCH, PAST_LEN, 128))

        yb = jnp.concatenate([yb_p, yb_s], axis=0)
        yc = jnp.concatenate([yc_p, yc_s], axis=0)
        yd = jnp.concatenate([yd_p, yd_s], axis=0)
    return dict(mod3=mod3, proj=proj, ya=ya, yb=yb, yc=yc, yd=yd, st_p=st_p)


def _ffn_stage(x, st, params, l, consts):
    w = GROUP_WIDTH
    w_out, norm2_g, router_w, router_b, moe_w_gu, moe_b_gu, moe_w_down, moe_b_down = (params[k] for k in (
        "w_out", "norm2_g", "router_w", "router_b", "moe_w_gu", "moe_b_gu", "moe_w_down", "moe_b_down"))
    w_out_l = w_out[l]
    w_out_l = jnp.concatenate([w_out_l[:3 * w], w_out_l[3 * w:][consts["dq_cols"]]], axis=0).astype(BF16)
    x1, h2_rows, top_i, top_w, rank, counts = _outproj_call(
        consts["seq_of_tile"], st["ya"], st["yb"], st["yc"], st["yd"], x, st["mod3"], w_out_l,
        norm2_g[l].reshape(1, D_MODEL), router_w[l], router_b[l].reshape(1, N_EXPERTS), consts["tri"])
    routing = _moe_routing(top_i[:, :TOP_K], top_w[:, :TOP_K], rank[:, :TOP_K], counts)
    moe = _moe_call(l, *routing, h2_rows, params["moe_w_gu_bf16"],
                    moe_b_gu.reshape(DEPTH, N_EXPERTS, 1, 2 * D_FF), params["moe_w_down_bf16"],
                    moe_b_down.reshape(DEPTH, N_EXPERTS, 1, D_MODEL))
    return _residual_call(consts["seq_of_tile"], x1, moe.reshape(N_TOK * ROW_PIECES, 128), st["mod3"])
```

```python
import functools
import math

import jax
import jax.numpy as jnp
import numpy as np
from jax import lax
from jax.experimental import pallas as pl
from jax.experimental.pallas import tpu as pltpu

F32 = jnp.float32
BF16 = jnp.bfloat16

D_MODEL = 1024
BATCH = 16
SEQ = 256
DEPTH = 2
DEC_BATCH = 4
DEC_SEQ = 2048
PAST_LEN = 512
GRID_W = 64
HEAD_DIM = 64
GROUP_WIDTH = 256
N_GROUP_HEADS = 4
CHUNK = 128
LRU_C = 8.0
CONV_W = 4
DIFF_QK_DIM = 32
SWA_KV_HEADS = 2
SWA_GROUPS = 2
SWA_HEADS = SWA_KV_HEADS * SWA_GROUPS
WINDOW = 128
N_EXPERTS = 32
TOP_K = 4
D_FF = 1024
SWIGLU_LIMIT = 7.0
SWIGLU_ALPHA = 1.702
ROPE_BASE = 10000.0
EPS = 1e-6
LOG2_E = math.log2(math.e)

N_PROMPT = BATCH * SEQ
N_SAMPLE = DEC_BATCH * DEC_SEQ
N_TOK = N_PROMPT + N_SAMPLE
N_COND = 8
TM = 256
N_TILES = N_TOK // TM
PROMPT_TILES = N_PROMPT // TM
SAMPLE_TILES_PER_SEQ = DEC_SEQ // TM
IN_WIDTH = 2304
OFF_AU, OFF_AV, OFF_BX, OFF_BG, OFF_CQ, OFF_CK, OFF_CV, OFF_DQ, OFF_DK, OFF_DV = (
    0, 256, 512, 768, 1024, 1280, 1536, 1792, 2048, 2176)
TOPK_PAD = 8
ROW_PIECES = D_MODEL // 128
MOE_TILE = 256
MOE_TB = 4096
MOE_MAX_TILES = (N_TOK // MOE_TB) * (MOE_TB * TOP_K // MOE_TILE + N_EXPERTS)
MOE_XT_STRIDE = MOE_TILE + 8
MOE_RMW_BATCH = 4
MOE_ROWS_PER_BLOCK = MOE_TB * TOP_K + N_EXPERTS * MOE_TILE
MOE_PAD_PAIR = MOE_TB * TOP_K
MOE_PAIR_SHIFT = TOP_K.bit_length() - 1
assert 1 << MOE_PAIR_SHIFT == TOP_K
MOE_KIND_SKIP, MOE_KIND_FIRST, MOE_KIND_MIDDLE, MOE_KIND_LAST = 0, 1, 2, 3
LRU_CHUNK = 256
VMEM_LIMIT = 56 * 1024 * 1024


def _cparams(sem):
    return pltpu.CompilerParams(dimension_semantics=sem, vmem_limit_bytes=VMEM_LIMIT)


def _dot(a, b):
    return jnp.dot(a, b, preferred_element_type=F32)


def _dot_nt(a, b):
    return lax.dot_general(a, b, (((1,), (1,)), ((), ())), preferred_element_type=F32)


def _split_bf16(x):
    hi = x.astype(BF16)
    lo = (x - hi.astype(F32)).astype(BF16)
    return hi, lo


def _seg_rms_norm(x, seg_mat, g):
    hi, lo = _split_bf16(x * x)
    ms = _dot(hi, seg_mat) + _dot(lo, seg_mat)
    return x * lax.rsqrt(ms + EPS) * g


def _rope(x, cos_t, sin_t, nf):
    n = x.shape[-1]
    lane = lax.broadcasted_iota(jnp.int32, x.shape, 1)
    first = (lane & (2 * nf - 1)) < nf
    partner = jnp.where(first, pltpu.roll(x, n - nf, axis=1), pltpu.roll(x, nf, axis=1))
    return x * cos_t + partner * sin_t


def _softplus(x):
    return jnp.maximum(x, 0.0) + jnp.log1p(jnp.exp(-jnp.abs(x)))


def _mod_kernel(cond_ref, w_ref, b_ref, o_ref):
    c = cond_ref[...]
    s = c * jax.nn.sigmoid(c)
    o_ref[0] = _dot(s.astype(BF16), w_ref[0].astype(BF16)) + b_ref[0]


def _mod_call(cond, w, b):
    nb = 6
    return pl.pallas_call(
        _mod_kernel,
        grid=(DEPTH, nb),
        in_specs=[pl.BlockSpec((N_COND, D_MODEL), lambda l, j: (0, 0)),
                  pl.BlockSpec((1, D_MODEL, D_MODEL), lambda l, j: (l, 0, j)),
                  pl.BlockSpec((1, 1, D_MODEL), lambda l, j: (l, 0, j))],
        out_specs=pl.BlockSpec((1, N_COND, D_MODEL), lambda l, j: (l, 0, j)),
        out_shape=jax.ShapeDtypeStruct((DEPTH, N_COND, 6 * D_MODEL), F32),
        compiler_params=_cparams(("arbitrary", "arbitrary")),
        name="mod",
    )(cond, w, b.reshape(DEPTH, 1, 6 * D_MODEL))


def _inproj_kernel(seq_ref, rope_ref, x_ref, mod_ref, g1_ref, w_ref, s32_ref, s64_ref,
                   gqc_ref, gkc_ref, gqd_ref, gkd_ref, cosc_ref, sinc_ref, cosd_ref, sind_ref, o_ref):
    del seq_ref, rope_ref
    x = x_ref[...]
    xn = x * lax.rsqrt(jnp.mean(x * x, axis=-1, keepdims=True) + EPS) * g1_ref[...]
    mod = mod_ref[0]
    sh1 = mod[:, 0:D_MODEL]
    sc1 = mod[:, D_MODEL:2 * D_MODEL]
    h = xn * (1.0 + sc1) + sh1
    p = _dot(h.astype(BF16), w_ref[0])
    o_ref[:, OFF_AU:OFF_BX] = jax.nn.gelu(p[:, OFF_AU:OFF_BX])
    o_ref[:, OFF_BX:OFF_BG] = p[:, OFF_BX:OFF_BG]
    o_ref[:, OFF_BG:OFF_CQ] = jax.nn.gelu(p[:, OFF_BG:OFF_CQ])
    s32 = s32_ref[...]
    cosc = cosc_ref[...]
    sinc = sinc_ref[...]
    cq = _seg_rms_norm(p[:, OFF_CQ:OFF_CK], s32, gqc_ref[...])
    ck = _seg_rms_norm(p[:, OFF_CK:OFF_CV], s32, gkc_ref[...])
    o_ref[:, OFF_CQ:OFF_CK] = _rope(cq, cosc, sinc, DIFF_QK_DIM // 4)
    o_ref[:, OFF_CK:OFF_CV] = _rope(ck, cosc, sinc, DIFF_QK_DIM // 4)
    o_ref[:, OFF_CV:OFF_DQ] = p[:, OFF_CV:OFF_DQ]
    s64 = s64_ref[...]
    cosd = cosd_ref[...]
    sind = sind_ref[...]
    dq = _seg_rms_norm(p[:, OFF_DQ:OFF_DK], s64, gqd_ref[...])
    dk = _seg_rms_norm(p[:, OFF_DK:OFF_DV], s64[0:128, 0:128], gkd_ref[...])
    o_ref[:, OFF_DQ:OFF_DK] = _rope(dq, cosd, sind, HEAD_DIM // 4)
    o_ref[:, OFF_DK:OFF_DV] = _rope(dk, cosd[:, 0:128], sind[:, 0:128], HEAD_DIM // 4)
    o_ref[:, OFF_DV:IN_WIDTH] = p[:, OFF_DV:IN_WIDTH]


def _inproj_call(layer, seq_of_tile, rope_of_tile, x, mod3, g1, w_in, s32, s64, gqc, gkc, gqd, gkd,
                 cosc, sinc, cosd, sind):
    full = lambda shape: pl.BlockSpec(shape, lambda i, s, r: (0,) * len(shape))
    rope_spec = pl.BlockSpec((TM, GROUP_WIDTH), lambda i, s, r: (r[i], 0))
    grid_spec = pltpu.PrefetchScalarGridSpec(
        num_scalar_prefetch=2,
        grid=(N_TILES,),
        in_specs=[pl.BlockSpec((TM, D_MODEL), lambda i, s, r: (i, 0)),
                  pl.BlockSpec((1, 1, 6 * D_MODEL), lambda i, s, r: (s[i], 0, 0)),
                  full((1, D_MODEL)),
                  pl.BlockSpec((1, D_MODEL, IN_WIDTH), lambda i, s, r: (layer, 0, 0)),
                  full((GROUP_WIDTH, GROUP_WIDTH)),
                  full((GROUP_WIDTH, GROUP_WIDTH)),
                  full((1, GROUP_WIDTH)), full((1, GROUP_WIDTH)), full((1, GROUP_WIDTH)), full((1, 128)),
                  rope_spec, rope_spec, rope_spec, rope_spec],
        out_specs=pl.BlockSpec((TM, IN_WIDTH), lambda i, s, r: (i, 0)),
    )
    return pl.pallas_call(
        _inproj_kernel,
        grid_spec=grid_spec,
        out_shape=jax.ShapeDtypeStruct((N_TOK, IN_WIDTH), F32),
        compiler_params=_cparams(("arbitrary",)),
        name="inproj",
    )(seq_of_tile, rope_of_tile, x, mod3, g1, w_in, s32, s64, gqc, gkc, gqd, gkd, cosc, sinc, cosd, sind)


def _gmlp_kernel(u_ref, v_ref, s64_ref, g_ref, ws_ref, bias_ref, o_ref):
    vh = _seg_rms_norm(v_ref[...], s64_ref[...], g_ref[...]).astype(BF16)
    head = lax.broadcasted_iota(jnp.int32, (CHUNK, GROUP_WIDTH), 1) // HEAD_DIM
    for ch in range(TM // CHUNK):
        rows = slice(ch * CHUNK, (ch + 1) * CHUNK)
        vc = vh[rows]
        mixed = bias_ref[...]
        for h in range(N_GROUP_HEADS):
            mixed = mixed + jnp.where(head == h, _dot(ws_ref[h], vc), 0.0)
        o_ref[rows, :] = u_ref[rows, :] * mixed


def _gmlp_call(proj, s64, g, ws, bias):
    return pl.pallas_call(
        _gmlp_kernel,
        grid=(N_TILES,),
        in_specs=[pl.BlockSpec((TM, GROUP_WIDTH), lambda i: (i, OFF_AU // GROUP_WIDTH)),
                  pl.BlockSpec((TM, GROUP_WIDTH), lambda i: (i, OFF_AV // GROUP_WIDTH)),
                  pl.BlockSpec((GROUP_WIDTH, GROUP_WIDTH), lambda i: (0, 0)),
                  pl.BlockSpec((1, GROUP_WIDTH), lambda i: (0, 0)),
                  pl.BlockSpec((N_GROUP_HEADS, CHUNK, CHUNK), lambda i: (0, 0, 0)),
                  pl.BlockSpec((CHUNK, GROUP_WIDTH), lambda i: (0, 0))],
        out_specs=pl.BlockSpec((TM, GROUP_WIDTH), lambda i: (i, 0)),
        out_shape=jax.ShapeDtypeStruct((N_TOK, GROUP_WIDTH), F32),
        compiler_params=_cparams(("arbitrary",)),
        name="gmlp",
    )(proj, proj, s64, g, ws, bias)


def _scan_chunk(a, b, reverse):
    n = a.shape[0]
    row = lax.broadcasted_iota(jnp.int32, a.shape, 0)
    s = 1
    while s < n:
        if reverse:
            keep = row < n - s
            shift = n - s
        else:
            keep = row >= s
            shift = s
        a_prev = jnp.where(keep, pltpu.roll(a, shift, axis=0), 1.0)
        b_prev = jnp.where(keep, pltpu.roll(b, shift, axis=0), 0.0)
        b = a * b_prev + b
        a = a * a_prev
        s *= 2
    return a, b


def _lru_kernel(x_ref, g_ref, h0_ref, cw_ref, cb_ref, wcat_ref, bcat_ref, lam_ref, y_ref, st_ref,
                xpad, a_f, b_f, a_b, b_b, h_f, *, seq_len):
    nc = seq_len // LRU_CHUNK
    w = GROUP_WIDTH
    zeros8 = jnp.zeros((8, w), F32)
    xpad[0:8, :] = zeros8
    xpad[seq_len + 8:seq_len + 16, :] = zeros8
    xpad[8:seq_len + 8, :] = x_ref[...]
    sp = _softplus(-lam_ref[...])
    cw = cw_ref[...]
    cb = cb_ref[...]
    win_rows = LRU_CHUNK + 16

    def gates(c, carry):
        r0 = pl.multiple_of(c * LRU_CHUNK, LRU_CHUNK)
        win = xpad[pl.ds(r0, win_rows), :]
        inner = slice(8, 8 + LRU_CHUNK)
        xc = cb + pltpu.roll(win, 2, axis=0)[inner] * cw[0:1]
        xc = xc + pltpu.roll(win, 1, axis=0)[inner] * cw[1:2]
        xc = xc + win[inner] * cw[2:3]
        xc = xc + pltpu.roll(win, win_rows - 1, axis=0)[inner] * cw[3:4]
        sg = jax.nn.sigmoid(_dot(xc.astype(BF16), wcat_ref[...]) + bcat_ref[...])
        for d, (a_ref, b_ref) in enumerate(((a_f, b_f), (a_b, b_b))):
            r = sg[:, (2 * d) * w:(2 * d + 1) * w]
            i = sg[:, (2 * d + 1) * w:(2 * d + 2) * w]
            log_a = (-LRU_C * r) * sp[d:d + 1]
            a = jnp.exp(log_a)
            a_ref[pl.ds(r0, LRU_CHUNK), :] = a
            b_ref[pl.ds(r0, LRU_CHUNK), :] = jnp.sqrt(-jnp.tanh(log_a) * (a * a + 1.0)) * (i * xc)
        return carry

    lax.fori_loop(0, nc, gates, 0)

    def fwd(c, carry):
        r0 = pl.multiple_of(c * LRU_CHUNK, LRU_CHUNK)
        a_cum, h_loc = _scan_chunk(a_f[pl.ds(r0, LRU_CHUNK), :], b_f[pl.ds(r0, LRU_CHUNK), :], False)
        h = h_loc + a_cum * carry
        h_f[pl.ds(r0, LRU_CHUNK), :] = h
        return h[LRU_CHUNK - 1:LRU_CHUNK, :]

    s_f = lax.fori_loop(0, nc, fwd, h0_ref[0, 0:1, :])

    def bwd(k, carry):
        c = nc - 1 - k
        r0 = pl.multiple_of(c * LRU_CHUNK, LRU_CHUNK)
        a_cum, h_loc = _scan_chunk(a_b[pl.ds(r0, LRU_CHUNK), :], b_b[pl.ds(r0, LRU_CHUNK), :], True)
        h = h_loc + a_cum * carry
        y_ref[pl.ds(r0, LRU_CHUNK), :] = (h_f[pl.ds(r0, LRU_CHUNK), :] + h) * g_ref[pl.ds(r0, LRU_CHUNK), :]
        return h[0:1, :]

    s_b = lax.fori_loop(0, nc, bwd, h0_ref[0, 1:2, :])
    st_ref[0, 0:1, :] = s_f
    st_ref[0, 1:2, :] = s_b


def _lru_call(proj, h0, cw, cb, wcat, bcat, lam, *, seq_len, n_seq, row_block0):
    w = GROUP_WIDTH
    full = lambda shape: pl.BlockSpec(shape, lambda b: (0,) * len(shape))
    seq_block = lambda col: pl.BlockSpec((seq_len, w), lambda b: (b + row_block0, col))
    return pl.pallas_call(
        functools.partial(_lru_kernel, seq_len=seq_len),
        grid=(n_seq,),
        in_specs=[seq_block(OFF_BX // w), seq_block(OFF_BG // w),
                  pl.BlockSpec((1, 2, w), lambda b: (b, 0, 0)),
                  full((CONV_W, w)), full((1, w)), full((w, 4 * w)), full((1, 4 * w)), full((2, w))],
        out_specs=[pl.BlockSpec((seq_len, w), lambda b: (b, 0)),
                   pl.BlockSpec((1, 2, w), lambda b: (b, 0, 0))],
        out_shape=[jax.ShapeDtypeStruct((n_seq * seq_len, w), F32),
                   jax.ShapeDtypeStruct((n_seq, 2, w), F32)],
        scratch_shapes=[pltpu.VMEM((seq_len + 16, w), F32)] + [pltpu.VMEM((seq_len, w), F32)] * 5,
        compiler_params=_cparams(("arbitrary",)),
        name="lru_%d" % seq_len,
    )(proj, proj, h0, cw, cb, wcat, bcat, lam)


def _diff_lambda(ld_ref, lam_init):
    ld = ld_ref[...]
    l1 = jnp.sum(ld[0:1] * ld[1:2], axis=-1, keepdims=True)
    l2 = jnp.sum(ld[2:3] * ld[3:4], axis=-1, keepdims=True)
    return jnp.exp(l1) - jnp.exp(l2) + lam_init


def _diff_attn_body(q, keys, values, lam, s64, g, lam_init):
    tq = q.shape[0]
    lane = lax.broadcasted_iota(jnp.int32, (1, GROUP_WIDTH), 1)
    qs = q * (DIFF_QK_DIM ** -0.5 * LOG2_E)
    o = jnp.zeros((tq, GROUP_WIDTH), F32)
    for h in range(N_GROUP_HEADS):
        e_rows, inv = [], []
        for i in range(2):
            seg = h * 2 + i
            qm = jnp.where(lane // DIFF_QK_DIM == seg, qs, 0.0).astype(BF16)
            s = [_dot_nt(qm, k) for k in keys]
            m = functools.reduce(jnp.maximum, [jnp.max(x, axis=-1, keepdims=True) for x in s])
            e = [jnp.exp2(x - m) for x in s]
            den = functools.reduce(jnp.add, [jnp.sum(x, axis=-1, keepdims=True) for x in e])
            inv.append(1.0 / den)
            e_rows.append([x.astype(BF16) for x in e])
        both = None
        for e0, e1, v in zip(e_rows[0], e_rows[1], values):
            part = _dot(jnp.concatenate([e0, e1], axis=0), v)
            both = part if both is None else both + part
        oh = both[:tq] * inv[0] - lam * (both[tq:] * inv[1])
        o = o + jnp.where(lane // HEAD_DIM == h, oh, 0.0)
    return _seg_rms_norm(o, s64, g) * (1.0 - lam_init)


def _diff_prompt_kernel(q_ref, k_ref, v_ref, ld_ref, s64_ref, g_ref, o_ref, *, lam_init):
    lam = _diff_lambda(ld_ref, lam_init)
    o_ref[...] = _diff_attn_body(q_ref[...], [k_ref[...].astype(BF16)], [v_ref[...].astype(BF16)],
                                 lam, s64_ref[...], g_ref[...], lam_init)


def _diff_sample_kernel(q_ref, k_ref, v_ref, ck_ref, cv_ref, ld_ref, s64_ref, g_ref, o_ref, *, lam_init):
    lam = _diff_lambda(ld_ref, lam_init)
    keys = [ck_ref[0, 0].astype(BF16), k_ref[...].astype(BF16)]
    values = [cv_ref[0, 0].astype(BF16), v_ref[...].astype(BF16)]
    o_ref[...] = _diff_attn_body(q_ref[...], keys, values, lam, s64_ref[...], g_ref[...], lam_init)


def _diff_prompt_call(proj, ld, s64, g, lam_init):
    w = GROUP_WIDTH
    full = lambda shape: pl.BlockSpec(shape, lambda b: (0,) * len(shape))
    return pl.pallas_call(
        functools.partial(_diff_prompt_kernel, lam_init=lam_init),
        grid=(BATCH,),
        in_specs=[pl.BlockSpec((SEQ, w), lambda b: (b, OFF_CQ // w)),
                  pl.BlockSpec((SEQ, w), lambda b: (b, OFF_CK // w)),
                  pl.BlockSpec((SEQ, w), lambda b: (b, OFF_CV // w)),
                  full((4, DIFF_QK_DIM)), full((w, w)), full((1, w))],
        out_specs=pl.BlockSpec((SEQ, w), lambda b: (b, 0)),
        out_shape=jax.ShapeDtypeStruct((N_PROMPT, w), F32),
        compiler_params=_cparams(("arbitrary",)),
        name="diff_prompt",
    )(proj, proj, proj, ld, s64, g)


def _diff_sample_call(layer, proj, ctx_k, ctx_v, ld, s64, g, lam_init):
    w = GROUP_WIDTH
    tq = 256
    nq = DEC_SEQ // tq
    full = lambda shape: pl.BlockSpec(shape, lambda b, i: (0,) * len(shape))
    seq_block0 = N_PROMPT // DEC_SEQ
    return pl.pallas_call(
        functools.partial(_diff_sample_kernel, lam_init=lam_init),
        grid=(DEC_BATCH, nq),
        in_specs=[pl.BlockSpec((tq, w), lambda b, i: (N_PROMPT // tq + b * nq + i, OFF_CQ // w)),
                  pl.BlockSpec((DEC_SEQ, w), lambda b, i: (seq_block0 + b, OFF_CK // w)),
                  pl.BlockSpec((DEC_SEQ, w), lambda b, i: (seq_block0 + b, OFF_CV // w)),
                  pl.BlockSpec((1, 1, PAST_LEN, w), lambda b, i: (b, layer, 0, 0)),
                  pl.BlockSpec((1, 1, PAST_LEN, w), lambda b, i: (b, layer, 0, 0)),
                  full((4, DIFF_QK_DIM)), full((w, w)), full((1, w))],
        out_specs=pl.BlockSpec((tq, w), lambda b, i: (b * nq + i, 0)),
        out_shape=jax.ShapeDtypeStruct((N_SAMPLE, w), F32),
        compiler_params=_cparams(("arbitrary", "arbitrary")),
        name="diff_sample",
    )(proj, proj, proj, ctx_k, ctx_v, ld, s64, g)


def _sink_attn_body(q, keys, values, masks, sink_ref):
    tq = q.shape[0]
    lane = lax.broadcasted_iota(jnp.int32, (1, 128), 1)
    qs = q * (HEAD_DIM ** -0.5 * LOG2_E)
    head_rows = lax.broadcasted_iota(jnp.int32, (SWA_HEADS * tq, 1), 0) // tq
    stacked, sink = [], jnp.zeros((SWA_HEADS * tq, 1), F32)
    for grp in range(SWA_GROUPS):
        for kh in range(SWA_KV_HEADS):
            stacked.append(jnp.where(lane // HEAD_DIM == kh, qs[:, grp * 128:(grp + 1) * 128], 0.0).astype(BF16))
            sink = jnp.where(head_rows == len(stacked) - 1, sink_ref[kh * SWA_GROUPS + grp] * LOG2_E, sink)
    q4 = jnp.concatenate(stacked, axis=0)
    s = []
    for k, msk in zip(keys, masks):
        x = _dot_nt(q4, k)
        s.append(x if msk is None else jnp.where(msk, x, -jnp.inf))
    m = functools.reduce(jnp.maximum, [jnp.max(x, axis=-1, keepdims=True) for x in s])
    m = jnp.maximum(m, sink)
    e = [jnp.exp2(x - m) for x in s]
    den = functools.reduce(jnp.add, [jnp.sum(x, axis=-1, keepdims=True) for x in e]) + jnp.exp2(sink - m)
    o4 = None
    for x, v in zip(e, values):
        part = _dot(x.astype(BF16), v)
        o4 = part if o4 is None else o4 + part
    o4 = o4 * (1.0 / den)
    outs = []
    for grp in range(SWA_GROUPS):
        r0 = grp * SWA_KV_HEADS * tq
        outs.append(jnp.where(lane // HEAD_DIM == 0, o4[r0:r0 + tq], o4[r0 + tq:r0 + 2 * tq]))
    return outs


def _swa_prompt_kernel(sink_ref, q_ref, k_ref, v_ref, o_ref):
    outs = _sink_attn_body(q_ref[...], [k_ref[...].astype(BF16)], [v_ref[...].astype(BF16)], [None], sink_ref)
    for grp in range(SWA_GROUPS):
        o_ref[:, grp * 128:(grp + 1) * 128] = outs[grp]


def _swa_sample_kernel(sink_ref, q_ref, kp_ref, kc_ref, kn_ref, vp_ref, vc_ref, vn_ref, ck_ref, cv_ref, o_ref):
    n = pl.program_id(1)
    nb = pl.num_programs(1)
    r = lax.broadcasted_iota(jnp.int32, (SWA_HEADS * WINDOW, WINDOW), 0) & (WINDOW - 1)
    c = lax.broadcasted_iota(jnp.int32, (SWA_HEADS * WINDOW, WINDOW), 1)
    mask_prev = c >= r + jnp.where(n > 0, 0, WINDOW)
    mask_next = c <= r - jnp.where(n < nb - 1, 0, WINDOW)
    keys = [ck_ref[0, 0].astype(BF16), kp_ref[...].astype(BF16), kc_ref[...].astype(BF16), kn_ref[...].astype(BF16)]
    values = [cv_ref[0, 0].astype(BF16), vp_ref[...].astype(BF16), vc_ref[...].astype(BF16), vn_ref[...].astype(BF16)]
    outs = _sink_attn_body(q_ref[...], keys, values, [None, mask_prev, None, mask_next], sink_ref)
    for grp in range(SWA_GROUPS):
        o_ref[:, grp * 128:(grp + 1) * 128] = outs[grp]


def _swa_prompt_call(sink, proj):
    w = GROUP_WIDTH
    return pl.pallas_call(
        _swa_prompt_kernel,
        grid=(BATCH,),
        in_specs=[pl.BlockSpec(memory_space=pltpu.SMEM),
                  pl.BlockSpec((SEQ, w), lambda b: (b, OFF_DQ // w)),
                  pl.BlockSpec((SEQ, 128), lambda b: (b, OFF_DK // 128)),
                  pl.BlockSpec((SEQ, 128), lambda b: (b, OFF_DV // 128))],
        out_specs=pl.BlockSpec((SEQ, w), lambda b: (b, 0)),
        out_shape=jax.ShapeDtypeStruct((N_PROMPT, w), F32),
        compiler_params=_cparams(("arbitrary",)),
        name="swa_prompt",
    )(sink, proj, proj, proj)


def _swa_sample_call(layer, sink, proj, ctx_k, ctx_v):
    w = GROUP_WIDTH
    tq = WINDOW
    nq = DEC_SEQ // tq
    row0 = N_PROMPT // tq

    def kv_spec(col, delta):
        def index(b, i):
            j = jnp.clip(i + delta, 0, nq - 1)
            return (row0 + b * nq + j, col)
        return pl.BlockSpec((tq, 128), index)

    ctx_spec = pl.BlockSpec((1, 1, PAST_LEN, 128), lambda b, i: (b, layer, 0, 0))
    return pl.pallas_call(
        _swa_sample_kernel,
        grid=(DEC_BATCH, nq),
        in_specs=[pl.BlockSpec(memory_space=pltpu.SMEM),
                  pl.BlockSpec((tq, w), lambda b, i: (row0 + b * nq + i, OFF_DQ // w)),
                  kv_spec(OFF_DK // 128, -1), kv_spec(OFF_DK // 128, 0), kv_spec(OFF_DK // 128, 1),
                  kv_spec(OFF_DV // 128, -1), kv_spec(OFF_DV // 128, 0), kv_spec(OFF_DV // 128, 1),
                  ctx_spec, ctx_spec],
        out_specs=pl.BlockSpec((tq, w), lambda b, i: (b * nq + i, 0)),
        out_shape=jax.ShapeDtypeStruct((N_SAMPLE, w), F32),
        compiler_params=_cparams(("arbitrary", "arbitrary")),
        name="swa_sample",
    )(sink, proj, proj, proj, proj, proj, proj, proj, ctx_k, ctx_v)


def _outproj_kernel(seq_ref, ya_ref, ybp_ref, ybs_ref, ycp_ref, ycs_ref, ydp_ref, yds_ref, x_ref, mod_ref, w_ref,
                    g2_ref, rw_ref, rb_ref, tri_ref,
                    x1_ref, h2r_ref, topi_ref, topw_ref, rank_ref, count_ref, cnt_ref):
    del seq_ref
    w = GROUP_WIDTH
    is_prompt = pl.program_id(0) < PROMPT_TILES
    yb = jnp.where(is_prompt, ybp_ref[...], ybs_ref[...])
    yc = jnp.where(is_prompt, ycp_ref[...], ycs_ref[...])
    yd = jnp.where(is_prompt, ydp_ref[...], yds_ref[...])
    mix = _dot(ya_ref[...].astype(BF16), w_ref[0, 0:w, :])
    mix = mix + _dot(yb.astype(BF16), w_ref[0, w:2 * w, :])
    mix = mix + _dot(yc.astype(BF16), w_ref[0, 2 * w:3 * w, :])
    mix = mix + _dot(yd.astype(BF16), w_ref[0, 3 * w:4 * w, :])
    mod = mod_ref[0]
    g1 = mod[:, 2 * D_MODEL:3 * D_MODEL]
    sh2 = mod[:, 3 * D_MODEL:4 * D_MODEL]
    sc2 = mod[:, 4 * D_MODEL:5 * D_MODEL]
    x1 = x_ref[...] + g1 * mix
    x1_ref[...] = x1
    xn = x1 * lax.rsqrt(jnp.mean(x1 * x1, axis=-1, keepdims=True) + EPS) * g2_ref[...]
    h2 = xn * (1.0 + sc2) + sh2
    for j in range(ROW_PIECES):
        h2r_ref[pl.ds(j, TM, stride=ROW_PIECES), :] = h2[:, j * 128:(j + 1) * 128]
    h_hi, h_lo = _split_bf16(h2)
    r_hi, r_lo = _split_bf16(rw_ref[...])
    logits = _dot(h_hi, r_hi) + (_dot(h_hi, r_lo) + _dot(h_lo, r_hi)) + rb_ref[...]
    lane = lax.broadcasted_iota(jnp.int32, logits.shape, 1)
    work = logits
    picks = []
    for _ in range(TOP_K):
        m = jnp.max(work, axis=-1, keepdims=True)
        idx = jnp.min(jnp.where(work == m, lane, N_EXPERTS), axis=-1, keepdims=True)
        picks.append((m, idx))
        work = jnp.where(lane == idx, -jnp.inf, work)
    top = picks[0][0]
    ex = [jnp.exp(m - top) for m, _ in picks]
    inv = 1.0 / functools.reduce(jnp.add, ex)
    @pl.when(pl.program_id(0) % (MOE_TB // TM) == 0)
    def _():
        cnt_ref[...] = jnp.zeros_like(cnt_ref)

    sel = jnp.zeros(logits.shape, F32)
    for _, idx in picks:
        sel = sel + jnp.where(lane == idx, 1.0, 0.0)
    csum = _dot(tri_ref[...], sel.astype(BF16))
    before = cnt_ref[...] + csum - sel
    cnt_ref[...] += csum[TM - 1:TM, :]
    count_ref[0] = cnt_ref[...].astype(jnp.int32)
    slot = lax.broadcasted_iota(jnp.int32, topi_ref.shape, 1)
    top_i = jnp.zeros(topi_ref.shape, jnp.int32)
    top_w = jnp.zeros(topw_ref.shape, F32)
    rank = jnp.zeros(rank_ref.shape, jnp.int32)
    for k, (e, (_, idx)) in enumerate(zip(ex, picks)):
        rank_k = jnp.sum(jnp.where(lane == idx, before, 0.0), axis=-1, keepdims=True).astype(jnp.int32)
        top_i = jnp.where(slot == k, idx, top_i)
        top_w = jnp.where(slot == k, e * inv, top_w)
        rank = jnp.where(slot == k, rank_k, rank)
    topi_ref[...] = top_i
    topw_ref[...] = top_w
    rank_ref[...] = rank


def _outproj_call(layer, seq_of_tile, ya, yb, yc, yd, x, mod3, w_out, g2, rw, rb, tri):
    w = GROUP_WIDTH
    full = lambda shape: pl.BlockSpec(shape, lambda i, s: (0,) * len(shape))
    tile = lambda width: pl.BlockSpec((TM, width), lambda i, s: (i, 0))
    prompt_tile = pl.BlockSpec((TM, w), lambda i, s: (jnp.minimum(i, PROMPT_TILES - 1), 0))
    sample_tile = pl.BlockSpec((TM, w), lambda i, s: (jnp.maximum(i - PROMPT_TILES, 0), 0))
    tiles_per_block = MOE_TB // TM
    return pl.pallas_call(
        _outproj_kernel,
        grid_spec=pltpu.PrefetchScalarGridSpec(
            num_scalar_prefetch=1,
            grid=(N_TILES,),
            in_specs=[tile(w), prompt_tile, sample_tile, prompt_tile, sample_tile, prompt_tile, sample_tile,
                      tile(D_MODEL),
                      pl.BlockSpec((1, 1, 6 * D_MODEL), lambda i, s: (s[i], 0, 0)),
                      pl.BlockSpec((1, D_MODEL, D_MODEL), lambda i, s: (layer, 0, 0)), full((1, D_MODEL)),
                      full((D_MODEL, N_EXPERTS)), full((1, N_EXPERTS)), full((TM, TM))],
            out_specs=[tile(D_MODEL), pl.BlockSpec((TM * ROW_PIECES, 128), lambda i, s: (i, 0)),
                       tile(TOPK_PAD), tile(TOPK_PAD), tile(TOPK_PAD),
                       pl.BlockSpec((1, 1, N_EXPERTS), lambda i, s: (i // tiles_per_block, 0, 0))],
            scratch_shapes=[pltpu.VMEM((1, N_EXPERTS), F32)],
        ),
        out_shape=[jax.ShapeDtypeStruct((N_TOK, D_MODEL), F32),
                   jax.ShapeDtypeStruct((N_TOK * ROW_PIECES, 128), F32),
                   jax.ShapeDtypeStruct((N_TOK, TOPK_PAD), jnp.int32),
                   jax.ShapeDtypeStruct((N_TOK, TOPK_PAD), F32),
                   jax.ShapeDtypeStruct((N_TOK, TOPK_PAD), jnp.int32),
                   jax.ShapeDtypeStruct((N_TOK // MOE_TB, 1, N_EXPERTS), jnp.int32)],
        compiler_params=_cparams(("arbitrary",)),
        name="outproj",
    )(seq_of_tile, ya, *yb, *yc, *yd, x, mod3, w_out, g2, rw, rb, tri)


def _moe_kernel(expert_ref, block_ref, kind_ref, local_ref, dest_ref, topw_ref, gstart_ref, count_ref,
                src_ref, wgu_ref, bgu_ref, wd_ref, bd_ref, acc_ref, pair_ref, xt_even, xt_odd, y3_even, y3_odd):
    del expert_ref, block_ref
    i = pl.program_id(0)
    kind = kind_ref[i]

    def token_of(pair):
        return (pair >> MOE_PAIR_SHIFT) & (MOE_TB - 1)

    def gather(tile, xt_ref, lo=0, hi=MOE_TILE):
        row0 = local_ref[tile]
        for m in range(lo, hi):
            t = token_of(pair_ref[row0 + m])
            slab = src_ref[pl.ds(pl.multiple_of(t * ROW_PIECES, ROW_PIECES), ROW_PIECES), :]
            xt_ref[pl.ds(m, ROW_PIECES, stride=MOE_XT_STRIDE), :] = slab

    def scatter(tile, y3_ref, lo=0, hi=MOE_TILE):
        row0 = local_ref[tile]
        for m0 in range(lo, hi, MOE_RMW_BATCH):
            rows = range(m0, m0 + MOE_RMW_BATCH)
            pairs = [pair_ref[row0 + m] for m in rows]
            toks = [token_of(p) for p in pairs]
            vals = [acc_ref[t] + topw_ref[0, 0, p] * y3_ref[m // 8, pl.ds(m % 8, 8, stride=8), :]
                    for t, p, m in zip(toks, pairs, rows)]
            for t, v in reversed(list(zip(toks, vals))):
                acc_ref[t] = v

    def compute(xt_ref, y3_ref):
        x = jnp.concatenate([xt_ref[j * MOE_XT_STRIDE:j * MOE_XT_STRIDE + MOE_TILE, :] for j in range(ROW_PIECES)],
                            axis=1).astype(BF16)
        gu = _dot(x, wgu_ref[0, 0]) + bgu_ref[0, 0]
        gate = jnp.minimum(gu[:, :D_FF], SWIGLU_LIMIT)
        up = jnp.clip(gu[:, D_FF:], -SWIGLU_LIMIT, SWIGLU_LIMIT)
        act = (up + 1.0) * gate * jax.nn.sigmoid(SWIGLU_ALPHA * gate)
        y = _dot(act.astype(BF16), wd_ref[0, 0]) + bd_ref[0, 0]
        for j in range(ROW_PIECES):
            y3_ref[:, j * 8:(j + 1) * 8, :] = y[:, j * 128:(j + 1) * 128].reshape(MOE_TILE // 8, 8, 128)

    @pl.when(kind == MOE_KIND_FIRST)
    def _():
        def zero(c, carry):
            acc_ref[pl.ds(pl.multiple_of(c * MOE_TILE, MOE_TILE), MOE_TILE)] = jnp.zeros((MOE_TILE, 8, 128), F32)
            return carry
        lax.fori_loop(0, MOE_TB // MOE_TILE, zero, 0)

        def pad_group(e, carry):
            cnt = count_ref[0, 0, e]
            first_pad = gstart_ref[0, 0, e] + cnt
            n_pad = (-cnt) & (MOE_TILE - 1)

            def pad_row(r, c2):
                pair_ref[first_pad + r] = MOE_PAD_PAIR
                return c2
            lax.fori_loop(0, n_pad, pad_row, 0)
            return carry
        lax.fori_loop(0, N_EXPERTS, pad_group, 0)

        def place(t, carry):
            for k in range(TOP_K):
                pair = t * TOP_K + k
                pair_ref[dest_ref[0, 0, pair]] = pair
            return carry
        lax.fori_loop(0, MOE_TB, place, 0)

    step = kind * 2 + (i & 1)
    for parity, (xt_cur, xt_nxt, y3_cur, y3_prv) in enumerate(((xt_even, xt_odd, y3_even, y3_odd),
                                                               (xt_odd, xt_even, y3_odd, y3_even))):
        @pl.when(step == MOE_KIND_FIRST * 2 + parity)
        def _():
            gather(i, xt_cur)
            gather(i + 1, xt_nxt)
            compute(xt_cur, y3_cur)

        @pl.when(step == MOE_KIND_MIDDLE * 2 + parity)
        def _():
            gather(i + 1, xt_nxt)
            compute(xt_cur, y3_cur)
            scatter(i - 1, y3_prv)

        @pl.when(step == MOE_KIND_LAST * 2 + parity)
        def _():
            compute(xt_cur, y3_cur)
            scatter(i - 1, y3_prv)
            scatter(i, y3_cur)


def _moe_call(layer, tile_expert, tile_block, tile_flags, tile_local, dest, topw, gstart, counts, h2_rows,
              wgu, bgu, wd, bd):
    pieces = ROW_PIECES
    once = pl.Buffered(1)
    n_pairs = MOE_TB * TOP_K

    def smem_block(width):
        return pl.BlockSpec((1, 1, width), lambda i, e, b, f, lo: (b[i], 0, 0), memory_space=pltpu.SMEM,
                            pipeline_mode=once)

    def expert_block(rows, cols):
        return pl.BlockSpec((1, 1, rows, cols), lambda i, e, b, f, lo: (layer, e[i], 0, 0))

    return pl.pallas_call(
        _moe_kernel,
        grid_spec=pltpu.PrefetchScalarGridSpec(
            num_scalar_prefetch=4,
            grid=(MOE_MAX_TILES,),
            in_specs=[smem_block(n_pairs), smem_block(n_pairs + 128), smem_block(N_EXPERTS), smem_block(N_EXPERTS),
                      pl.BlockSpec((MOE_TB * pieces, 128), lambda i, e, b, f, lo: (b[i], 0), pipeline_mode=once),
                      expert_block(D_MODEL, 2 * D_FF), expert_block(1, 2 * D_FF),
                      expert_block(D_FF, D_MODEL), expert_block(1, D_MODEL)],
            out_specs=pl.BlockSpec((MOE_TB, 8, 128), lambda i, e, b, f, lo: (b[i], 0, 0), pipeline_mode=once),
            scratch_shapes=[pltpu.SMEM((MOE_ROWS_PER_BLOCK,), jnp.int32),
                            pltpu.VMEM((pieces * MOE_XT_STRIDE, 128), F32),
                            pltpu.VMEM((pieces * MOE_XT_STRIDE, 128), F32),
                            pltpu.VMEM((MOE_TILE // 8, 8 * pieces, 128), F32),
                            pltpu.VMEM((MOE_TILE // 8, 8 * pieces, 128), F32)],
        ),
        out_shape=jax.ShapeDtypeStruct((N_TOK, 8, 128), F32),
        compiler_params=_cparams(("arbitrary",)),
        name="moe",
    )(tile_expert, tile_block, tile_flags, tile_local, dest, topw, gstart, counts, h2_rows, wgu, bgu, wd, bd)


def _moe_routing(top_i, top_w, rank, counts):
    nb = N_TOK // MOE_TB
    n_groups = nb * N_EXPERTS
    counts = counts.reshape(nb, N_EXPERTS)
    padded = ((counts + MOE_TILE - 1) // MOE_TILE) * MOE_TILE
    group_end = jnp.cumsum(padded.reshape(-1))
    group_start = (group_end - padded.reshape(-1)).reshape(nb, N_EXPERTS)
    block_row0 = group_start[:, 0]
    gstart_local = group_start - block_row0[:, None]
    experts = jnp.arange(N_EXPERTS, dtype=jnp.int32)
    picked = top_i.reshape(nb, MOE_TB, TOP_K, 1) == experts
    dest = jnp.sum(jnp.where(picked, gstart_local[:, None, None, :], 0), axis=-1) + rank.reshape(nb, MOE_TB, TOP_K)
    dest = dest.reshape(nb, 1, MOE_TB * TOP_K).astype(jnp.int32)
    topw = jnp.concatenate([top_w.reshape(nb, 1, MOE_TB * TOP_K), jnp.zeros((nb, 1, 128), F32)], axis=-1)
    tile_start = jnp.arange(MOE_MAX_TILES, dtype=jnp.int32) * MOE_TILE
    tile_group = jnp.sum((group_end[None, :] <= tile_start[:, None]).astype(jnp.int32), axis=1)
    tile_group = jnp.minimum(tile_group, n_groups - 1)
    valid = tile_start < group_end[-1]
    tile_block = tile_group // N_EXPERTS
    new_block = tile_block[1:] != tile_block[:-1]
    first = jnp.concatenate([jnp.ones((1,), bool), new_block])
    last = jnp.concatenate([new_block | ~valid[1:], jnp.ones((1,), bool)])
    kinds = jnp.where(first, MOE_KIND_FIRST, jnp.where(last, MOE_KIND_LAST, MOE_KIND_MIDDLE))
    flags = jnp.where(valid, kinds, MOE_KIND_SKIP).astype(jnp.int32)
    blocks = jnp.arange(nb, dtype=jnp.int32)
    tile_row0 = jnp.sum(jnp.where(tile_block[:, None] == blocks[None, :], block_row0[None, :], 0), axis=1)
    tile_local = (tile_start - tile_row0).astype(jnp.int32)
    return (tile_group % N_EXPERTS, tile_block, flags, tile_local, dest, topw,
            gstart_local.reshape(nb, 1, N_EXPERTS).astype(jnp.int32), counts.reshape(nb, 1, N_EXPERTS))


def _residual_kernel(seq_ref, x1_ref, moe_ref, mod_ref, o_ref):
    del seq_ref
    g2 = mod_ref[0][:, 5 * D_MODEL:6 * D_MODEL]
    moe = jnp.concatenate([moe_ref[pl.ds(j, TM, stride=ROW_PIECES), :] for j in range(ROW_PIECES)], axis=1)
    o_ref[...] = x1_ref[...] + g2 * moe


def _residual_call(seq_of_tile, x1, moe_rows, mod3):
    tile = pl.BlockSpec((TM, D_MODEL), lambda i, s: (i, 0))
    return pl.pallas_call(
        _residual_kernel,
        grid_spec=pltpu.PrefetchScalarGridSpec(
            num_scalar_prefetch=1,
            grid=(N_TILES,),
            in_specs=[tile, pl.BlockSpec((TM * ROW_PIECES, 128), lambda i, s: (i, 0)),
                      pl.BlockSpec((1, 1, 6 * D_MODEL), lambda i, s: (s[i], 0, 0))],
            out_specs=tile,
        ),
        out_shape=jax.ShapeDtypeStruct((N_TOK, D_MODEL), F32),
        compiler_params=_cparams(("arbitrary",)),
        name="residual",
    )(seq_of_tile, x1, moe_rows, mod3)


def _segment_matrix(width, seg):
    idx = np.arange(width) // seg
    return jnp.asarray((idx[:, None] == idx[None, :]).astype(np.float32) / seg, BF16)


def _rope_tables(rot_dim, width):
    rows = DEC_SEQ // GRID_W
    nf = rot_dim // 4
    f32 = np.float32
    inv = (f32(1.0) / (f32(ROPE_BASE) ** (np.arange(nf, dtype=f32) / f32(nf)))).astype(f32)
    row = np.repeat(np.arange(rows, dtype=f32), GRID_W)
    col = np.tile(np.arange(GRID_W, dtype=f32), rows)
    ang = np.stack([row[:, None] * inv, col[:, None] * inv], axis=1).astype(f32)
    cos, sin = np.cos(ang).astype(f32), np.sin(ang).astype(f32)
    cos_r = np.concatenate([cos[:, 0], cos[:, 0], cos[:, 1], cos[:, 1]], axis=-1)
    sin_r = np.concatenate([-sin[:, 0], sin[:, 0], -sin[:, 1], sin[:, 1]], axis=-1)
    reps = width // rot_dim
    cos_t = np.concatenate([np.ones((TM, width), f32), np.tile(cos_r, (1, reps))], axis=0)
    sin_t = np.concatenate([np.zeros((TM, width), f32), np.tile(sin_r, (1, reps))], axis=0)
    return jnp.asarray(cos_t), jnp.asarray(sin_t)


def _block_diag(wb):
    nb, n = wb.shape[-3], wb.shape[-1]
    rows = wb.reshape(wb.shape[:-3] + (nb * n, n))
    tiled = jnp.tile(rows, (1,) * (wb.ndim - 2) + (nb,))
    blk = np.arange(nb * n) // n
    return jnp.where(jnp.asarray(blk[:, None] == blk[None, :]), tiled, 0.0)


_DQ_HEADS = [kh * SWA_GROUPS + g for g in range(SWA_GROUPS) for kh in range(SWA_KV_HEADS)]


def kernel(x_prompt, x_sample, c, cache_diff_k, cache_diff_v, cache_swa_k, cache_swa_v, state_lru, c_ctx, mod_w, mod_b, norm1_g, norm2_g, w_in, w_out, mlp_vnorm_g, mlp_ws, mlp_bs, lru_conv_w, lru_conv_b, lru_wa, lru_ba, lru_wx, lru_bx, lru_lambda, diff_qnorm_g, diff_knorm_g, diff_lambda, diff_subln_g, swa_qnorm_g, swa_knorm_g, swa_sink, router_w, router_b, moe_w_gu, moe_b_gu, moe_w_down, moe_b_down):
    params = dict(locals())
    consts = _constants()
    prep = _prepare(params, consts)
    x = jnp.concatenate([x_prompt.reshape(N_PROMPT, D_MODEL), x_sample.reshape(N_SAMPLE, D_MODEL)], axis=0)
    dk_l, dv_l, sk_l, sv_l, lru_l = [], [], [], [], []
    for l in range(DEPTH):
        st = _mixer_stage(x, params, prep, l, consts)
        x = _ffn_stage(x, st, params, prep, l, consts)
        pp = st["proj"][:N_PROMPT]
        dk_l.append(pp[:, OFF_CK:OFF_CV].reshape(BATCH, SEQ, N_GROUP_HEADS, 2, DIFF_QK_DIM))
        dv_l.append(pp[:, OFF_CV:OFF_DQ].reshape(BATCH, SEQ, N_GROUP_HEADS, HEAD_DIM))
        sk_l.append(pp[:, OFF_DK:OFF_DV].reshape(BATCH, SEQ, SWA_KV_HEADS, HEAD_DIM))
        sv_l.append(pp[:, OFF_DV:IN_WIDTH].reshape(BATCH, SEQ, SWA_KV_HEADS, HEAD_DIM))
        lru_l.append(st["st_p"])

    y_p = x[:N_PROMPT].reshape(BATCH, SEQ, D_MODEL)
    y_s = x[N_PROMPT:].reshape(DEC_BATCH, DEC_SEQ, D_MODEL)
    return (y_p, y_s, jnp.stack(dk_l, axis=1), jnp.stack(dv_l, axis=1), jnp.stack(sk_l, axis=1),
            jnp.stack(sv_l, axis=1), jnp.stack(lru_l, axis=1))


def _constants():
    w = GROUP_WIDTH
    tiles = np.arange(N_TILES)
    seq_np = np.where(tiles < PROMPT_TILES, 0, 1 + (tiles - PROMPT_TILES) // SAMPLE_TILES_PER_SEQ)
    rope_np = np.where(tiles < PROMPT_TILES, 0, 1 + (tiles - PROMPT_TILES) % SAMPLE_TILES_PER_SEQ)
    seq_of_tile = [jnp.asarray(seq_np + l * N_COND, jnp.int32) for l in range(DEPTH)]
    rope_of_tile = jnp.asarray(rope_np, jnp.int32)

    s32 = _segment_matrix(w, DIFF_QK_DIM)
    s64 = _segment_matrix(w, HEAD_DIM)
    cosc, sinc = _rope_tables(DIFF_QK_DIM, w)
    cosd, sind = _rope_tables(HEAD_DIM, w)
    dq_cols = np.concatenate([np.arange(h * HEAD_DIM, (h + 1) * HEAD_DIM) for h in _DQ_HEADS])
    tri = jnp.asarray(np.tril(np.ones((TM, TM), np.float32)), BF16)
    return dict(seq_of_tile=seq_of_tile, rope_of_tile=rope_of_tile, tri=tri, s32=s32, s64=s64,
                cosc=cosc, sinc=sinc, cosd=cosd, sind=sind, dq_cols=dq_cols)


def _prepare(params, consts):
    w = GROUP_WIDTH
    p = params
    dq = consts["dq_cols"]
    heads = lambda g, reps: jnp.tile(g.reshape(DEPTH, 1, -1), (1, 1, reps))
    w_in, w_out = p["w_in"], p["w_out"]
    lru_w = jnp.stack([p["lru_wa"][:, 0], p["lru_wx"][:, 0], p["lru_wa"][:, 1], p["lru_wx"][:, 1]], axis=1)
    lru_b = jnp.stack([p["lru_ba"][:, 0], p["lru_bx"][:, 0], p["lru_ba"][:, 1], p["lru_bx"][:, 1]], axis=1)
    cond = jnp.concatenate([p["c_ctx"][None], p["c"], jnp.zeros((N_COND - 1 - DEC_BATCH, D_MODEL), F32)], axis=0)
    return dict(
        mod3=_mod_call(cond, p["mod_w"], p["mod_b"]).reshape(DEPTH * N_COND, 1, 6 * D_MODEL),
        w_in=jnp.concatenate([w_in[..., :OFF_DQ], w_in[..., OFF_DQ:OFF_DK][..., dq], w_in[..., OFF_DK:]],
                             axis=-1).astype(BF16),
        w_out=jnp.concatenate([w_out[:, :3 * w], w_out[:, 3 * w:][:, dq]], axis=1).astype(BF16),
        gqc=heads(p["diff_qnorm_g"], N_GROUP_HEADS), gkc=heads(p["diff_knorm_g"], N_GROUP_HEADS),
        gqd=heads(p["swa_qnorm_g"], SWA_HEADS), gkd=heads(p["swa_knorm_g"], SWA_KV_HEADS),
        g_sub=heads(p["diff_subln_g"], N_GROUP_HEADS), g_mlp=p["mlp_vnorm_g"].reshape(DEPTH, 1, w),
        mlp_bias=jnp.repeat(jnp.swapaxes(p["mlp_bs"], 1, 2), HEAD_DIM, axis=2),
        mlp_ws=p["mlp_ws"].astype(BF16),
        wcat=jnp.swapaxes(_block_diag(lru_w), 1, 2).reshape(DEPTH, w, 4 * w).astype(BF16),
        bcat=lru_b.reshape(DEPTH, 1, 4 * w),
        conv_b=p["lru_conv_b"].reshape(DEPTH, 1, w),
        g1=p["norm1_g"].reshape(DEPTH, 1, D_MODEL), g2=p["norm2_g"].reshape(DEPTH, 1, D_MODEL),
        router_b=p["router_b"].reshape(DEPTH, 1, N_EXPERTS),
        moe_w_gu=p["moe_w_gu"].astype(BF16), moe_w_down=p["moe_w_down"].astype(BF16),
        moe_b_gu=p["moe_b_gu"].reshape(DEPTH, N_EXPERTS, 1, 2 * D_FF),
        moe_b_down=p["moe_b_down"].reshape(DEPTH, N_EXPERTS, 1, D_MODEL),
        ctx_diff_k=p["cache_diff_k"].reshape(DEC_BATCH, DEPTH, PAST_LEN, w),
        ctx_diff_v=p["cache_diff_v"].reshape(DEC_BATCH, DEPTH, PAST_LEN, w),
        ctx_swa_k=p["cache_swa_k"].reshape(DEC_BATCH, DEPTH, PAST_LEN, SWA_KV_HEADS * HEAD_DIM),
        ctx_swa_v=p["cache_swa_v"].reshape(DEC_BATCH, DEPTH, PAST_LEN, SWA_KV_HEADS * HEAD_DIM),
    )


def _mixer_stage(x, params, prep, l, consts):
    w = GROUP_WIDTH
    s32, s64 = consts["s32"], consts["s64"]
    lam_init = 0.8 - 0.6 * math.exp(-0.3 * l)
    proj = _inproj_call(l, consts["seq_of_tile"][l], consts["rope_of_tile"], x, prep["mod3"], prep["g1"][l],
                        prep["w_in"], s32, s64, prep["gqc"][l], prep["gkc"][l], prep["gqd"][l], prep["gkd"][l],
                        consts["cosc"], consts["sinc"], consts["cosd"], consts["sind"])
    ya = _gmlp_call(proj, s64, prep["g_mlp"][l], prep["mlp_ws"][l], prep["mlp_bias"][l])

    lru_args = (params["lru_conv_w"][l], prep["conv_b"][l], prep["wcat"][l], prep["bcat"][l], params["lru_lambda"][l])
    yb_p, st_p = _lru_call(proj, jnp.zeros((BATCH, 2, w), F32), *lru_args, seq_len=SEQ, n_seq=BATCH, row_block0=0)
    yb_s, _ = _lru_call(proj, params["state_lru"][:, l], *lru_args, seq_len=DEC_SEQ, n_seq=DEC_BATCH,
                        row_block0=N_PROMPT // DEC_SEQ)

    ld = params["diff_lambda"][l]
    yc_p = _diff_prompt_call(proj, ld, s64, prep["g_sub"][l], lam_init)
    yc_s = _diff_sample_call(l, proj, prep["ctx_diff_k"], prep["ctx_diff_v"], ld, s64, prep["g_sub"][l], lam_init)

    sink = params["swa_sink"][l]
    yd_p = _swa_prompt_call(sink, proj)
    yd_s = _swa_sample_call(l, sink, proj, prep["ctx_swa_k"], prep["ctx_swa_v"])
    return dict(proj=proj, ya=ya, yb=(yb_p, yb_s), yc=(yc_p, yc_s), yd=(yd_p, yd_s), st_p=st_p)


def _ffn_stage(x, st, params, prep, l, consts):
    seq_of_tile = consts["seq_of_tile"][l]
    x1, h2_rows, top_i, top_w, rank, counts = _outproj_call(
        l, seq_of_tile, st["ya"], st["yb"], st["yc"], st["yd"], x, prep["mod3"], prep["w_out"], prep["g2"][l],
        params["router_w"][l], prep["router_b"][l], consts["tri"])
    routing = _moe_routing(top_i[:, :TOP_K], top_w[:, :TOP_K], rank[:, :TOP_K], counts)
    moe = _moe_call(l, *routing, h2_rows, prep["moe_w_gu"], prep["moe_b_gu"], prep["moe_w_down"], prep["moe_b_down"])
    return _residual_call(seq_of_tile, x1, moe.reshape(N_TOK * ROW_PIECES, 128), prep["mod3"])
```

```python
import functools
import math

import jax
import jax.numpy as jnp
import numpy as np
from jax import lax
from jax.experimental import pallas as pl
from jax.experimental.pallas import tpu as pltpu

F32 = jnp.float32
BF16 = jnp.bfloat16

D_MODEL = 1024
BATCH = 16
SEQ = 256
DEPTH = 2
DEC_BATCH = 4
DEC_SEQ = 2048
PAST_LEN = 512
GRID_W = 64
HEAD_DIM = 64
GROUP_WIDTH = 256
N_GROUP_HEADS = 4
CHUNK = 128
LRU_C = 8.0
CONV_W = 4
DIFF_QK_DIM = 32
SWA_KV_HEADS = 2
SWA_GROUPS = 2
SWA_HEADS = SWA_KV_HEADS * SWA_GROUPS
WINDOW = 128
N_EXPERTS = 32
TOP_K = 4
D_FF = 1024
SWIGLU_LIMIT = 7.0
SWIGLU_ALPHA = 1.702
ROPE_BASE = 10000.0
EPS = 1e-6
LOG2_E = math.log2(math.e)

N_PROMPT = BATCH * SEQ
N_SAMPLE = DEC_BATCH * DEC_SEQ
N_TOK = N_PROMPT + N_SAMPLE
N_COND = 8
TM = 256
N_TILES = N_TOK // TM
PROMPT_TILES = N_PROMPT // TM
SAMPLE_TILES_PER_SEQ = DEC_SEQ // TM
IN_WIDTH = 2304
OFF_AU, OFF_AV, OFF_BX, OFF_BG, OFF_CQ, OFF_CK, OFF_CV, OFF_DQ, OFF_DK, OFF_DV = (
    0, 256, 512, 768, 1024, 1280, 1536, 1792, 2048, 2176)
TOPK_PAD = 8
ROW_PIECES = D_MODEL // 128
MOE_TILE = 256
MOE_TB = 4096
MOE_MAX_TILES = (N_TOK // MOE_TB) * (MOE_TB * TOP_K // MOE_TILE + N_EXPERTS)
MOE_XT_STRIDE = MOE_TILE + 8
MOE_RMW_BATCH = 4
MOE_PLACE_UNROLL = 16
MOE_ROWS_PER_BLOCK = MOE_TB * TOP_K + N_EXPERTS * MOE_TILE
MOE_PAD_PAIR = MOE_TB * TOP_K
MOE_PAIR_SHIFT = TOP_K.bit_length() - 1
assert 1 << MOE_PAIR_SHIFT == TOP_K
MOE_KIND_SKIP, MOE_KIND_FIRST, MOE_KIND_MIDDLE, MOE_KIND_LAST = 0, 1, 2, 3
LRU_CHUNK = 256
VMEM_LIMIT = 56 * 1024 * 1024


def _cparams(sem):
    return pltpu.CompilerParams(dimension_semantics=sem, vmem_limit_bytes=VMEM_LIMIT)


def _dot(a, b):
    return jnp.dot(a, b, preferred_element_type=F32)


def _dot_nt(a, b):
    return lax.dot_general(a, b, (((1,), (1,)), ((), ())), preferred_element_type=F32)


def _split_bf16(x):
    hi = x.astype(BF16)
    lo = (x - hi.astype(F32)).astype(BF16)
    return hi, lo


def _seg_rms_norm(x, seg_mat, g):
    hi, lo = _split_bf16(x * x)
    ms = _dot(hi, seg_mat) + _dot(lo, seg_mat)
    return x * lax.rsqrt(ms + EPS) * g


def _rope(x, cos_t, sin_t, nf):
    n = x.shape[-1]
    lane = lax.broadcasted_iota(jnp.int32, x.shape, 1)
    first = (lane & (2 * nf - 1)) < nf
    partner = jnp.where(first, pltpu.roll(x, n - nf, axis=1), pltpu.roll(x, nf, axis=1))
    return x * cos_t + partner * sin_t


def _softplus(x):
    return jnp.maximum(x, 0.0) + jnp.log1p(jnp.exp(-jnp.abs(x)))


def _mod_kernel(cond_ref, w_ref, b_ref, o_ref):
    c = cond_ref[...]
    s = c * jax.nn.sigmoid(c)
    o_ref[0] = _dot(s.astype(BF16), w_ref[0].astype(BF16)) + b_ref[0]


def _mod_call(cond, w, b):
    nb = 6
    return pl.pallas_call(
        _mod_kernel,
        grid=(DEPTH, nb),
        in_specs=[pl.BlockSpec((N_COND, D_MODEL), lambda l, j: (0, 0)),
                  pl.BlockSpec((1, D_MODEL, D_MODEL), lambda l, j: (l, 0, j)),
                  pl.BlockSpec((1, 1, D_MODEL), lambda l, j: (l, 0, j))],
        out_specs=pl.BlockSpec((1, N_COND, D_MODEL), lambda l, j: (l, 0, j)),
        out_shape=jax.ShapeDtypeStruct((DEPTH, N_COND, 6 * D_MODEL), F32),
        compiler_params=_cparams(("arbitrary", "arbitrary")),
        name="mod",
    )(cond, w, b.reshape(DEPTH, 1, 6 * D_MODEL))


def _inproj_kernel(seq_ref, rope_ref, xp_ref, xs_ref, mod_ref, g1_ref, w_ref, s32_ref, s64_ref,
                   gqc_ref, gkc_ref, gqd_ref, gkd_ref, cosc_ref, sinc_ref, cosd_ref, sind_ref, o_ref):
    del seq_ref, rope_ref
    x = jnp.where(pl.program_id(0) < PROMPT_TILES, xp_ref[...], xs_ref[...])
    xn = x * lax.rsqrt(jnp.mean(x * x, axis=-1, keepdims=True) + EPS) * g1_ref[...]
    mod = mod_ref[0]
    sh1 = mod[:, 0:D_MODEL]
    sc1 = mod[:, D_MODEL:2 * D_MODEL]
    h = xn * (1.0 + sc1) + sh1
    p = _dot(h.astype(BF16), w_ref[0])
    o_ref[:, OFF_AU:OFF_BX] = jax.nn.gelu(p[:, OFF_AU:OFF_BX])
    o_ref[:, OFF_BX:OFF_BG] = p[:, OFF_BX:OFF_BG]
    o_ref[:, OFF_BG:OFF_CQ] = jax.nn.gelu(p[:, OFF_BG:OFF_CQ])
    s32 = s32_ref[...]
    cosc = cosc_ref[...]
    sinc = sinc_ref[...]
    cq = _seg_rms_norm(p[:, OFF_CQ:OFF_CK], s32, gqc_ref[...])
    ck = _seg_rms_norm(p[:, OFF_CK:OFF_CV], s32, gkc_ref[...])
    o_ref[:, OFF_CQ:OFF_CK] = _rope(cq, cosc, sinc, DIFF_QK_DIM // 4)
    o_ref[:, OFF_CK:OFF_CV] = _rope(ck, cosc, sinc, DIFF_QK_DIM // 4)
    o_ref[:, OFF_CV:OFF_DQ] = p[:, OFF_CV:OFF_DQ]
    s64 = s64_ref[...]
    cosd = cosd_ref[...]
    sind = sind_ref[...]
    dq = _seg_rms_norm(p[:, OFF_DQ:OFF_DK], s64, gqd_ref[...])
    dk = _seg_rms_norm(p[:, OFF_DK:OFF_DV], s64[0:128, 0:128], gkd_ref[...])
    o_ref[:, OFF_DQ:OFF_DK] = _rope(dq, cosd, sind, HEAD_DIM // 4)
    o_ref[:, OFF_DK:OFF_DV] = _rope(dk, cosd[:, 0:128], sind[:, 0:128], HEAD_DIM // 4)
    o_ref[:, OFF_DV:IN_WIDTH] = p[:, OFF_DV:IN_WIDTH]


def _inproj_call(layer, seq_of_tile, rope_of_tile, x, mod3, g1, w_in, s32, s64, gqc, gkc, gqd, gkd,
                 cosc, sinc, cosd, sind):
    full = lambda shape: pl.BlockSpec(shape, lambda i, s, r: (0,) * len(shape))
    rope_spec = pl.BlockSpec((TM, GROUP_WIDTH), lambda i, s, r: (r[i], 0))
    grid_spec = pltpu.PrefetchScalarGridSpec(
        num_scalar_prefetch=2,
        grid=(N_TILES,),
        in_specs=[pl.BlockSpec((TM, D_MODEL), lambda i, s, r: (jnp.minimum(i, PROMPT_TILES - 1), 0)),
                  pl.BlockSpec((TM, D_MODEL), lambda i, s, r: (jnp.maximum(i - PROMPT_TILES, 0), 0)),
                  pl.BlockSpec((1, 1, 6 * D_MODEL), lambda i, s, r: (s[i], 0, 0)),
                  full((1, D_MODEL)),
                  pl.BlockSpec((1, D_MODEL, IN_WIDTH), lambda i, s, r: (layer, 0, 0)),
                  full((GROUP_WIDTH, GROUP_WIDTH)),
                  full((GROUP_WIDTH, GROUP_WIDTH)),
                  full((1, GROUP_WIDTH)), full((1, GROUP_WIDTH)), full((1, GROUP_WIDTH)), full((1, 128)),
                  rope_spec, rope_spec, rope_spec, rope_spec],
        out_specs=pl.BlockSpec((TM, IN_WIDTH), lambda i, s, r: (i, 0)),
    )
    return pl.pallas_call(
        _inproj_kernel,
        grid_spec=grid_spec,
        out_shape=jax.ShapeDtypeStruct((N_TOK, IN_WIDTH), F32),
        compiler_params=_cparams(("arbitrary",)),
        name="inproj",
    )(seq_of_tile, rope_of_tile, *x, mod3, g1, w_in, s32, s64, gqc, gkc, gqd, gkd, cosc, sinc, cosd, sind)


def _gmlp_kernel(u_ref, v_ref, s64_ref, g_ref, ws_ref, bias_ref, o_ref):
    vh = _seg_rms_norm(v_ref[...], s64_ref[...], g_ref[...]).astype(BF16)
    head = lax.broadcasted_iota(jnp.int32, (CHUNK, GROUP_WIDTH), 1) // HEAD_DIM
    for ch in range(TM // CHUNK):
        rows = slice(ch * CHUNK, (ch + 1) * CHUNK)
        vc = vh[rows]
        mixed = bias_ref[...]
        for h in range(N_GROUP_HEADS):
            mixed = mixed + jnp.where(head == h, _dot(ws_ref[h], vc), 0.0)
        o_ref[rows, :] = u_ref[rows, :] * mixed


def _gmlp_call(proj, s64, g, ws, bias):
    return pl.pallas_call(
        _gmlp_kernel,
        grid=(N_TILES,),
        in_specs=[pl.BlockSpec((TM, GROUP_WIDTH), lambda i: (i, OFF_AU // GROUP_WIDTH)),
                  pl.BlockSpec((TM, GROUP_WIDTH), lambda i: (i, OFF_AV // GROUP_WIDTH)),
                  pl.BlockSpec((GROUP_WIDTH, GROUP_WIDTH), lambda i: (0, 0)),
                  pl.BlockSpec((1, GROUP_WIDTH), lambda i: (0, 0)),
                  pl.BlockSpec((N_GROUP_HEADS, CHUNK, CHUNK), lambda i: (0, 0, 0)),
                  pl.BlockSpec((CHUNK, GROUP_WIDTH), lambda i: (0, 0))],
        out_specs=pl.BlockSpec((TM, GROUP_WIDTH), lambda i: (i, 0)),
        out_shape=jax.ShapeDtypeStruct((N_TOK, GROUP_WIDTH), F32),
        compiler_params=_cparams(("arbitrary",)),
        name="gmlp",
    )(proj, proj, s64, g, ws, bias)


def _scan_chunk(a, b, reverse):
    n = a.shape[0]
    row = lax.broadcasted_iota(jnp.int32, a.shape, 0)
    s = 1
    while s < n:
        if reverse:
            keep = row < n - s
            shift = n - s
        else:
            keep = row >= s
            shift = s
        a_prev = jnp.where(keep, pltpu.roll(a, shift, axis=0), 1.0)
        b_prev = jnp.where(keep, pltpu.roll(b, shift, axis=0), 0.0)
        b = a * b_prev + b
        a = a * a_prev
        s *= 2
    return a, b


def _lru_kernel(x_ref, g_ref, h0_ref, cw_ref, cb_ref, wcat_ref, bcat_ref, lam_ref, y_ref, st_ref,
                xpad, a_f, b_f, a_b, b_b, h_f, *, seq_len):
    nc = seq_len // LRU_CHUNK
    w = GROUP_WIDTH
    zeros8 = jnp.zeros((8, w), F32)
    xpad[0:8, :] = zeros8
    xpad[seq_len + 8:seq_len + 16, :] = zeros8
    xpad[8:seq_len + 8, :] = x_ref[...]
    sp = _softplus(-lam_ref[...])
    cw = cw_ref[...]
    cb = cb_ref[...]
    win_rows = LRU_CHUNK + 16

    def gates(c, carry):
        r0 = pl.multiple_of(c * LRU_CHUNK, LRU_CHUNK)
        win = xpad[pl.ds(r0, win_rows), :]
        inner = slice(8, 8 + LRU_CHUNK)
        xc = cb + pltpu.roll(win, 2, axis=0)[inner] * cw[0:1]
        xc = xc + pltpu.roll(win, 1, axis=0)[inner] * cw[1:2]
        xc = xc + win[inner] * cw[2:3]
        xc = xc + pltpu.roll(win, win_rows - 1, axis=0)[inner] * cw[3:4]
        sg = jax.nn.sigmoid(_dot(xc.astype(BF16), wcat_ref[...]) + bcat_ref[...])
        for d, (a_ref, b_ref) in enumerate(((a_f, b_f), (a_b, b_b))):
            r = sg[:, (2 * d) * w:(2 * d + 1) * w]
            i = sg[:, (2 * d + 1) * w:(2 * d + 2) * w]
            log_a = (-LRU_C * r) * sp[d:d + 1]
            a = jnp.exp(log_a)
            a_ref[pl.ds(r0, LRU_CHUNK), :] = a
            b_ref[pl.ds(r0, LRU_CHUNK), :] = jnp.sqrt(-jnp.tanh(log_a) * (a * a + 1.0)) * (i * xc)
        return carry

    lax.fori_loop(0, nc, gates, 0)

    def fwd(c, carry):
        r0 = pl.multiple_of(c * LRU_CHUNK, LRU_CHUNK)
        a_cum, h_loc = _scan_chunk(a_f[pl.ds(r0, LRU_CHUNK), :], b_f[pl.ds(r0, LRU_CHUNK), :], False)
        h = h_loc + a_cum * carry
        h_f[pl.ds(r0, LRU_CHUNK), :] = h
        return h[LRU_CHUNK - 1:LRU_CHUNK, :]

    s_f = lax.fori_loop(0, nc, fwd, h0_ref[0, 0:1, :])

    def bwd(k, carry):
        c = nc - 1 - k
        r0 = pl.multiple_of(c * LRU_CHUNK, LRU_CHUNK)
        a_cum, h_loc = _scan_chunk(a_b[pl.ds(r0, LRU_CHUNK), :], b_b[pl.ds(r0, LRU_CHUNK), :], True)
        h = h_loc + a_cum * carry
        y_ref[pl.ds(r0, LRU_CHUNK), :] = (h_f[pl.ds(r0, LRU_CHUNK), :] + h) * g_ref[pl.ds(r0, LRU_CHUNK), :]
        return h[0:1, :]

    s_b = lax.fori_loop(0, nc, bwd, h0_ref[0, 1:2, :])
    st_ref[0, 0:1, :] = s_f
    st_ref[0, 1:2, :] = s_b


def _lru_call(proj, h0, cw, cb, wcat, bcat, lam, *, seq_len, n_seq, row_block0):
    w = GROUP_WIDTH
    full = lambda shape: pl.BlockSpec(shape, lambda b: (0,) * len(shape))
    seq_block = lambda col: pl.BlockSpec((seq_len, w), lambda b: (b + row_block0, col))
    return pl.pallas_call(
        functools.partial(_lru_kernel, seq_len=seq_len),
        grid=(n_seq,),
        in_specs=[seq_block(OFF_BX // w), seq_block(OFF_BG // w),
                  pl.BlockSpec((1, 2, w), lambda b: (b, 0, 0)),
                  full((CONV_W, w)), full((1, w)), full((w, 4 * w)), full((1, 4 * w)), full((2, w))],
        out_specs=[pl.BlockSpec((seq_len, w), lambda b: (b, 0)),
                   pl.BlockSpec((1, 2, w), lambda b: (b, 0, 0))],
        out_shape=[jax.ShapeDtypeStruct((n_seq * seq_len, w), F32),
                   jax.ShapeDtypeStruct((n_seq, 2, w), F32)],
        scratch_shapes=[pltpu.VMEM((seq_len + 16, w), F32)] + [pltpu.VMEM((seq_len, w), F32)] * 5,
        compiler_params=_cparams(("arbitrary",)),
        name="lru_%d" % seq_len,
    )(proj, proj, h0, cw, cb, wcat, bcat, lam)


def _diff_lambda(ld_ref, lam_init):
    ld = ld_ref[...]
    l1 = jnp.sum(ld[0:1] * ld[1:2], axis=-1, keepdims=True)
    l2 = jnp.sum(ld[2:3] * ld[3:4], axis=-1, keepdims=True)
    return jnp.exp(l1) - jnp.exp(l2) + lam_init


def _diff_attn_body(q, keys, values, lam, s64, g, lam_init):
    tq = q.shape[0]
    lane = lax.broadcasted_iota(jnp.int32, (1, GROUP_WIDTH), 1)
    qs = q * (DIFF_QK_DIM ** -0.5 * LOG2_E)
    o = jnp.zeros((tq, GROUP_WIDTH), F32)
    for h in range(N_GROUP_HEADS):
        e_rows, inv = [], []
        for i in range(2):
            seg = h * 2 + i
            qm = jnp.where(lane // DIFF_QK_DIM == seg, qs, 0.0).astype(BF16)
            s = [_dot_nt(qm, k) for k in keys]
            m = functools.reduce(jnp.maximum, [jnp.max(x, axis=-1, keepdims=True) for x in s])
            e = [jnp.exp2(x - m) for x in s]
            den = functools.reduce(jnp.add, [jnp.sum(x, axis=-1, keepdims=True) for x in e])
            inv.append(1.0 / den)
            e_rows.append([x.astype(BF16) for x in e])
        both = None
        for e0, e1, v in zip(e_rows[0], e_rows[1], values):
            part = _dot(jnp.concatenate([e0, e1], axis=0), v)
            both = part if both is None else both + part
        oh = both[:tq] * inv[0] - lam * (both[tq:] * inv[1])
        o = o + jnp.where(lane // HEAD_DIM == h, oh, 0.0)
    return _seg_rms_norm(o, s64, g) * (1.0 - lam_init)


def _diff_prompt_kernel(q_ref, k_ref, v_ref, ld_ref, s64_ref, g_ref, o_ref, *, lam_init):
    lam = _diff_lambda(ld_ref, lam_init)
    o_ref[...] = _diff_attn_body(q_ref[...], [k_ref[...].astype(BF16)], [v_ref[...].astype(BF16)],
                                 lam, s64_ref[...], g_ref[...], lam_init)


def _diff_sample_kernel(q_ref, k_ref, v_ref, ck_ref, cv_ref, ld_ref, s64_ref, g_ref, o_ref, *, lam_init):
    lam = _diff_lambda(ld_ref, lam_init)
    keys = [ck_ref[0, 0].astype(BF16), k_ref[...].astype(BF16)]
    values = [cv_ref[0, 0].astype(BF16), v_ref[...].astype(BF16)]
    o_ref[...] = _diff_attn_body(q_ref[...], keys, values, lam, s64_ref[...], g_ref[...], lam_init)


def _diff_prompt_call(proj, ld, s64, g, lam_init):
    w = GROUP_WIDTH
    full = lambda shape: pl.BlockSpec(shape, lambda b: (0,) * len(shape))
    return pl.pallas_call(
        functools.partial(_diff_prompt_kernel, lam_init=lam_init),
        grid=(BATCH,),
        in_specs=[pl.BlockSpec((SEQ, w), lambda b: (b, OFF_CQ // w)),
                  pl.BlockSpec((SEQ, w), lambda b: (b, OFF_CK // w)),
                  pl.BlockSpec((SEQ, w), lambda b: (b, OFF_CV // w)),
                  full((4, DIFF_QK_DIM)), full((w, w)), full((1, w))],
        out_specs=pl.BlockSpec((SEQ, w), lambda b: (b, 0)),
        out_shape=jax.ShapeDtypeStruct((N_PROMPT, w), F32),
        compiler_params=_cparams(("arbitrary",)),
        name="diff_prompt",
    )(proj, proj, proj, ld, s64, g)


def _diff_sample_call(layer, proj, ctx_k, ctx_v, ld, s64, g, lam_init):
    w = GROUP_WIDTH
    tq = 256
    nq = DEC_SEQ // tq
    full = lambda shape: pl.BlockSpec(shape, lambda b, i: (0,) * len(shape))
    seq_block0 = N_PROMPT // DEC_SEQ
    return pl.pallas_call(
        functools.partial(_diff_sample_kernel, lam_init=lam_init),
        grid=(DEC_BATCH, nq),
        in_specs=[pl.BlockSpec((tq, w), lambda b, i: (N_PROMPT // tq + b * nq + i, OFF_CQ // w)),
                  pl.BlockSpec((DEC_SEQ, w), lambda b, i: (seq_block0 + b, OFF_CK // w)),
                  pl.BlockSpec((DEC_SEQ, w), lambda b, i: (seq_block0 + b, OFF_CV // w)),
                  pl.BlockSpec((1, 1, PAST_LEN, w), lambda b, i: (b, layer, 0, 0)),
                  pl.BlockSpec((1, 1, PAST_LEN, w), lambda b, i: (b, layer, 0, 0)),
                  full((4, DIFF_QK_DIM)), full((w, w)), full((1, w))],
        out_specs=pl.BlockSpec((tq, w), lambda b, i: (b * nq + i, 0)),
        out_shape=jax.ShapeDtypeStruct((N_SAMPLE, w), F32),
        compiler_params=_cparams(("arbitrary", "arbitrary")),
        name="diff_sample",
    )(proj, proj, proj, ctx_k, ctx_v, ld, s64, g)


def _sink_attn_body(q, keys, values, masks, sink_ref):
    tq = q.shape[0]
    lane = lax.broadcasted_iota(jnp.int32, (1, 128), 1)
    qs = q * (HEAD_DIM ** -0.5 * LOG2_E)
    head_rows = lax.broadcasted_iota(jnp.int32, (SWA_HEADS * tq, 1), 0) // tq
    stacked, sink = [], jnp.zeros((SWA_HEADS * tq, 1), F32)
    for grp in range(SWA_GROUPS):
        for kh in range(SWA_KV_HEADS):
            stacked.append(jnp.where(lane // HEAD_DIM == kh, qs[:, grp * 128:(grp + 1) * 128], 0.0).astype(BF16))
            sink = jnp.where(head_rows == len(stacked) - 1, sink_ref[kh * SWA_GROUPS + grp] * LOG2_E, sink)
    q4 = jnp.concatenate(stacked, axis=0)
    s = []
    for k, msk in zip(keys, masks):
        x = _dot_nt(q4, k)
        s.append(x if msk is None else jnp.where(msk, x, -jnp.inf))
    m = functools.reduce(jnp.maximum, [jnp.max(x, axis=-1, keepdims=True) for x in s])
    m = jnp.maximum(m, sink)
    e = [jnp.exp2(x - m) for x in s]
    den = functools.reduce(jnp.add, [jnp.sum(x, axis=-1, keepdims=True) for x in e]) + jnp.exp2(sink - m)
    o4 = None
    for x, v in zip(e, values):
        part = _dot(x.astype(BF16), v)
        o4 = part if o4 is None else o4 + part
    o4 = o4 * (1.0 / den)
    outs = []
    for grp in range(SWA_GROUPS):
        r0 = grp * SWA_KV_HEADS * tq
        outs.append(jnp.where(lane // HEAD_DIM == 0, o4[r0:r0 + tq], o4[r0 + tq:r0 + 2 * tq]))
    return outs


def _swa_prompt_kernel(sink_ref, q_ref, k_ref, v_ref, o_ref):
    outs = _sink_attn_body(q_ref[...], [k_ref[...].astype(BF16)], [v_ref[...].astype(BF16)], [None], sink_ref)
    for grp in range(SWA_GROUPS):
        o_ref[:, grp * 128:(grp + 1) * 128] = outs[grp]


def _swa_sample_kernel(sink_ref, q_ref, kp_ref, kc_ref, kn_ref, vp_ref, vc_ref, vn_ref, ck_ref, cv_ref, o_ref):
    n = pl.program_id(1)
    nb = pl.num_programs(1)
    r = lax.broadcasted_iota(jnp.int32, (SWA_HEADS * WINDOW, WINDOW), 0) & (WINDOW - 1)
    c = lax.broadcasted_iota(jnp.int32, (SWA_HEADS * WINDOW, WINDOW), 1)
    mask_prev = c >= r + jnp.where(n > 0, 0, WINDOW)
    mask_next = c <= r - jnp.where(n < nb - 1, 0, WINDOW)
    keys = [ck_ref[0, 0].astype(BF16), kp_ref[...].astype(BF16), kc_ref[...].astype(BF16), kn_ref[...].astype(BF16)]
    values = [cv_ref[0, 0].astype(BF16), vp_ref[...].astype(BF16), vc_ref[...].astype(BF16), vn_ref[...].astype(BF16)]
    outs = _sink_attn_body(q_ref[...], keys, values, [None, mask_prev, None, mask_next], sink_ref)
    for grp in range(SWA_GROUPS):
        o_ref[:, grp * 128:(grp + 1) * 128] = outs[grp]


def _swa_prompt_call(sink, proj):
    w = GROUP_WIDTH
    return pl.pallas_call(
        _swa_prompt_kernel,
        grid=(BATCH,),
        in_specs=[pl.BlockSpec(memory_space=pltpu.SMEM),
                  pl.BlockSpec((SEQ, w), lambda b: (b, OFF_DQ // w)),
                  pl.BlockSpec((SEQ, 128), lambda b: (b, OFF_DK // 128)),
                  pl.BlockSpec((SEQ, 128), lambda b: (b, OFF_DV // 128))],
        out_specs=pl.BlockSpec((SEQ, w), lambda b: (b, 0)),
        out_shape=jax.ShapeDtypeStruct((N_PROMPT, w), F32),
        compiler_params=_cparams(("arbitrary",)),
        name="swa_prompt",
    )(sink, proj, proj, proj)


def _swa_sample_call(layer, sink, proj, ctx_k, ctx_v):
    w = GROUP_WIDTH
    tq = WINDOW
    nq = DEC_SEQ // tq
    row0 = N_PROMPT // tq

    def kv_spec(col, delta):
        def index(b, i):
            j = jnp.clip(i + delta, 0, nq - 1)
            return (row0 + b * nq + j, col)
        return pl.BlockSpec((tq, 128), index)

    ctx_spec = pl.BlockSpec((1, 1, PAST_LEN, 128), lambda b, i: (b, layer, 0, 0))
    return pl.pallas_call(
        _swa_sample_kernel,
        grid=(DEC_BATCH, nq),
        in_specs=[pl.BlockSpec(memory_space=pltpu.SMEM),
                  pl.BlockSpec((tq, w), lambda b, i: (row0 + b * nq + i, OFF_DQ // w)),
                  kv_spec(OFF_DK // 128, -1), kv_spec(OFF_DK // 128, 0), kv_spec(OFF_DK // 128, 1),
                  kv_spec(OFF_DV // 128, -1), kv_spec(OFF_DV // 128, 0), kv_spec(OFF_DV // 128, 1),
                  ctx_spec, ctx_spec],
        out_specs=pl.BlockSpec((tq, w), lambda b, i: (b * nq + i, 0)),
        out_shape=jax.ShapeDtypeStruct((N_SAMPLE, w), F32),
        compiler_params=_cparams(("arbitrary", "arbitrary")),
        name="swa_sample",
    )(sink, proj, proj, proj, proj, proj, proj, proj, ctx_k, ctx_v)


def _outproj_kernel(seq_ref, ya_ref, ybp_ref, ybs_ref, ycp_ref, ycs_ref, ydp_ref, yds_ref, xp_ref, xs_ref, mod_ref,
                    w_ref, g2_ref, rwt_ref, rbt_ref, triu_ref,
                    x1_ref, h2r_ref, topi_ref, topw_ref, rank_ref, count_ref, cnt_ref):
    del seq_ref
    w = GROUP_WIDTH
    is_prompt = pl.program_id(0) < PROMPT_TILES
    yb = jnp.where(is_prompt, ybp_ref[...], ybs_ref[...])
    yc = jnp.where(is_prompt, ycp_ref[...], ycs_ref[...])
    yd = jnp.where(is_prompt, ydp_ref[...], yds_ref[...])
    mix = _dot(ya_ref[...].astype(BF16), w_ref[0, 0:w, :])
    mix = mix + _dot(yb.astype(BF16), w_ref[0, w:2 * w, :])
    mix = mix + _dot(yc.astype(BF16), w_ref[0, 2 * w:3 * w, :])
    mix = mix + _dot(yd.astype(BF16), w_ref[0, 3 * w:4 * w, :])
    mod = mod_ref[0]
    g1 = mod[:, 2 * D_MODEL:3 * D_MODEL]
    sh2 = mod[:, 3 * D_MODEL:4 * D_MODEL]
    sc2 = mod[:, 4 * D_MODEL:5 * D_MODEL]
    x1 = jnp.where(is_prompt, xp_ref[...], xs_ref[...]) + g1 * mix
    x1_ref[...] = x1
    xn = x1 * lax.rsqrt(jnp.mean(x1 * x1, axis=-1, keepdims=True) + EPS) * g2_ref[...]
    h2 = xn * (1.0 + sc2) + sh2
    for j in range(ROW_PIECES):
        h2r_ref[pl.ds(j, TM, stride=ROW_PIECES), :] = h2[:, j * 128:(j + 1) * 128]
    h_hi, h_lo = _split_bf16(h2)
    r_hi, r_lo = _split_bf16(rwt_ref[...])
    logits = _dot_nt(r_hi, h_hi) + (_dot_nt(r_lo, h_hi) + _dot_nt(r_hi, h_lo)) + rbt_ref[...]
    expert = lax.broadcasted_iota(jnp.int32, logits.shape, 0)
    work = logits
    picks = []
    for _ in range(TOP_K):
        m = jnp.max(work, axis=0, keepdims=True)
        idx = jnp.min(jnp.where(work == m, expert, N_EXPERTS), axis=0, keepdims=True)
        picks.append((m, idx))
        work = jnp.where(expert == idx, -jnp.inf, work)
    top = picks[0][0]
    ex = [jnp.exp(m - top) for m, _ in picks]
    inv = 1.0 / functools.reduce(jnp.add, ex)
    @pl.when(pl.program_id(0) % (MOE_TB // TM) == 0)
    def _():
        cnt_ref[...] = jnp.zeros_like(cnt_ref)

    sel = jnp.zeros(logits.shape, F32)
    for _, idx in picks:
        sel = sel + jnp.where(expert == idx, 1.0, 0.0)
    csum = _dot(sel.astype(BF16), triu_ref[...])
    before = cnt_ref[...] + csum - sel
    cnt_ref[...] += csum[:, TM - 1:TM]
    count_ref[0] = cnt_ref[...].astype(jnp.int32)
    slot = lax.broadcasted_iota(jnp.int32, topi_ref.shape, 0)
    top_i = jnp.zeros(topi_ref.shape, jnp.int32)
    top_w = jnp.zeros(topw_ref.shape, F32)
    rank = jnp.zeros(rank_ref.shape, jnp.int32)
    for k, (e, (_, idx)) in enumerate(zip(ex, picks)):
        rank_k = jnp.sum(jnp.where(expert == idx, before, 0.0), axis=0, keepdims=True).astype(jnp.int32)
        top_i = jnp.where(slot == k, idx, top_i)
        top_w = jnp.where(slot == k, e * inv, top_w)
        rank = jnp.where(slot == k, rank_k, rank)
    topi_ref[...] = top_i
    topw_ref[...] = top_w
    rank_ref[...] = rank


def _outproj_call(layer, seq_of_tile, ya, yb, yc, yd, x, mod3, w_out, g2, rwt, rbt, triu):
    w = GROUP_WIDTH
    full = lambda shape: pl.BlockSpec(shape, lambda i, s: (0,) * len(shape))
    tile = lambda width: pl.BlockSpec((TM, width), lambda i, s: (i, 0))
    prompt_tile = lambda width: pl.BlockSpec((TM, width), lambda i, s: (jnp.minimum(i, PROMPT_TILES - 1), 0))
    sample_tile = lambda width: pl.BlockSpec((TM, width), lambda i, s: (jnp.maximum(i - PROMPT_TILES, 0), 0))
    slots = pl.BlockSpec((TOPK_PAD, TM), lambda i, s: (0, i))
    tiles_per_block = MOE_TB // TM
    return pl.pallas_call(
        _outproj_kernel,
        grid_spec=pltpu.PrefetchScalarGridSpec(
            num_scalar_prefetch=1,
            grid=(N_TILES,),
            in_specs=[tile(w), prompt_tile(w), sample_tile(w), prompt_tile(w), sample_tile(w),
                      prompt_tile(w), sample_tile(w), prompt_tile(D_MODEL), sample_tile(D_MODEL),
                      pl.BlockSpec((1, 1, 6 * D_MODEL), lambda i, s: (s[i], 0, 0)),
                      pl.BlockSpec((1, D_MODEL, D_MODEL), lambda i, s: (layer, 0, 0)), full((1, D_MODEL)),
                      full((N_EXPERTS, D_MODEL)), full((N_EXPERTS, 1)), full((TM, TM))],
            out_specs=[tile(D_MODEL), pl.BlockSpec((TM * ROW_PIECES, 128), lambda i, s: (i, 0)),
                       slots, slots, slots,
                       pl.BlockSpec((1, N_EXPERTS, 1), lambda i, s: (i // tiles_per_block, 0, 0))],
            scratch_shapes=[pltpu.VMEM((N_EXPERTS, 1), F32)],
        ),
        out_shape=[jax.ShapeDtypeStruct((N_TOK, D_MODEL), F32),
                   jax.ShapeDtypeStruct((N_TOK * ROW_PIECES, 128), F32),
                   jax.ShapeDtypeStruct((TOPK_PAD, N_TOK), jnp.int32),
                   jax.ShapeDtypeStruct((TOPK_PAD, N_TOK), F32),
                   jax.ShapeDtypeStruct((TOPK_PAD, N_TOK), jnp.int32),
                   jax.ShapeDtypeStruct((N_TOK // MOE_TB, N_EXPERTS, 1), jnp.int32)],
        compiler_params=_cparams(("arbitrary",)),
        name="outproj",
    )(seq_of_tile, ya, *yb, *yc, *yd, *x, mod3, w_out, g2, rwt, rbt, triu)


def _moe_kernel(expert_ref, block_ref, kind_ref, local_ref, dest_ref, topw_ref, gstart_ref, count_ref,
                src_ref, wgu_ref, bgu_ref, wd_ref, bd_ref, acc_ref, pair_ref, xt_even, xt_odd, y3_even, y3_odd):
    del expert_ref, block_ref
    i = pl.program_id(0)
    kind = kind_ref[i]

    def token_of(pair):
        return (pair >> MOE_PAIR_SHIFT) & (MOE_TB - 1)

    def gather(tile, xt_ref, lo=0, hi=MOE_TILE):
        row0 = local_ref[tile]
        for m in range(lo, hi):
            t = token_of(pair_ref[row0 + m])
            slab = src_ref[pl.ds(pl.multiple_of(t * ROW_PIECES, ROW_PIECES), ROW_PIECES), :]
            xt_ref[pl.ds(m, ROW_PIECES, stride=MOE_XT_STRIDE), :] = slab

    def scatter(tile, y3_ref, lo=0, hi=MOE_TILE):
        row0 = local_ref[tile]
        for m0 in range(lo, hi, MOE_RMW_BATCH):
            rows = range(m0, m0 + MOE_RMW_BATCH)
            pairs = [pair_ref[row0 + m] for m in rows]
            toks = [token_of(p) for p in pairs]
            vals = [acc_ref[t] + topw_ref[0, 0, p] * y3_ref[m // 8, pl.ds(m % 8, 8, stride=8), :]
                    for t, p, m in zip(toks, pairs, rows)]
            for t, v in reversed(list(zip(toks, vals))):
                acc_ref[t] = v

    def compute(xt_ref, y3_ref):
        x = jnp.concatenate([xt_ref[j * MOE_XT_STRIDE:j * MOE_XT_STRIDE + MOE_TILE, :] for j in range(ROW_PIECES)],
                            axis=1).astype(BF16)
        gu = _dot(x, wgu_ref[0, 0]) + bgu_ref[0, 0]
        gate = jnp.minimum(gu[:, :D_FF], SWIGLU_LIMIT)
        up = jnp.clip(gu[:, D_FF:], -SWIGLU_LIMIT, SWIGLU_LIMIT)
        act = (up + 1.0) * gate * jax.nn.sigmoid(SWIGLU_ALPHA * gate)
        y = _dot(act.astype(BF16), wd_ref[0, 0]) + bd_ref[0, 0]
        for j in range(ROW_PIECES):
            y3_ref[:, j * 8:(j + 1) * 8, :] = y[:, j * 128:(j + 1) * 128].reshape(MOE_TILE // 8, 8, 128)

    @pl.when(kind == MOE_KIND_FIRST)
    def _():
        def zero(c, carry):
            acc_ref[pl.ds(pl.multiple_of(c * MOE_TILE, MOE_TILE), MOE_TILE)] = jnp.zeros((MOE_TILE, 8, 128), F32)
            return carry
        lax.fori_loop(0, MOE_TB // MOE_TILE, zero, 0)

        def pad_group(e, carry):
            cnt = count_ref[0, 0, e]
            first_pad = gstart_ref[0, 0, e] + cnt
            n_pad = (-cnt) & (MOE_TILE - 1)

            def pad_row(r, c2):
                pair_ref[first_pad + r] = MOE_PAD_PAIR
                return c2
            lax.fori_loop(0, n_pad, pad_row, 0)
            return carry
        lax.fori_loop(0, N_EXPERTS, pad_group, 0)

        def place(c, carry):
            for k in range(MOE_PLACE_UNROLL):
                pair = c * MOE_PLACE_UNROLL + k
                pair_ref[dest_ref[0, 0, pair]] = pair
            return carry
        lax.fori_loop(0, MOE_TB * TOP_K // MOE_PLACE_UNROLL, place, 0)

    step = kind * 2 + (i & 1)
    for parity, (xt_cur, xt_nxt, y3_cur, y3_prv) in enumerate(((xt_even, xt_odd, y3_even, y3_odd),
                                                               (xt_odd, xt_even, y3_odd, y3_even))):
        @pl.when(step == MOE_KIND_FIRST * 2 + parity)
        def _():
            gather(i, xt_cur)
            gather(i + 1, xt_nxt)
            compute(xt_cur, y3_cur)

        @pl.when(step == MOE_KIND_MIDDLE * 2 + parity)
        def _():
            gather(i + 1, xt_nxt)
            compute(xt_cur, y3_cur)
            scatter(i - 1, y3_prv)

        @pl.when(step == MOE_KIND_LAST * 2 + parity)
        def _():
            compute(xt_cur, y3_cur)
            scatter(i - 1, y3_prv)
            scatter(i, y3_cur)


def _moe_call(layer, tile_expert, tile_block, tile_flags, tile_local, dest, topw, gstart, counts, h2_rows,
              wgu, bgu, wd, bd):
    pieces = ROW_PIECES
    once = pl.Buffered(1)
    n_pairs = MOE_TB * TOP_K

    def smem_block(width):
        return pl.BlockSpec((1, 1, width), lambda i, e, b, f, lo: (b[i], 0, 0), memory_space=pltpu.SMEM,
                            pipeline_mode=once)

    def expert_block(rows, cols):
        return pl.BlockSpec((1, 1, rows, cols), lambda i, e, b, f, lo: (layer, e[i], 0, 0))

    return pl.pallas_call(
        _moe_kernel,
        grid_spec=pltpu.PrefetchScalarGridSpec(
            num_scalar_prefetch=4,
            grid=(MOE_MAX_TILES,),
            in_specs=[smem_block(n_pairs), smem_block(n_pairs + 128), smem_block(N_EXPERTS), smem_block(N_EXPERTS),
                      pl.BlockSpec((MOE_TB * pieces, 128), lambda i, e, b, f, lo: (b[i], 0), pipeline_mode=once),
                      expert_block(D_MODEL, 2 * D_FF), expert_block(1, 2 * D_FF),
                      expert_block(D_FF, D_MODEL), expert_block(1, D_MODEL)],
            out_specs=pl.BlockSpec((MOE_TB, 8, 128), lambda i, e, b, f, lo: (b[i], 0, 0), pipeline_mode=once),
            scratch_shapes=[pltpu.SMEM((MOE_ROWS_PER_BLOCK,), jnp.int32),
                            pltpu.VMEM((pieces * MOE_XT_STRIDE, 128), F32),
                            pltpu.VMEM((pieces * MOE_XT_STRIDE, 128), F32),
                            pltpu.VMEM((MOE_TILE // 8, 8 * pieces, 128), F32),
                            pltpu.VMEM((MOE_TILE // 8, 8 * pieces, 128), F32)],
        ),
        out_shape=jax.ShapeDtypeStruct((N_TOK, 8, 128), F32),
        compiler_params=_cparams(("arbitrary",)),
        name="moe",
    )(tile_expert, tile_block, tile_flags, tile_local, dest, topw, gstart, counts, h2_rows, wgu, bgu, wd, bd)


def _moe_routing(top_i, top_w, rank, counts):
    nb = N_TOK // MOE_TB
    n_groups = nb * N_EXPERTS
    counts = counts.reshape(nb, N_EXPERTS)
    padded = ((counts + MOE_TILE - 1) // MOE_TILE) * MOE_TILE
    group_end = jnp.cumsum(padded.reshape(-1))
    group_start = (group_end - padded.reshape(-1)).reshape(nb, N_EXPERTS)
    block_row0 = group_start[:, 0]
    gstart_local = group_start - block_row0[:, None]
    experts = jnp.arange(N_EXPERTS, dtype=jnp.int32)
    pair_major = lambda a: jnp.transpose(a.reshape(TOP_K, nb, MOE_TB), (1, 2, 0)).reshape(nb, 1, MOE_TB * TOP_K)
    picked = top_i.reshape(TOP_K, nb, MOE_TB, 1) == experts
    dest = jnp.sum(jnp.where(picked, gstart_local[None, :, None, :], 0), axis=-1) + rank.reshape(TOP_K, nb, MOE_TB)
    dest = pair_major(dest).astype(jnp.int32)
    topw = jnp.concatenate([pair_major(top_w), jnp.zeros((nb, 1, 128), F32)], axis=-1)
    tile_start = jnp.arange(MOE_MAX_TILES, dtype=jnp.int32) * MOE_TILE
    tile_group = jnp.sum((group_end[None, :] <= tile_start[:, None]).astype(jnp.int32), axis=1)
    tile_group = jnp.minimum(tile_group, n_groups - 1)
    valid = tile_start < group_end[-1]
    tile_block = tile_group // N_EXPERTS
    new_block = tile_block[1:] != tile_block[:-1]
    first = jnp.concatenate([jnp.ones((1,), bool), new_block])
    last = jnp.concatenate([new_block | ~valid[1:], jnp.ones((1,), bool)])
    kinds = jnp.where(first, MOE_KIND_FIRST, jnp.where(last, MOE_KIND_LAST, MOE_KIND_MIDDLE))
    flags = jnp.where(valid, kinds, MOE_KIND_SKIP).astype(jnp.int32)
    blocks = jnp.arange(nb, dtype=jnp.int32)
    tile_row0 = jnp.sum(jnp.where(tile_block[:, None] == blocks[None, :], block_row0[None, :], 0), axis=1)
    tile_local = (tile_start - tile_row0).astype(jnp.int32)
    return (tile_group % N_EXPERTS, tile_block, flags, tile_local, dest, topw,
            gstart_local.reshape(nb, 1, N_EXPERTS).astype(jnp.int32), counts.reshape(nb, 1, N_EXPERTS))


def _residual_kernel(seq_ref, x1_ref, moe_ref, mod_ref, o_ref):
    del seq_ref
    g2 = mod_ref[0][:, 5 * D_MODEL:6 * D_MODEL]
    moe = jnp.concatenate([moe_ref[pl.ds(j, TM, stride=ROW_PIECES), :] for j in range(ROW_PIECES)], axis=1)
    o_ref[...] = x1_ref[...] + g2 * moe


def _residual_call(seq_of_tile, x1, moe_rows, mod3, first_tile, n_tiles):
    return pl.pallas_call(
        _residual_kernel,
        grid_spec=pltpu.PrefetchScalarGridSpec(
            num_scalar_prefetch=1,
            grid=(n_tiles,),
            in_specs=[pl.BlockSpec((TM, D_MODEL), lambda i, s: (i + first_tile, 0)),
                      pl.BlockSpec((TM * ROW_PIECES, 128), lambda i, s: (i + first_tile, 0)),
                      pl.BlockSpec((1, 1, 6 * D_MODEL), lambda i, s: (s[i + first_tile], 0, 0))],
            out_specs=pl.BlockSpec((TM, D_MODEL), lambda i, s: (i, 0)),
        ),
        out_shape=jax.ShapeDtypeStruct((n_tiles * TM, D_MODEL), F32),
        compiler_params=_cparams(("arbitrary",)),
        name="residual",
    )(seq_of_tile, x1, moe_rows, mod3)


def _segment_matrix(width, seg):
    idx = np.arange(width) // seg
    return jnp.asarray((idx[:, None] == idx[None, :]).astype(np.float32) / seg, BF16)


def _rope_tables(rot_dim, width):
    rows = DEC_SEQ // GRID_W
    nf = rot_dim // 4
    f32 = np.float32
    inv = (f32(1.0) / (f32(ROPE_BASE) ** (np.arange(nf, dtype=f32) / f32(nf)))).astype(f32)
    row = np.repeat(np.arange(rows, dtype=f32), GRID_W)
    col = np.tile(np.arange(GRID_W, dtype=f32), rows)
    ang = np.stack([row[:, None] * inv, col[:, None] * inv], axis=1).astype(f32)
    cos, sin = np.cos(ang).astype(f32), np.sin(ang).astype(f32)
    cos_r = np.concatenate([cos[:, 0], cos[:, 0], cos[:, 1], cos[:, 1]], axis=-1)
    sin_r = np.concatenate([-sin[:, 0], sin[:, 0], -sin[:, 1], sin[:, 1]], axis=-1)
    reps = width // rot_dim
    cos_t = np.concatenate([np.ones((TM, width), f32), np.tile(cos_r, (1, reps))], axis=0)
    sin_t = np.concatenate([np.zeros((TM, width), f32), np.tile(sin_r, (1, reps))], axis=0)
    return jnp.asarray(cos_t), jnp.asarray(sin_t)


def _block_diag(wb):
    nb, n = wb.shape[-3], wb.shape[-1]
    rows = wb.reshape(wb.shape[:-3] + (nb * n, n))
    tiled = jnp.tile(rows, (1,) * (wb.ndim - 2) + (nb,))
    blk = np.arange(nb * n) // n
    return jnp.where(jnp.asarray(blk[:, None] == blk[None, :]), tiled, 0.0)


_DQ_HEADS = [kh * SWA_GROUPS + g for g in range(SWA_GROUPS) for kh in range(SWA_KV_HEADS)]


def kernel(x_prompt, x_sample, c, cache_diff_k, cache_diff_v, cache_swa_k, cache_swa_v, state_lru, c_ctx, mod_w, mod_b, norm1_g, norm2_g, w_in, w_out, mlp_vnorm_g, mlp_ws, mlp_bs, lru_conv_w, lru_conv_b, lru_wa, lru_ba, lru_wx, lru_bx, lru_lambda, diff_qnorm_g, diff_knorm_g, diff_lambda, diff_subln_g, swa_qnorm_g, swa_knorm_g, swa_sink, router_w, router_b, moe_w_gu, moe_b_gu, moe_w_down, moe_b_down):
    params = dict(locals())
    consts = _constants()
    prep = _prepare(params, consts)
    x = (x_prompt.reshape(N_PROMPT, D_MODEL), x_sample.reshape(N_SAMPLE, D_MODEL))
    dk_l, dv_l, sk_l, sv_l, lru_l = [], [], [], [], []
    for l in range(DEPTH):
        st = _mixer_stage(x, params, prep, l, consts)
        x = _ffn_stage(x, st, params, prep, l, consts)
        pp = st["proj"][:N_PROMPT]
        dk_l.append(pp[:, OFF_CK:OFF_CV].reshape(BATCH, SEQ, N_GROUP_HEADS, 2, DIFF_QK_DIM))
        dv_l.append(pp[:, OFF_CV:OFF_DQ].reshape(BATCH, SEQ, N_GROUP_HEADS, HEAD_DIM))
        sk_l.append(pp[:, OFF_DK:OFF_DV].reshape(BATCH, SEQ, SWA_KV_HEADS, HEAD_DIM))
        sv_l.append(pp[:, OFF_DV:IN_WIDTH].reshape(BATCH, SEQ, SWA_KV_HEADS, HEAD_DIM))
        lru_l.append(st["st_p"])

    y_p = x[0].reshape(BATCH, SEQ, D_MODEL)
    y_s = x[1].reshape(DEC_BATCH, DEC_SEQ, D_MODEL)
    return (y_p, y_s, jnp.stack(dk_l, axis=1), jnp.stack(dv_l, axis=1), jnp.stack(sk_l, axis=1),
            jnp.stack(sv_l, axis=1), jnp.stack(lru_l, axis=1))


def _constants():
    w = GROUP_WIDTH
    tiles = np.arange(N_TILES)
    seq_np = np.where(tiles < PROMPT_TILES, 0, 1 + (tiles - PROMPT_TILES) // SAMPLE_TILES_PER_SEQ)
    rope_np = np.where(tiles < PROMPT_TILES, 0, 1 + (tiles - PROMPT_TILES) % SAMPLE_TILES_PER_SEQ)
    seq_of_tile = [jnp.asarray(seq_np + l * N_COND, jnp.int32) for l in range(DEPTH)]
    rope_of_tile = jnp.asarray(rope_np, jnp.int32)

    s32 = _segment_matrix(w, DIFF_QK_DIM)
    s64 = _segment_matrix(w, HEAD_DIM)
    cosc, sinc = _rope_tables(DIFF_QK_DIM, w)
    cosd, sind = _rope_tables(HEAD_DIM, w)
    dq_cols = np.concatenate([np.arange(h * HEAD_DIM, (h + 1) * HEAD_DIM) for h in _DQ_HEADS])
    triu = jnp.asarray(np.triu(np.ones((TM, TM), np.float32)), BF16)
    return dict(seq_of_tile=seq_of_tile, rope_of_tile=rope_of_tile, triu=triu, s32=s32, s64=s64,
                cosc=cosc, sinc=sinc, cosd=cosd, sind=sind, dq_cols=dq_cols)


def _prepare(params, consts):
    w = GROUP_WIDTH
    p = params
    dq = consts["dq_cols"]
    heads = lambda g, reps: jnp.tile(g.reshape(DEPTH, 1, -1), (1, 1, reps))
    w_in, w_out = p["w_in"], p["w_out"]
    lru_w = jnp.stack([p["lru_wa"][:, 0], p["lru_wx"][:, 0], p["lru_wa"][:, 1], p["lru_wx"][:, 1]], axis=1)
    lru_b = jnp.stack([p["lru_ba"][:, 0], p["lru_bx"][:, 0], p["lru_ba"][:, 1], p["lru_bx"][:, 1]], axis=1)
    cond = jnp.concatenate([p["c_ctx"][None], p["c"], jnp.zeros((N_COND - 1 - DEC_BATCH, D_MODEL), F32)], axis=0)
    return dict(
        mod3=_mod_call(cond, p["mod_w"], p["mod_b"]).reshape(DEPTH * N_COND, 1, 6 * D_MODEL),
        w_in=jnp.concatenate([w_in[..., :OFF_DQ], w_in[..., OFF_DQ:OFF_DK][..., dq], w_in[..., OFF_DK:]],
                             axis=-1).astype(BF16),
        w_out=jnp.concatenate([w_out[:, :3 * w], w_out[:, 3 * w:][:, dq]], axis=1).astype(BF16),
        gqc=heads(p["diff_qnorm_g"], N_GROUP_HEADS), gkc=heads(p["diff_knorm_g"], N_GROUP_HEADS),
        gqd=heads(p["swa_qnorm_g"], SWA_HEADS), gkd=heads(p["swa_knorm_g"], SWA_KV_HEADS),
        g_sub=heads(p["diff_subln_g"], N_GROUP_HEADS), g_mlp=p["mlp_vnorm_g"].reshape(DEPTH, 1, w),
        mlp_bias=jnp.repeat(jnp.swapaxes(p["mlp_bs"], 1, 2), HEAD_DIM, axis=2),
        mlp_ws=p["mlp_ws"].astype(BF16),
        wcat=jnp.swapaxes(_block_diag(lru_w), 1, 2).reshape(DEPTH, w, 4 * w).astype(BF16),
        bcat=lru_b.reshape(DEPTH, 1, 4 * w),
        conv_b=p["lru_conv_b"].reshape(DEPTH, 1, w),
        g1=p["norm1_g"].reshape(DEPTH, 1, D_MODEL), g2=p["norm2_g"].reshape(DEPTH, 1, D_MODEL),
        router_wt=jnp.swapaxes(p["router_w"], 1, 2), router_bt=p["router_b"].reshape(DEPTH, N_EXPERTS, 1),
        moe_w_gu=p["moe_w_gu"].astype(BF16), moe_w_down=p["moe_w_down"].astype(BF16),
        moe_b_gu=p["moe_b_gu"].reshape(DEPTH, N_EXPERTS, 1, 2 * D_FF),
        moe_b_down=p["moe_b_down"].reshape(DEPTH, N_EXPERTS, 1, D_MODEL),
        ctx_diff_k=p["cache_diff_k"].reshape(DEC_BATCH, DEPTH, PAST_LEN, w),
        ctx_diff_v=p["cache_diff_v"].reshape(DEC_BATCH, DEPTH, PAST_LEN, w),
        ctx_swa_k=p["cache_swa_k"].reshape(DEC_BATCH, DEPTH, PAST_LEN, SWA_KV_HEADS * HEAD_DIM),
        ctx_swa_v=p["cache_swa_v"].reshape(DEC_BATCH, DEPTH, PAST_LEN, SWA_KV_HEADS * HEAD_DIM),
    )


def _mixer_stage(x, params, prep, l, consts):
    w = GROUP_WIDTH
    s32, s64 = consts["s32"], consts["s64"]
    lam_init = 0.8 - 0.6 * math.exp(-0.3 * l)
    proj = _inproj_call(l, consts["seq_of_tile"][l], consts["rope_of_tile"], x, prep["mod3"], prep["g1"][l],
                        prep["w_in"], s32, s64, prep["gqc"][l], prep["gkc"][l], prep["gqd"][l], prep["gkd"][l],
                        consts["cosc"], consts["sinc"], consts["cosd"], consts["sind"])
    ya = _gmlp_call(proj, s64, prep["g_mlp"][l], prep["mlp_ws"][l], prep["mlp_bias"][l])

    lru_args = (params["lru_conv_w"][l], prep["conv_b"][l], prep["wcat"][l], prep["bcat"][l], params["lru_lambda"][l])
    yb_p, st_p = _lru_call(proj, jnp.zeros((BATCH, 2, w), F32), *lru_args, seq_len=SEQ, n_seq=BATCH, row_block0=0)
    yb_s, _ = _lru_call(proj, params["state_lru"][:, l], *lru_args, seq_len=DEC_SEQ, n_seq=DEC_BATCH,
                        row_block0=N_PROMPT // DEC_SEQ)

    ld = params["diff_lambda"][l]
    yc_p = _diff_prompt_call(proj, ld, s64, prep["g_sub"][l], lam_init)
    yc_s = _diff_sample_call(l, proj, prep["ctx_diff_k"], prep["ctx_diff_v"], ld, s64, prep["g_sub"][l], lam_init)

    sink = params["swa_sink"][l]
    yd_p = _swa_prompt_call(sink, proj)
    yd_s = _swa_sample_call(l, sink, proj, prep["ctx_swa_k"], prep["ctx_swa_v"])
    return dict(proj=proj, ya=ya, yb=(yb_p, yb_s), yc=(yc_p, yc_s), yd=(yd_p, yd_s), st_p=st_p)


def _ffn_stage(x, st, params, prep, l, consts):
    seq_of_tile = consts["seq_of_tile"][l]
    x1, h2_rows, top_i, top_w, rank, counts = _outproj_call(
        l, seq_of_tile, st["ya"], st["yb"], st["yc"], st["yd"], x, prep["mod3"], prep["w_out"], prep["g2"][l],
        prep["router_wt"][l], prep["router_bt"][l], consts["triu"])
    routing = _moe_routing(top_i[:TOP_K], top_w[:TOP_K], rank[:TOP_K], counts)
    moe = _moe_call(l, *routing, h2_rows, prep["moe_w_gu"], prep["moe_b_gu"], prep["moe_w_down"], prep["moe_b_down"])
    moe_rows = moe.reshape(N_TOK * ROW_PIECES, 128)
    return (_residual_call(seq_of_tile, x1, moe_rows, prep["mod3"], 0, PROMPT_TILES),
            _residual_call(seq_of_tile, x1, moe_rows, prep["mod3"], PROMPT_TILES, N_TILES - PROMPT_TILES))
```

```python
import functools
import math

import jax
import jax.numpy as jnp
import numpy as np
from jax import lax
from jax.experimental import pallas as pl
from jax.experimental.pallas import tpu as pltpu

F32 = jnp.float32
BF16 = jnp.bfloat16

D_MODEL = 1024
BATCH = 16
SEQ = 256
DEPTH = 2
DEC_BATCH = 4
DEC_SEQ = 2048
PAST_LEN = 512
GRID_W = 64
HEAD_DIM = 64
GROUP_WIDTH = 256
N_GROUP_HEADS = 4
CHUNK = 128
LRU_C = 8.0
CONV_W = 4
DIFF_QK_DIM = 32
SWA_KV_HEADS = 2
SWA_GROUPS = 2
SWA_HEADS = SWA_KV_HEADS * SWA_GROUPS
WINDOW = 128
N_EXPERTS = 32
TOP_K = 4
D_FF = 1024
SWIGLU_LIMIT = 7.0
SWIGLU_ALPHA = 1.702
ROPE_BASE = 10000.0
EPS = 1e-6
LOG2_E = math.log2(math.e)

N_PROMPT = BATCH * SEQ
N_SAMPLE = DEC_BATCH * DEC_SEQ
N_TOK = N_PROMPT + N_SAMPLE
N_COND = 8
TM = 256
N_TILES = N_TOK // TM
PROMPT_TILES = N_PROMPT // TM
SAMPLE_TILES_PER_SEQ = DEC_SEQ // TM
IN_WIDTH = 2304
OFF_AU, OFF_AV, OFF_BX, OFF_BG, OFF_CQ, OFF_CK, OFF_CV, OFF_DQ, OFF_DK, OFF_DV = (
    0, 256, 512, 768, 1024, 1280, 1536, 1792, 2048, 2176)
TOPK_PAD = 8
ROW_PIECES = D_MODEL // 128
MOE_TILE = 256
MOE_TB = 4096
MOE_XT_STRIDE = MOE_TILE + 8
MOE_RMW_BATCH = 4
MOE_PLACE_UNROLL = 16
MOE_ROWS_PER_BLOCK = MOE_TB * TOP_K + N_EXPERTS * MOE_TILE
MOE_PAD_PAIR = MOE_TB * TOP_K
MOE_PAIR_SHIFT = TOP_K.bit_length() - 1
assert 1 << MOE_PAIR_SHIFT == TOP_K
LRU_CHUNK = 256
VMEM_LIMIT = 56 * 1024 * 1024


def _cparams(sem):
    return pltpu.CompilerParams(dimension_semantics=sem, vmem_limit_bytes=VMEM_LIMIT)


def _dot(a, b):
    return jnp.dot(a, b, preferred_element_type=F32)


def _dot_nt(a, b):
    return lax.dot_general(a, b, (((1,), (1,)), ((), ())), preferred_element_type=F32)


def _split_bf16(x):
    hi = x.astype(BF16)
    lo = (x - hi.astype(F32)).astype(BF16)
    return hi, lo


def _seg_rms_norm(x, seg_mat, g):
    hi, lo = _split_bf16(x * x)
    ms = _dot(hi, seg_mat) + _dot(lo, seg_mat)
    return x * lax.rsqrt(ms + EPS) * g


def _rope(x, cos_t, sin_t, nf):
    n = x.shape[-1]
    lane = lax.broadcasted_iota(jnp.int32, x.shape, 1)
    first = (lane & (2 * nf - 1)) < nf
    partner = jnp.where(first, pltpu.roll(x, n - nf, axis=1), pltpu.roll(x, nf, axis=1))
    return x * cos_t + partner * sin_t


def _softplus(x):
    return jnp.maximum(x, 0.0) + jnp.log1p(jnp.exp(-jnp.abs(x)))


def _mod_kernel(cond_ref, w_ref, b_ref, o_ref):
    c = cond_ref[...]
    s = c * jax.nn.sigmoid(c)
    o_ref[0] = _dot(s.astype(BF16), w_ref[0].astype(BF16)) + b_ref[0]


def _mod_call(cond, w, b):
    nb = 6
    return pl.pallas_call(
        _mod_kernel,
        grid=(DEPTH, nb),
        in_specs=[pl.BlockSpec((N_COND, D_MODEL), lambda l, j: (0, 0)),
                  pl.BlockSpec((1, D_MODEL, D_MODEL), lambda l, j: (l, 0, j)),
                  pl.BlockSpec((1, 1, D_MODEL), lambda l, j: (l, 0, j))],
        out_specs=pl.BlockSpec((1, N_COND, D_MODEL), lambda l, j: (l, 0, j)),
        out_shape=jax.ShapeDtypeStruct((DEPTH, N_COND, 6 * D_MODEL), F32),
        compiler_params=_cparams(("arbitrary", "arbitrary")),
        name="mod",
    )(cond, w, b.reshape(DEPTH, 1, 6 * D_MODEL))


def _inproj_kernel(seq_ref, rope_ref, xp_ref, xs_ref, mod_ref, g1_ref, w_ref, s32_ref, s64_ref,
                   gqc_ref, gkc_ref, gqd_ref, gkd_ref, cosc_ref, sinc_ref, cosd_ref, sind_ref, o_ref):
    del seq_ref, rope_ref
    x = jnp.where(pl.program_id(0) < PROMPT_TILES, xp_ref[...], xs_ref[...])
    xn = x * lax.rsqrt(jnp.mean(x * x, axis=-1, keepdims=True) + EPS) * g1_ref[...]
    mod = mod_ref[0]
    sh1 = mod[:, 0:D_MODEL]
    sc1 = mod[:, D_MODEL:2 * D_MODEL]
    h = xn * (1.0 + sc1) + sh1
    p = _dot(h.astype(BF16), w_ref[0])
    o_ref[:, OFF_AU:OFF_BX] = jax.nn.gelu(p[:, OFF_AU:OFF_BX])
    o_ref[:, OFF_BX:OFF_BG] = p[:, OFF_BX:OFF_BG]
    o_ref[:, OFF_BG:OFF_CQ] = jax.nn.gelu(p[:, OFF_BG:OFF_CQ])
    s32 = s32_ref[...]
    cosc = cosc_ref[...]
    sinc = sinc_ref[...]
    cq = _seg_rms_norm(p[:, OFF_CQ:OFF_CK], s32, gqc_ref[...])
    ck = _seg_rms_norm(p[:, OFF_CK:OFF_CV], s32, gkc_ref[...])
    o_ref[:, OFF_CQ:OFF_CK] = _rope(cq, cosc, sinc, DIFF_QK_DIM // 4)
    o_ref[:, OFF_CK:OFF_CV] = _rope(ck, cosc, sinc, DIFF_QK_DIM // 4)
    o_ref[:, OFF_CV:OFF_DQ] = p[:, OFF_CV:OFF_DQ]
    s64 = s64_ref[...]
    cosd = cosd_ref[...]
    sind = sind_ref[...]
    dq = _seg_rms_norm(p[:, OFF_DQ:OFF_DK], s64, gqd_ref[...])
    dk = _seg_rms_norm(p[:, OFF_DK:OFF_DV], s64[0:128, 0:128], gkd_ref[...])
    o_ref[:, OFF_DQ:OFF_DK] = _rope(dq, cosd, sind, HEAD_DIM // 4)
    o_ref[:, OFF_DK:OFF_DV] = _rope(dk, cosd[:, 0:128], sind[:, 0:128], HEAD_DIM // 4)
    o_ref[:, OFF_DV:IN_WIDTH] = p[:, OFF_DV:IN_WIDTH]


def _inproj_call(layer, seq_of_tile, rope_of_tile, x, mod3, g1, w_in, s32, s64, gqc, gkc, gqd, gkd,
                 cosc, sinc, cosd, sind):
    full = lambda shape: pl.BlockSpec(shape, lambda i, s, r: (0,) * len(shape))
    rope_spec = pl.BlockSpec((TM, GROUP_WIDTH), lambda i, s, r: (r[i], 0))
    grid_spec = pltpu.PrefetchScalarGridSpec(
        num_scalar_prefetch=2,
        grid=(N_TILES,),
        in_specs=[pl.BlockSpec((TM, D_MODEL), lambda i, s, r: (jnp.minimum(i, PROMPT_TILES - 1), 0)),
                  pl.BlockSpec((TM, D_MODEL), lambda i, s, r: (jnp.maximum(i - PROMPT_TILES, 0), 0)),
                  pl.BlockSpec((1, 1, 6 * D_MODEL), lambda i, s, r: (s[i], 0, 0)),
                  full((1, D_MODEL)),
                  pl.BlockSpec((1, D_MODEL, IN_WIDTH), lambda i, s, r: (layer, 0, 0)),
                  full((GROUP_WIDTH, GROUP_WIDTH)),
                  full((GROUP_WIDTH, GROUP_WIDTH)),
                  full((1, GROUP_WIDTH)), full((1, GROUP_WIDTH)), full((1, GROUP_WIDTH)), full((1, 128)),
                  rope_spec, rope_spec, rope_spec, rope_spec],
        out_specs=pl.BlockSpec((TM, IN_WIDTH), lambda i, s, r: (i, 0)),
    )
    return pl.pallas_call(
        _inproj_kernel,
        grid_spec=grid_spec,
        out_shape=jax.ShapeDtypeStruct((N_TOK, IN_WIDTH), F32),
        compiler_params=_cparams(("arbitrary",)),
        name="inproj",
    )(seq_of_tile, rope_of_tile, *x, mod3, g1, w_in, s32, s64, gqc, gkc, gqd, gkd, cosc, sinc, cosd, sind)


def _gmlp_kernel(u_ref, v_ref, s64_ref, g_ref, ws_ref, bias_ref, o_ref):
    vh = _seg_rms_norm(v_ref[...], s64_ref[...], g_ref[...]).astype(BF16)
    head = lax.broadcasted_iota(jnp.int32, (CHUNK, GROUP_WIDTH), 1) // HEAD_DIM
    for ch in range(TM // CHUNK):
        rows = slice(ch * CHUNK, (ch + 1) * CHUNK)
        vc = vh[rows]
        mixed = bias_ref[...]
        for h in range(N_GROUP_HEADS):
            mixed = mixed + jnp.where(head == h, _dot(ws_ref[h], vc), 0.0)
        o_ref[rows, :] = u_ref[rows, :] * mixed


def _gmlp_call(proj, s64, g, ws, bias):
    return pl.pallas_call(
        _gmlp_kernel,
        grid=(N_TILES,),
        in_specs=[pl.BlockSpec((TM, GROUP_WIDTH), lambda i: (i, OFF_AU // GROUP_WIDTH)),
                  pl.BlockSpec((TM, GROUP_WIDTH), lambda i: (i, OFF_AV // GROUP_WIDTH)),
                  pl.BlockSpec((GROUP_WIDTH, GROUP_WIDTH), lambda i: (0, 0)),
                  pl.BlockSpec((1, GROUP_WIDTH), lambda i: (0, 0)),
                  pl.BlockSpec((N_GROUP_HEADS, CHUNK, CHUNK), lambda i: (0, 0, 0)),
                  pl.BlockSpec((CHUNK, GROUP_WIDTH), lambda i: (0, 0))],
        out_specs=pl.BlockSpec((TM, GROUP_WIDTH), lambda i: (i, 0)),
        out_shape=jax.ShapeDtypeStruct((N_TOK, GROUP_WIDTH), F32),
        compiler_params=_cparams(("arbitrary",)),
        name="gmlp",
    )(proj, proj, s64, g, ws, bias)


def _scan_chunk(a, b, reverse):
    n = a.shape[0]
    row = lax.broadcasted_iota(jnp.int32, a.shape, 0)
    s = 1
    while s < n:
        if reverse:
            keep = row < n - s
            shift = n - s
        else:
            keep = row >= s
            shift = s
        a_prev = jnp.where(keep, pltpu.roll(a, shift, axis=0), 1.0)
        b_prev = jnp.where(keep, pltpu.roll(b, shift, axis=0), 0.0)
        b = a * b_prev + b
        a = a * a_prev
        s *= 2
    return a, b


def _lru_kernel(x_ref, g_ref, h0_ref, cw_ref, cb_ref, wcat_ref, bcat_ref, lam_ref, y_ref, st_ref,
                xpad, a_f, b_f, a_b, b_b, h_f, *, seq_len):
    nc = seq_len // LRU_CHUNK
    w = GROUP_WIDTH
    zeros8 = jnp.zeros((8, w), F32)
    xpad[0:8, :] = zeros8
    xpad[seq_len + 8:seq_len + 16, :] = zeros8
    xpad[8:seq_len + 8, :] = x_ref[...]
    sp = _softplus(-lam_ref[...])
    cw = cw_ref[...]
    cb = cb_ref[...]
    win_rows = LRU_CHUNK + 16

    def gates(c, carry):
        r0 = pl.multiple_of(c * LRU_CHUNK, LRU_CHUNK)
        win = xpad[pl.ds(r0, win_rows), :]
        inner = slice(8, 8 + LRU_CHUNK)
        xc = cb + pltpu.roll(win, 2, axis=0)[inner] * cw[0:1]
        xc = xc + pltpu.roll(win, 1, axis=0)[inner] * cw[1:2]
        xc = xc + win[inner] * cw[2:3]
        xc = xc + pltpu.roll(win, win_rows - 1, axis=0)[inner] * cw[3:4]
        sg = jax.nn.sigmoid(_dot(xc.astype(BF16), wcat_ref[...]) + bcat_ref[...])
        for d, (a_ref, b_ref) in enumerate(((a_f, b_f), (a_b, b_b))):
            r = sg[:, (2 * d) * w:(2 * d + 1) * w]
            i = sg[:, (2 * d + 1) * w:(2 * d + 2) * w]
            log_a = (-LRU_C * r) * sp[d:d + 1]
            a = jnp.exp(log_a)
            a_ref[pl.ds(r0, LRU_CHUNK), :] = a
            b_ref[pl.ds(r0, LRU_CHUNK), :] = jnp.sqrt(-jnp.tanh(log_a) * (a * a + 1.0)) * (i * xc)
        return carry

    lax.fori_loop(0, nc, gates, 0)

    def fwd(c, carry):
        r0 = pl.multiple_of(c * LRU_CHUNK, LRU_CHUNK)
        a_cum, h_loc = _scan_chunk(a_f[pl.ds(r0, LRU_CHUNK), :], b_f[pl.ds(r0, LRU_CHUNK), :], False)
        h = h_loc + a_cum * carry
        h_f[pl.ds(r0, LRU_CHUNK), :] = h
        return h[LRU_CHUNK - 1:LRU_CHUNK, :]

    s_f = lax.fori_loop(0, nc, fwd, h0_ref[0, 0:1, :])

    def bwd(k, carry):
        c = nc - 1 - k
        r0 = pl.multiple_of(c * LRU_CHUNK, LRU_CHUNK)
        a_cum, h_loc = _scan_chunk(a_b[pl.ds(r0, LRU_CHUNK), :], b_b[pl.ds(r0, LRU_CHUNK), :], True)
        h = h_loc + a_cum * carry
        y_ref[pl.ds(r0, LRU_CHUNK), :] = (h_f[pl.ds(r0, LRU_CHUNK), :] + h) * g_ref[pl.ds(r0, LRU_CHUNK), :]
        return h[0:1, :]

    s_b = lax.fori_loop(0, nc, bwd, h0_ref[0, 1:2, :])
    st_ref[0, 0:1, :] = s_f
    st_ref[0, 1:2, :] = s_b


def _lru_call(proj, h0, cw, cb, wcat, bcat, lam, *, seq_len, n_seq, row_block0):
    w = GROUP_WIDTH
    full = lambda shape: pl.BlockSpec(shape, lambda b: (0,) * len(shape))
    seq_block = lambda col: pl.BlockSpec((seq_len, w), lambda b: (b + row_block0, col))
    return pl.pallas_call(
        functools.partial(_lru_kernel, seq_len=seq_len),
        grid=(n_seq,),
        in_specs=[seq_block(OFF_BX // w), seq_block(OFF_BG // w),
                  pl.BlockSpec((1, 2, w), lambda b: (b, 0, 0)),
                  full((CONV_W, w)), full((1, w)), full((w, 4 * w)), full((1, 4 * w)), full((2, w))],
        out_specs=[pl.BlockSpec((seq_len, w), lambda b: (b, 0)),
                   pl.BlockSpec((1, 2, w), lambda b: (b, 0, 0))],
        out_shape=[jax.ShapeDtypeStruct((n_seq * seq_len, w), F32),
                   jax.ShapeDtypeStruct((n_seq, 2, w), F32)],
        scratch_shapes=[pltpu.VMEM((seq_len + 16, w), F32)] + [pltpu.VMEM((seq_len, w), F32)] * 5,
        compiler_params=_cparams(("arbitrary",)),
        name="lru_%d" % seq_len,
    )(proj, proj, h0, cw, cb, wcat, bcat, lam)


def _diff_lambda(ld_ref, lam_init):
    ld = ld_ref[...]
    l1 = jnp.sum(ld[0:1] * ld[1:2], axis=-1, keepdims=True)
    l2 = jnp.sum(ld[2:3] * ld[3:4], axis=-1, keepdims=True)
    return jnp.exp(l1) - jnp.exp(l2) + lam_init


def _diff_attn_body(q, keys, values, lam, s64, g, lam_init):
    tq = q.shape[0]
    lane = lax.broadcasted_iota(jnp.int32, (1, GROUP_WIDTH), 1)
    qs = q * (DIFF_QK_DIM ** -0.5 * LOG2_E)
    o = jnp.zeros((tq, GROUP_WIDTH), F32)
    for h in range(N_GROUP_HEADS):
        e_rows, inv = [], []
        for i in range(2):
            seg = h * 2 + i
            qm = jnp.where(lane // DIFF_QK_DIM == seg, qs, 0.0).astype(BF16)
            s = [_dot_nt(qm, k) for k in keys]
            m = functools.reduce(jnp.maximum, [jnp.max(x, axis=-1, keepdims=True) for x in s])
            e = [jnp.exp2(x - m) for x in s]
            den = functools.reduce(jnp.add, [jnp.sum(x, axis=-1, keepdims=True) for x in e])
            inv.append(1.0 / den)
            e_rows.append([x.astype(BF16) for x in e])
        both = None
        for e0, e1, v in zip(e_rows[0], e_rows[1], values):
            part = _dot(jnp.concatenate([e0, e1], axis=0), v)
            both = part if both is None else both + part
        oh = both[:tq] * inv[0] - lam * (both[tq:] * inv[1])
        o = o + jnp.where(lane // HEAD_DIM == h, oh, 0.0)
    return _seg_rms_norm(o, s64, g) * (1.0 - lam_init)


def _diff_prompt_kernel(q_ref, k_ref, v_ref, ld_ref, s64_ref, g_ref, o_ref, *, lam_init):
    lam = _diff_lambda(ld_ref, lam_init)
    o_ref[...] = _diff_attn_body(q_ref[...], [k_ref[...].astype(BF16)], [v_ref[...].astype(BF16)],
                                 lam, s64_ref[...], g_ref[...], lam_init)


def _diff_sample_kernel(q_ref, k_ref, v_ref, ck_ref, cv_ref, ld_ref, s64_ref, g_ref, o_ref, *, lam_init):
    lam = _diff_lambda(ld_ref, lam_init)
    keys = [ck_ref[0, 0].astype(BF16), k_ref[...].astype(BF16)]
    values = [cv_ref[0, 0].astype(BF16), v_ref[...].astype(BF16)]
    o_ref[...] = _diff_attn_body(q_ref[...], keys, values, lam, s64_ref[...], g_ref[...], lam_init)


def _diff_prompt_call(proj, ld, s64, g, lam_init):
    w = GROUP_WIDTH
    full = lambda shape: pl.BlockSpec(shape, lambda b: (0,) * len(shape))
    return pl.pallas_call(
        functools.partial(_diff_prompt_kernel, lam_init=lam_init),
        grid=(BATCH,),
        in_specs=[pl.BlockSpec((SEQ, w), lambda b: (b, OFF_CQ // w)),
                  pl.BlockSpec((SEQ, w), lambda b: (b, OFF_CK // w)),
                  pl.BlockSpec((SEQ, w), lambda b: (b, OFF_CV // w)),
                  full((4, DIFF_QK_DIM)), full((w, w)), full((1, w))],
        out_specs=pl.BlockSpec((SEQ, w), lambda b: (b, 0)),
        out_shape=jax.ShapeDtypeStruct((N_PROMPT, w), F32),
        compiler_params=_cparams(("arbitrary",)),
        name="diff_prompt",
    )(proj, proj, proj, ld, s64, g)


def _diff_sample_call(layer, proj, ctx_k, ctx_v, ld, s64, g, lam_init):
    w = GROUP_WIDTH
    tq = 256
    nq = DEC_SEQ // tq
    full = lambda shape: pl.BlockSpec(shape, lambda b, i: (0,) * len(shape))
    seq_block0 = N_PROMPT // DEC_SEQ
    return pl.pallas_call(
        functools.partial(_diff_sample_kernel, lam_init=lam_init),
        grid=(DEC_BATCH, nq),
        in_specs=[pl.BlockSpec((tq, w), lambda b, i: (N_PROMPT // tq + b * nq + i, OFF_CQ // w)),
                  pl.BlockSpec((DEC_SEQ, w), lambda b, i: (seq_block0 + b, OFF_CK // w)),
                  pl.BlockSpec((DEC_SEQ, w), lambda b, i: (seq_block0 + b, OFF_CV // w)),
                  pl.BlockSpec((1, 1, PAST_LEN, w), lambda b, i: (b, layer, 0, 0)),
                  pl.BlockSpec((1, 1, PAST_LEN, w), lambda b, i: (b, layer, 0, 0)),
                  full((4, DIFF_QK_DIM)), full((w, w)), full((1, w))],
        out_specs=pl.BlockSpec((tq, w), lambda b, i: (b * nq + i, 0)),
        out_shape=jax.ShapeDtypeStruct((N_SAMPLE, w), F32),
        compiler_params=_cparams(("arbitrary", "arbitrary")),
        name="diff_sample",
    )(proj, proj, proj, ctx_k, ctx_v, ld, s64, g)


def _sink_attn_body(q, keys, values, masks, sink_ref):
    tq = q.shape[0]
    lane = lax.broadcasted_iota(jnp.int32, (1, 128), 1)
    qs = q * (HEAD_DIM ** -0.5 * LOG2_E)
    head_rows = lax.broadcasted_iota(jnp.int32, (SWA_HEADS * tq, 1), 0) // tq
    stacked, sink = [], jnp.zeros((SWA_HEADS * tq, 1), F32)
    for grp in range(SWA_GROUPS):
        for kh in range(SWA_KV_HEADS):
            stacked.append(jnp.where(lane // HEAD_DIM == kh, qs[:, grp * 128:(grp + 1) * 128], 0.0).astype(BF16))
            sink = jnp.where(head_rows == len(stacked) - 1, sink_ref[kh * SWA_GROUPS + grp] * LOG2_E, sink)
    q4 = jnp.concatenate(stacked, axis=0)
    s = []
    for k, msk in zip(keys, masks):
        x = _dot_nt(q4, k)
        s.append(x if msk is None else jnp.where(msk, x, -jnp.inf))
    m = functools.reduce(jnp.maximum, [jnp.max(x, axis=-1, keepdims=True) for x in s])
    m = jnp.maximum(m, sink)
    e = [jnp.exp2(x - m) for x in s]
    den = functools.reduce(jnp.add, [jnp.sum(x, axis=-1, keepdims=True) for x in e]) + jnp.exp2(sink - m)
    o4 = None
    for x, v in zip(e, values):
        part = _dot(x.astype(BF16), v)
        o4 = part if o4 is None else o4 + part
    o4 = o4 * (1.0 / den)
    outs = []
    for grp in range(SWA_GROUPS):
        r0 = grp * SWA_KV_HEADS * tq
        outs.append(jnp.where(lane // HEAD_DIM == 0, o4[r0:r0 + tq], o4[r0 + tq:r0 + 2 * tq]))
    return outs


def _swa_prompt_kernel(sink_ref, q_ref, k_ref, v_ref, o_ref):
    outs = _sink_attn_body(q_ref[...], [k_ref[...].astype(BF16)], [v_ref[...].astype(BF16)], [None], sink_ref)
    for grp in range(SWA_GROUPS):
        o_ref[:, grp * 128:(grp + 1) * 128] = outs[grp]


def _swa_sample_kernel(sink_ref, q_ref, kp_ref, kc_ref, kn_ref, vp_ref, vc_ref, vn_ref, ck_ref, cv_ref, o_ref):
    n = pl.program_id(1)
    nb = pl.num_programs(1)
    r = lax.broadcasted_iota(jnp.int32, (SWA_HEADS * WINDOW, WINDOW), 0) & (WINDOW - 1)
    c = lax.broadcasted_iota(jnp.int32, (SWA_HEADS * WINDOW, WINDOW), 1)
    mask_prev = c >= r + jnp.where(n > 0, 0, WINDOW)
    mask_next = c <= r - jnp.where(n < nb - 1, 0, WINDOW)
    keys = [ck_ref[0, 0].astype(BF16), kp_ref[...].astype(BF16), kc_ref[...].astype(BF16), kn_ref[...].astype(BF16)]
    values = [cv_ref[0, 0].astype(BF16), vp_ref[...].astype(BF16), vc_ref[...].astype(BF16), vn_ref[...].astype(BF16)]
    outs = _sink_attn_body(q_ref[...], keys, values, [None, mask_prev, None, mask_next], sink_ref)
    for grp in range(SWA_GROUPS):
        o_ref[:, grp * 128:(grp + 1) * 128] = outs[grp]


def _swa_prompt_call(sink, proj):
    w = GROUP_WIDTH
    return pl.pallas_call(
        _swa_prompt_kernel,
        grid=(BATCH,),
        in_specs=[pl.BlockSpec(memory_space=pltpu.SMEM),
                  pl.BlockSpec((SEQ, w), lambda b: (b, OFF_DQ // w)),
                  pl.BlockSpec((SEQ, 128), lambda b: (b, OFF_DK // 128)),
                  pl.BlockSpec((SEQ, 128), lambda b: (b, OFF_DV // 128))],
        out_specs=pl.BlockSpec((SEQ, w), lambda b: (b, 0)),
        out_shape=jax.ShapeDtypeStruct((N_PROMPT, w), F32),
        compiler_params=_cparams(("arbitrary",)),
        name="swa_prompt",
    )(sink, proj, proj, proj)


def _swa_sample_call(layer, sink, proj, ctx_k, ctx_v):
    w = GROUP_WIDTH
    tq = WINDOW
    nq = DEC_SEQ // tq
    row0 = N_PROMPT // tq

    def kv_spec(col, delta):
        def index(b, i):
            j = jnp.clip(i + delta, 0, nq - 1)
            return (row0 + b * nq + j, col)
        return pl.BlockSpec((tq, 128), index)

    ctx_spec = pl.BlockSpec((1, 1, PAST_LEN, 128), lambda b, i: (b, layer, 0, 0))
    return pl.pallas_call(
        _swa_sample_kernel,
        grid=(DEC_BATCH, nq),
        in_specs=[pl.BlockSpec(memory_space=pltpu.SMEM),
                  pl.BlockSpec((tq, w), lambda b, i: (row0 + b * nq + i, OFF_DQ // w)),
                  kv_spec(OFF_DK // 128, -1), kv_spec(OFF_DK // 128, 0), kv_spec(OFF_DK // 128, 1),
                  kv_spec(OFF_DV // 128, -1), kv_spec(OFF_DV // 128, 0), kv_spec(OFF_DV // 128, 1),
                  ctx_spec, ctx_spec],
        out_specs=pl.BlockSpec((tq, w), lambda b, i: (b * nq + i, 0)),
        out_shape=jax.ShapeDtypeStruct((N_SAMPLE, w), F32),
        compiler_params=_cparams(("arbitrary", "arbitrary")),
        name="swa_sample",
    )(sink, proj, proj, proj, proj, proj, proj, proj, ctx_k, ctx_v)


def _outproj_kernel(seq_ref, ya_ref, ybp_ref, ybs_ref, ycp_ref, ycs_ref, ydp_ref, yds_ref, xp_ref, xs_ref, mod_ref,
                    w_ref, g2_ref, rwt_ref, rbt_ref, triu_ref,
                    x1_ref, h2r_ref, topi_ref, topw_ref, rank_ref, count_ref, cnt_ref):
    del seq_ref
    w = GROUP_WIDTH
    is_prompt = pl.program_id(0) < PROMPT_TILES
    yb = jnp.where(is_prompt, ybp_ref[...], ybs_ref[...])
    yc = jnp.where(is_prompt, ycp_ref[...], ycs_ref[...])
    yd = jnp.where(is_prompt, ydp_ref[...], yds_ref[...])
    mix = _dot(ya_ref[...].astype(BF16), w_ref[0, 0:w, :])
    mix = mix + _dot(yb.astype(BF16), w_ref[0, w:2 * w, :])
    mix = mix + _dot(yc.astype(BF16), w_ref[0, 2 * w:3 * w, :])
    mix = mix + _dot(yd.astype(BF16), w_ref[0, 3 * w:4 * w, :])
    mod = mod_ref[0]
    g1 = mod[:, 2 * D_MODEL:3 * D_MODEL]
    sh2 = mod[:, 3 * D_MODEL:4 * D_MODEL]
    sc2 = mod[:, 4 * D_MODEL:5 * D_MODEL]
    x1 = jnp.where(is_prompt, xp_ref[...], xs_ref[...]) + g1 * mix
    x1_ref[...] = x1
    xn = x1 * lax.rsqrt(jnp.mean(x1 * x1, axis=-1, keepdims=True) + EPS) * g2_ref[...]
    h2 = xn * (1.0 + sc2) + sh2
    for j in range(ROW_PIECES):
        h2r_ref[pl.ds(j, TM, stride=ROW_PIECES), :] = h2[:, j * 128:(j + 1) * 128]
    h_hi, h_lo = _split_bf16(h2)
    r_hi, r_lo = _split_bf16(rwt_ref[...])
    logits = _dot_nt(r_hi, h_hi) + (_dot_nt(r_lo, h_hi) + _dot_nt(r_hi, h_lo)) + rbt_ref[...]
    expert = lax.broadcasted_iota(jnp.int32, logits.shape, 0)
    work = logits
    picks = []
    for _ in range(TOP_K):
        m = jnp.max(work, axis=0, keepdims=True)
        idx = jnp.min(jnp.where(work == m, expert, N_EXPERTS), axis=0, keepdims=True)
        picks.append((m, idx))
        work = jnp.where(expert == idx, -jnp.inf, work)
    top = picks[0][0]
    ex = [jnp.exp(m - top) for m, _ in picks]
    inv = 1.0 / functools.reduce(jnp.add, ex)
    @pl.when(pl.program_id(0) % (MOE_TB // TM) == 0)
    def _():
        cnt_ref[...] = jnp.zeros_like(cnt_ref)

    sel = jnp.zeros(logits.shape, F32)
    for _, idx in picks:
        sel = sel + jnp.where(expert == idx, 1.0, 0.0)
    csum = _dot(sel.astype(BF16), triu_ref[...])
    before = cnt_ref[...] + csum - sel
    cnt_ref[...] += csum[:, TM - 1:TM]
    count_ref[0] = cnt_ref[...].astype(jnp.int32)
    slot = lax.broadcasted_iota(jnp.int32, topi_ref.shape, 0)
    top_i = jnp.zeros(topi_ref.shape, jnp.int32)
    top_w = jnp.zeros(topw_ref.shape, F32)
    rank = jnp.zeros(rank_ref.shape, jnp.int32)
    for k, (e, (_, idx)) in enumerate(zip(ex, picks)):
        rank_k = jnp.sum(jnp.where(expert == idx, before, 0.0), axis=0, keepdims=True).astype(jnp.int32)
        top_i = jnp.where(slot == k, idx, top_i)
        top_w = jnp.where(slot == k, e * inv, top_w)
        rank = jnp.where(slot == k, rank_k, rank)
    topi_ref[...] = top_i
    topw_ref[...] = top_w
    rank_ref[...] = rank


def _outproj_call(layer, seq_of_tile, ya, yb, yc, yd, x, mod3, w_out, g2, rwt, rbt, triu):
    w = GROUP_WIDTH
    full = lambda shape: pl.BlockSpec(shape, lambda i, s: (0,) * len(shape))
    tile = lambda width: pl.BlockSpec((TM, width), lambda i, s: (i, 0))
    prompt_tile = lambda width: pl.BlockSpec((TM, width), lambda i, s: (jnp.minimum(i, PROMPT_TILES - 1), 0))
    sample_tile = lambda width: pl.BlockSpec((TM, width), lambda i, s: (jnp.maximum(i - PROMPT_TILES, 0), 0))
    slots = pl.BlockSpec((TOPK_PAD, TM), lambda i, s: (0, i))
    tiles_per_block = MOE_TB // TM
    return pl.pallas_call(
        _outproj_kernel,
        grid_spec=pltpu.PrefetchScalarGridSpec(
            num_scalar_prefetch=1,
            grid=(N_TILES,),
            in_specs=[tile(w), prompt_tile(w), sample_tile(w), prompt_tile(w), sample_tile(w),
                      prompt_tile(w), sample_tile(w), prompt_tile(D_MODEL), sample_tile(D_MODEL),
                      pl.BlockSpec((1, 1, 6 * D_MODEL), lambda i, s: (s[i], 0, 0)),
                      pl.BlockSpec((1, D_MODEL, D_MODEL), lambda i, s: (layer, 0, 0)), full((1, D_MODEL)),
                      full((N_EXPERTS, D_MODEL)), full((N_EXPERTS, 1)), full((TM, TM))],
            out_specs=[tile(D_MODEL), pl.BlockSpec((TM * ROW_PIECES, 128), lambda i, s: (i, 0)),
                       slots, slots, slots,
                       pl.BlockSpec((1, N_EXPERTS, 1), lambda i, s: (i // tiles_per_block, 0, 0))],
            scratch_shapes=[pltpu.VMEM((N_EXPERTS, 1), F32)],
        ),
        out_shape=[jax.ShapeDtypeStruct((N_TOK, D_MODEL), F32),
                   jax.ShapeDtypeStruct((N_TOK * ROW_PIECES, 128), F32),
                   jax.ShapeDtypeStruct((TOPK_PAD, N_TOK), jnp.int32),
                   jax.ShapeDtypeStruct((TOPK_PAD, N_TOK), F32),
                   jax.ShapeDtypeStruct((TOPK_PAD, N_TOK), jnp.int32),
                   jax.ShapeDtypeStruct((N_TOK // MOE_TB, N_EXPERTS, 1), jnp.int32)],
        compiler_params=_cparams(("arbitrary",)),
        name="outproj",
    )(seq_of_tile, ya, *yb, *yc, *yd, *x, mod3, w_out, g2, rwt, rbt, triu)


def _moe_kernel(dest_ref, topw_ref, gstart_ref, count_ref, src_ref, wgu_ref, bgu_ref, wd_ref, bd_ref,
                acc_ref, pair_ref, xt_ref, y3_ref):
    expert = pl.program_id(0) % N_EXPERTS

    def token_of(pair):
        return (pair >> MOE_PAIR_SHIFT) & (MOE_TB - 1)

    @pl.when(expert == 0)
    def _():
        def zero(c, carry):
            acc_ref[pl.ds(pl.multiple_of(c * MOE_TILE, MOE_TILE), MOE_TILE)] = jnp.zeros((MOE_TILE, 8, 128), F32)
            return carry
        lax.fori_loop(0, MOE_TB // MOE_TILE, zero, 0)

        def pad_group(e, carry):
            cnt = count_ref[0, 0, e]
            first_pad = gstart_ref[0, 0, e] + cnt
            n_pad = (-cnt) & (MOE_TILE - 1)

            def pad_row(r, c2):
                pair_ref[first_pad + r] = MOE_PAD_PAIR
                return c2
            lax.fori_loop(0, n_pad, pad_row, 0)
            return carry
        lax.fori_loop(0, N_EXPERTS, pad_group, 0)

        def place(c, carry):
            for k in range(MOE_PLACE_UNROLL):
                pair = c * MOE_PLACE_UNROLL + k
                pair_ref[dest_ref[0, 0, pair]] = pair
            return carry
        lax.fori_loop(0, MOE_TB * TOP_K // MOE_PLACE_UNROLL, place, 0)

    def tile(t, carry):
        row0 = gstart_ref[0, 0, expert] + t * MOE_TILE
        for m in range(MOE_TILE):
            tok = token_of(pair_ref[row0 + m])
            slab = src_ref[pl.ds(pl.multiple_of(tok * ROW_PIECES, ROW_PIECES), ROW_PIECES), :]
            xt_ref[pl.ds(m, ROW_PIECES, stride=MOE_XT_STRIDE), :] = slab
        x = jnp.concatenate([xt_ref[j * MOE_XT_STRIDE:j * MOE_XT_STRIDE + MOE_TILE, :] for j in range(ROW_PIECES)],
                            axis=1).astype(BF16)
        gu = _dot(x, wgu_ref[0, 0]) + bgu_ref[0, 0]
        gate = jnp.minimum(gu[:, :D_FF], SWIGLU_LIMIT)
        up = jnp.clip(gu[:, D_FF:], -SWIGLU_LIMIT, SWIGLU_LIMIT)
        act = (up + 1.0) * gate * jax.nn.sigmoid(SWIGLU_ALPHA * gate)
        y = _dot(act.astype(BF16), wd_ref[0, 0]) + bd_ref[0, 0]
        for j in range(ROW_PIECES):
            y3_ref[:, j * 8:(j + 1) * 8, :] = y[:, j * 128:(j + 1) * 128].reshape(MOE_TILE // 8, 8, 128)
        for m0 in range(0, MOE_TILE, MOE_RMW_BATCH):
            rows = range(m0, m0 + MOE_RMW_BATCH)
            pairs = [pair_ref[row0 + m] for m in rows]
            toks = [token_of(p) for p in pairs]
            vals = [acc_ref[tok] + topw_ref[0, 0, p] * y3_ref[m // 8, pl.ds(m % 8, 8, stride=8), :]
                    for tok, p, m in zip(toks, pairs, rows)]
            for tok, v in reversed(list(zip(toks, vals))):
                acc_ref[tok] = v
        return carry

    n_tiles = (count_ref[0, 0, expert] + (MOE_TILE - 1)) // MOE_TILE
    lax.fori_loop(0, n_tiles, tile, 0)


def _moe_call(layer, dest, topw, gstart, counts, h2_rows, wgu, bgu, wd, bd):
    once = pl.Buffered(1)
    n_pairs = MOE_TB * TOP_K
    n_blocks = N_TOK // MOE_TB

    def smem_block(width):
        return pl.BlockSpec((1, 1, width), lambda g: (g // N_EXPERTS, 0, 0), memory_space=pltpu.SMEM,
                            pipeline_mode=once)

    def expert_block(rows, cols):
        return pl.BlockSpec((1, 1, rows, cols), lambda g: (layer, g % N_EXPERTS, 0, 0))

    return pl.pallas_call(
        _moe_kernel,
        grid=(n_blocks * N_EXPERTS,),
        in_specs=[smem_block(n_pairs), smem_block(n_pairs + 128), smem_block(N_EXPERTS), smem_block(N_EXPERTS),
                  pl.BlockSpec((MOE_TB * ROW_PIECES, 128), lambda g: (g // N_EXPERTS, 0), pipeline_mode=once),
                  expert_block(D_MODEL, 2 * D_FF), expert_block(1, 2 * D_FF),
                  expert_block(D_FF, D_MODEL), expert_block(1, D_MODEL)],
        out_specs=pl.BlockSpec((MOE_TB, 8, 128), lambda g: (g // N_EXPERTS, 0, 0), pipeline_mode=once),
        scratch_shapes=[pltpu.SMEM((MOE_ROWS_PER_BLOCK,), jnp.int32),
                        pltpu.VMEM((ROW_PIECES * MOE_XT_STRIDE, 128), F32),
                        pltpu.VMEM((MOE_TILE // 8, 8 * ROW_PIECES, 128), F32)],
        out_shape=jax.ShapeDtypeStruct((N_TOK, 8, 128), F32),
        compiler_params=_cparams(("arbitrary",)),
        name="moe",
    )(dest, topw, gstart, counts, h2_rows, wgu, bgu, wd, bd)


def _moe_routing(top_i, top_w, rank, counts):
    nb = N_TOK // MOE_TB
    counts = counts.reshape(nb, N_EXPERTS)
    padded = ((counts + MOE_TILE - 1) // MOE_TILE) * MOE_TILE
    gstart_local = jnp.cumsum(padded, axis=1) - padded
    experts = jnp.arange(N_EXPERTS, dtype=jnp.int32)
    pair_major = lambda a: jnp.transpose(a.reshape(TOP_K, nb, MOE_TB), (1, 2, 0)).reshape(nb, 1, MOE_TB * TOP_K)
    picked = top_i.reshape(TOP_K, nb, MOE_TB, 1) == experts
    dest = jnp.sum(jnp.where(picked, gstart_local[None, :, None, :], 0), axis=-1) + rank.reshape(TOP_K, nb, MOE_TB)
    dest = pair_major(dest).astype(jnp.int32)
    topw = jnp.concatenate([pair_major(top_w), jnp.zeros((nb, 1, 128), F32)], axis=-1)
    return (dest, topw, gstart_local.reshape(nb, 1, N_EXPERTS).astype(jnp.int32),
            counts.reshape(nb, 1, N_EXPERTS))


def _residual_kernel(seq_ref, x1_ref, moe_ref, mod_ref, o_ref):
    del seq_ref
    g2 = mod_ref[0][:, 5 * D_MODEL:6 * D_MODEL]
    moe = jnp.concatenate([moe_ref[pl.ds(j, TM, stride=ROW_PIECES), :] for j in range(ROW_PIECES)], axis=1)
    o_ref[...] = x1_ref[...] + g2 * moe


def _residual_call(seq_of_tile, x1, moe_rows, mod3, first_tile, n_tiles):
    return pl.pallas_call(
        _residual_kernel,
        grid_spec=pltpu.PrefetchScalarGridSpec(
            num_scalar_prefetch=1,
            grid=(n_tiles,),
            in_specs=[pl.BlockSpec((TM, D_MODEL), lambda i, s: (i + first_tile, 0)),
                      pl.BlockSpec((TM * ROW_PIECES, 128), lambda i, s: (i + first_tile, 0)),
                      pl.BlockSpec((1, 1, 6 * D_MODEL), lambda i, s: (s[i + first_tile], 0, 0))],
            out_specs=pl.BlockSpec((TM, D_MODEL), lambda i, s: (i, 0)),
        ),
        out_shape=jax.ShapeDtypeStruct((n_tiles * TM, D_MODEL), F32),
        compiler_params=_cparams(("arbitrary",)),
        name="residual",
    )(seq_of_tile, x1, moe_rows, mod3)


def _segment_matrix(width, seg):
    idx = np.arange(width) // seg
    return jnp.asarray((idx[:, None] == idx[None, :]).astype(np.float32) / seg, BF16)


def _rope_tables(rot_dim, width):
    rows = DEC_SEQ // GRID_W
    nf = rot_dim // 4
    f32 = np.float32
    inv = (f32(1.0) / (f32(ROPE_BASE) ** (np.arange(nf, dtype=f32) / f32(nf)))).astype(f32)
    row = np.repeat(np.arange(rows, dtype=f32), GRID_W)
    col = np.tile(np.arange(GRID_W, dtype=f32), rows)
    ang = np.stack([row[:, None] * inv, col[:, None] * inv], axis=1).astype(f32)
    cos, sin = np.cos(ang).astype(f32), np.sin(ang).astype(f32)
    cos_r = np.concatenate([cos[:, 0], cos[:, 0], cos[:, 1], cos[:, 1]], axis=-1)
    sin_r = np.concatenate([-sin[:, 0], sin[:, 0], -sin[:, 1], sin[:, 1]], axis=-1)
    reps = width // rot_dim
    cos_t = np.concatenate([np.ones((TM, width), f32), np.tile(cos_r, (1, reps))], axis=0)
    sin_t = np.concatenate([np.zeros((TM, width), f32), np.tile(sin_r, (1, reps))], axis=0)
    return jnp.asarray(cos_t), jnp.asarray(sin_t)


def _block_diag(wb):
    nb, n = wb.shape[-3], wb.shape[-1]
    rows = wb.reshape(wb.shape[:-3] + (nb * n, n))
    tiled = jnp.tile(rows, (1,) * (wb.ndim - 2) + (nb,))
    blk = np.arange(nb * n) // n
    return jnp.where(jnp.asarray(blk[:, None] == blk[None, :]), tiled, 0.0)


_DQ_HEADS = [kh * SWA_GROUPS + g for g in range(SWA_GROUPS) for kh in range(SWA_KV_HEADS)]


def kernel(x_prompt, x_sample, c, cache_diff_k, cache_diff_v, cache_swa_k, cache_swa_v, state_lru, c_ctx, mod_w, mod_b, norm1_g, norm2_g, w_in, w_out, mlp_vnorm_g, mlp_ws, mlp_bs, lru_conv_w, lru_conv_b, lru_wa, lru_ba, lru_wx, lru_bx, lru_lambda, diff_qnorm_g, diff_knorm_g, diff_lambda, diff_subln_g, swa_qnorm_g, swa_knorm_g, swa_sink, router_w, router_b, moe_w_gu, moe_b_gu, moe_w_down, moe_b_down):
    params = dict(locals())
    consts = _constants()
    prep = _prepare(params, consts)
    x = (x_prompt.reshape(N_PROMPT, D_MODEL), x_sample.reshape(N_SAMPLE, D_MODEL))
    dk_l, dv_l, sk_l, sv_l, lru_l = [], [], [], [], []
    for l in range(DEPTH):
        st = _mixer_stage(x, params, prep, l, consts)
        x = _ffn_stage(x, st, params, prep, l, consts)
        pp = st["proj"][:N_PROMPT]
        dk_l.append(pp[:, OFF_CK:OFF_CV].reshape(BATCH, SEQ, N_GROUP_HEADS, 2, DIFF_QK_DIM))
        dv_l.append(pp[:, OFF_CV:OFF_DQ].reshape(BATCH, SEQ, N_GROUP_HEADS, HEAD_DIM))
        sk_l.append(pp[:, OFF_DK:OFF_DV].reshape(BATCH, SEQ, SWA_KV_HEADS, HEAD_DIM))
        sv_l.append(pp[:, OFF_DV:IN_WIDTH].reshape(BATCH, SEQ, SWA_KV_HEADS, HEAD_DIM))
        lru_l.append(st["st_p"])

    y_p = x[0].reshape(BATCH, SEQ, D_MODEL)
    y_s = x[1].reshape(DEC_BATCH, DEC_SEQ, D_MODEL)
    return (y_p, y_s, jnp.stack(dk_l, axis=1), jnp.stack(dv_l, axis=1), jnp.stack(sk_l, axis=1),
            jnp.stack(sv_l, axis=1), jnp.stack(lru_l, axis=1))


def _constants():
    w = GROUP_WIDTH
    tiles = np.arange(N_TILES)
    seq_np = np.where(tiles < PROMPT_TILES, 0, 1 + (tiles - PROMPT_TILES) // SAMPLE_TILES_PER_SEQ)
    rope_np = np.where(tiles < PROMPT_TILES, 0, 1 + (tiles - PROMPT_TILES) % SAMPLE_TILES_PER_SEQ)
    seq_of_tile = [jnp.asarray(seq_np + l * N_COND, jnp.int32) for l in range(DEPTH)]
    rope_of_tile = jnp.asarray(rope_np, jnp.int32)

    s32 = _segment_matrix(w, DIFF_QK_DIM)
    s64 = _segment_matrix(w, HEAD_DIM)
    cosc, sinc = _rope_tables(DIFF_QK_DIM, w)
    cosd, sind = _rope_tables(HEAD_DIM, w)
    dq_cols = np.concatenate([np.arange(h * HEAD_DIM, (h + 1) * HEAD_DIM) for h in _DQ_HEADS])
    triu = jnp.asarray(np.triu(np.ones((TM, TM), np.float32)), BF16)
    return dict(seq_of_tile=seq_of_tile, rope_of_tile=rope_of_tile, triu=triu, s32=s32, s64=s64,
                cosc=cosc, sinc=sinc, cosd=cosd, sind=sind, dq_cols=dq_cols)


def _prepare(params, consts):
    w = GROUP_WIDTH
    p = params
    dq = consts["dq_cols"]
    heads = lambda g, reps: jnp.tile(g.reshape(DEPTH, 1, -1), (1, 1, reps))
    w_in, w_out = p["w_in"], p["w_out"]
    lru_w = jnp.stack([p["lru_wa"][:, 0], p["lru_wx"][:, 0], p["lru_wa"][:, 1], p["lru_wx"][:, 1]], axis=1)
    lru_b = jnp.stack([p["lru_ba"][:, 0], p["lru_bx"][:, 0], p["lru_ba"][:, 1], p["lru_bx"][:, 1]], axis=1)
    cond = jnp.concatenate([p["c_ctx"][None], p["c"], jnp.zeros((N_COND - 1 - DEC_BATCH, D_MODEL), F32)], axis=0)
    return dict(
        mod3=_mod_call(cond, p["mod_w"], p["mod_b"]).reshape(DEPTH * N_COND, 1, 6 * D_MODEL),
        w_in=jnp.concatenate([w_in[..., :OFF_DQ], w_in[..., OFF_DQ:OFF_DK][..., dq], w_in[..., OFF_DK:]],
                             axis=-1).astype(BF16),
        w_out=jnp.concatenate([w_out[:, :3 * w], w_out[:, 3 * w:][:, dq]], axis=1).astype(BF16),
        gqc=heads(p["diff_qnorm_g"], N_GROUP_HEADS), gkc=heads(p["diff_knorm_g"], N_GROUP_HEADS),
        gqd=heads(p["swa_qnorm_g"], SWA_HEADS), gkd=heads(p["swa_knorm_g"], SWA_KV_HEADS),
        g_sub=heads(p["diff_subln_g"], N_GROUP_HEADS), g_mlp=p["mlp_vnorm_g"].reshape(DEPTH, 1, w),
        mlp_bias=jnp.repeat(jnp.swapaxes(p["mlp_bs"], 1, 2), HEAD_DIM, axis=2),
        mlp_ws=p["mlp_ws"].astype(BF16),
        wcat=jnp.swapaxes(_block_diag(lru_w), 1, 2).reshape(DEPTH, w, 4 * w).astype(BF16),
        bcat=lru_b.reshape(DEPTH, 1, 4 * w),
        conv_b=p["lru_conv_b"].reshape(DEPTH, 1, w),
        g1=p["norm1_g"].reshape(DEPTH, 1, D_MODEL), g2=p["norm2_g"].reshape(DEPTH, 1, D_MODEL),
        router_wt=jnp.swapaxes(p["router_w"], 1, 2), router_bt=p["router_b"].reshape(DEPTH, N_EXPERTS, 1),
        moe_w_gu=p["moe_w_gu"].astype(BF16), moe_w_down=p["moe_w_down"].astype(BF16),
        moe_b_gu=p["moe_b_gu"].reshape(DEPTH, N_EXPERTS, 1, 2 * D_FF),
        moe_b_down=p["moe_b_down"].reshape(DEPTH, N_EXPERTS, 1, D_MODEL),
        ctx_diff_k=p["cache_diff_k"].reshape(DEC_BATCH, DEPTH, PAST_LEN, w),
        ctx_diff_v=p["cache_diff_v"].reshape(DEC_BATCH, DEPTH, PAST_LEN, w),
        ctx_swa_k=p["cache_swa_k"].reshape(DEC_BATCH, DEPTH, PAST_LEN, SWA_KV_HEADS * HEAD_DIM),
        ctx_swa_v=p["cache_swa_v"].reshape(DEC_BATCH, DEPTH, PAST_LEN, SWA_KV_HEADS * HEAD_DIM),
    )


def _mixer_stage(x, params, prep, l, consts):
    w = GROUP_WIDTH
    s32, s64 = consts["s32"], consts["s64"]
    lam_init = 0.8 - 0.6 * math.exp(-0.3 * l)
    proj = _inproj_call(l, consts["seq_of_tile"][l], consts["rope_of_tile"], x, prep["mod3"], prep["g1"][l],
                        prep["w_in"], s32, s64, prep["gqc"][l], prep["gkc"][l], prep["gqd"][l], prep["gkd"][l],
                        consts["cosc"], consts["sinc"], consts["cosd"], consts["sind"])
    ya = _gmlp_call(proj, s64, prep["g_mlp"][l], prep["mlp_ws"][l], prep["mlp_bias"][l])

    lru_args = (params["lru_conv_w"][l], prep["conv_b"][l], prep["wcat"][l], prep["bcat"][l], params["lru_lambda"][l])
    yb_p, st_p = _lru_call(proj, jnp.zeros((BATCH, 2, w), F32), *lru_args, seq_len=SEQ, n_seq=BATCH, row_block0=0)
    yb_s, _ = _lru_call(proj, params["state_lru"][:, l], *lru_args, seq_len=DEC_SEQ, n_seq=DEC_BATCH,
                        row_block0=N_PROMPT // DEC_SEQ)

    ld = params["diff_lambda"][l]
    yc_p = _diff_prompt_call(proj, ld, s64, prep["g_sub"][l], lam_init)
    yc_s = _diff_sample_call(l, proj, prep["ctx_diff_k"], prep["ctx_diff_v"], ld, s64, prep["g_sub"][l], lam_init)

    sink = params["swa_sink"][l]
    yd_p = _swa_prompt_call(sink, proj)
    yd_s = _swa_sample_call(l, sink, proj, prep["ctx_swa_k"], prep["ctx_swa_v"])
    return dict(proj=proj, ya=ya, yb=(yb_p, yb_s), yc=(yc_p, yc_s), yd=(yd_p, yd_s), st_p=st_p)


def _ffn_stage(x, st, params, prep, l, consts):
    seq_of_tile = consts["seq_of_tile"][l]
    x1, h2_rows, top_i, top_w, rank, counts = _outproj_call(
        l, seq_of_tile, st["ya"], st["yb"], st["yc"], st["yd"], x, prep["mod3"], prep["w_out"], prep["g2"][l],
        prep["router_wt"][l], prep["router_bt"][l], consts["triu"])
    routing = _moe_routing(top_i[:TOP_K], top_w[:TOP_K], rank[:TOP_K], counts)
    moe = _moe_call(l, *routing, h2_rows, prep["moe_w_gu"], prep["moe_b_gu"], prep["moe_w_down"], prep["moe_b_down"])
    moe_rows = moe.reshape(N_TOK * ROW_PIECES, 128)
    return (_residual_call(seq_of_tile, x1, moe_rows, prep["mod3"], 0, PROMPT_TILES),
            _residual_call(seq_of_tile, x1, moe_rows, prep["mod3"], PROMPT_TILES, N_TILES - PROMPT_TILES))
```

```python
import functools
import math

import jax
import jax.numpy as jnp
import numpy as np
from jax import lax
from jax.experimental import pallas as pl
from jax.experimental.pallas import tpu as pltpu

F32 = jnp.float32
BF16 = jnp.bfloat16

D_MODEL = 1024
BATCH = 16
SEQ = 256
DEPTH = 2
DEC_BATCH = 4
DEC_SEQ = 2048
PAST_LEN = 512
GRID_W = 64
HEAD_DIM = 64
GROUP_WIDTH = 256
N_GROUP_HEADS = 4
CHUNK = 128
LRU_C = 8.0
CONV_W = 4
DIFF_QK_DIM = 32
SWA_KV_HEADS = 2
SWA_GROUPS = 2
SWA_HEADS = SWA_KV_HEADS * SWA_GROUPS
WINDOW = 128
N_EXPERTS = 32
TOP_K = 4
D_FF = 1024
SWIGLU_LIMIT = 7.0
SWIGLU_ALPHA = 1.702
ROPE_BASE = 10000.0
EPS = 1e-6
LOG2_E = math.log2(math.e)

N_PROMPT = BATCH * SEQ
N_SAMPLE = DEC_BATCH * DEC_SEQ
N_TOK = N_PROMPT + N_SAMPLE
N_COND = 8
TM = 256
N_TILES = N_TOK // TM
PROMPT_TILES = N_PROMPT // TM
SAMPLE_TILES_PER_SEQ = DEC_SEQ // TM
IN_WIDTH = 2304
OFF_AU, OFF_AV, OFF_BX, OFF_BG, OFF_CQ, OFF_CK, OFF_CV, OFF_DQ, OFF_DK, OFF_DV = (
    0, 256, 512, 768, 1024, 1280, 1536, 1792, 2048, 2176)
TOPK_PAD = 8
ROW_PIECES = D_MODEL // 128
MOE_TILE = 256
MOE_TB = 4096
MOE_XT_STRIDE = MOE_TILE + 8
MOE_RMW_BATCH = 4
MOE_PLACE_UNROLL = 16
MOE_ROWS_PER_BLOCK = MOE_TB * TOP_K + N_EXPERTS * MOE_TILE
MOE_PAD_PAIR = MOE_TB * TOP_K
MOE_PAIR_SHIFT = TOP_K.bit_length() - 1
assert 1 << MOE_PAIR_SHIFT == TOP_K
LRU_CHUNK = 256
VMEM_LIMIT = 56 * 1024 * 1024


def _cparams(sem):
    return pltpu.CompilerParams(dimension_semantics=sem, vmem_limit_bytes=VMEM_LIMIT)


def _dot(a, b):
    return jnp.dot(a, b, preferred_element_type=F32)


def _dot_nt(a, b):
    return lax.dot_general(a, b, (((1,), (1,)), ((), ())), preferred_element_type=F32)


def _split_bf16(x):
    hi = x.astype(BF16)
    lo = (x - hi.astype(F32)).astype(BF16)
    return hi, lo


def _seg_rms_norm(x, seg_mat, g):
    hi, lo = _split_bf16(x * x)
    ms = _dot(hi, seg_mat) + _dot(lo, seg_mat)
    return x * lax.rsqrt(ms + EPS) * g


def _rope(x, cos_t, sin_t, nf):
    n = x.shape[-1]
    lane = lax.broadcasted_iota(jnp.int32, x.shape, 1)
    first = (lane & (2 * nf - 1)) < nf
    partner = jnp.where(first, pltpu.roll(x, n - nf, axis=1), pltpu.roll(x, nf, axis=1))
    return x * cos_t + partner * sin_t


def _softplus(x):
    return jnp.maximum(x, 0.0) + jnp.log1p(jnp.exp(-jnp.abs(x)))


def _mod_kernel(cond_ref, w_ref, b_ref, o_ref):
    c = cond_ref[...]
    s = c * jax.nn.sigmoid(c)
    o_ref[0] = _dot(s.astype(BF16), w_ref[0].astype(BF16)) + b_ref[0]


def _mod_call(cond, w, b):
    nb = 6
    return pl.pallas_call(
        _mod_kernel,
        grid=(DEPTH, nb),
        in_specs=[pl.BlockSpec((N_COND, D_MODEL), lambda l, j: (0, 0)),
                  pl.BlockSpec((1, D_MODEL, D_MODEL), lambda l, j: (l, 0, j)),
                  pl.BlockSpec((1, 1, D_MODEL), lambda l, j: (l, 0, j))],
        out_specs=pl.BlockSpec((1, N_COND, D_MODEL), lambda l, j: (l, 0, j)),
        out_shape=jax.ShapeDtypeStruct((DEPTH, N_COND, 6 * D_MODEL), F32),
        compiler_params=_cparams(("arbitrary", "arbitrary")),
        name="mod",
    )(cond, w, b.reshape(DEPTH, 1, 6 * D_MODEL))


def _inproj_kernel(seq_ref, rope_ref, xp_ref, xs_ref, mod_ref, g1_ref, w_ref, s32_ref, s64_ref,
                   gqc_ref, gkc_ref, gqd_ref, gkd_ref, cosc_ref, sinc_ref, cosd_ref, sind_ref, o_ref):
    del seq_ref, rope_ref
    x = jnp.where(pl.program_id(0) < PROMPT_TILES, xp_ref[...], xs_ref[...])
    xn = x * lax.rsqrt(jnp.mean(x * x, axis=-1, keepdims=True) + EPS) * g1_ref[...]
    mod = mod_ref[0]
    sh1 = mod[:, 0:D_MODEL]
    sc1 = mod[:, D_MODEL:2 * D_MODEL]
    h = xn * (1.0 + sc1) + sh1
    p = _dot(h.astype(BF16), w_ref[0])
    o_ref[:, OFF_AU:OFF_BX] = jax.nn.gelu(p[:, OFF_AU:OFF_BX])
    o_ref[:, OFF_BX:OFF_BG] = p[:, OFF_BX:OFF_BG]
    o_ref[:, OFF_BG:OFF_CQ] = jax.nn.gelu(p[:, OFF_BG:OFF_CQ])
    s32 = s32_ref[...]
    cosc = cosc_ref[...]
    sinc = sinc_ref[...]
    cq = _seg_rms_norm(p[:, OFF_CQ:OFF_CK], s32, gqc_ref[...])
    ck = _seg_rms_norm(p[:, OFF_CK:OFF_CV], s32, gkc_ref[...])
    o_ref[:, OFF_CQ:OFF_CK] = _rope(cq, cosc, sinc, DIFF_QK_DIM // 4)
    o_ref[:, OFF_CK:OFF_CV] = _rope(ck, cosc, sinc, DIFF_QK_DIM // 4)
    o_ref[:, OFF_CV:OFF_DQ] = p[:, OFF_CV:OFF_DQ]
    s64 = s64_ref[...]
    cosd = cosd_ref[...]
    sind = sind_ref[...]
    dq = _seg_rms_norm(p[:, OFF_DQ:OFF_DK], s64, gqd_ref[...])
    dk = _seg_rms_norm(p[:, OFF_DK:OFF_DV], s64[0:128, 0:128], gkd_ref[...])
    o_ref[:, OFF_DQ:OFF_DK] = _rope(dq, cosd, sind, HEAD_DIM // 4)
    o_ref[:, OFF_DK:OFF_DV] = _rope(dk, cosd[:, 0:128], sind[:, 0:128], HEAD_DIM // 4)
    o_ref[:, OFF_DV:IN_WIDTH] = p[:, OFF_DV:IN_WIDTH]


def _inproj_call(layer, seq_of_tile, rope_of_tile, x, mod3, g1, w_in, s32, s64, gqc, gkc, gqd, gkd,
                 cosc, sinc, cosd, sind):
    full = lambda shape: pl.BlockSpec(shape, lambda i, s, r: (0,) * len(shape))
    rope_spec = pl.BlockSpec((TM, GROUP_WIDTH), lambda i, s, r: (r[i], 0))
    grid_spec = pltpu.PrefetchScalarGridSpec(
        num_scalar_prefetch=2,
        grid=(N_TILES,),
        in_specs=[pl.BlockSpec((TM, D_MODEL), lambda i, s, r: (jnp.minimum(i, PROMPT_TILES - 1), 0)),
                  pl.BlockSpec((TM, D_MODEL), lambda i, s, r: (jnp.maximum(i - PROMPT_TILES, 0), 0)),
                  pl.BlockSpec((1, 1, 6 * D_MODEL), lambda i, s, r: (s[i], 0, 0)),
                  full((1, D_MODEL)),
                  pl.BlockSpec((1, D_MODEL, IN_WIDTH), lambda i, s, r: (layer, 0, 0)),
                  full((GROUP_WIDTH, GROUP_WIDTH)),
                  full((GROUP_WIDTH, GROUP_WIDTH)),
                  full((1, GROUP_WIDTH)), full((1, GROUP_WIDTH)), full((1, GROUP_WIDTH)), full((1, 128)),
                  rope_spec, rope_spec, rope_spec, rope_spec],
        out_specs=pl.BlockSpec((TM, IN_WIDTH), lambda i, s, r: (i, 0)),
    )
    return pl.pallas_call(
        _inproj_kernel,
        grid_spec=grid_spec,
        out_shape=jax.ShapeDtypeStruct((N_TOK, IN_WIDTH), F32),
        compiler_params=_cparams(("arbitrary",)),
        name="inproj",
    )(seq_of_tile, rope_of_tile, *x, mod3, g1, w_in, s32, s64, gqc, gkc, gqd, gkd, cosc, sinc, cosd, sind)


def _gmlp_kernel(u_ref, v_ref, s64_ref, g_ref, ws_ref, bias_ref, o_ref):
    vh = _seg_rms_norm(v_ref[...], s64_ref[...], g_ref[...]).astype(BF16)
    head = lax.broadcasted_iota(jnp.int32, (CHUNK, GROUP_WIDTH), 1) // HEAD_DIM
    for ch in range(TM // CHUNK):
        rows = slice(ch * CHUNK, (ch + 1) * CHUNK)
        vc = vh[rows]
        mixed = bias_ref[...]
        for h in range(N_GROUP_HEADS):
            mixed = mixed + jnp.where(head == h, _dot(ws_ref[h], vc), 0.0)
        o_ref[rows, :] = u_ref[rows, :] * mixed


def _gmlp_call(proj, s64, g, ws, bias):
    return pl.pallas_call(
        _gmlp_kernel,
        grid=(N_TILES,),
        in_specs=[pl.BlockSpec((TM, GROUP_WIDTH), lambda i: (i, OFF_AU // GROUP_WIDTH)),
                  pl.BlockSpec((TM, GROUP_WIDTH), lambda i: (i, OFF_AV // GROUP_WIDTH)),
                  pl.BlockSpec((GROUP_WIDTH, GROUP_WIDTH), lambda i: (0, 0)),
                  pl.BlockSpec((1, GROUP_WIDTH), lambda i: (0, 0)),
                  pl.BlockSpec((N_GROUP_HEADS, CHUNK, CHUNK), lambda i: (0, 0, 0)),
                  pl.BlockSpec((CHUNK, GROUP_WIDTH), lambda i: (0, 0))],
        out_specs=pl.BlockSpec((TM, GROUP_WIDTH), lambda i: (i, 0)),
        out_shape=jax.ShapeDtypeStruct((N_TOK, GROUP_WIDTH), F32),
        compiler_params=_cparams(("arbitrary",)),
        name="gmlp",
    )(proj, proj, s64, g, ws, bias)


def _scan_chunk(a, b, reverse):
    n = a.shape[0]
    row = lax.broadcasted_iota(jnp.int32, a.shape, 0)
    s = 1
    while s < n:
        if reverse:
            keep = row < n - s
            shift = n - s
        else:
            keep = row >= s
            shift = s
        a_prev = jnp.where(keep, pltpu.roll(a, shift, axis=0), 1.0)
        b_prev = jnp.where(keep, pltpu.roll(b, shift, axis=0), 0.0)
        b = a * b_prev + b
        a = a * a_prev
        s *= 2
    return a, b


def _lru_kernel(x_ref, g_ref, h0_ref, cw_ref, cb_ref, wcat_ref, bcat_ref, lam_ref, y_ref, st_ref,
                xpad, a_f, b_f, a_b, b_b, h_f, *, seq_len):
    nc = seq_len // LRU_CHUNK
    w = GROUP_WIDTH
    zeros8 = jnp.zeros((8, w), F32)
    xpad[0:8, :] = zeros8
    xpad[seq_len + 8:seq_len + 16, :] = zeros8
    xpad[8:seq_len + 8, :] = x_ref[...]
    sp = _softplus(-lam_ref[...])
    cw = cw_ref[...]
    cb = cb_ref[...]
    win_rows = LRU_CHUNK + 16

    def gates(c, carry):
        r0 = pl.multiple_of(c * LRU_CHUNK, LRU_CHUNK)
        win = xpad[pl.ds(r0, win_rows), :]
        inner = slice(8, 8 + LRU_CHUNK)
        xc = cb + pltpu.roll(win, 2, axis=0)[inner] * cw[0:1]
        xc = xc + pltpu.roll(win, 1, axis=0)[inner] * cw[1:2]
        xc = xc + win[inner] * cw[2:3]
        xc = xc + pltpu.roll(win, win_rows - 1, axis=0)[inner] * cw[3:4]
        sg = jax.nn.sigmoid(_dot(xc.astype(BF16), wcat_ref[...]) + bcat_ref[...])
        for d, (a_ref, b_ref) in enumerate(((a_f, b_f), (a_b, b_b))):
            r = sg[:, (2 * d) * w:(2 * d + 1) * w]
            i = sg[:, (2 * d + 1) * w:(2 * d + 2) * w]
            log_a = (-LRU_C * r) * sp[d:d + 1]
            a = jnp.exp(log_a)
            a_ref[pl.ds(r0, LRU_CHUNK), :] = a
            b_ref[pl.ds(r0, LRU_CHUNK), :] = jnp.sqrt(-jnp.tanh(log_a) * (a * a + 1.0)) * (i * xc)
        return carry

    lax.fori_loop(0, nc, gates, 0)

    def fwd(c, carry):
        r0 = pl.multiple_of(c * LRU_CHUNK, LRU_CHUNK)
        a_cum, h_loc = _scan_chunk(a_f[pl.ds(r0, LRU_CHUNK), :], b_f[pl.ds(r0, LRU_CHUNK), :], False)
        h = h_loc + a_cum * carry
        h_f[pl.ds(r0, LRU_CHUNK), :] = h
        return h[LRU_CHUNK - 1:LRU_CHUNK, :]

    s_f = lax.fori_loop(0, nc, fwd, h0_ref[0, 0:1, :])

    def bwd(k, carry):
        c = nc - 1 - k
        r0 = pl.multiple_of(c * LRU_CHUNK, LRU_CHUNK)
        a_cum, h_loc = _scan_chunk(a_b[pl.ds(r0, LRU_CHUNK), :], b_b[pl.ds(r0, LRU_CHUNK), :], True)
        h = h_loc + a_cum * carry
        y_ref[pl.ds(r0, LRU_CHUNK), :] = (h_f[pl.ds(r0, LRU_CHUNK), :] + h) * g_ref[pl.ds(r0, LRU_CHUNK), :]
        return h[0:1, :]

    s_b = lax.fori_loop(0, nc, bwd, h0_ref[0, 1:2, :])
    st_ref[0, 0:1, :] = s_f
    st_ref[0, 1:2, :] = s_b


def _lru_call(proj, h0, cw, cb, wcat, bcat, lam, *, seq_len, n_seq, row_block0):
    w = GROUP_WIDTH
    full = lambda shape: pl.BlockSpec(shape, lambda b: (0,) * len(shape))
    seq_block = lambda col: pl.BlockSpec((seq_len, w), lambda b: (b + row_block0, col))
    return pl.pallas_call(
        functools.partial(_lru_kernel, seq_len=seq_len),
        grid=(n_seq,),
        in_specs=[seq_block(OFF_BX // w), seq_block(OFF_BG // w),
                  pl.BlockSpec((1, 2, w), lambda b: (b, 0, 0)),
                  full((CONV_W, w)), full((1, w)), full((w, 4 * w)), full((1, 4 * w)), full((2, w))],
        out_specs=[pl.BlockSpec((seq_len, w), lambda b: (b, 0)),
                   pl.BlockSpec((1, 2, w), lambda b: (b, 0, 0))],
        out_shape=[jax.ShapeDtypeStruct((n_seq * seq_len, w), F32),
                   jax.ShapeDtypeStruct((n_seq, 2, w), F32)],
        scratch_shapes=[pltpu.VMEM((seq_len + 16, w), F32)] + [pltpu.VMEM((seq_len, w), F32)] * 5,
        compiler_params=_cparams(("arbitrary",)),
        name="lru_%d" % seq_len,
    )(proj, proj, h0, cw, cb, wcat, bcat, lam)


def _diff_lambda(ld_ref, lam_init):
    ld = ld_ref[...]
    l1 = jnp.sum(ld[0:1] * ld[1:2], axis=-1, keepdims=True)
    l2 = jnp.sum(ld[2:3] * ld[3:4], axis=-1, keepdims=True)
    return jnp.exp(l1) - jnp.exp(l2) + lam_init


def _diff_attn_body(q, keys, values, lam, s64, g, lam_init):
    tq = q.shape[0]
    lane = lax.broadcasted_iota(jnp.int32, (1, GROUP_WIDTH), 1)
    qs = q * (DIFF_QK_DIM ** -0.5 * LOG2_E)
    o = jnp.zeros((tq, GROUP_WIDTH), F32)
    for h in range(N_GROUP_HEADS):
        e_rows, inv = [], []
        for i in range(2):
            seg = h * 2 + i
            qm = jnp.where(lane // DIFF_QK_DIM == seg, qs, 0.0).astype(BF16)
            s = [_dot_nt(qm, k) for k in keys]
            m = functools.reduce(jnp.maximum, [jnp.max(x, axis=-1, keepdims=True) for x in s])
            e = [jnp.exp2(x - m) for x in s]
            den = functools.reduce(jnp.add, [jnp.sum(x, axis=-1, keepdims=True) for x in e])
            inv.append(1.0 / den)
            e_rows.append([x.astype(BF16) for x in e])
        both = None
        for e0, e1, v in zip(e_rows[0], e_rows[1], values):
            part = _dot(jnp.concatenate([e0, e1], axis=0), v)
            both = part if both is None else both + part
        oh = both[:tq] * inv[0] - lam * (both[tq:] * inv[1])
        o = o + jnp.where(lane // HEAD_DIM == h, oh, 0.0)
    return _seg_rms_norm(o, s64, g) * (1.0 - lam_init)


def _diff_prompt_kernel(q_ref, k_ref, v_ref, ld_ref, s64_ref, g_ref, o_ref, *, lam_init):
    lam = _diff_lambda(ld_ref, lam_init)
    o_ref[...] = _diff_attn_body(q_ref[...], [k_ref[...].astype(BF16)], [v_ref[...].astype(BF16)],
                                 lam, s64_ref[...], g_ref[...], lam_init)


def _diff_sample_kernel(q_ref, k_ref, v_ref, ck_ref, cv_ref, ld_ref, s64_ref, g_ref, o_ref, *, lam_init):
    lam = _diff_lambda(ld_ref, lam_init)
    keys = [ck_ref[0, 0].astype(BF16), k_ref[...].astype(BF16)]
    values = [cv_ref[0, 0].astype(BF16), v_ref[...].astype(BF16)]
    o_ref[...] = _diff_attn_body(q_ref[...], keys, values, lam, s64_ref[...], g_ref[...], lam_init)


def _diff_prompt_call(proj, ld, s64, g, lam_init):
    w = GROUP_WIDTH
    full = lambda shape: pl.BlockSpec(shape, lambda b: (0,) * len(shape))
    return pl.pallas_call(
        functools.partial(_diff_prompt_kernel, lam_init=lam_init),
        grid=(BATCH,),
        in_specs=[pl.BlockSpec((SEQ, w), lambda b: (b, OFF_CQ // w)),
                  pl.BlockSpec((SEQ, w), lambda b: (b, OFF_CK // w)),
                  pl.BlockSpec((SEQ, w), lambda b: (b, OFF_CV // w)),
                  full((4, DIFF_QK_DIM)), full((w, w)), full((1, w))],
        out_specs=pl.BlockSpec((SEQ, w), lambda b: (b, 0)),
        out_shape=jax.ShapeDtypeStruct((N_PROMPT, w), F32),
        compiler_params=_cparams(("arbitrary",)),
        name="diff_prompt",
    )(proj, proj, proj, ld, s64, g)


def _diff_sample_call(layer, proj, ctx_k, ctx_v, ld, s64, g, lam_init):
    w = GROUP_WIDTH
    tq = 256
    nq = DEC_SEQ // tq
    full = lambda shape: pl.BlockSpec(shape, lambda b, i: (0,) * len(shape))
    seq_block0 = N_PROMPT // DEC_SEQ
    return pl.pallas_call(
        functools.partial(_diff_sample_kernel, lam_init=lam_init),
        grid=(DEC_BATCH, nq),
        in_specs=[pl.BlockSpec((tq, w), lambda b, i: (N_PROMPT // tq + b * nq + i, OFF_CQ // w)),
                  pl.BlockSpec((DEC_SEQ, w), lambda b, i: (seq_block0 + b, OFF_CK // w)),
                  pl.BlockSpec((DEC_SEQ, w), lambda b, i: (seq_block0 + b, OFF_CV // w)),
                  pl.BlockSpec((1, 1, PAST_LEN, w), lambda b, i: (b, layer, 0, 0)),
                  pl.BlockSpec((1, 1, PAST_LEN, w), lambda b, i: (b, layer, 0, 0)),
                  full((4, DIFF_QK_DIM)), full((w, w)), full((1, w))],
        out_specs=pl.BlockSpec((tq, w), lambda b, i: (b * nq + i, 0)),
        out_shape=jax.ShapeDtypeStruct((N_SAMPLE, w), F32),
        compiler_params=_cparams(("arbitrary", "arbitrary")),
        name="diff_sample",
    )(proj, proj, proj, ctx_k, ctx_v, ld, s64, g)


def _sink_attn_body(q, keys, values, masks, sink_ref):
    tq = q.shape[0]
    lane = lax.broadcasted_iota(jnp.int32, (1, 128), 1)
    qs = q * (HEAD_DIM ** -0.5 * LOG2_E)
    head_rows = lax.broadcasted_iota(jnp.int32, (SWA_HEADS * tq, 1), 0) // tq
    stacked, sink = [], jnp.zeros((SWA_HEADS * tq, 1), F32)
    for grp in range(SWA_GROUPS):
        for kh in range(SWA_KV_HEADS):
            stacked.append(jnp.where(lane // HEAD_DIM == kh, qs[:, grp * 128:(grp + 1) * 128], 0.0).astype(BF16))
            sink = jnp.where(head_rows == len(stacked) - 1, sink_ref[kh * SWA_GROUPS + grp] * LOG2_E, sink)
    q4 = jnp.concatenate(stacked, axis=0)
    s = []
    for k, msk in zip(keys, masks):
        x = _dot_nt(q4, k)
        s.append(x if msk is None else jnp.where(msk, x, -jnp.inf))
    m = functools.reduce(jnp.maximum, [jnp.max(x, axis=-1, keepdims=True) for x in s])
    m = jnp.maximum(m, sink)
    e = [jnp.exp2(x - m) for x in s]
    den = functools.reduce(jnp.add, [jnp.sum(x, axis=-1, keepdims=True) for x in e]) + jnp.exp2(sink - m)
    o4 = None
    for x, v in zip(e, values):
        part = _dot(x.astype(BF16), v)
        o4 = part if o4 is None else o4 + part
    o4 = o4 * (1.0 / den)
    outs = []
    for grp in range(SWA_GROUPS):
        r0 = grp * SWA_KV_HEADS * tq
        outs.append(jnp.where(lane // HEAD_DIM == 0, o4[r0:r0 + tq], o4[r0 + tq:r0 + 2 * tq]))
    return outs


def _swa_prompt_kernel(sink_ref, q_ref, k_ref, v_ref, o_ref):
    outs = _sink_attn_body(q_ref[...], [k_ref[...].astype(BF16)], [v_ref[...].astype(BF16)], [None], sink_ref)
    for grp in range(SWA_GROUPS):
        o_ref[:, grp * 128:(grp + 1) * 128] = outs[grp]


def _swa_sample_kernel(sink_ref, q_ref, kp_ref, kc_ref, kn_ref, vp_ref, vc_ref, vn_ref, ck_ref, cv_ref, o_ref):
    n = pl.program_id(1)
    nb = pl.num_programs(1)
    r = lax.broadcasted_iota(jnp.int32, (SWA_HEADS * WINDOW, WINDOW), 0) & (WINDOW - 1)
    c = lax.broadcasted_iota(jnp.int32, (SWA_HEADS * WINDOW, WINDOW), 1)
    mask_prev = c >= r + jnp.where(n > 0, 0, WINDOW)
    mask_next = c <= r - jnp.where(n < nb - 1, 0, WINDOW)
    keys = [ck_ref[0, 0].astype(BF16), kp_ref[...].astype(BF16), kc_ref[...].astype(BF16), kn_ref[...].astype(BF16)]
    values = [cv_ref[0, 0].astype(BF16), vp_ref[...].astype(BF16), vc_ref[...].astype(BF16), vn_ref[...].astype(BF16)]
    outs = _sink_attn_body(q_ref[...], keys, values, [None, mask_prev, None, mask_next], sink_ref)
    for grp in range(SWA_GROUPS):
        o_ref[:, grp * 128:(grp + 1) * 128] = outs[grp]


def _swa_prompt_call(sink, proj):
    w = GROUP_WIDTH
    return pl.pallas_call(
        _swa_prompt_kernel,
        grid=(BATCH,),
        in_specs=[pl.BlockSpec(memory_space=pltpu.SMEM),
                  pl.BlockSpec((SEQ, w), lambda b: (b, OFF_DQ // w)),
                  pl.BlockSpec((SEQ, 128), lambda b: (b, OFF_DK // 128)),
                  pl.BlockSpec((SEQ, 128), lambda b: (b, OFF_DV // 128))],
        out_specs=pl.BlockSpec((SEQ, w), lambda b: (b, 0)),
        out_shape=jax.ShapeDtypeStruct((N_PROMPT, w), F32),
        compiler_params=_cparams(("arbitrary",)),
        name="swa_prompt",
    )(sink, proj, proj, proj)


def _swa_sample_call(layer, sink, proj, ctx_k, ctx_v):
    w = GROUP_WIDTH
    tq = WINDOW
    nq = DEC_SEQ // tq
    row0 = N_PROMPT // tq

    def kv_spec(col, delta):
        def index(b, i):
            j = jnp.clip(i + delta, 0, nq - 1)
            return (row0 + b * nq + j, col)
        return pl.BlockSpec((tq, 128), index)

    ctx_spec = pl.BlockSpec((1, 1, PAST_LEN, 128), lambda b, i: (b, layer, 0, 0))
    return pl.pallas_call(
        _swa_sample_kernel,
        grid=(DEC_BATCH, nq),
        in_specs=[pl.BlockSpec(memory_space=pltpu.SMEM),
                  pl.BlockSpec((tq, w), lambda b, i: (row0 + b * nq + i, OFF_DQ // w)),
                  kv_spec(OFF_DK // 128, -1), kv_spec(OFF_DK // 128, 0), kv_spec(OFF_DK // 128, 1),
                  kv_spec(OFF_DV // 128, -1), kv_spec(OFF_DV // 128, 0), kv_spec(OFF_DV // 128, 1),
                  ctx_spec, ctx_spec],
        out_specs=pl.BlockSpec((tq, w), lambda b, i: (b * nq + i, 0)),
        out_shape=jax.ShapeDtypeStruct((N_SAMPLE, w), F32),
        compiler_params=_cparams(("arbitrary", "arbitrary")),
        name="swa_sample",
    )(sink, proj, proj, proj, proj, proj, proj, proj, ctx_k, ctx_v)


def _outproj_kernel(seq_ref, ya_ref, ybp_ref, ybs_ref, ycp_ref, ycs_ref, ydp_ref, yds_ref, xp_ref, xs_ref, mod_ref,
                    w_ref, g2_ref, rwt_ref, rbt_ref, triu_ref,
                    x1_ref, h2r_ref, topi_ref, topw_ref, rank_ref, count_ref, cnt_ref):
    del seq_ref
    w = GROUP_WIDTH
    is_prompt = pl.program_id(0) < PROMPT_TILES
    yb = jnp.where(is_prompt, ybp_ref[...], ybs_ref[...])
    yc = jnp.where(is_prompt, ycp_ref[...], ycs_ref[...])
    yd = jnp.where(is_prompt, ydp_ref[...], yds_ref[...])
    mix = _dot(ya_ref[...].astype(BF16), w_ref[0, 0:w, :])
    mix = mix + _dot(yb.astype(BF16), w_ref[0, w:2 * w, :])
    mix = mix + _dot(yc.astype(BF16), w_ref[0, 2 * w:3 * w, :])
    mix = mix + _dot(yd.astype(BF16), w_ref[0, 3 * w:4 * w, :])
    mod = mod_ref[0]
    g1 = mod[:, 2 * D_MODEL:3 * D_MODEL]
    sh2 = mod[:, 3 * D_MODEL:4 * D_MODEL]
    sc2 = mod[:, 4 * D_MODEL:5 * D_MODEL]
    x1 = jnp.where(is_prompt, xp_ref[...], xs_ref[...]) + g1 * mix
    x1_ref[...] = x1
    xn = x1 * lax.rsqrt(jnp.mean(x1 * x1, axis=-1, keepdims=True) + EPS) * g2_ref[...]
    h2 = xn * (1.0 + sc2) + sh2
    for j in range(ROW_PIECES):
        h2r_ref[pl.ds(j, TM, stride=ROW_PIECES), :] = h2[:, j * 128:(j + 1) * 128]
    h_hi, h_lo = _split_bf16(h2)
    r_hi, r_lo = _split_bf16(rwt_ref[...])
    logits = _dot_nt(r_hi, h_hi) + (_dot_nt(r_lo, h_hi) + _dot_nt(r_hi, h_lo)) + rbt_ref[...]
    expert = lax.broadcasted_iota(jnp.int32, logits.shape, 0)
    work = logits
    picks = []
    for _ in range(TOP_K):
        m = jnp.max(work, axis=0, keepdims=True)
        idx = jnp.min(jnp.where(work == m, expert, N_EXPERTS), axis=0, keepdims=True)
        picks.append((m, idx))
        work = jnp.where(expert == idx, -jnp.inf, work)
    top = picks[0][0]
    ex = [jnp.exp(m - top) for m, _ in picks]
    inv = 1.0 / functools.reduce(jnp.add, ex)
    @pl.when(pl.program_id(0) % (MOE_TB // TM) == 0)
    def _():
        cnt_ref[...] = jnp.zeros_like(cnt_ref)

    sel = jnp.zeros(logits.shape, F32)
    for _, idx in picks:
        sel = sel + jnp.where(expert == idx, 1.0, 0.0)
    csum = _dot(sel.astype(BF16), triu_ref[...])
    before = cnt_ref[...] + csum - sel
    cnt_ref[...] += csum[:, TM - 1:TM]
    count_ref[0] = cnt_ref[...].astype(jnp.int32)
    slot = lax.broadcasted_iota(jnp.int32, topi_ref.shape, 0)
    top_i = jnp.zeros(topi_ref.shape, jnp.int32)
    top_w = jnp.zeros(topw_ref.shape, F32)
    rank = jnp.zeros(rank_ref.shape, jnp.int32)
    for k, (e, (_, idx)) in enumerate(zip(ex, picks)):
        rank_k = jnp.sum(jnp.where(expert == idx, before, 0.0), axis=0, keepdims=True).astype(jnp.int32)
        top_i = jnp.where(slot == k, idx, top_i)
        top_w = jnp.where(slot == k, e * inv, top_w)
        rank = jnp.where(slot == k, rank_k, rank)
    topi_ref[...] = top_i
    topw_ref[...] = top_w
    rank_ref[...] = rank


def _outproj_call(layer, seq_of_tile, ya, yb, yc, yd, x, mod3, w_out, g2, rwt, rbt, triu):
    w = GROUP_WIDTH
    full = lambda shape: pl.BlockSpec(shape, lambda i, s: (0,) * len(shape))
    tile = lambda width: pl.BlockSpec((TM, width), lambda i, s: (i, 0))
    prompt_tile = lambda width: pl.BlockSpec((TM, width), lambda i, s: (jnp.minimum(i, PROMPT_TILES - 1), 0))
    sample_tile = lambda width: pl.BlockSpec((TM, width), lambda i, s: (jnp.maximum(i - PROMPT_TILES, 0), 0))
    slots = pl.BlockSpec((TOPK_PAD, TM), lambda i, s: (0, i))
    tiles_per_block = MOE_TB // TM
    return pl.pallas_call(
        _outproj_kernel,
        grid_spec=pltpu.PrefetchScalarGridSpec(
            num_scalar_prefetch=1,
            grid=(N_TILES,),
            in_specs=[tile(w), prompt_tile(w), sample_tile(w), prompt_tile(w), sample_tile(w),
                      prompt_tile(w), sample_tile(w), prompt_tile(D_MODEL), sample_tile(D_MODEL),
                      pl.BlockSpec((1, 1, 6 * D_MODEL), lambda i, s: (s[i], 0, 0)),
                      pl.BlockSpec((1, D_MODEL, D_MODEL), lambda i, s: (layer, 0, 0)), full((1, D_MODEL)),
                      full((N_EXPERTS, D_MODEL)), full((N_EXPERTS, 1)), full((TM, TM))],
            out_specs=[tile(D_MODEL), pl.BlockSpec((TM * ROW_PIECES, 128), lambda i, s: (i, 0)),
                       slots, slots, slots,
                       pl.BlockSpec((1, N_EXPERTS, 1), lambda i, s: (i // tiles_per_block, 0, 0))],
            scratch_shapes=[pltpu.VMEM((N_EXPERTS, 1), F32)],
        ),
        out_shape=[jax.ShapeDtypeStruct((N_TOK, D_MODEL), F32),
                   jax.ShapeDtypeStruct((N_TOK * ROW_PIECES, 128), F32),
                   jax.ShapeDtypeStruct((TOPK_PAD, N_TOK), jnp.int32),
                   jax.ShapeDtypeStruct((TOPK_PAD, N_TOK), F32),
                   jax.ShapeDtypeStruct((TOPK_PAD, N_TOK), jnp.int32),
                   jax.ShapeDtypeStruct((N_TOK // MOE_TB, N_EXPERTS, 1), jnp.int32)],
        compiler_params=_cparams(("arbitrary",)),
        name="outproj",
    )(seq_of_tile, ya, *yb, *yc, *yd, *x, mod3, w_out, g2, rwt, rbt, triu)


def _moe_kernel(dest_ref, topw_ref, gstart_ref, count_ref, src_ref, wgu_ref, bgu_ref, wd_ref, bd_ref,
                acc_ref, pair_ref, xt_ref, y3_ref):
    expert = pl.program_id(0) % N_EXPERTS

    def token_of(pair):
        return (pair >> MOE_PAIR_SHIFT) & (MOE_TB - 1)

    @pl.when(expert == 0)
    def _():
        def zero(c, carry):
            acc_ref[pl.ds(pl.multiple_of(c * MOE_TILE, MOE_TILE), MOE_TILE)] = jnp.zeros((MOE_TILE, 8, 128), F32)
            return carry
        lax.fori_loop(0, MOE_TB // MOE_TILE, zero, 0)

        def pad_group(e, carry):
            cnt = count_ref[0, 0, e]
            first_pad = gstart_ref[0, 0, e] + cnt
            n_pad = (-cnt) & (MOE_TILE - 1)

            def pad_row(r, c2):
                pair_ref[first_pad + r] = MOE_PAD_PAIR
                return c2
            lax.fori_loop(0, n_pad, pad_row, 0)
            return carry
        lax.fori_loop(0, N_EXPERTS, pad_group, 0)

        def place(c, carry):
            for k in range(MOE_PLACE_UNROLL):
                pair = c * MOE_PLACE_UNROLL + k
                pair_ref[dest_ref[0, 0, pair]] = pair
            return carry
        lax.fori_loop(0, MOE_TB * TOP_K // MOE_PLACE_UNROLL, place, 0)

    def tile(t, carry):
        row0 = gstart_ref[0, 0, expert] + t * MOE_TILE
        for m in range(MOE_TILE):
            tok = token_of(pair_ref[row0 + m])
            slab = src_ref[pl.ds(pl.multiple_of(tok * ROW_PIECES, ROW_PIECES), ROW_PIECES), :]
            xt_ref[pl.ds(m, ROW_PIECES, stride=MOE_XT_STRIDE), :] = slab
        x = jnp.concatenate([xt_ref[j * MOE_XT_STRIDE:j * MOE_XT_STRIDE + MOE_TILE, :] for j in range(ROW_PIECES)],
                            axis=1).astype(BF16)
        gu = _dot_nt(wgu_ref[0, 0], x) + bgu_ref[0, 0]
        gate = jnp.minimum(gu[:D_FF], SWIGLU_LIMIT)
        up = jnp.clip(gu[D_FF:], -SWIGLU_LIMIT, SWIGLU_LIMIT)
        act = (up + 1.0) * gate * jax.nn.sigmoid(SWIGLU_ALPHA * gate)
        y = (_dot(wd_ref[0, 0], act.astype(BF16)) + bd_ref[0, 0]).T
        for j in range(ROW_PIECES):
            y3_ref[:, j * 8:(j + 1) * 8, :] = y[:, j * 128:(j + 1) * 128].reshape(MOE_TILE // 8, 8, 128)
        for m0 in range(0, MOE_TILE, MOE_RMW_BATCH):
            rows = range(m0, m0 + MOE_RMW_BATCH)
            pairs = [pair_ref[row0 + m] for m in rows]
            toks = [token_of(p) for p in pairs]
            vals = [acc_ref[tok] + topw_ref[0, 0, p] * y3_ref[m // 8, pl.ds(m % 8, 8, stride=8), :]
                    for tok, p, m in zip(toks, pairs, rows)]
            for tok, v in reversed(list(zip(toks, vals))):
                acc_ref[tok] = v
        return carry

    n_tiles = (count_ref[0, 0, expert] + (MOE_TILE - 1)) // MOE_TILE
    lax.fori_loop(0, n_tiles, tile, 0)


def _moe_call(layer, dest, topw, gstart, counts, h2_rows, wgu, bgu, wd, bd):
    once = pl.Buffered(1)
    n_pairs = MOE_TB * TOP_K
    n_blocks = N_TOK // MOE_TB

    def smem_block(width):
        return pl.BlockSpec((1, 1, width), lambda g: (g // N_EXPERTS, 0, 0), memory_space=pltpu.SMEM,
                            pipeline_mode=once)

    def expert_block(rows, cols):
        return pl.BlockSpec((1, 1, rows, cols), lambda g: (layer, g % N_EXPERTS, 0, 0))

    return pl.pallas_call(
        _moe_kernel,
        grid=(n_blocks * N_EXPERTS,),
        in_specs=[smem_block(n_pairs), smem_block(n_pairs + 128), smem_block(N_EXPERTS), smem_block(N_EXPERTS),
                  pl.BlockSpec((MOE_TB * ROW_PIECES, 128), lambda g: (g // N_EXPERTS, 0), pipeline_mode=once),
                  expert_block(2 * D_FF, D_MODEL), expert_block(2 * D_FF, 1),
                  expert_block(D_MODEL, D_FF), expert_block(D_MODEL, 1)],
        out_specs=pl.BlockSpec((MOE_TB, 8, 128), lambda g: (g // N_EXPERTS, 0, 0), pipeline_mode=once),
        scratch_shapes=[pltpu.SMEM((MOE_ROWS_PER_BLOCK,), jnp.int32),
                        pltpu.VMEM((ROW_PIECES * MOE_XT_STRIDE, 128), F32),
                        pltpu.VMEM((MOE_TILE // 8, 8 * ROW_PIECES, 128), F32)],
        out_shape=jax.ShapeDtypeStruct((N_TOK, 8, 128), F32),
        compiler_params=_cparams(("arbitrary",)),
        name="moe",
    )(dest, topw, gstart, counts, h2_rows, wgu, bgu, wd, bd)


def _moe_routing(top_i, top_w, rank, counts):
    nb = N_TOK // MOE_TB
    counts = counts.reshape(nb, N_EXPERTS)
    padded = ((counts + MOE_TILE - 1) // MOE_TILE) * MOE_TILE
    gstart_local = jnp.cumsum(padded, axis=1) - padded
    experts = jnp.arange(N_EXPERTS, dtype=jnp.int32)
    pair_major = lambda a: jnp.transpose(a.reshape(TOP_K, nb, MOE_TB), (1, 2, 0)).reshape(nb, 1, MOE_TB * TOP_K)
    picked = top_i.reshape(TOP_K, nb, MOE_TB, 1) == experts
    dest = jnp.sum(jnp.where(picked, gstart_local[None, :, None, :], 0), axis=-1) + rank.reshape(TOP_K, nb, MOE_TB)
    dest = pair_major(dest).astype(jnp.int32)
    topw = jnp.concatenate([pair_major(top_w), jnp.zeros((nb, 1, 128), F32)], axis=-1)
    return (dest, topw, gstart_local.reshape(nb, 1, N_EXPERTS).astype(jnp.int32),
            counts.reshape(nb, 1, N_EXPERTS))


def _residual_kernel(seq_ref, x1_ref, moe_ref, mod_ref, o_ref):
    del seq_ref
    g2 = mod_ref[0][:, 5 * D_MODEL:6 * D_MODEL]
    moe = jnp.concatenate([moe_ref[pl.ds(j, TM, stride=ROW_PIECES), :] for j in range(ROW_PIECES)], axis=1)
    o_ref[...] = x1_ref[...] + g2 * moe


def _residual_call(seq_of_tile, x1, moe_rows, mod3, first_tile, n_tiles):
    return pl.pallas_call(
        _residual_kernel,
        grid_spec=pltpu.PrefetchScalarGridSpec(
            num_scalar_prefetch=1,
            grid=(n_tiles,),
            in_specs=[pl.BlockSpec((TM, D_MODEL), lambda i, s: (i + first_tile, 0)),
                      pl.BlockSpec((TM * ROW_PIECES, 128), lambda i, s: (i + first_tile, 0)),
                      pl.BlockSpec((1, 1, 6 * D_MODEL), lambda i, s: (s[i + first_tile], 0, 0))],
            out_specs=pl.BlockSpec((TM, D_MODEL), lambda i, s: (i, 0)),
        ),
        out_shape=jax.ShapeDtypeStruct((n_tiles * TM, D_MODEL), F32),
        compiler_params=_cparams(("arbitrary",)),
        name="residual",
    )(seq_of_tile, x1, moe_rows, mod3)


def _segment_matrix(width, seg):
    idx = np.arange(width) // seg
    return jnp.asarray((idx[:, None] == idx[None, :]).astype(np.float32) / seg, BF16)


def _rope_tables(rot_dim, width):
    rows = DEC_SEQ // GRID_W
    nf = rot_dim // 4
    f32 = np.float32
    inv = (f32(1.0) / (f32(ROPE_BASE) ** (np.arange(nf, dtype=f32) / f32(nf)))).astype(f32)
    row = np.repeat(np.arange(rows, dtype=f32), GRID_W)
    col = np.tile(np.arange(GRID_W, dtype=f32), rows)
    ang = np.stack([row[:, None] * inv, col[:, None] * inv], axis=1).astype(f32)
    cos, sin = np.cos(ang).astype(f32), np.sin(ang).astype(f32)
    cos_r = np.concatenate([cos[:, 0], cos[:, 0], cos[:, 1], cos[:, 1]], axis=-1)
    sin_r = np.concatenate([-sin[:, 0], sin[:, 0], -sin[:, 1], sin[:, 1]], axis=-1)
    reps = width // rot_dim
    cos_t = np.concatenate([np.ones((TM, width), f32), np.tile(cos_r, (1, reps))], axis=0)
    sin_t = np.concatenate([np.zeros((TM, width), f32), np.tile(sin_r, (1, reps))], axis=0)
    return jnp.asarray(cos_t), jnp.asarray(sin_t)


def _block_diag(wb):
    nb, n = wb.shape[-3], wb.shape[-1]
    rows = wb.reshape(wb.shape[:-3] + (nb * n, n))
    tiled = jnp.tile(rows, (1,) * (wb.ndim - 2) + (nb,))
    blk = np.arange(nb * n) // n
    return jnp.where(jnp.asarray(blk[:, None] == blk[None, :]), tiled, 0.0)


_DQ_HEADS = [kh * SWA_GROUPS + g for g in range(SWA_GROUPS) for kh in range(SWA_KV_HEADS)]


def kernel(x_prompt, x_sample, c, cache_diff_k, cache_diff_v, cache_swa_k, cache_swa_v, state_lru, c_ctx, mod_w, mod_b, norm1_g, norm2_g, w_in, w_out, mlp_vnorm_g, mlp_ws, mlp_bs, lru_conv_w, lru_conv_b, lru_wa, lru_ba, lru_wx, lru_bx, lru_lambda, diff_qnorm_g, diff_knorm_g, diff_lambda, diff_subln_g, swa_qnorm_g, swa_knorm_g, swa_sink, router_w, router_b, moe_w_gu, moe_b_gu, moe_w_down, moe_b_down):
    params = dict(locals())
    consts = _constants()
    prep = _prepare(params, consts)
    x = (x_prompt.reshape(N_PROMPT, D_MODEL), x_sample.reshape(N_SAMPLE, D_MODEL))
    dk_l, dv_l, sk_l, sv_l, lru_l = [], [], [], [], []
    for l in range(DEPTH):
        st = _mixer_stage(x, params, prep, l, consts)
        x = _ffn_stage(x, st, params, prep, l, consts)
        pp = st["proj"][:N_PROMPT]
        dk_l.append(pp[:, OFF_CK:OFF_CV].reshape(BATCH, SEQ, N_GROUP_HEADS, 2, DIFF_QK_DIM))
        dv_l.append(pp[:, OFF_CV:OFF_DQ].reshape(BATCH, SEQ, N_GROUP_HEADS, HEAD_DIM))
        sk_l.append(pp[:, OFF_DK:OFF_DV].reshape(BATCH, SEQ, SWA_KV_HEADS, HEAD_DIM))
        sv_l.append(pp[:, OFF_DV:IN_WIDTH].reshape(BATCH, SEQ, SWA_KV_HEADS, HEAD_DIM))
        lru_l.append(st["st_p"])

    y_p = x[0].reshape(BATCH, SEQ, D_MODEL)
    y_s = x[1].reshape(DEC_BATCH, DEC_SEQ, D_MODEL)
    return (y_p, y_s, jnp.stack(dk_l, axis=1), jnp.stack(dv_l, axis=1), jnp.stack(sk_l, axis=1),
            jnp.stack(sv_l, axis=1), jnp.stack(lru_l, axis=1))


def _constants():
    w = GROUP_WIDTH
    tiles = np.arange(N_TILES)
    seq_np = np.where(tiles < PROMPT_TILES, 0, 1 + (tiles - PROMPT_TILES) // SAMPLE_TILES_PER_SEQ)
    rope_np = np.where(tiles < PROMPT_TILES, 0, 1 + (tiles - PROMPT_TILES) % SAMPLE_TILES_PER_SEQ)
    seq_of_tile = [jnp.asarray(seq_np + l * N_COND, jnp.int32) for l in range(DEPTH)]
    rope_of_tile = jnp.asarray(rope_np, jnp.int32)

    s32 = _segment_matrix(w, DIFF_QK_DIM)
    s64 = _segment_matrix(w, HEAD_DIM)
    cosc, sinc = _rope_tables(DIFF_QK_DIM, w)
    cosd, sind = _rope_tables(HEAD_DIM, w)
    dq_cols = np.concatenate([np.arange(h * HEAD_DIM, (h + 1) * HEAD_DIM) for h in _DQ_HEADS])
    triu = jnp.asarray(np.triu(np.ones((TM, TM), np.float32)), BF16)
    return dict(seq_of_tile=seq_of_tile, rope_of_tile=rope_of_tile, triu=triu, s32=s32, s64=s64,
                cosc=cosc, sinc=sinc, cosd=cosd, sind=sind, dq_cols=dq_cols)


def _prepare(params, consts):
    w = GROUP_WIDTH
    p = params
    dq = consts["dq_cols"]
    heads = lambda g, reps: jnp.tile(g.reshape(DEPTH, 1, -1), (1, 1, reps))
    w_in, w_out = p["w_in"], p["w_out"]
    lru_w = jnp.stack([p["lru_wa"][:, 0], p["lru_wx"][:, 0], p["lru_wa"][:, 1], p["lru_wx"][:, 1]], axis=1)
    lru_b = jnp.stack([p["lru_ba"][:, 0], p["lru_bx"][:, 0], p["lru_ba"][:, 1], p["lru_bx"][:, 1]], axis=1)
    cond = jnp.concatenate([p["c_ctx"][None], p["c"], jnp.zeros((N_COND - 1 - DEC_BATCH, D_MODEL), F32)], axis=0)
    return dict(
        mod3=_mod_call(cond, p["mod_w"], p["mod_b"]).reshape(DEPTH * N_COND, 1, 6 * D_MODEL),
        w_in=jnp.concatenate([w_in[..., :OFF_DQ], w_in[..., OFF_DQ:OFF_DK][..., dq], w_in[..., OFF_DK:]],
                             axis=-1).astype(BF16),
        w_out=jnp.concatenate([w_out[:, :3 * w], w_out[:, 3 * w:][:, dq]], axis=1).astype(BF16),
        gqc=heads(p["diff_qnorm_g"], N_GROUP_HEADS), gkc=heads(p["diff_knorm_g"], N_GROUP_HEADS),
        gqd=heads(p["swa_qnorm_g"], SWA_HEADS), gkd=heads(p["swa_knorm_g"], SWA_KV_HEADS),
        g_sub=heads(p["diff_subln_g"], N_GROUP_HEADS), g_mlp=p["mlp_vnorm_g"].reshape(DEPTH, 1, w),
        mlp_bias=jnp.repeat(jnp.swapaxes(p["mlp_bs"], 1, 2), HEAD_DIM, axis=2),
        mlp_ws=p["mlp_ws"].astype(BF16),
        wcat=jnp.swapaxes(_block_diag(lru_w), 1, 2).reshape(DEPTH, w, 4 * w).astype(BF16),
        bcat=lru_b.reshape(DEPTH, 1, 4 * w),
        conv_b=p["lru_conv_b"].reshape(DEPTH, 1, w),
        g1=p["norm1_g"].reshape(DEPTH, 1, D_MODEL), g2=p["norm2_g"].reshape(DEPTH, 1, D_MODEL),
        router_wt=jnp.swapaxes(p["router_w"], 1, 2), router_bt=p["router_b"].reshape(DEPTH, N_EXPERTS, 1),
        moe_w_gu=jnp.swapaxes(p["moe_w_gu"], 2, 3).astype(BF16),
        moe_w_down=jnp.swapaxes(p["moe_w_down"], 2, 3).astype(BF16),
        moe_b_gu=p["moe_b_gu"].reshape(DEPTH, N_EXPERTS, 2 * D_FF, 1),
        moe_b_down=p["moe_b_down"].reshape(DEPTH, N_EXPERTS, D_MODEL, 1),
        ctx_diff_k=p["cache_diff_k"].reshape(DEC_BATCH, DEPTH, PAST_LEN, w),
        ctx_diff_v=p["cache_diff_v"].reshape(DEC_BATCH, DEPTH, PAST_LEN, w),
        ctx_swa_k=p["cache_swa_k"].reshape(DEC_BATCH, DEPTH, PAST_LEN, SWA_KV_HEADS * HEAD_DIM),
        ctx_swa_v=p["cache_swa_v"].reshape(DEC_BATCH, DEPTH, PAST_LEN, SWA_KV_HEADS * HEAD_DIM),
    )


def _mixer_stage(x, params, prep, l, consts):
    w = GROUP_WIDTH
    s32, s64 = consts["s32"], consts["s64"]
    lam_init = 0.8 - 0.6 * math.exp(-0.3 * l)
    proj = _inproj_call(l, consts["seq_of_tile"][l], consts["rope_of_tile"], x, prep["mod3"], prep["g1"][l],
                        prep["w_in"], s32, s64, prep["gqc"][l], prep["gkc"][l], prep["gqd"][l], prep["gkd"][l],
                        consts["cosc"], consts["sinc"], consts["cosd"], consts["sind"])
    ya = _gmlp_call(proj, s64, prep["g_mlp"][l], prep["mlp_ws"][l], prep["mlp_bias"][l])

    lru_args = (params["lru_conv_w"][l], prep["conv_b"][l], prep["wcat"][l], prep["bcat"][l], params["lru_lambda"][l])
    yb_p, st_p = _lru_call(proj, jnp.zeros((BATCH, 2, w), F32), *lru_args, seq_len=SEQ, n_seq=BATCH, row_block0=0)
    yb_s, _ = _lru_call(proj, params["state_lru"][:, l], *lru_args, seq_len=DEC_SEQ, n_seq=DEC_BATCH,
                        row_block0=N_PROMPT // DEC_SEQ)

    ld = params["diff_lambda"][l]
    yc_p = _diff_prompt_call(proj, ld, s64, prep["g_sub"][l], lam_init)
    yc_s = _diff_sample_call(l, proj, prep["ctx_diff_k"], prep["ctx_diff_v"], ld, s64, prep["g_sub"][l], lam_init)

    sink = params["swa_sink"][l]
    yd_p = _swa_prompt_call(sink, proj)
    yd_s = _swa_sample_call(l, sink, proj, prep["ctx_swa_k"], prep["ctx_swa_v"])
    return dict(proj=proj, ya=ya, yb=(yb_p, yb_s), yc=(yc_p, yc_s), yd=(yd_p, yd_s), st_p=st_p)


def _ffn_stage(x, st, params, prep, l, consts):
    seq_of_tile = consts["seq_of_tile"][l]
    x1, h2_rows, top_i, top_w, rank, counts = _outproj_call(
        l, seq_of_tile, st["ya"], st["yb"], st["yc"], st["yd"], x, prep["mod3"], prep["w_out"], prep["g2"][l],
        prep["router_wt"][l], prep["router_bt"][l], consts["triu"])
    routing = _moe_routing(top_i[:TOP_K], top_w[:TOP_K], rank[:TOP_K], counts)
    moe = _moe_call(l, *routing, h2_rows, prep["moe_w_gu"], prep["moe_b_gu"], prep["moe_w_down"], prep["moe_b_down"])
    moe_rows = moe.reshape(N_TOK * ROW_PIECES, 128)
    return (_residual_call(seq_of_tile, x1, moe_rows, prep["mod3"], 0, PROMPT_TILES),
            _residual_call(seq_of_tile, x1, moe_rows, prep["mod3"], PROMPT_TILES, N_TILES - PROMPT_TILES))
```

```python
import functools
import math

import jax
import jax.numpy as jnp
import numpy as np
from jax import lax
from jax.experimental import pallas as pl
from jax.experimental.pallas import tpu as pltpu

F32 = jnp.float32
BF16 = jnp.bfloat16

D_MODEL = 1024
BATCH = 16
SEQ = 256
DEPTH = 2
DEC_BATCH = 4
DEC_SEQ = 2048
PAST_LEN = 512
GRID_W = 64
HEAD_DIM = 64
GROUP_WIDTH = 256
N_GROUP_HEADS = 4
CHUNK = 128
LRU_C = 8.0
CONV_W = 4
DIFF_QK_DIM = 32
SWA_KV_HEADS = 2
SWA_GROUPS = 2
SWA_HEADS = SWA_KV_HEADS * SWA_GROUPS
WINDOW = 128
N_EXPERTS = 32
TOP_K = 4
D_FF = 1024
SWIGLU_LIMIT = 7.0
SWIGLU_ALPHA = 1.702
ROPE_BASE = 10000.0
EPS = 1e-6
LOG2_E = math.log2(math.e)

N_PROMPT = BATCH * SEQ
N_SAMPLE = DEC_BATCH * DEC_SEQ
N_TOK = N_PROMPT + N_SAMPLE
N_COND = 8
TM = 256
N_TILES = N_TOK // TM
PROMPT_TILES = N_PROMPT // TM
SAMPLE_TILES_PER_SEQ = DEC_SEQ // TM
IN_WIDTH = 2304
OFF_AU, OFF_AV, OFF_BX, OFF_BG, OFF_CQ, OFF_CK, OFF_CV, OFF_DQ, OFF_DK, OFF_DV = (
    0, 256, 512, 768, 1024, 1280, 1536, 1792, 2048, 2176)
TOPK_PAD = 8
ROW_PIECES = D_MODEL // 128
MOE_TILE = 256
MOE_TB = 4096
MOE_MAX_TILES = (N_TOK // MOE_TB) * (MOE_TB * TOP_K // MOE_TILE + N_EXPERTS)
MOE_XT_STRIDE = MOE_TILE + 8
MOE_RMW_BATCH = 4
MOE_PLACE_UNROLL = 16
MOE_ROWS_PER_BLOCK = MOE_TB * TOP_K + N_EXPERTS * MOE_TILE
MOE_PAD_PAIR = MOE_TB * TOP_K
MOE_PAIR_SHIFT = TOP_K.bit_length() - 1
assert 1 << MOE_PAIR_SHIFT == TOP_K
MOE_KIND_SKIP, MOE_KIND_FIRST, MOE_KIND_MIDDLE, MOE_KIND_LAST = 0, 1, 2, 3
OUT_TILES = 2
assert PROMPT_TILES % OUT_TILES == 0 and SAMPLE_TILES_PER_SEQ % OUT_TILES == 0
LRU_CHUNK = 256
DIFF_HEADS_PER_PASS = 1
VMEM_LIMIT = 56 * 1024 * 1024


def _cparams(sem):
    return pltpu.CompilerParams(dimension_semantics=sem, vmem_limit_bytes=VMEM_LIMIT)


def _dot(a, b):
    return jnp.dot(a, b, preferred_element_type=F32)


def _dot_nt(a, b):
    return lax.dot_general(a, b, (((1,), (1,)), ((), ())), preferred_element_type=F32)


def _split_bf16(x):
    hi = x.astype(BF16)
    lo = (x - hi.astype(F32)).astype(BF16)
    return hi, lo


def _seg_rms_norm(x, seg_mat, g):
    hi, lo = _split_bf16(x * x)
    ms = _dot(hi, seg_mat) + _dot(lo, seg_mat)
    return x * lax.rsqrt(ms + EPS) * g


def _rope(x, cos_t, sin_t, nf):
    n = x.shape[-1]
    lane = lax.broadcasted_iota(jnp.int32, x.shape, 1)
    first = (lane & (2 * nf - 1)) < nf
    partner = jnp.where(first, pltpu.roll(x, n - nf, axis=1), pltpu.roll(x, nf, axis=1))
    return x * cos_t + partner * sin_t


def _softplus(x):
    return jnp.maximum(x, 0.0) + jnp.log1p(jnp.exp(-jnp.abs(x)))


def _mod_kernel(cond_ref, w_ref, b_ref, o_ref):
    c = cond_ref[...]
    s = c * jax.nn.sigmoid(c)
    o_ref[0] = _dot(s.astype(BF16), w_ref[0].astype(BF16)) + b_ref[0]


def _mod_call(cond, w, b):
    nb = 6
    return pl.pallas_call(
        _mod_kernel,
        grid=(DEPTH, nb),
        in_specs=[pl.BlockSpec((N_COND, D_MODEL), lambda l, j: (0, 0)),
                  pl.BlockSpec((1, D_MODEL, D_MODEL), lambda l, j: (l, 0, j)),
                  pl.BlockSpec((1, 1, D_MODEL), lambda l, j: (l, 0, j))],
        out_specs=pl.BlockSpec((1, N_COND, D_MODEL), lambda l, j: (l, 0, j)),
        out_shape=jax.ShapeDtypeStruct((DEPTH, N_COND, 6 * D_MODEL), F32),
        compiler_params=_cparams(("arbitrary", "arbitrary")),
        name="mod",
    )(cond, w, b.reshape(DEPTH, 1, 6 * D_MODEL))


def _inproj_kernel(seq_ref, rope_ref, xp_ref, xs_ref, mod_ref, g1_ref, w_ref, s32_ref, s64_ref,
                   gqc_ref, gkc_ref, gqd_ref, gkd_ref, cosc_ref, sinc_ref, cosd_ref, sind_ref, o_ref):
    del seq_ref, rope_ref
    x = jnp.where(pl.program_id(0) < PROMPT_TILES, xp_ref[...], xs_ref[...])
    xn = x * lax.rsqrt(jnp.mean(x * x, axis=-1, keepdims=True) + EPS) * g1_ref[...]
    mod = mod_ref[0]
    sh1 = mod[:, 0:D_MODEL]
    sc1 = mod[:, D_MODEL:2 * D_MODEL]
    h = xn * (1.0 + sc1) + sh1
    p = _dot(h.astype(BF16), w_ref[0])
    o_ref[:, OFF_AU:OFF_BX] = jax.nn.gelu(p[:, OFF_AU:OFF_BX])
    o_ref[:, OFF_BX:OFF_BG] = p[:, OFF_BX:OFF_BG]
    o_ref[:, OFF_BG:OFF_CQ] = jax.nn.gelu(p[:, OFF_BG:OFF_CQ])
    s32 = s32_ref[...]
    cosc = cosc_ref[...]
    sinc = sinc_ref[...]
    cq = _seg_rms_norm(p[:, OFF_CQ:OFF_CK], s32, gqc_ref[...])
    ck = _seg_rms_norm(p[:, OFF_CK:OFF_CV], s32, gkc_ref[...])
    o_ref[:, OFF_CQ:OFF_CK] = _rope(cq, cosc, sinc, DIFF_QK_DIM // 4)
    o_ref[:, OFF_CK:OFF_CV] = _rope(ck, cosc, sinc, DIFF_QK_DIM // 4)
    o_ref[:, OFF_CV:OFF_DQ] = p[:, OFF_CV:OFF_DQ]
    s64 = s64_ref[...]
    cosd = cosd_ref[...]
    sind = sind_ref[...]
    dq = _seg_rms_norm(p[:, OFF_DQ:OFF_DK], s64, gqd_ref[...])
    dk = _seg_rms_norm(p[:, OFF_DK:OFF_DV], s64[0:128, 0:128], gkd_ref[...])
    o_ref[:, OFF_DQ:OFF_DK] = _rope(dq, cosd, sind, HEAD_DIM // 4)
    o_ref[:, OFF_DK:OFF_DV] = _rope(dk, cosd[:, 0:128], sind[:, 0:128], HEAD_DIM // 4)
    o_ref[:, OFF_DV:IN_WIDTH] = p[:, OFF_DV:IN_WIDTH]


def _inproj_call(layer, seq_of_tile, rope_of_tile, x, mod3, g1, w_in, s32, s64, gqc, gkc, gqd, gkd,
                 cosc, sinc, cosd, sind):
    full = lambda shape: pl.BlockSpec(shape, lambda i, s, r: (0,) * len(shape))
    rope_spec = pl.BlockSpec((TM, GROUP_WIDTH), lambda i, s, r: (r[i], 0))
    grid_spec = pltpu.PrefetchScalarGridSpec(
        num_scalar_prefetch=2,
        grid=(N_TILES,),
        in_specs=[pl.BlockSpec((TM, D_MODEL), lambda i, s, r: (jnp.minimum(i, PROMPT_TILES - 1), 0)),
                  pl.BlockSpec((TM, D_MODEL), lambda i, s, r: (jnp.maximum(i - PROMPT_TILES, 0), 0)),
                  pl.BlockSpec((1, 1, 6 * D_MODEL), lambda i, s, r: (s[i], 0, 0)),
                  full((1, D_MODEL)),
                  pl.BlockSpec((1, D_MODEL, IN_WIDTH), lambda i, s, r: (layer, 0, 0)),
                  full((GROUP_WIDTH, GROUP_WIDTH)),
                  full((GROUP_WIDTH, GROUP_WIDTH)),
                  full((1, GROUP_WIDTH)), full((1, GROUP_WIDTH)), full((1, GROUP_WIDTH)), full((1, 128)),
                  rope_spec, rope_spec, rope_spec, rope_spec],
        out_specs=pl.BlockSpec((TM, IN_WIDTH), lambda i, s, r: (i, 0)),
    )
    return pl.pallas_call(
        _inproj_kernel,
        grid_spec=grid_spec,
        out_shape=jax.ShapeDtypeStruct((N_TOK, IN_WIDTH), F32),
        compiler_params=_cparams(("arbitrary",)),
        name="inproj",
    )(seq_of_tile, rope_of_tile, *x, mod3, g1, w_in, s32, s64, gqc, gkc, gqd, gkd, cosc, sinc, cosd, sind)


def _gmlp_kernel(u_ref, v_ref, s64_ref, g_ref, ws_ref, bias_ref, o_ref):
    vh = _seg_rms_norm(v_ref[...], s64_ref[...], g_ref[...]).astype(BF16)
    head = lax.broadcasted_iota(jnp.int32, (CHUNK, GROUP_WIDTH), 1) // HEAD_DIM
    for ch in range(TM // CHUNK):
        rows = slice(ch * CHUNK, (ch + 1) * CHUNK)
        vc = vh[rows]
        mixed = bias_ref[...]
        for h in range(N_GROUP_HEADS):
            mixed = mixed + jnp.where(head == h, _dot(ws_ref[h], vc), 0.0)
        o_ref[rows, :] = u_ref[rows, :] * mixed


def _gmlp_call(proj, s64, g, ws, bias):
    return pl.pallas_call(
        _gmlp_kernel,
        grid=(N_TILES,),
        in_specs=[pl.BlockSpec((TM, GROUP_WIDTH), lambda i: (i, OFF_AU // GROUP_WIDTH)),
                  pl.BlockSpec((TM, GROUP_WIDTH), lambda i: (i, OFF_AV // GROUP_WIDTH)),
                  pl.BlockSpec((GROUP_WIDTH, GROUP_WIDTH), lambda i: (0, 0)),
                  pl.BlockSpec((1, GROUP_WIDTH), lambda i: (0, 0)),
                  pl.BlockSpec((N_GROUP_HEADS, CHUNK, CHUNK), lambda i: (0, 0, 0)),
                  pl.BlockSpec((CHUNK, GROUP_WIDTH), lambda i: (0, 0))],
        out_specs=pl.BlockSpec((TM, GROUP_WIDTH), lambda i: (i, 0)),
        out_shape=jax.ShapeDtypeStruct((N_TOK, GROUP_WIDTH), F32),
        compiler_params=_cparams(("arbitrary",)),
        name="gmlp",
    )(proj, proj, s64, g, ws, bias)


def _scan_chunk(a, b, reverse):
    n = a.shape[0]
    row = lax.broadcasted_iota(jnp.int32, a.shape, 0)
    s = 1
    while s < n:
        if reverse:
            keep = row < n - s
            shift = n - s
        else:
            keep = row >= s
            shift = s
        a_prev = jnp.where(keep, pltpu.roll(a, shift, axis=0), 1.0)
        b_prev = jnp.where(keep, pltpu.roll(b, shift, axis=0), 0.0)
        b = a * b_prev + b
        a = a * a_prev
        s *= 2
    return a, b


def _lru_kernel(x_ref, g_ref, h0_ref, cw_ref, cb_ref, wcat_ref, bcat_ref, lam_ref, y_ref, st_ref,
                xpad, a_f, b_f, a_b, b_b, h_f, *, seq_len):
    nc = seq_len // LRU_CHUNK
    w = GROUP_WIDTH
    zeros8 = jnp.zeros((8, w), F32)
    xpad[0:8, :] = zeros8
    xpad[seq_len + 8:seq_len + 16, :] = zeros8
    xpad[8:seq_len + 8, :] = x_ref[...]
    sp = _softplus(-lam_ref[...])
    cw = cw_ref[...]
    cb = cb_ref[...]
    win_rows = LRU_CHUNK + 16

    def gates(c, carry):
        r0 = pl.multiple_of(c * LRU_CHUNK, LRU_CHUNK)
        win = xpad[pl.ds(r0, win_rows), :]
        inner = slice(8, 8 + LRU_CHUNK)
        xc = cb + pltpu.roll(win, 2, axis=0)[inner] * cw[0:1]
        xc = xc + pltpu.roll(win, 1, axis=0)[inner] * cw[1:2]
        xc = xc + win[inner] * cw[2:3]
        xc = xc + pltpu.roll(win, win_rows - 1, axis=0)[inner] * cw[3:4]
        sg = jax.nn.sigmoid(_dot(xc.astype(BF16), wcat_ref[...]) + bcat_ref[...])
        for d, (a_ref, b_ref) in enumerate(((a_f, b_f), (a_b, b_b))):
            r = sg[:, (2 * d) * w:(2 * d + 1) * w]
            i = sg[:, (2 * d + 1) * w:(2 * d + 2) * w]
            log_a = (-LRU_C * r) * sp[d:d + 1]
            a = jnp.exp(log_a)
            a_ref[pl.ds(r0, LRU_CHUNK), :] = a
            b_ref[pl.ds(r0, LRU_CHUNK), :] = jnp.sqrt(-jnp.tanh(log_a) * (a * a + 1.0)) * (i * xc)
        return carry

    lax.fori_loop(0, nc, gates, 0)

    def fwd(c, carry):
        r0 = pl.multiple_of(c * LRU_CHUNK, LRU_CHUNK)
        a_cum, h_loc = _scan_chunk(a_f[pl.ds(r0, LRU_CHUNK), :], b_f[pl.ds(r0, LRU_CHUNK), :], False)
        h = h_loc + a_cum * carry
        h_f[pl.ds(r0, LRU_CHUNK), :] = h
        return h[LRU_CHUNK - 1:LRU_CHUNK, :]

    s_f = lax.fori_loop(0, nc, fwd, h0_ref[0, 0:1, :])

    def bwd(k, carry):
        c = nc - 1 - k
        r0 = pl.multiple_of(c * LRU_CHUNK, LRU_CHUNK)
        a_cum, h_loc = _scan_chunk(a_b[pl.ds(r0, LRU_CHUNK), :], b_b[pl.ds(r0, LRU_CHUNK), :], True)
        h = h_loc + a_cum * carry
        y_ref[pl.ds(r0, LRU_CHUNK), :] = (h_f[pl.ds(r0, LRU_CHUNK), :] + h) * g_ref[pl.ds(r0, LRU_CHUNK), :]
        return h[0:1, :]

    s_b = lax.fori_loop(0, nc, bwd, h0_ref[0, 1:2, :])
    st_ref[0, 0:1, :] = s_f
    st_ref[0, 1:2, :] = s_b


def _lru_call(proj, h0, cw, cb, wcat, bcat, lam, *, seq_len, n_seq, row_block0):
    w = GROUP_WIDTH
    full = lambda shape: pl.BlockSpec(shape, lambda b: (0,) * len(shape))
    seq_block = lambda col: pl.BlockSpec((seq_len, w), lambda b: (b + row_block0, col))
    return pl.pallas_call(
        functools.partial(_lru_kernel, seq_len=seq_len),
        grid=(n_seq,),
        in_specs=[seq_block(OFF_BX // w), seq_block(OFF_BG // w),
                  pl.BlockSpec((1, 2, w), lambda b: (b, 0, 0)),
                  full((CONV_W, w)), full((1, w)), full((w, 4 * w)), full((1, 4 * w)), full((2, w))],
        out_specs=[pl.BlockSpec((seq_len, w), lambda b: (b, 0)),
                   pl.BlockSpec((1, 2, w), lambda b: (b, 0, 0))],
        out_shape=[jax.ShapeDtypeStruct((n_seq * seq_len, w), F32),
                   jax.ShapeDtypeStruct((n_seq, 2, w), F32)],
        scratch_shapes=[pltpu.VMEM((seq_len + 16, w), F32)] + [pltpu.VMEM((seq_len, w), F32)] * 5,
        compiler_params=_cparams(("arbitrary",)),
        name="lru_%d" % seq_len,
    )(proj, proj, h0, cw, cb, wcat, bcat, lam)


def _diff_lambda(ld_ref, lam_init):
    ld = ld_ref[...]
    l1 = jnp.sum(ld[0:1] * ld[1:2], axis=-1, keepdims=True)
    l2 = jnp.sum(ld[2:3] * ld[3:4], axis=-1, keepdims=True)
    return jnp.exp(l1) - jnp.exp(l2) + lam_init


def _diff_attn_body(q, keys, values, lam, s64, g, lam_init):
    tq = q.shape[0]
    lane = lax.broadcasted_iota(jnp.int32, (1, GROUP_WIDTH), 1)
    qs = q * (DIFF_QK_DIM ** -0.5 * LOG2_E)
    o = jnp.zeros((tq, GROUP_WIDTH), F32)
    for h0 in range(0, N_GROUP_HEADS, DIFF_HEADS_PER_PASS):
        segs = range(h0 * 2, (h0 + DIFF_HEADS_PER_PASS) * 2)
        qn = jnp.concatenate([jnp.where(lane // DIFF_QK_DIM == seg, qs, 0.0).astype(BF16) for seg in segs], axis=0)
        s = [_dot_nt(qn, k) for k in keys]
        m = functools.reduce(jnp.maximum, [jnp.max(x, axis=-1, keepdims=True) for x in s])
        e = [jnp.exp2(x - m) for x in s]
        den = functools.reduce(jnp.add, [jnp.sum(x, axis=-1, keepdims=True) for x in e])
        att = None
        for x, v in zip(e, values):
            part = _dot(x.astype(BF16), v)
            att = part if att is None else att + part
        att = att * (1.0 / den)
        for j in range(DIFF_HEADS_PER_PASS):
            oh = att[2 * j * tq:(2 * j + 1) * tq] - lam * att[(2 * j + 1) * tq:(2 * j + 2) * tq]
            o = o + jnp.where(lane // HEAD_DIM == h0 + j, oh, 0.0)
    return _seg_rms_norm(o, s64, g) * (1.0 - lam_init)


def _diff_prompt_kernel(q_ref, k_ref, v_ref, ld_ref, s64_ref, g_ref, o_ref, *, lam_init):
    lam = _diff_lambda(ld_ref, lam_init)
    o_ref[...] = _diff_attn_body(q_ref[...], [k_ref[...].astype(BF16)], [v_ref[...].astype(BF16)],
                                 lam, s64_ref[...], g_ref[...], lam_init)


def _diff_sample_kernel(q_ref, k_ref, v_ref, ck_ref, cv_ref, ld_ref, s64_ref, g_ref, o_ref, *, lam_init):
    lam = _diff_lambda(ld_ref, lam_init)
    keys = [ck_ref[0, 0].astype(BF16), k_ref[...].astype(BF16)]
    values = [cv_ref[0, 0].astype(BF16), v_ref[...].astype(BF16)]
    o_ref[...] = _diff_attn_body(q_ref[...], keys, values, lam, s64_ref[...], g_ref[...], lam_init)


def _diff_prompt_call(proj, ld, s64, g, lam_init):
    w = GROUP_WIDTH
    full = lambda shape: pl.BlockSpec(shape, lambda b: (0,) * len(shape))
    return pl.pallas_call(
        functools.partial(_diff_prompt_kernel, lam_init=lam_init),
        grid=(BATCH,),
        in_specs=[pl.BlockSpec((SEQ, w), lambda b: (b, OFF_CQ // w)),
                  pl.BlockSpec((SEQ, w), lambda b: (b, OFF_CK // w)),
                  pl.BlockSpec((SEQ, w), lambda b: (b, OFF_CV // w)),
                  full((4, DIFF_QK_DIM)), full((w, w)), full((1, w))],
        out_specs=pl.BlockSpec((SEQ, w), lambda b: (b, 0)),
        out_shape=jax.ShapeDtypeStruct((N_PROMPT, w), F32),
        compiler_params=_cparams(("arbitrary",)),
        name="diff_prompt",
    )(proj, proj, proj, ld, s64, g)


def _diff_sample_call(layer, proj, ctx_k, ctx_v, ld, s64, g, lam_init):
    w = GROUP_WIDTH
    tq = 256
    nq = DEC_SEQ // tq
    full = lambda shape: pl.BlockSpec(shape, lambda b, i: (0,) * len(shape))
    seq_block0 = N_PROMPT // DEC_SEQ
    return pl.pallas_call(
        functools.partial(_diff_sample_kernel, lam_init=lam_init),
        grid=(DEC_BATCH, nq),
        in_specs=[pl.BlockSpec((tq, w), lambda b, i: (N_PROMPT // tq + b * nq + i, OFF_CQ // w)),
                  pl.BlockSpec((DEC_SEQ, w), lambda b, i: (seq_block0 + b, OFF_CK // w)),
                  pl.BlockSpec((DEC_SEQ, w), lambda b, i: (seq_block0 + b, OFF_CV // w)),
                  pl.BlockSpec((1, 1, PAST_LEN, w), lambda b, i: (b, layer, 0, 0)),
                  pl.BlockSpec((1, 1, PAST_LEN, w), lambda b, i: (b, layer, 0, 0)),
                  full((4, DIFF_QK_DIM)), full((w, w)), full((1, w))],
        out_specs=pl.BlockSpec((tq, w), lambda b, i: (b * nq + i, 0)),
        out_shape=jax.ShapeDtypeStruct((N_SAMPLE, w), F32),
        compiler_params=_cparams(("arbitrary", "arbitrary")),
        name="diff_sample",
    )(proj, proj, proj, ctx_k, ctx_v, ld, s64, g)


def _sink_attn_body(q, keys, values, masks, sink_ref):
    tq = q.shape[0]
    lane = lax.broadcasted_iota(jnp.int32, (1, 128), 1)
    qs = q * (HEAD_DIM ** -0.5 * LOG2_E)
    head_rows = lax.broadcasted_iota(jnp.int32, (SWA_HEADS * tq, 1), 0) // tq
    stacked, sink = [], jnp.zeros((SWA_HEADS * tq, 1), F32)
    for grp in range(SWA_GROUPS):
        for kh in range(SWA_KV_HEADS):
            stacked.append(jnp.where(lane // HEAD_DIM == kh, qs[:, grp * 128:(grp + 1) * 128], 0.0).astype(BF16))
            sink = jnp.where(head_rows == len(stacked) - 1, sink_ref[kh * SWA_GROUPS + grp] * LOG2_E, sink)
    q4 = jnp.concatenate(stacked, axis=0)
    s = []
    for k, msk in zip(keys, masks):
        x = _dot_nt(q4, k)
        s.append(x if msk is None else jnp.where(msk, x, -jnp.inf))
    m = functools.reduce(jnp.maximum, [jnp.max(x, axis=-1, keepdims=True) for x in s])
    m = jnp.maximum(m, sink)
    e = [jnp.exp2(x - m) for x in s]
    den = functools.reduce(jnp.add, [jnp.sum(x, axis=-1, keepdims=True) for x in e]) + jnp.exp2(sink - m)
    o4 = None
    for x, v in zip(e, values):
        part = _dot(x.astype(BF16), v)
        o4 = part if o4 is None else o4 + part
    o4 = o4 * (1.0 / den)
    outs = []
    for grp in range(SWA_GROUPS):
        r0 = grp * SWA_KV_HEADS * tq
        outs.append(jnp.where(lane // HEAD_DIM == 0, o4[r0:r0 + tq], o4[r0 + tq:r0 + 2 * tq]))
    return outs


def _swa_prompt_kernel(sink_ref, q_ref, k_ref, v_ref, o_ref):
    outs = _sink_attn_body(q_ref[...], [k_ref[...].astype(BF16)], [v_ref[...].astype(BF16)], [None], sink_ref)
    for grp in range(SWA_GROUPS):
        o_ref[:, grp * 128:(grp + 1) * 128] = outs[grp]


def _swa_sample_kernel(sink_ref, q_ref, kp_ref, kc_ref, kn_ref, vp_ref, vc_ref, vn_ref, ck_ref, cv_ref, o_ref):
    n = pl.program_id(1)
    nb = pl.num_programs(1)
    r = lax.broadcasted_iota(jnp.int32, (SWA_HEADS * WINDOW, WINDOW), 0) & (WINDOW - 1)
    c = lax.broadcasted_iota(jnp.int32, (SWA_HEADS * WINDOW, WINDOW), 1)
    mask_prev = c >= r + jnp.where(n > 0, 0, WINDOW)
    mask_next = c <= r - jnp.where(n < nb - 1, 0, WINDOW)
    keys = [ck_ref[0, 0].astype(BF16), kp_ref[...].astype(BF16), kc_ref[...].astype(BF16), kn_ref[...].astype(BF16)]
    values = [cv_ref[0, 0].astype(BF16), vp_ref[...].astype(BF16), vc_ref[...].astype(BF16), vn_ref[...].astype(BF16)]
    outs = _sink_attn_body(q_ref[...], keys, values, [None, mask_prev, None, mask_next], sink_ref)
    for grp in range(SWA_GROUPS):
        o_ref[:, grp * 128:(grp + 1) * 128] = outs[grp]


def _swa_prompt_call(sink, proj):
    w = GROUP_WIDTH
    return pl.pallas_call(
        _swa_prompt_kernel,
        grid=(BATCH,),
        in_specs=[pl.BlockSpec(memory_space=pltpu.SMEM),
                  pl.BlockSpec((SEQ, w), lambda b: (b, OFF_DQ // w)),
                  pl.BlockSpec((SEQ, 128), lambda b: (b, OFF_DK // 128)),
                  pl.BlockSpec((SEQ, 128), lambda b: (b, OFF_DV // 128))],
        out_specs=pl.BlockSpec((SEQ, w), lambda b: (b, 0)),
        out_shape=jax.ShapeDtypeStruct((N_PROMPT, w), F32),
        compiler_params=_cparams(("arbitrary",)),
        name="swa_prompt",
    )(sink, proj, proj, proj)


def _swa_sample_call(layer, sink, proj, ctx_k, ctx_v):
    w = GROUP_WIDTH
    tq = WINDOW
    nq = DEC_SEQ // tq
    row0 = N_PROMPT // tq

    def kv_spec(col, delta):
        def index(b, i):
            j = jnp.clip(i + delta, 0, nq - 1)
            return (row0 + b * nq + j, col)
        return pl.BlockSpec((tq, 128), index)

    ctx_spec = pl.BlockSpec((1, 1, PAST_LEN, 128), lambda b, i: (b, layer, 0, 0))
    return pl.pallas_call(
        _swa_sample_kernel,
        grid=(DEC_BATCH, nq),
        in_specs=[pl.BlockSpec(memory_space=pltpu.SMEM),
                  pl.BlockSpec((tq, w), lambda b, i: (row0 + b * nq + i, OFF_DQ // w)),
                  kv_spec(OFF_DK // 128, -1), kv_spec(OFF_DK // 128, 0), kv_spec(OFF_DK // 128, 1),
                  kv_spec(OFF_DV // 128, -1), kv_spec(OFF_DV // 128, 0), kv_spec(OFF_DV // 128, 1),
                  ctx_spec, ctx_spec],
        out_specs=pl.BlockSpec((tq, w), lambda b, i: (b * nq + i, 0)),
        out_shape=jax.ShapeDtypeStruct((N_SAMPLE, w), F32),
        compiler_params=_cparams(("arbitrary", "arbitrary")),
        name="swa_sample",
    )(sink, proj, proj, proj, proj, proj, proj, proj, ctx_k, ctx_v)


def _outproj_kernel(seq_ref, ya_ref, ybp_ref, ybs_ref, ycp_ref, ycs_ref, ydp_ref, yds_ref, xp_ref, xs_ref, mod_ref,
                    w_ref, g2_ref, rwt_ref, rbt_ref, triu_ref,
                    x1_ref, h2r_ref, topi_ref, topw_ref, rank_ref, count_ref, cnt_ref):
    del seq_ref
    w = GROUP_WIDTH
    is_prompt = pl.program_id(0) < PROMPT_TILES // OUT_TILES
    mod = mod_ref[0]
    g1 = mod[:, 2 * D_MODEL:3 * D_MODEL]
    sh2 = mod[:, 3 * D_MODEL:4 * D_MODEL]
    sc2 = mod[:, 4 * D_MODEL:5 * D_MODEL]

    @pl.when(pl.program_id(0) % (MOE_TB // (TM * OUT_TILES)) == 0)
    def _():
        cnt_ref[...] = jnp.zeros_like(cnt_ref)

    count = cnt_ref[...]
    for t in range(OUT_TILES):
        rows = slice(t * TM, (t + 1) * TM)
        yb = jnp.where(is_prompt, ybp_ref[rows, :], ybs_ref[rows, :])
        yc = jnp.where(is_prompt, ycp_ref[rows, :], ycs_ref[rows, :])
        yd = jnp.where(is_prompt, ydp_ref[rows, :], yds_ref[rows, :])
        mix = _dot(ya_ref[rows, :].astype(BF16), w_ref[0, 0:w, :])
        mix = mix + _dot(yb.astype(BF16), w_ref[0, w:2 * w, :])
        mix = mix + _dot(yc.astype(BF16), w_ref[0, 2 * w:3 * w, :])
        mix = mix + _dot(yd.astype(BF16), w_ref[0, 3 * w:4 * w, :])
        x1 = jnp.where(is_prompt, xp_ref[rows, :], xs_ref[rows, :]) + g1 * mix
        x1_ref[rows, :] = x1
        xn = x1 * lax.rsqrt(jnp.mean(x1 * x1, axis=-1, keepdims=True) + EPS) * g2_ref[...]
        h2 = xn * (1.0 + sc2) + sh2
        for j in range(ROW_PIECES):
            h2r_ref[pl.ds(t * TM * ROW_PIECES + j, TM, stride=ROW_PIECES), :] = h2[:, j * 128:(j + 1) * 128]
        h_hi, h_lo = _split_bf16(h2)
        r_hi, r_lo = _split_bf16(rwt_ref[...])
        logits = _dot_nt(r_hi, h_hi) + (_dot_nt(r_lo, h_hi) + _dot_nt(r_hi, h_lo)) + rbt_ref[...]
        expert = lax.broadcasted_iota(jnp.int32, logits.shape, 0)
        work = logits
        picks = []
        for _ in range(TOP_K):
            m = jnp.max(work, axis=0, keepdims=True)
            idx = jnp.min(jnp.where(work == m, expert, N_EXPERTS), axis=0, keepdims=True)
            picks.append((m, idx))
            work = jnp.where(expert == idx, -jnp.inf, work)
        top = picks[0][0]
        ex = [jnp.exp(m - top) for m, _ in picks]
        inv = 1.0 / functools.reduce(jnp.add, ex)
        sel = jnp.zeros(logits.shape, F32)
        for _, idx in picks:
            sel = sel + jnp.where(expert == idx, 1.0, 0.0)
        csum = _dot(sel.astype(BF16), triu_ref[...])
        before = count + csum - sel
        count = count + csum[:, TM - 1:TM]
        slot = lax.broadcasted_iota(jnp.int32, (TOPK_PAD, TM), 0)
        top_i = jnp.zeros((TOPK_PAD, TM), jnp.int32)
        top_w = jnp.zeros((TOPK_PAD, TM), F32)
        rank = jnp.zeros((TOPK_PAD, TM), jnp.int32)
        for k, (e, (_, idx)) in enumerate(zip(ex, picks)):
            rank_k = jnp.sum(jnp.where(expert == idx, before, 0.0), axis=0, keepdims=True).astype(jnp.int32)
            top_i = jnp.where(slot == k, idx, top_i)
            top_w = jnp.where(slot == k, e * inv, top_w)
            rank = jnp.where(slot == k, rank_k, rank)
        topi_ref[:, rows] = top_i
        topw_ref[:, rows] = top_w
        rank_ref[:, rows] = rank
    cnt_ref[...] = count
    count_ref[0] = count.astype(jnp.int32)


def _outproj_call(layer, seq_of_tile, ya, yb, yc, yd, x, mod3, w_out, g2, rwt, rbt, triu):
    w = GROUP_WIDTH
    rows = TM * OUT_TILES
    prompt_steps = PROMPT_TILES // OUT_TILES
    full = lambda shape: pl.BlockSpec(shape, lambda i, s: (0,) * len(shape))
    tile = lambda width: pl.BlockSpec((rows, width), lambda i, s: (i, 0))
    prompt_tile = lambda width: pl.BlockSpec((rows, width), lambda i, s: (jnp.minimum(i, prompt_steps - 1), 0))
    sample_tile = lambda width: pl.BlockSpec((rows, width), lambda i, s: (jnp.maximum(i - prompt_steps, 0), 0))
    slots = pl.BlockSpec((TOPK_PAD, rows), lambda i, s: (0, i))
    tiles_per_block = MOE_TB // rows
    return pl.pallas_call(
        _outproj_kernel,
        grid_spec=pltpu.PrefetchScalarGridSpec(
            num_scalar_prefetch=1,
            grid=(N_TILES // OUT_TILES,),
            in_specs=[tile(w), prompt_tile(w), sample_tile(w), prompt_tile(w), sample_tile(w),
                      prompt_tile(w), sample_tile(w), prompt_tile(D_MODEL), sample_tile(D_MODEL),
                      pl.BlockSpec((1, 1, 6 * D_MODEL), lambda i, s: (s[i], 0, 0)),
                      pl.BlockSpec((1, D_MODEL, D_MODEL), lambda i, s: (layer, 0, 0)), full((1, D_MODEL)),
                      full((N_EXPERTS, D_MODEL)), full((N_EXPERTS, 1)), full((TM, TM))],
            out_specs=[tile(D_MODEL), pl.BlockSpec((rows * ROW_PIECES, 128), lambda i, s: (i, 0)),
                       slots, slots, slots,
                       pl.BlockSpec((1, N_EXPERTS, 1), lambda i, s: (i // tiles_per_block, 0, 0))],
            scratch_shapes=[pltpu.VMEM((N_EXPERTS, 1), F32)],
        ),
        out_shape=[jax.ShapeDtypeStruct((N_TOK, D_MODEL), F32),
                   jax.ShapeDtypeStruct((N_TOK * ROW_PIECES, 128), F32),
                   jax.ShapeDtypeStruct((TOPK_PAD, N_TOK), jnp.int32),
                   jax.ShapeDtypeStruct((TOPK_PAD, N_TOK), F32),
                   jax.ShapeDtypeStruct((TOPK_PAD, N_TOK), jnp.int32),
                   jax.ShapeDtypeStruct((N_TOK // MOE_TB, N_EXPERTS, 1), jnp.int32)],
        compiler_params=_cparams(("arbitrary",)),
        name="outproj",
    )(seq_of_tile, ya, *yb, *yc, *yd, *x, mod3, w_out, g2, rwt, rbt, triu)


def _moe_kernel(expert_ref, block_ref, kind_ref, local_ref, dest_ref, topw_ref, gstart_ref, count_ref,
                src_ref, wgu_ref, bgu_ref, wd_ref, bd_ref, acc_ref, pair_ref, xt_even, xt_odd, y3_even, y3_odd):
    del expert_ref, block_ref
    i = pl.program_id(0)
    kind = kind_ref[i]

    def token_of(pair):
        return (pair >> MOE_PAIR_SHIFT) & (MOE_TB - 1)

    def gather(tile, xt_ref):
        row0 = local_ref[tile]
        for m in range(MOE_TILE):
            t = token_of(pair_ref[row0 + m])
            slab = src_ref[pl.ds(pl.multiple_of(t * ROW_PIECES, ROW_PIECES), ROW_PIECES), :]
            xt_ref[pl.ds(m, ROW_PIECES, stride=MOE_XT_STRIDE), :] = slab

    def scatter(tile, y3_ref):
        row0 = local_ref[tile]
        for m0 in range(0, MOE_TILE, MOE_RMW_BATCH):
            rows = range(m0, m0 + MOE_RMW_BATCH)
            pairs = [pair_ref[row0 + m] for m in rows]
            toks = [token_of(p) for p in pairs]
            vals = [acc_ref[t] + topw_ref[0, 0, p] * y3_ref[m // 8, pl.ds(m % 8, 8, stride=8), :]
                    for t, p, m in zip(toks, pairs, rows)]
            for t, v in reversed(list(zip(toks, vals))):
                acc_ref[t] = v

    def compute(xt_ref, y3_ref):
        x = jnp.concatenate([xt_ref[j * MOE_XT_STRIDE:j * MOE_XT_STRIDE + MOE_TILE, :] for j in range(ROW_PIECES)],
                            axis=1).astype(BF16)
        gu = _dot(x, wgu_ref[0, 0]) + bgu_ref[0, 0]
        gate = jnp.minimum(gu[:, :D_FF], SWIGLU_LIMIT)
        up = jnp.clip(gu[:, D_FF:], -SWIGLU_LIMIT, SWIGLU_LIMIT)
        act = (up + 1.0) * gate * jax.nn.sigmoid(SWIGLU_ALPHA * gate)
        y = _dot(act.astype(BF16), wd_ref[0, 0]) + bd_ref[0, 0]
        for j in range(ROW_PIECES):
            y3_ref[:, j * 8:(j + 1) * 8, :] = y[:, j * 128:(j + 1) * 128].reshape(MOE_TILE // 8, 8, 128)

    @pl.when(kind == MOE_KIND_FIRST)
    def _():
        def zero(c, carry):
            acc_ref[pl.ds(pl.multiple_of(c * MOE_TILE, MOE_TILE), MOE_TILE)] = jnp.zeros((MOE_TILE, 8, 128), F32)
            return carry
        lax.fori_loop(0, MOE_TB // MOE_TILE, zero, 0)

        def pad_group(e, carry):
            cnt = count_ref[0, 0, e]
            first_pad = gstart_ref[0, 0, e] + cnt
            n_pad = (-cnt) & (MOE_TILE - 1)

            def pad_row(r, c2):
                pair_ref[first_pad + r] = MOE_PAD_PAIR
                return c2
            lax.fori_loop(0, n_pad, pad_row, 0)
            return carry
        lax.fori_loop(0, N_EXPERTS, pad_group, 0)

        def place(c, carry):
            for k in range(MOE_PLACE_UNROLL):
                pair = c * MOE_PLACE_UNROLL + k
                pair_ref[dest_ref[0, 0, pair]] = pair
            return carry
        lax.fori_loop(0, MOE_TB * TOP_K // MOE_PLACE_UNROLL, place, 0)

    step = kind * 2 + (i & 1)
    for parity, (xt_cur, xt_nxt, y3_cur, y3_prv) in enumerate(((xt_even, xt_odd, y3_even, y3_odd),
                                                               (xt_odd, xt_even, y3_odd, y3_even))):
        @pl.when(step == MOE_KIND_FIRST * 2 + parity)
        def _():
            gather(i, xt_cur)
            gather(i + 1, xt_nxt)
            compute(xt_cur, y3_cur)

        @pl.when(step == MOE_KIND_MIDDLE * 2 + parity)
        def _():
            gather(i + 1, xt_nxt)
            compute(xt_cur, y3_cur)
            scatter(i - 1, y3_prv)

        @pl.when(step == MOE_KIND_LAST * 2 + parity)
        def _():
            compute(xt_cur, y3_cur)
            scatter(i - 1, y3_prv)
            scatter(i, y3_cur)


def _moe_call(layer, tile_expert, tile_block, tile_flags, tile_local, dest, topw, gstart, counts, h2_rows,
              wgu, bgu, wd, bd):
    pieces = ROW_PIECES
    once = pl.Buffered(1)
    n_pairs = MOE_TB * TOP_K

    def smem_block(width):
        return pl.BlockSpec((1, 1, width), lambda i, e, b, f, lo: (b[i], 0, 0), memory_space=pltpu.SMEM,
                            pipeline_mode=once)

    def expert_block(rows, cols):
        return pl.BlockSpec((1, 1, rows, cols), lambda i, e, b, f, lo: (layer, e[i], 0, 0))

    return pl.pallas_call(
        _moe_kernel,
        grid_spec=pltpu.PrefetchScalarGridSpec(
            num_scalar_prefetch=4,
            grid=(MOE_MAX_TILES,),
            in_specs=[smem_block(n_pairs), smem_block(n_pairs + 128), smem_block(N_EXPERTS), smem_block(N_EXPERTS),
                      pl.BlockSpec((MOE_TB * pieces, 128), lambda i, e, b, f, lo: (b[i], 0), pipeline_mode=once),
                      expert_block(D_MODEL, 2 * D_FF), expert_block(1, 2 * D_FF),
                      expert_block(D_FF, D_MODEL), expert_block(1, D_MODEL)],
            out_specs=pl.BlockSpec((MOE_TB, 8, 128), lambda i, e, b, f, lo: (b[i], 0, 0), pipeline_mode=once),
            scratch_shapes=[pltpu.SMEM((MOE_ROWS_PER_BLOCK,), jnp.int32),
                            pltpu.VMEM((pieces * MOE_XT_STRIDE, 128), F32),
                            pltpu.VMEM((pieces * MOE_XT_STRIDE, 128), F32),
                            pltpu.VMEM((MOE_TILE // 8, 8 * pieces, 128), F32),
                            pltpu.VMEM((MOE_TILE // 8, 8 * pieces, 128), F32)],
        ),
        out_shape=jax.ShapeDtypeStruct((N_TOK, 8, 128), F32),
        compiler_params=_cparams(("arbitrary",)),
        name="moe",
    )(tile_expert, tile_block, tile_flags, tile_local, dest, topw, gstart, counts, h2_rows, wgu, bgu, wd, bd)


def _moe_routing(top_i, top_w, rank, counts):
    nb = N_TOK // MOE_TB
    n_groups = nb * N_EXPERTS
    counts = counts.reshape(nb, N_EXPERTS)
    padded = ((counts + MOE_TILE - 1) // MOE_TILE) * MOE_TILE
    group_end = jnp.cumsum(padded.reshape(-1))
    group_start = (group_end - padded.reshape(-1)).reshape(nb, N_EXPERTS)
    block_row0 = group_start[:, 0]
    gstart_local = group_start - block_row0[:, None]
    experts = jnp.arange(N_EXPERTS, dtype=jnp.int32)
    pair_major = lambda a: jnp.transpose(a.reshape(TOP_K, nb, MOE_TB), (1, 2, 0)).reshape(nb, 1, MOE_TB * TOP_K)
    picked = top_i.reshape(TOP_K, nb, MOE_TB, 1) == experts
    dest = jnp.sum(jnp.where(picked, gstart_local[None, :, None, :], 0), axis=-1) + rank.reshape(TOP_K, nb, MOE_TB)
    dest = pair_major(dest).astype(jnp.int32)
    topw = jnp.concatenate([pair_major(top_w), jnp.zeros((nb, 1, 128), F32)], axis=-1)
    tile_start = jnp.arange(MOE_MAX_TILES, dtype=jnp.int32) * MOE_TILE
    tile_group = jnp.sum((group_end[None, :] <= tile_start[:, None]).astype(jnp.int32), axis=1)
    tile_group = jnp.minimum(tile_group, n_groups - 1)
    valid = tile_start < group_end[-1]
    tile_block = tile_group // N_EXPERTS
    new_block = tile_block[1:] != tile_block[:-1]
    first = jnp.concatenate([jnp.ones((1,), bool), new_block])
    last = jnp.concatenate([new_block | ~valid[1:], jnp.ones((1,), bool)])
    kinds = jnp.where(first, MOE_KIND_FIRST, jnp.where(last, MOE_KIND_LAST, MOE_KIND_MIDDLE))
    flags = jnp.where(valid, kinds, MOE_KIND_SKIP).astype(jnp.int32)
    blocks = jnp.arange(nb, dtype=jnp.int32)
    tile_row0 = jnp.sum(jnp.where(tile_block[:, None] == blocks[None, :], block_row0[None, :], 0), axis=1)
    tile_local = (tile_start - tile_row0).astype(jnp.int32)
    return (tile_group % N_EXPERTS, tile_block, flags, tile_local, dest, topw,
            gstart_local.reshape(nb, 1, N_EXPERTS).astype(jnp.int32), counts.reshape(nb, 1, N_EXPERTS))


def _residual_kernel(seq_ref, x1_ref, moe_ref, mod_ref, o_ref):
    del seq_ref
    g2 = mod_ref[0][:, 5 * D_MODEL:6 * D_MODEL]
    moe = jnp.concatenate([moe_ref[pl.ds(j, TM, stride=ROW_PIECES), :] for j in range(ROW_PIECES)], axis=1)
    o_ref[...] = x1_ref[...] + g2 * moe


def _residual_call(seq_of_tile, x1, moe_rows, mod3, first_tile, n_tiles):
    return pl.pallas_call(
        _residual_kernel,
        grid_spec=pltpu.PrefetchScalarGridSpec(
            num_scalar_prefetch=1,
            grid=(n_tiles,),
            in_specs=[pl.BlockSpec((TM, D_MODEL), lambda i, s: (i + first_tile, 0)),
                      pl.BlockSpec((TM * ROW_PIECES, 128), lambda i, s: (i + first_tile, 0)),
                      pl.BlockSpec((1, 1, 6 * D_MODEL), lambda i, s: (s[i + first_tile], 0, 0))],
            out_specs=pl.BlockSpec((TM, D_MODEL), lambda i, s: (i, 0)),
        ),
        out_shape=jax.ShapeDtypeStruct((n_tiles * TM, D_MODEL), F32),
        compiler_params=_cparams(("arbitrary",)),
        name="residual",
    )(seq_of_tile, x1, moe_rows, mod3)


def _segment_matrix(width, seg):
    idx = np.arange(width) // seg
    return jnp.asarray((idx[:, None] == idx[None, :]).astype(np.float32) / seg, BF16)


def _rope_tables(rot_dim, width):
    rows = DEC_SEQ // GRID_W
    nf = rot_dim // 4
    f32 = np.float32
    inv = (f32(1.0) / (f32(ROPE_BASE) ** (np.arange(nf, dtype=f32) / f32(nf)))).astype(f32)
    row = np.repeat(np.arange(rows, dtype=f32), GRID_W)
    col = np.tile(np.arange(GRID_W, dtype=f32), rows)
    ang = np.stack([row[:, None] * inv, col[:, None] * inv], axis=1).astype(f32)
    cos, sin = np.cos(ang).astype(f32), np.sin(ang).astype(f32)
    cos_r = np.concatenate([cos[:, 0], cos[:, 0], cos[:, 1], cos[:, 1]], axis=-1)
    sin_r = np.concatenate([-sin[:, 0], sin[:, 0], -sin[:, 1], sin[:, 1]], axis=-1)
    reps = width // rot_dim
    cos_t = np.concatenate([np.ones((TM, width), f32), np.tile(cos_r, (1, reps))], axis=0)
    sin_t = np.concatenate([np.zeros((TM, width), f32), np.tile(sin_r, (1, reps))], axis=0)
    return jnp.asarray(cos_t), jnp.asarray(sin_t)


def _block_diag(wb):
    nb, n = wb.shape[-3], wb.shape[-1]
    rows = wb.reshape(wb.shape[:-3] + (nb * n, n))
    tiled = jnp.tile(rows, (1,) * (wb.ndim - 2) + (nb,))
    blk = np.arange(nb * n) // n
    return jnp.where(jnp.asarray(blk[:, None] == blk[None, :]), tiled, 0.0)


_DQ_HEADS = [kh * SWA_GROUPS + g for g in range(SWA_GROUPS) for kh in range(SWA_KV_HEADS)]


def kernel(x_prompt, x_sample, c, cache_diff_k, cache_diff_v, cache_swa_k, cache_swa_v, state_lru, c_ctx, mod_w, mod_b, norm1_g, norm2_g, w_in, w_out, mlp_vnorm_g, mlp_ws, mlp_bs, lru_conv_w, lru_conv_b, lru_wa, lru_ba, lru_wx, lru_bx, lru_lambda, diff_qnorm_g, diff_knorm_g, diff_lambda, diff_subln_g, swa_qnorm_g, swa_knorm_g, swa_sink, router_w, router_b, moe_w_gu, moe_b_gu, moe_w_down, moe_b_down):
    params = dict(locals())
    consts = _constants()
    prep = _prepare(params, consts)
    x = (x_prompt.reshape(N_PROMPT, D_MODEL), x_sample.reshape(N_SAMPLE, D_MODEL))
    dk_l, dv_l, sk_l, sv_l, lru_l = [], [], [], [], []
    for l in range(DEPTH):
        st = _mixer_stage(x, params, prep, l, consts)
        x = _ffn_stage(x, st, params, prep, l, consts)
        pp = st["proj"][:N_PROMPT]
        dk_l.append(pp[:, OFF_CK:OFF_CV].reshape(BATCH, SEQ, N_GROUP_HEADS, 2, DIFF_QK_DIM))
        dv_l.append(pp[:, OFF_CV:OFF_DQ].reshape(BATCH, SEQ, N_GROUP_HEADS, HEAD_DIM))
        sk_l.append(pp[:, OFF_DK:OFF_DV].reshape(BATCH, SEQ, SWA_KV_HEADS, HEAD_DIM))
        sv_l.append(pp[:, OFF_DV:IN_WIDTH].reshape(BATCH, SEQ, SWA_KV_HEADS, HEAD_DIM))
        lru_l.append(st["st_p"])

    y_p = x[0].reshape(BATCH, SEQ, D_MODEL)
    y_s = x[1].reshape(DEC_BATCH, DEC_SEQ, D_MODEL)
    return (y_p, y_s, jnp.stack(dk_l, axis=1), jnp.stack(dv_l, axis=1), jnp.stack(sk_l, axis=1),
            jnp.stack(sv_l, axis=1), jnp.stack(lru_l, axis=1))


def _constants():
    w = GROUP_WIDTH
    tiles = np.arange(N_TILES)
    seq_np = np.where(tiles < PROMPT_TILES, 0, 1 + (tiles - PROMPT_TILES) // SAMPLE_TILES_PER_SEQ)
    rope_np = np.where(tiles < PROMPT_TILES, 0, 1 + (tiles - PROMPT_TILES) % SAMPLE_TILES_PER_SEQ)
    seq_of_tile = [jnp.asarray(seq_np + l * N_COND, jnp.int32) for l in range(DEPTH)]
    seq_of_out_step = [jnp.asarray(seq_np[::OUT_TILES] + l * N_COND, jnp.int32) for l in range(DEPTH)]
    rope_of_tile = jnp.asarray(rope_np, jnp.int32)

    s32 = _segment_matrix(w, DIFF_QK_DIM)
    s64 = _segment_matrix(w, HEAD_DIM)
    cosc, sinc = _rope_tables(DIFF_QK_DIM, w)
    cosd, sind = _rope_tables(HEAD_DIM, w)
    dq_cols = np.concatenate([np.arange(h * HEAD_DIM, (h + 1) * HEAD_DIM) for h in _DQ_HEADS])
    triu = jnp.asarray(np.triu(np.ones((TM, TM), np.float32)), BF16)
    return dict(seq_of_tile=seq_of_tile, seq_of_out_step=seq_of_out_step, rope_of_tile=rope_of_tile, triu=triu,
                s32=s32, s64=s64,
                cosc=cosc, sinc=sinc, cosd=cosd, sind=sind, dq_cols=dq_cols)


def _prepare(params, consts):
    w = GROUP_WIDTH
    p = params
    dq = consts["dq_cols"]
    heads = lambda g, reps: jnp.tile(g.reshape(DEPTH, 1, -1), (1, 1, reps))
    w_in, w_out = p["w_in"], p["w_out"]
    lru_w = jnp.stack([p["lru_wa"][:, 0], p["lru_wx"][:, 0], p["lru_wa"][:, 1], p["lru_wx"][:, 1]], axis=1)
    lru_b = jnp.stack([p["lru_ba"][:, 0], p["lru_bx"][:, 0], p["lru_ba"][:, 1], p["lru_bx"][:, 1]], axis=1)
    cond = jnp.concatenate([p["c_ctx"][None], p["c"], jnp.zeros((N_COND - 1 - DEC_BATCH, D_MODEL), F32)], axis=0)
    return dict(
        mod3=_mod_call(cond, p["mod_w"], p["mod_b"]).reshape(DEPTH * N_COND, 1, 6 * D_MODEL),
        w_in=jnp.concatenate([w_in[..., :OFF_DQ], w_in[..., OFF_DQ:OFF_DK][..., dq], w_in[..., OFF_DK:]],
                             axis=-1).astype(BF16),
        w_out=jnp.concatenate([w_out[:, :3 * w], w_out[:, 3 * w:][:, dq]], axis=1).astype(BF16),
        gqc=heads(p["diff_qnorm_g"], N_GROUP_HEADS), gkc=heads(p["diff_knorm_g"], N_GROUP_HEADS),
        gqd=heads(p["swa_qnorm_g"], SWA_HEADS), gkd=heads(p["swa_knorm_g"], SWA_KV_HEADS),
        g_sub=heads(p["diff_subln_g"], N_GROUP_HEADS), g_mlp=p["mlp_vnorm_g"].reshape(DEPTH, 1, w),
        mlp_bias=jnp.repeat(jnp.swapaxes(p["mlp_bs"], 1, 2), HEAD_DIM, axis=2),
        mlp_ws=p["mlp_ws"].astype(BF16),
        wcat=jnp.swapaxes(_block_diag(lru_w), 1, 2).reshape(DEPTH, w, 4 * w).astype(BF16),
        bcat=lru_b.reshape(DEPTH, 1, 4 * w),
        conv_b=p["lru_conv_b"].reshape(DEPTH, 1, w),
        g1=p["norm1_g"].reshape(DEPTH, 1, D_MODEL), g2=p["norm2_g"].reshape(DEPTH, 1, D_MODEL),
        router_wt=jnp.swapaxes(p["router_w"], 1, 2), router_bt=p["router_b"].reshape(DEPTH, N_EXPERTS, 1),
        moe_w_gu=p["moe_w_gu"].astype(BF16), moe_w_down=p["moe_w_down"].astype(BF16),
        moe_b_gu=p["moe_b_gu"].reshape(DEPTH, N_EXPERTS, 1, 2 * D_FF),
        moe_b_down=p["moe_b_down"].reshape(DEPTH, N_EXPERTS, 1, D_MODEL),
        ctx_diff_k=p["cache_diff_k"].reshape(DEC_BATCH, DEPTH, PAST_LEN, w),
        ctx_diff_v=p["cache_diff_v"].reshape(DEC_BATCH, DEPTH, PAST_LEN, w),
        ctx_swa_k=p["cache_swa_k"].reshape(DEC_BATCH, DEPTH, PAST_LEN, SWA_KV_HEADS * HEAD_DIM),
        ctx_swa_v=p["cache_swa_v"].reshape(DEC_BATCH, DEPTH, PAST_LEN, SWA_KV_HEADS * HEAD_DIM),
    )


def _mixer_stage(x, params, prep, l, consts):
    w = GROUP_WIDTH
    s32, s64 = consts["s32"], consts["s64"]
    lam_init = 0.8 - 0.6 * math.exp(-0.3 * l)
    proj = _inproj_call(l, consts["seq_of_tile"][l], consts["rope_of_tile"], x, prep["mod3"], prep["g1"][l],
                        prep["w_in"], s32, s64, prep["gqc"][l], prep["gkc"][l], prep["gqd"][l], prep["gkd"][l],
                        consts["cosc"], consts["sinc"], consts["cosd"], consts["sind"])
    ya = _gmlp_call(proj, s64, prep["g_mlp"][l], prep["mlp_ws"][l], prep["mlp_bias"][l])

    lru_args = (params["lru_conv_w"][l], prep["conv_b"][l], prep["wcat"][l], prep["bcat"][l], params["lru_lambda"][l])
    yb_p, st_p = _lru_call(proj, jnp.zeros((BATCH, 2, w), F32), *lru_args, seq_len=SEQ, n_seq=BATCH, row_block0=0)
    yb_s, _ = _lru_call(proj, params["state_lru"][:, l], *lru_args, seq_len=DEC_SEQ, n_seq=DEC_BATCH,
                        row_block0=N_PROMPT // DEC_SEQ)

    ld = params["diff_lambda"][l]
    yc_p = _diff_prompt_call(proj, ld, s64, prep["g_sub"][l], lam_init)
    yc_s = _diff_sample_call(l, proj, prep["ctx_diff_k"], prep["ctx_diff_v"], ld, s64, prep["g_sub"][l], lam_init)

    sink = params["swa_sink"][l]
    yd_p = _swa_prompt_call(sink, proj)
    yd_s = _swa_sample_call(l, sink, proj, prep["ctx_swa_k"], prep["ctx_swa_v"])
    return dict(proj=proj, ya=ya, yb=(yb_p, yb_s), yc=(yc_p, yc_s), yd=(yd_p, yd_s), st_p=st_p)


def _ffn_stage(x, st, params, prep, l, consts):
    seq_of_tile = consts["seq_of_tile"][l]
    x1, h2_rows, top_i, top_w, rank, counts = _outproj_call(
        l, consts["seq_of_out_step"][l], st["ya"], st["yb"], st["yc"], st["yd"], x, prep["mod3"], prep["w_out"],
        prep["g2"][l],
        prep["router_wt"][l], prep["router_bt"][l], consts["triu"])
    routing = _moe_routing(top_i[:TOP_K], top_w[:TOP_K], rank[:TOP_K], counts)
    moe = _moe_call(l, *routing, h2_rows, prep["moe_w_gu"], prep["moe_b_gu"], prep["moe_w_down"], prep["moe_b_down"])
    moe_rows = moe.reshape(N_TOK * ROW_PIECES, 128)
    return (_residual_call(seq_of_tile, x1, moe_rows, prep["mod3"], 0, PROMPT_TILES),
            _residual_call(seq_of_tile, x1, moe_rows, prep["mod3"], PROMPT_TILES, N_TILES - PROMPT_TILES))
```

```python
import functools
import math

import jax
import jax.numpy as jnp
import numpy as np
from jax import lax
from jax.experimental import pallas as pl
from jax.experimental.pallas import tpu as pltpu

F32 = jnp.float32
BF16 = jnp.bfloat16

D_MODEL = 1024
BATCH = 16
SEQ = 256
DEPTH = 2
DEC_BATCH = 4
DEC_SEQ = 2048
PAST_LEN = 512
GRID_W = 64
HEAD_DIM = 64
GROUP_WIDTH = 256
N_GROUP_HEADS = 4
CHUNK = 128
LRU_C = 8.0
CONV_W = 4
DIFF_QK_DIM = 32
SWA_KV_HEADS = 2
SWA_GROUPS = 2
SWA_HEADS = SWA_KV_HEADS * SWA_GROUPS
WINDOW = 128
N_EXPERTS = 32
TOP_K = 4
D_FF = 1024
SWIGLU_LIMIT = 7.0
SWIGLU_ALPHA = 1.702
ROPE_BASE = 10000.0
EPS = 1e-6
LOG2_E = math.log2(math.e)

N_PROMPT = BATCH * SEQ
N_SAMPLE = DEC_BATCH * DEC_SEQ
N_TOK = N_PROMPT + N_SAMPLE
N_COND = 8
TM = 256
N_TILES = N_TOK // TM
PROMPT_TILES = N_PROMPT // TM
SAMPLE_TILES_PER_SEQ = DEC_SEQ // TM
IN_WIDTH = 2304
OFF_AU, OFF_AV, OFF_BX, OFF_BG, OFF_CQ, OFF_CK, OFF_CV, OFF_DQ, OFF_DK, OFF_DV = (
    0, 256, 512, 768, 1024, 1280, 1536, 1792, 2048, 2176)
TOPK_PAD = 8
ROW_PIECES = D_MODEL // 128
MOE_TILE = 256
MOE_TB = 4096
MOE_MAX_TILES = (N_TOK // MOE_TB) * (MOE_TB * TOP_K // MOE_TILE + N_EXPERTS)
MOE_XT_STRIDE = MOE_TILE + 8
MOE_RMW_BATCH = 4
MOE_PLACE_UNROLL = 16
MOE_ROWS_PER_BLOCK = MOE_TB * TOP_K + N_EXPERTS * MOE_TILE
MOE_PAD_PAIR = MOE_TB * TOP_K
MOE_PAIR_SHIFT = TOP_K.bit_length() - 1
assert 1 << MOE_PAIR_SHIFT == TOP_K
MOE_KIND_SKIP, MOE_KIND_FIRST, MOE_KIND_MIDDLE, MOE_KIND_LAST = 0, 1, 2, 3
OUT_TILES = 2
IN_TILES = 2
assert PROMPT_TILES % OUT_TILES == 0 and SAMPLE_TILES_PER_SEQ % OUT_TILES == 0
assert PROMPT_TILES % IN_TILES == 0 and SAMPLE_TILES_PER_SEQ % IN_TILES == 0
LRU_CHUNK = 256
DIFF_HEADS_PER_PASS = 1
VMEM_LIMIT = 56 * 1024 * 1024


def _cparams(sem):
    return pltpu.CompilerParams(dimension_semantics=sem, vmem_limit_bytes=VMEM_LIMIT)


def _dot(a, b):
    return jnp.dot(a, b, preferred_element_type=F32)


def _dot_nt(a, b):
    return lax.dot_general(a, b, (((1,), (1,)), ((), ())), preferred_element_type=F32)


def _split_bf16(x):
    hi = x.astype(BF16)
    lo = (x - hi.astype(F32)).astype(BF16)
    return hi, lo


def _seg_rms_norm(x, seg_mat, g):
    hi, lo = _split_bf16(x * x)
    ms = _dot(hi, seg_mat) + _dot(lo, seg_mat)
    return x * lax.rsqrt(ms + EPS) * g


def _rope(x, cos_t, sin_t, nf):
    n = x.shape[-1]
    lane = lax.broadcasted_iota(jnp.int32, x.shape, 1)
    first = (lane & (2 * nf - 1)) < nf
    partner = jnp.where(first, pltpu.roll(x, n - nf, axis=1), pltpu.roll(x, nf, axis=1))
    return x * cos_t + partner * sin_t


def _softplus(x):
    return jnp.maximum(x, 0.0) + jnp.log1p(jnp.exp(-jnp.abs(x)))


def _mod_kernel(cond_ref, w_ref, b_ref, o_ref):
    c = cond_ref[...]
    s = c * jax.nn.sigmoid(c)
    o_ref[0] = _dot(s.astype(BF16), w_ref[0].astype(BF16)) + b_ref[0]


def _mod_call(cond, w, b):
    nb = 6
    return pl.pallas_call(
        _mod_kernel,
        grid=(DEPTH, nb),
        in_specs=[pl.BlockSpec((N_COND, D_MODEL), lambda l, j: (0, 0)),
                  pl.BlockSpec((1, D_MODEL, D_MODEL), lambda l, j: (l, 0, j)),
                  pl.BlockSpec((1, 1, D_MODEL), lambda l, j: (l, 0, j))],
        out_specs=pl.BlockSpec((1, N_COND, D_MODEL), lambda l, j: (l, 0, j)),
        out_shape=jax.ShapeDtypeStruct((DEPTH, N_COND, 6 * D_MODEL), F32),
        compiler_params=_cparams(("arbitrary", "arbitrary")),
        name="mod",
    )(cond, w, b.reshape(DEPTH, 1, 6 * D_MODEL))


def _inproj_kernel(seq_ref, rope_ref, xp_ref, xs_ref, mod_ref, g1_ref, w_ref, s32_ref, s64_ref,
                   gqc_ref, gkc_ref, gqd_ref, gkd_ref, cosc_ref, sinc_ref, cosd_ref, sind_ref, o_ref):
    del seq_ref, rope_ref
    x = jnp.where(pl.program_id(0) < PROMPT_TILES // IN_TILES, xp_ref[...], xs_ref[...])
    xn = x * lax.rsqrt(jnp.mean(x * x, axis=-1, keepdims=True) + EPS) * g1_ref[...]
    mod = mod_ref[0]
    sh1 = mod[:, 0:D_MODEL]
    sc1 = mod[:, D_MODEL:2 * D_MODEL]
    h = xn * (1.0 + sc1) + sh1
    p = _dot(h.astype(BF16), w_ref[0])
    o_ref[:, OFF_AU:OFF_BX] = jax.nn.gelu(p[:, OFF_AU:OFF_BX])
    o_ref[:, OFF_BX:OFF_BG] = p[:, OFF_BX:OFF_BG]
    o_ref[:, OFF_BG:OFF_CQ] = jax.nn.gelu(p[:, OFF_BG:OFF_CQ])
    s32 = s32_ref[...]
    cosc = cosc_ref[...]
    sinc = sinc_ref[...]
    cq = _seg_rms_norm(p[:, OFF_CQ:OFF_CK], s32, gqc_ref[...])
    ck = _seg_rms_norm(p[:, OFF_CK:OFF_CV], s32, gkc_ref[...])
    o_ref[:, OFF_CQ:OFF_CK] = _rope(cq, cosc, sinc, DIFF_QK_DIM // 4)
    o_ref[:, OFF_CK:OFF_CV] = _rope(ck, cosc, sinc, DIFF_QK_DIM // 4)
    o_ref[:, OFF_CV:OFF_DQ] = p[:, OFF_CV:OFF_DQ]
    s64 = s64_ref[...]
    cosd = cosd_ref[...]
    sind = sind_ref[...]
    dq = _seg_rms_norm(p[:, OFF_DQ:OFF_DK], s64, gqd_ref[...])
    dk = _seg_rms_norm(p[:, OFF_DK:OFF_DV], s64[0:128, 0:128], gkd_ref[...])
    o_ref[:, OFF_DQ:OFF_DK] = _rope(dq, cosd, sind, HEAD_DIM // 4)
    o_ref[:, OFF_DK:OFF_DV] = _rope(dk, cosd[:, 0:128], sind[:, 0:128], HEAD_DIM // 4)
    o_ref[:, OFF_DV:IN_WIDTH] = p[:, OFF_DV:IN_WIDTH]


def _inproj_call(layer, seq_of_tile, rope_of_tile, x, mod3, g1, w_in, s32, s64, gqc, gkc, gqd, gkd,
                 cosc, sinc, cosd, sind):
    full = lambda shape: pl.BlockSpec(shape, lambda i, s, r: (0,) * len(shape))
    rows = TM * IN_TILES
    prompt_steps = PROMPT_TILES // IN_TILES
    rope_spec = pl.BlockSpec((rows, GROUP_WIDTH), lambda i, s, r: (r[i], 0))
    grid_spec = pltpu.PrefetchScalarGridSpec(
        num_scalar_prefetch=2,
        grid=(N_TILES // IN_TILES,),
        in_specs=[pl.BlockSpec((rows, D_MODEL), lambda i, s, r: (jnp.minimum(i, prompt_steps - 1), 0)),
                  pl.BlockSpec((rows, D_MODEL), lambda i, s, r: (jnp.maximum(i - prompt_steps, 0), 0)),
                  pl.BlockSpec((1, 1, 6 * D_MODEL), lambda i, s, r: (s[i], 0, 0)),
                  full((1, D_MODEL)),
                  pl.BlockSpec((1, D_MODEL, IN_WIDTH), lambda i, s, r: (layer, 0, 0)),
                  full((GROUP_WIDTH, GROUP_WIDTH)),
                  full((GROUP_WIDTH, GROUP_WIDTH)),
                  full((1, GROUP_WIDTH)), full((1, GROUP_WIDTH)), full((1, GROUP_WIDTH)), full((1, 128)),
                  rope_spec, rope_spec, rope_spec, rope_spec],
        out_specs=pl.BlockSpec((rows, IN_WIDTH), lambda i, s, r: (i, 0)),
    )
    return pl.pallas_call(
        _inproj_kernel,
        grid_spec=grid_spec,
        out_shape=jax.ShapeDtypeStruct((N_TOK, IN_WIDTH), F32),
        compiler_params=_cparams(("arbitrary",)),
        name="inproj",
    )(seq_of_tile, rope_of_tile, *x, mod3, g1, w_in, s32, s64, gqc, gkc, gqd, gkd, cosc, sinc, cosd, sind)


def _gmlp_kernel(u_ref, v_ref, s64_ref, g_ref, ws_ref, bias_ref, o_ref):
    vh = _seg_rms_norm(v_ref[...], s64_ref[...], g_ref[...]).astype(BF16)
    head = lax.broadcasted_iota(jnp.int32, (CHUNK, GROUP_WIDTH), 1) // HEAD_DIM
    for ch in range(TM // CHUNK):
        rows = slice(ch * CHUNK, (ch + 1) * CHUNK)
        vc = vh[rows]
        mixed = bias_ref[...]
        for h in range(N_GROUP_HEADS):
            mixed = mixed + jnp.where(head == h, _dot(ws_ref[h], vc), 0.0)
        o_ref[rows, :] = u_ref[rows, :] * mixed


def _gmlp_call(proj, s64, g, ws, bias):
    return pl.pallas_call(
        _gmlp_kernel,
        grid=(N_TILES,),
        in_specs=[pl.BlockSpec((TM, GROUP_WIDTH), lambda i: (i, OFF_AU // GROUP_WIDTH)),
                  pl.BlockSpec((TM, GROUP_WIDTH), lambda i: (i, OFF_AV // GROUP_WIDTH)),
                  pl.BlockSpec((GROUP_WIDTH, GROUP_WIDTH), lambda i: (0, 0)),
                  pl.BlockSpec((1, GROUP_WIDTH), lambda i: (0, 0)),
                  pl.BlockSpec((N_GROUP_HEADS, CHUNK, CHUNK), lambda i: (0, 0, 0)),
                  pl.BlockSpec((CHUNK, GROUP_WIDTH), lambda i: (0, 0))],
        out_specs=pl.BlockSpec((TM, GROUP_WIDTH), lambda i: (i, 0)),
        out_shape=jax.ShapeDtypeStruct((N_TOK, GROUP_WIDTH), F32),
        compiler_params=_cparams(("arbitrary",)),
        name="gmlp",
    )(proj, proj, s64, g, ws, bias)


def _scan_chunk(a, b, reverse):
    n = a.shape[0]
    row = lax.broadcasted_iota(jnp.int32, a.shape, 0)
    s = 1
    while s < n:
        if reverse:
            keep = row < n - s
            shift = n - s
        else:
            keep = row >= s
            shift = s
        a_prev = jnp.where(keep, pltpu.roll(a, shift, axis=0), 1.0)
        b_prev = jnp.where(keep, pltpu.roll(b, shift, axis=0), 0.0)
        b = a * b_prev + b
        a = a * a_prev
        s *= 2
    return a, b


def _lru_kernel(x_ref, g_ref, h0_ref, cw_ref, cb_ref, wcat_ref, bcat_ref, lam_ref, y_ref, st_ref,
                xpad, a_f, b_f, a_b, b_b, h_f, *, seq_len):
    nc = seq_len // LRU_CHUNK
    w = GROUP_WIDTH
    zeros8 = jnp.zeros((8, w), F32)
    xpad[0:8, :] = zeros8
    xpad[seq_len + 8:seq_len + 16, :] = zeros8
    xpad[8:seq_len + 8, :] = x_ref[...]
    sp = _softplus(-lam_ref[...])
    cw = cw_ref[...]
    cb = cb_ref[...]
    win_rows = LRU_CHUNK + 16

    def gates(c, carry):
        r0 = pl.multiple_of(c * LRU_CHUNK, LRU_CHUNK)
        win = xpad[pl.ds(r0, win_rows), :]
        inner = slice(8, 8 + LRU_CHUNK)
        xc = cb + pltpu.roll(win, 2, axis=0)[inner] * cw[0:1]
        xc = xc + pltpu.roll(win, 1, axis=0)[inner] * cw[1:2]
        xc = xc + win[inner] * cw[2:3]
        xc = xc + pltpu.roll(win, win_rows - 1, axis=0)[inner] * cw[3:4]
        sg = jax.nn.sigmoid(_dot(xc.astype(BF16), wcat_ref[...]) + bcat_ref[...])
        for d, (a_ref, b_ref) in enumerate(((a_f, b_f), (a_b, b_b))):
            r = sg[:, (2 * d) * w:(2 * d + 1) * w]
            i = sg[:, (2 * d + 1) * w:(2 * d + 2) * w]
            log_a = (-LRU_C * r) * sp[d:d + 1]
            a = jnp.exp(log_a)
            a_ref[pl.ds(r0, LRU_CHUNK), :] = a
            b_ref[pl.ds(r0, LRU_CHUNK), :] = jnp.sqrt(-jnp.tanh(log_a) * (a * a + 1.0)) * (i * xc)
        return carry

    lax.fori_loop(0, nc, gates, 0)

    def fwd(c, carry):
        r0 = pl.multiple_of(c * LRU_CHUNK, LRU_CHUNK)
        a_cum, h_loc = _scan_chunk(a_f[pl.ds(r0, LRU_CHUNK), :], b_f[pl.ds(r0, LRU_CHUNK), :], False)
        h = h_loc + a_cum * carry
        h_f[pl.ds(r0, LRU_CHUNK), :] = h
        return h[LRU_CHUNK - 1:LRU_CHUNK, :]

    s_f = lax.fori_loop(0, nc, fwd, h0_ref[0, 0:1, :])

    def bwd(k, carry):
        c = nc - 1 - k
        r0 = pl.multiple_of(c * LRU_CHUNK, LRU_CHUNK)
        a_cum, h_loc = _scan_chunk(a_b[pl.ds(r0, LRU_CHUNK), :], b_b[pl.ds(r0, LRU_CHUNK), :], True)
        h = h_loc + a_cum * carry
        y_ref[pl.ds(r0, LRU_CHUNK), :] = (h_f[pl.ds(r0, LRU_CHUNK), :] + h) * g_ref[pl.ds(r0, LRU_CHUNK), :]
        return h[0:1, :]

    s_b = lax.fori_loop(0, nc, bwd, h0_ref[0, 1:2, :])
    st_ref[0, 0:1, :] = s_f
    st_ref[0, 1:2, :] = s_b


def _lru_call(proj, h0, cw, cb, wcat, bcat, lam, *, seq_len, n_seq, row_block0):
    w = GROUP_WIDTH
    full = lambda shape: pl.BlockSpec(shape, lambda b: (0,) * len(shape))
    seq_block = lambda col: pl.BlockSpec((seq_len, w), lambda b: (b + row_block0, col))
    return pl.pallas_call(
        functools.partial(_lru_kernel, seq_len=seq_len),
        grid=(n_seq,),
        in_specs=[seq_block(OFF_BX // w), seq_block(OFF_BG // w),
                  pl.BlockSpec((1, 2, w), lambda b: (b, 0, 0)),
                  full((CONV_W, w)), full((1, w)), full((w, 4 * w)), full((1, 4 * w)), full((2, w))],
        out_specs=[pl.BlockSpec((seq_len, w), lambda b: (b, 0)),
                   pl.BlockSpec((1, 2, w), lambda b: (b, 0, 0))],
        out_shape=[jax.ShapeDtypeStruct((n_seq * seq_len, w), F32),
                   jax.ShapeDtypeStruct((n_seq, 2, w), F32)],
        scratch_shapes=[pltpu.VMEM((seq_len + 16, w), F32)] + [pltpu.VMEM((seq_len, w), F32)] * 5,
        compiler_params=_cparams(("arbitrary",)),
        name="lru_%d" % seq_len,
    )(proj, proj, h0, cw, cb, wcat, bcat, lam)


def _diff_lambda(ld_ref, lam_init):
    ld = ld_ref[...]
    l1 = jnp.sum(ld[0:1] * ld[1:2], axis=-1, keepdims=True)
    l2 = jnp.sum(ld[2:3] * ld[3:4], axis=-1, keepdims=True)
    return jnp.exp(l1) - jnp.exp(l2) + lam_init


def _diff_attn_body(q, keys, values, lam, s64, g, lam_init):
    tq = q.shape[0]
    lane = lax.broadcasted_iota(jnp.int32, (1, GROUP_WIDTH), 1)
    qs = q * (DIFF_QK_DIM ** -0.5 * LOG2_E)
    o = jnp.zeros((tq, GROUP_WIDTH), F32)
    for h0 in range(0, N_GROUP_HEADS, DIFF_HEADS_PER_PASS):
        segs = range(h0 * 2, (h0 + DIFF_HEADS_PER_PASS) * 2)
        qn = jnp.concatenate([jnp.where(lane // DIFF_QK_DIM == seg, qs, 0.0).astype(BF16) for seg in segs], axis=0)
        s = [_dot_nt(qn, k) for k in keys]
        m = functools.reduce(jnp.maximum, [jnp.max(x, axis=-1, keepdims=True) for x in s])
        e = [jnp.exp2(x - m) for x in s]
        den = functools.reduce(jnp.add, [jnp.sum(x, axis=-1, keepdims=True) for x in e])
        att = None
        for x, v in zip(e, values):
            part = _dot(x.astype(BF16), v)
            att = part if att is None else att + part
        att = att * (1.0 / den)
        for j in range(DIFF_HEADS_PER_PASS):
            oh = att[2 * j * tq:(2 * j + 1) * tq] - lam * att[(2 * j + 1) * tq:(2 * j + 2) * tq]
            o = o + jnp.where(lane // HEAD_DIM == h0 + j, oh, 0.0)
    return _seg_rms_norm(o, s64, g) * (1.0 - lam_init)


def _diff_prompt_kernel(q_ref, k_ref, v_ref, ld_ref, s64_ref, g_ref, o_ref, *, lam_init):
    lam = _diff_lambda(ld_ref, lam_init)
    o_ref[...] = _diff_attn_body(q_ref[...], [k_ref[...].astype(BF16)], [v_ref[...].astype(BF16)],
                                 lam, s64_ref[...], g_ref[...], lam_init)


def _diff_sample_kernel(q_ref, k_ref, v_ref, ck_ref, cv_ref, ld_ref, s64_ref, g_ref, o_ref, *, lam_init):
    lam = _diff_lambda(ld_ref, lam_init)
    keys = [ck_ref[0, 0].astype(BF16), k_ref[...].astype(BF16)]
    values = [cv_ref[0, 0].astype(BF16), v_ref[...].astype(BF16)]
    o_ref[...] = _diff_attn_body(q_ref[...], keys, values, lam, s64_ref[...], g_ref[...], lam_init)


def _diff_prompt_call(proj, ld, s64, g, lam_init):
    w = GROUP_WIDTH
    full = lambda shape: pl.BlockSpec(shape, lambda b: (0,) * len(shape))
    return pl.pallas_call(
        functools.partial(_diff_prompt_kernel, lam_init=lam_init),
        grid=(BATCH,),
        in_specs=[pl.BlockSpec((SEQ, w), lambda b: (b, OFF_CQ // w)),
                  pl.BlockSpec((SEQ, w), lambda b: (b, OFF_CK // w)),
                  pl.BlockSpec((SEQ, w), lambda b: (b, OFF_CV // w)),
                  full((4, DIFF_QK_DIM)), full((w, w)), full((1, w))],
        out_specs=pl.BlockSpec((SEQ, w), lambda b: (b, 0)),
        out_shape=jax.ShapeDtypeStruct((N_PROMPT, w), F32),
        compiler_params=_cparams(("arbitrary",)),
        name="diff_prompt",
    )(proj, proj, proj, ld, s64, g)


def _diff_sample_call(layer, proj, ctx_k, ctx_v, ld, s64, g, lam_init):
    w = GROUP_WIDTH
    tq = 256
    nq = DEC_SEQ // tq
    full = lambda shape: pl.BlockSpec(shape, lambda b, i: (0,) * len(shape))
    seq_block0 = N_PROMPT // DEC_SEQ
    return pl.pallas_call(
        functools.partial(_diff_sample_kernel, lam_init=lam_init),
        grid=(DEC_BATCH, nq),
        in_specs=[pl.BlockSpec((tq, w), lambda b, i: (N_PROMPT // tq + b * nq + i, OFF_CQ // w)),
                  pl.BlockSpec((DEC_SEQ, w), lambda b, i: (seq_block0 + b, OFF_CK // w)),
                  pl.BlockSpec((DEC_SEQ, w), lambda b, i: (seq_block0 + b, OFF_CV // w)),
                  pl.BlockSpec((1, 1, PAST_LEN, w), lambda b, i: (b, layer, 0, 0)),
                  pl.BlockSpec((1, 1, PAST_LEN, w), lambda b, i: (b, layer, 0, 0)),
                  full((4, DIFF_QK_DIM)), full((w, w)), full((1, w))],
        out_specs=pl.BlockSpec((tq, w), lambda b, i: (b * nq + i, 0)),
        out_shape=jax.ShapeDtypeStruct((N_SAMPLE, w), F32),
        compiler_params=_cparams(("arbitrary", "arbitrary")),
        name="diff_sample",
    )(proj, proj, proj, ctx_k, ctx_v, ld, s64, g)


def _sink_attn_body(q, keys, values, masks, sink_ref):
    tq = q.shape[0]
    lane = lax.broadcasted_iota(jnp.int32, (1, 128), 1)
    qs = q * (HEAD_DIM ** -0.5 * LOG2_E)
    head_rows = lax.broadcasted_iota(jnp.int32, (SWA_HEADS * tq, 1), 0) // tq
    stacked, sink = [], jnp.zeros((SWA_HEADS * tq, 1), F32)
    for grp in range(SWA_GROUPS):
        for kh in range(SWA_KV_HEADS):
            stacked.append(jnp.where(lane // HEAD_DIM == kh, qs[:, grp * 128:(grp + 1) * 128], 0.0).astype(BF16))
            sink = jnp.where(head_rows == len(stacked) - 1, sink_ref[kh * SWA_GROUPS + grp] * LOG2_E, sink)
    q4 = jnp.concatenate(stacked, axis=0)
    s = []
    for k, msk in zip(keys, masks):
        x = _dot_nt(q4, k)
        s.append(x if msk is None else jnp.where(msk, x, -jnp.inf))
    m = functools.reduce(jnp.maximum, [jnp.max(x, axis=-1, keepdims=True) for x in s])
    m = jnp.maximum(m, sink)
    e = [jnp.exp2(x - m) for x in s]
    den = functools.reduce(jnp.add, [jnp.sum(x, axis=-1, keepdims=True) for x in e]) + jnp.exp2(sink - m)
    o4 = None
    for x, v in zip(e, values):
        part = _dot(x.astype(BF16), v)
        o4 = part if o4 is None else o4 + part
    o4 = o4 * (1.0 / den)
    outs = []
    for grp in range(SWA_GROUPS):
        r0 = grp * SWA_KV_HEADS * tq
        outs.append(jnp.where(lane // HEAD_DIM == 0, o4[r0:r0 + tq], o4[r0 + tq:r0 + 2 * tq]))
    return outs


def _swa_prompt_kernel(sink_ref, q_ref, k_ref, v_ref, o_ref):
    outs = _sink_attn_body(q_ref[...], [k_ref[...].astype(BF16)], [v_ref[...].astype(BF16)], [None], sink_ref)
    for grp in range(SWA_GROUPS):
        o_ref[:, grp * 128:(grp + 1) * 128] = outs[grp]


def _swa_sample_kernel(sink_ref, q_ref, kp_ref, kc_ref, kn_ref, vp_ref, vc_ref, vn_ref, ck_ref, cv_ref, o_ref):
    n = pl.program_id(1)
    nb = pl.num_programs(1)
    r = lax.broadcasted_iota(jnp.int32, (SWA_HEADS * WINDOW, WINDOW), 0) & (WINDOW - 1)
    c = lax.broadcasted_iota(jnp.int32, (SWA_HEADS * WINDOW, WINDOW), 1)
    mask_prev = c >= r + jnp.where(n > 0, 0, WINDOW)
    mask_next = c <= r - jnp.where(n < nb - 1, 0, WINDOW)
    keys = [ck_ref[0, 0].astype(BF16), kp_ref[...].astype(BF16), kc_ref[...].astype(BF16), kn_ref[...].astype(BF16)]
    values = [cv_ref[0, 0].astype(BF16), vp_ref[...].astype(BF16), vc_ref[...].astype(BF16), vn_ref[...].astype(BF16)]
    outs = _sink_attn_body(q_ref[...], keys, values, [None, mask_prev, None, mask_next], sink_ref)
    for grp in range(SWA_GROUPS):
        o_ref[:, grp * 128:(grp + 1) * 128] = outs[grp]


def _swa_prompt_call(sink, proj):
    w = GROUP_WIDTH
    return pl.pallas_call(
        _swa_prompt_kernel,
        grid=(BATCH,),
        in_specs=[pl.BlockSpec(memory_space=pltpu.SMEM),
                  pl.BlockSpec((SEQ, w), lambda b: (b, OFF_DQ // w)),
                  pl.BlockSpec((SEQ, 128), lambda b: (b, OFF_DK // 128)),
                  pl.BlockSpec((SEQ, 128), lambda b: (b, OFF_DV // 128))],
        out_specs=pl.BlockSpec((SEQ, w), lambda b: (b, 0)),
        out_shape=jax.ShapeDtypeStruct((N_PROMPT, w), F32),
        compiler_params=_cparams(("arbitrary",)),
        name="swa_prompt",
    )(sink, proj, proj, proj)


def _swa_sample_call(layer, sink, proj, ctx_k, ctx_v):
    w = GROUP_WIDTH
    tq = WINDOW
    nq = DEC_SEQ // tq
    row0 = N_PROMPT // tq

    def kv_spec(col, delta):
        def index(b, i):
            j = jnp.clip(i + delta, 0, nq - 1)
            return (row0 + b * nq + j, col)
        return pl.BlockSpec((tq, 128), index)

    ctx_spec = pl.BlockSpec((1, 1, PAST_LEN, 128), lambda b, i: (b, layer, 0, 0))
    return pl.pallas_call(
        _swa_sample_kernel,
        grid=(DEC_BATCH, nq),
        in_specs=[pl.BlockSpec(memory_space=pltpu.SMEM),
                  pl.BlockSpec((tq, w), lambda b, i: (row0 + b * nq + i, OFF_DQ // w)),
                  kv_spec(OFF_DK // 128, -1), kv_spec(OFF_DK // 128, 0), kv_spec(OFF_DK // 128, 1),
                  kv_spec(OFF_DV // 128, -1), kv_spec(OFF_DV // 128, 0), kv_spec(OFF_DV // 128, 1),
                  ctx_spec, ctx_spec],
        out_specs=pl.BlockSpec((tq, w), lambda b, i: (b * nq + i, 0)),
        out_shape=jax.ShapeDtypeStruct((N_SAMPLE, w), F32),
        compiler_params=_cparams(("arbitrary", "arbitrary")),
        name="swa_sample",
    )(sink, proj, proj, proj, proj, proj, proj, proj, ctx_k, ctx_v)


def _outproj_kernel(seq_ref, ya_ref, ybp_ref, ybs_ref, ycp_ref, ycs_ref, ydp_ref, yds_ref, xp_ref, xs_ref, mod_ref,
                    w_ref, g2_ref, rwt_ref, rbt_ref, triu_ref,
                    x1_ref, h2r_ref, topi_ref, topw_ref, rank_ref, count_ref, cnt_ref):
    del seq_ref
    w = GROUP_WIDTH
    is_prompt = pl.program_id(0) < PROMPT_TILES // OUT_TILES
    mod = mod_ref[0]
    g1 = mod[:, 2 * D_MODEL:3 * D_MODEL]
    sh2 = mod[:, 3 * D_MODEL:4 * D_MODEL]
    sc2 = mod[:, 4 * D_MODEL:5 * D_MODEL]

    @pl.when(pl.program_id(0) % (MOE_TB // (TM * OUT_TILES)) == 0)
    def _():
        cnt_ref[...] = jnp.zeros_like(cnt_ref)

    count = cnt_ref[...]
    for t in range(OUT_TILES):
        rows = slice(t * TM, (t + 1) * TM)
        yb = jnp.where(is_prompt, ybp_ref[rows, :], ybs_ref[rows, :])
        yc = jnp.where(is_prompt, ycp_ref[rows, :], ycs_ref[rows, :])
        yd = jnp.where(is_prompt, ydp_ref[rows, :], yds_ref[rows, :])
        mix = _dot(ya_ref[rows, :].astype(BF16), w_ref[0, 0:w, :])
        mix = mix + _dot(yb.astype(BF16), w_ref[0, w:2 * w, :])
        mix = mix + _dot(yc.astype(BF16), w_ref[0, 2 * w:3 * w, :])
        mix = mix + _dot(yd.astype(BF16), w_ref[0, 3 * w:4 * w, :])
        x1 = jnp.where(is_prompt, xp_ref[rows, :], xs_ref[rows, :]) + g1 * mix
        x1_ref[rows, :] = x1
        xn = x1 * lax.rsqrt(jnp.mean(x1 * x1, axis=-1, keepdims=True) + EPS) * g2_ref[...]
        h2 = xn * (1.0 + sc2) + sh2
        for j in range(ROW_PIECES):
            h2r_ref[pl.ds(t * TM * ROW_PIECES + j, TM, stride=ROW_PIECES), :] = h2[:, j * 128:(j + 1) * 128]
        h_hi, h_lo = _split_bf16(h2)
        r_hi, r_lo = _split_bf16(rwt_ref[...])
        logits = _dot_nt(r_hi, h_hi) + (_dot_nt(r_lo, h_hi) + _dot_nt(r_hi, h_lo)) + rbt_ref[...]
        expert = lax.broadcasted_iota(jnp.int32, logits.shape, 0)
        work = logits
        picks = []
        for _ in range(TOP_K):
            m = jnp.max(work, axis=0, keepdims=True)
            idx = jnp.min(jnp.where(work == m, expert, N_EXPERTS), axis=0, keepdims=True)
            picks.append((m, idx))
            work = jnp.where(expert == idx, -jnp.inf, work)
        top = picks[0][0]
        ex = [jnp.exp(m - top) for m, _ in picks]
        inv = 1.0 / functools.reduce(jnp.add, ex)
        sel = jnp.zeros(logits.shape, F32)
        for _, idx in picks:
            sel = sel + jnp.where(expert == idx, 1.0, 0.0)
        csum = _dot(sel.astype(BF16), triu_ref[...])
        before = count + csum - sel
        count = count + csum[:, TM - 1:TM]
        slot = lax.broadcasted_iota(jnp.int32, (TOPK_PAD, TM), 0)
        top_i = jnp.zeros((TOPK_PAD, TM), jnp.int32)
        top_w = jnp.zeros((TOPK_PAD, TM), F32)
        rank = jnp.zeros((TOPK_PAD, TM), jnp.int32)
        for k, (e, (_, idx)) in enumerate(zip(ex, picks)):
            rank_k = jnp.sum(jnp.where(expert == idx, before, 0.0), axis=0, keepdims=True).astype(jnp.int32)
            top_i = jnp.where(slot == k, idx, top_i)
            top_w = jnp.where(slot == k, e * inv, top_w)
            rank = jnp.where(slot == k, rank_k, rank)
        topi_ref[:, rows] = top_i
        topw_ref[:, rows] = top_w
        rank_ref[:, rows] = rank
    cnt_ref[...] = count
    count_ref[0] = count.astype(jnp.int32)


def _outproj_call(layer, seq_of_tile, ya, yb, yc, yd, x, mod3, w_out, g2, rwt, rbt, triu):
    w = GROUP_WIDTH
    rows = TM * OUT_TILES
    prompt_steps = PROMPT_TILES // OUT_TILES
    full = lambda shape: pl.BlockSpec(shape, lambda i, s: (0,) * len(shape))
    tile = lambda width: pl.BlockSpec((rows, width), lambda i, s: (i, 0))
    prompt_tile = lambda width: pl.BlockSpec((rows, width), lambda i, s: (jnp.minimum(i, prompt_steps - 1), 0))
    sample_tile = lambda width: pl.BlockSpec((rows, width), lambda i, s: (jnp.maximum(i - prompt_steps, 0), 0))
    slots = pl.BlockSpec((TOPK_PAD, rows), lambda i, s: (0, i))
    tiles_per_block = MOE_TB // rows
    return pl.pallas_call(
        _outproj_kernel,
        grid_spec=pltpu.PrefetchScalarGridSpec(
            num_scalar_prefetch=1,
            grid=(N_TILES // OUT_TILES,),
            in_specs=[tile(w), prompt_tile(w), sample_tile(w), prompt_tile(w), sample_tile(w),
                      prompt_tile(w), sample_tile(w), prompt_tile(D_MODEL), sample_tile(D_MODEL),
                      pl.BlockSpec((1, 1, 6 * D_MODEL), lambda i, s: (s[i], 0, 0)),
                      pl.BlockSpec((1, D_MODEL, D_MODEL), lambda i, s: (layer, 0, 0)), full((1, D_MODEL)),
                      full((N_EXPERTS, D_MODEL)), full((N_EXPERTS, 1)), full((TM, TM))],
            out_specs=[tile(D_MODEL), pl.BlockSpec((rows * ROW_PIECES, 128), lambda i, s: (i, 0)),
                       slots, slots, slots,
                       pl.BlockSpec((1, N_EXPERTS, 1), lambda i, s: (i // tiles_per_block, 0, 0))],
            scratch_shapes=[pltpu.VMEM((N_EXPERTS, 1), F32)],
        ),
        out_shape=[jax.ShapeDtypeStruct((N_TOK, D_MODEL), F32),
                   jax.ShapeDtypeStruct((N_TOK * ROW_PIECES, 128), F32),
                   jax.ShapeDtypeStruct((TOPK_PAD, N_TOK), jnp.int32),
                   jax.ShapeDtypeStruct((TOPK_PAD, N_TOK), F32),
                   jax.ShapeDtypeStruct((TOPK_PAD, N_TOK), jnp.int32),
                   jax.ShapeDtypeStruct((N_TOK // MOE_TB, N_EXPERTS, 1), jnp.int32)],
        compiler_params=_cparams(("arbitrary",)),
        name="outproj",
    )(seq_of_tile, ya, *yb, *yc, *yd, *x, mod3, w_out, g2, rwt, rbt, triu)


def _moe_kernel(expert_ref, block_ref, kind_ref, local_ref, dest_ref, topw_ref, gstart_ref, count_ref,
                src_ref, wgu_ref, bgu_ref, wd_ref, bd_ref, acc_ref, pair_ref, xt_even, xt_odd, y3_even, y3_odd):
    del expert_ref, block_ref
    i = pl.program_id(0)
    kind = kind_ref[i]

    def token_of(pair):
        return (pair >> MOE_PAIR_SHIFT) & (MOE_TB - 1)

    def gather(tile, xt_ref):
        row0 = local_ref[tile]
        for m in range(MOE_TILE):
            t = token_of(pair_ref[row0 + m])
            slab = src_ref[pl.ds(pl.multiple_of(t * ROW_PIECES, ROW_PIECES), ROW_PIECES), :]
            xt_ref[pl.ds(m, ROW_PIECES, stride=MOE_XT_STRIDE), :] = slab

    def scatter(tile, y3_ref):
        row0 = local_ref[tile]
        for m0 in range(0, MOE_TILE, MOE_RMW_BATCH):
            rows = range(m0, m0 + MOE_RMW_BATCH)
            pairs = [pair_ref[row0 + m] for m in rows]
            toks = [token_of(p) for p in pairs]
            vals = [acc_ref[t] + topw_ref[0, 0, p] * y3_ref[m // 8, pl.ds(m % 8, 8, stride=8), :]
                    for t, p, m in zip(toks, pairs, rows)]
            for t, v in reversed(list(zip(toks, vals))):
                acc_ref[t] = v

    def compute(xt_ref, y3_ref):
        x = jnp.concatenate([xt_ref[j * MOE_XT_STRIDE:j * MOE_XT_STRIDE + MOE_TILE, :] for j in range(ROW_PIECES)],
                            axis=1).astype(BF16)
        gu = _dot(x, wgu_ref[0, 0]) + bgu_ref[0, 0]
        gate = jnp.minimum(gu[:, :D_FF], SWIGLU_LIMIT)
        up = jnp.clip(gu[:, D_FF:], -SWIGLU_LIMIT, SWIGLU_LIMIT)
        act = (up + 1.0) * gate * jax.nn.sigmoid(SWIGLU_ALPHA * gate)
        y = _dot(act.astype(BF16), wd_ref[0, 0]) + bd_ref[0, 0]
        for j in range(ROW_PIECES):
            y3_ref[:, j * 8:(j + 1) * 8, :] = y[:, j * 128:(j + 1) * 128].reshape(MOE_TILE // 8, 8, 128)

    @pl.when(kind == MOE_KIND_FIRST)
    def _():
        def zero(c, carry):
            acc_ref[pl.ds(pl.multiple_of(c * MOE_TILE, MOE_TILE), MOE_TILE)] = jnp.zeros((MOE_TILE, 8, 128), F32)
            return carry
        lax.fori_loop(0, MOE_TB // MOE_TILE, zero, 0)

        def pad_group(e, carry):
            cnt = count_ref[0, 0, e]
            first_pad = gstart_ref[0, 0, e] + cnt
            n_pad = (-cnt) & (MOE_TILE - 1)

            def pad_row(r, c2):
                pair_ref[first_pad + r] = MOE_PAD_PAIR
                return c2
            lax.fori_loop(0, n_pad, pad_row, 0)
            return carry
        lax.fori_loop(0, N_EXPERTS, pad_group, 0)

        def place(c, carry):
            for k in range(MOE_PLACE_UNROLL):
                pair = c * MOE_PLACE_UNROLL + k
                pair_ref[dest_ref[0, 0, pair]] = pair
            return carry
        lax.fori_loop(0, MOE_TB * TOP_K // MOE_PLACE_UNROLL, place, 0)

    step = kind * 2 + (i & 1)
    for parity, (xt_cur, xt_nxt, y3_cur, y3_prv) in enumerate(((xt_even, xt_odd, y3_even, y3_odd),
                                                               (xt_odd, xt_even, y3_odd, y3_even))):
        @pl.when(step == MOE_KIND_FIRST * 2 + parity)
        def _():
            gather(i, xt_cur)
            gather(i + 1, xt_nxt)
            compute(xt_cur, y3_cur)

        @pl.when(step == MOE_KIND_MIDDLE * 2 + parity)
        def _():
            gather(i + 1, xt_nxt)
            compute(xt_cur, y3_cur)
            scatter(i - 1, y3_prv)

        @pl.when(step == MOE_KIND_LAST * 2 + parity)
        def _():
            compute(xt_cur, y3_cur)
            scatter(i - 1, y3_prv)
            scatter(i, y3_cur)


def _moe_call(layer, tile_expert, tile_block, tile_flags, tile_local, dest, topw, gstart, counts, h2_rows,
              wgu, bgu, wd, bd):
    pieces = ROW_PIECES
    once = pl.Buffered(1)
    n_pairs = MOE_TB * TOP_K

    def smem_block(width):
        return pl.BlockSpec((1, 1, width), lambda i, e, b, f, lo: (b[i], 0, 0), memory_space=pltpu.SMEM,
                            pipeline_mode=once)

    def expert_block(rows, cols):
        return pl.BlockSpec((1, 1, rows, cols), lambda i, e, b, f, lo: (layer, e[i], 0, 0))

    return pl.pallas_call(
        _moe_kernel,
        grid_spec=pltpu.PrefetchScalarGridSpec(
            num_scalar_prefetch=4,
            grid=(MOE_MAX_TILES,),
            in_specs=[smem_block(n_pairs), smem_block(n_pairs + 128), smem_block(N_EXPERTS), smem_block(N_EXPERTS),
                      pl.BlockSpec((MOE_TB * pieces, 128), lambda i, e, b, f, lo: (b[i], 0), pipeline_mode=once),
                      expert_block(D_MODEL, 2 * D_FF), expert_block(1, 2 * D_FF),
                      expert_block(D_FF, D_MODEL), expert_block(1, D_MODEL)],
            out_specs=pl.BlockSpec((MOE_TB, 8, 128), lambda i, e, b, f, lo: (b[i], 0, 0), pipeline_mode=once),
            scratch_shapes=[pltpu.SMEM((MOE_ROWS_PER_BLOCK,), jnp.int32),
                            pltpu.VMEM((pieces * MOE_XT_STRIDE, 128), F32),
                            pltpu.VMEM((pieces * MOE_XT_STRIDE, 128), F32),
                            pltpu.VMEM((MOE_TILE // 8, 8 * pieces, 128), F32),
                            pltpu.VMEM((MOE_TILE // 8, 8 * pieces, 128), F32)],
        ),
        out_shape=jax.ShapeDtypeStruct((N_TOK, 8, 128), F32),
        compiler_params=_cparams(("arbitrary",)),
        name="moe",
    )(tile_expert, tile_block, tile_flags, tile_local, dest, topw, gstart, counts, h2_rows, wgu, bgu, wd, bd)


def _moe_routing(top_i, top_w, rank, counts):
    nb = N_TOK // MOE_TB
    n_groups = nb * N_EXPERTS
    counts = counts.reshape(nb, N_EXPERTS)
    padded = ((counts + MOE_TILE - 1) // MOE_TILE) * MOE_TILE
    group_end = jnp.cumsum(padded.reshape(-1))
    group_start = (group_end - padded.reshape(-1)).reshape(nb, N_EXPERTS)
    block_row0 = group_start[:, 0]
    gstart_local = group_start - block_row0[:, None]
    experts = jnp.arange(N_EXPERTS, dtype=jnp.int32)
    pair_major = lambda a: jnp.transpose(a.reshape(TOP_K, nb, MOE_TB), (1, 2, 0)).reshape(nb, 1, MOE_TB * TOP_K)
    picked = top_i.reshape(TOP_K, nb, MOE_TB, 1) == experts
    dest = jnp.sum(jnp.where(picked, gstart_local[None, :, None, :], 0), axis=-1) + rank.reshape(TOP_K, nb, MOE_TB)
    dest = pair_major(dest).astype(jnp.int32)
    topw = jnp.concatenate([pair_major(top_w), jnp.zeros((nb, 1, 128), F32)], axis=-1)
    tile_start = jnp.arange(MOE_MAX_TILES, dtype=jnp.int32) * MOE_TILE
    tile_group = jnp.sum((group_end[None, :] <= tile_start[:, None]).astype(jnp.int32), axis=1)
    tile_group = jnp.minimum(tile_group, n_groups - 1)
    valid = tile_start < group_end[-1]
    tile_block = tile_group // N_EXPERTS
    new_block = tile_block[1:] != tile_block[:-1]
    first = jnp.concatenate([jnp.ones((1,), bool), new_block])
    last = jnp.concatenate([new_block | ~valid[1:], jnp.ones((1,), bool)])
    kinds = jnp.where(first, MOE_KIND_FIRST, jnp.where(last, MOE_KIND_LAST, MOE_KIND_MIDDLE))
    flags = jnp.where(valid, kinds, MOE_KIND_SKIP).astype(jnp.int32)
    blocks = jnp.arange(nb, dtype=jnp.int32)
    tile_row0 = jnp.sum(jnp.where(tile_block[:, None] == blocks[None, :], block_row0[None, :], 0), axis=1)
    tile_local = (tile_start - tile_row0).astype(jnp.int32)
    return (tile_group % N_EXPERTS, tile_block, flags, tile_local, dest, topw,
            gstart_local.reshape(nb, 1, N_EXPERTS).astype(jnp.int32), counts.reshape(nb, 1, N_EXPERTS))


def _residual_kernel(seq_ref, x1_ref, moe_ref, mod_ref, o_ref):
    del seq_ref
    g2 = mod_ref[0][:, 5 * D_MODEL:6 * D_MODEL]
    moe = jnp.concatenate([moe_ref[pl.ds(j, TM, stride=ROW_PIECES), :] for j in range(ROW_PIECES)], axis=1)
    o_ref[...] = x1_ref[...] + g2 * moe


def _residual_call(seq_of_tile, x1, moe_rows, mod3, first_tile, n_tiles):
    return pl.pallas_call(
        _residual_kernel,
        grid_spec=pltpu.PrefetchScalarGridSpec(
            num_scalar_prefetch=1,
            grid=(n_tiles,),
            in_specs=[pl.BlockSpec((TM, D_MODEL), lambda i, s: (i + first_tile, 0)),
                      pl.BlockSpec((TM * ROW_PIECES, 128), lambda i, s: (i + first_tile, 0)),
                      pl.BlockSpec((1, 1, 6 * D_MODEL), lambda i, s: (s[i + first_tile], 0, 0))],
            out_specs=pl.BlockSpec((TM, D_MODEL), lambda i, s: (i, 0)),
        ),
        out_shape=jax.ShapeDtypeStruct((n_tiles * TM, D_MODEL), F32),
        compiler_params=_cparams(("arbitrary",)),
        name="residual",
    )(seq_of_tile, x1, moe_rows, mod3)


def _segment_matrix(width, seg):
    idx = np.arange(width) // seg
    return jnp.asarray((idx[:, None] == idx[None, :]).astype(np.float32) / seg, BF16)


def _rope_tables(rot_dim, width):
    rows = DEC_SEQ // GRID_W
    nf = rot_dim // 4
    f32 = np.float32
    inv = (f32(1.0) / (f32(ROPE_BASE) ** (np.arange(nf, dtype=f32) / f32(nf)))).astype(f32)
    row = np.repeat(np.arange(rows, dtype=f32), GRID_W)
    col = np.tile(np.arange(GRID_W, dtype=f32), rows)
    ang = np.stack([row[:, None] * inv, col[:, None] * inv], axis=1).astype(f32)
    cos, sin = np.cos(ang).astype(f32), np.sin(ang).astype(f32)
    cos_r = np.concatenate([cos[:, 0], cos[:, 0], cos[:, 1], cos[:, 1]], axis=-1)
    sin_r = np.concatenate([-sin[:, 0], sin[:, 0], -sin[:, 1], sin[:, 1]], axis=-1)
    reps = width // rot_dim
    cos_t = np.concatenate([np.ones((TM * IN_TILES, width), f32), np.tile(cos_r, (1, reps))], axis=0)
    sin_t = np.concatenate([np.zeros((TM * IN_TILES, width), f32), np.tile(sin_r, (1, reps))], axis=0)
    return jnp.asarray(cos_t), jnp.asarray(sin_t)


def _block_diag(wb):
    nb, n = wb.shape[-3], wb.shape[-1]
    rows = wb.reshape(wb.shape[:-3] + (nb * n, n))
    tiled = jnp.tile(rows, (1,) * (wb.ndim - 2) + (nb,))
    blk = np.arange(nb * n) // n
    return jnp.where(jnp.asarray(blk[:, None] == blk[None, :]), tiled, 0.0)


_DQ_HEADS = [kh * SWA_GROUPS + g for g in range(SWA_GROUPS) for kh in range(SWA_KV_HEADS)]


def kernel(x_prompt, x_sample, c, cache_diff_k, cache_diff_v, cache_swa_k, cache_swa_v, state_lru, c_ctx, mod_w, mod_b, norm1_g, norm2_g, w_in, w_out, mlp_vnorm_g, mlp_ws, mlp_bs, lru_conv_w, lru_conv_b, lru_wa, lru_ba, lru_wx, lru_bx, lru_lambda, diff_qnorm_g, diff_knorm_g, diff_lambda, diff_subln_g, swa_qnorm_g, swa_knorm_g, swa_sink, router_w, router_b, moe_w_gu, moe_b_gu, moe_w_down, moe_b_down):
    params = dict(locals())
    consts = _constants()
    prep = _prepare(params, consts)
    x = (x_prompt.reshape(N_PROMPT, D_MODEL), x_sample.reshape(N_SAMPLE, D_MODEL))
    dk_l, dv_l, sk_l, sv_l, lru_l = [], [], [], [], []
    for l in range(DEPTH):
        st = _mixer_stage(x, params, prep, l, consts)
        x = _ffn_stage(x, st, params, prep, l, consts)
        pp = st["proj"][:N_PROMPT]
        dk_l.append(pp[:, OFF_CK:OFF_CV].reshape(BATCH, SEQ, N_GROUP_HEADS, 2, DIFF_QK_DIM))
        dv_l.append(pp[:, OFF_CV:OFF_DQ].reshape(BATCH, SEQ, N_GROUP_HEADS, HEAD_DIM))
        sk_l.append(pp[:, OFF_DK:OFF_DV].reshape(BATCH, SEQ, SWA_KV_HEADS, HEAD_DIM))
        sv_l.append(pp[:, OFF_DV:IN_WIDTH].reshape(BATCH, SEQ, SWA_KV_HEADS, HEAD_DIM))
        lru_l.append(st["st_p"])

    y_p = x[0].reshape(BATCH, SEQ, D_MODEL)
    y_s = x[1].reshape(DEC_BATCH, DEC_SEQ, D_MODEL)
    return (y_p, y_s, jnp.stack(dk_l, axis=1), jnp.stack(dv_l, axis=1), jnp.stack(sk_l, axis=1),
            jnp.stack(sv_l, axis=1), jnp.stack(lru_l, axis=1))


def _constants():
    w = GROUP_WIDTH
    tiles = np.arange(N_TILES)
    seq_np = np.where(tiles < PROMPT_TILES, 0, 1 + (tiles - PROMPT_TILES) // SAMPLE_TILES_PER_SEQ)
    seq_of_tile = [jnp.asarray(seq_np + l * N_COND, jnp.int32) for l in range(DEPTH)]
    seq_of_out_step = [jnp.asarray(seq_np[::OUT_TILES] + l * N_COND, jnp.int32) for l in range(DEPTH)]
    steps = np.arange(N_TILES // IN_TILES)
    in_prompt_steps, in_steps_per_seq = PROMPT_TILES // IN_TILES, SAMPLE_TILES_PER_SEQ // IN_TILES
    seq_of_in_step = [jnp.asarray(seq_np[::IN_TILES] + l * N_COND, jnp.int32) for l in range(DEPTH)]
    rope_of_tile = jnp.asarray(np.where(steps < in_prompt_steps, 0, 1 + (steps - in_prompt_steps) % in_steps_per_seq),
                               jnp.int32)

    s32 = _segment_matrix(w, DIFF_QK_DIM)
    s64 = _segment_matrix(w, HEAD_DIM)
    cosc, sinc = _rope_tables(DIFF_QK_DIM, w)
    cosd, sind = _rope_tables(HEAD_DIM, w)
    dq_cols = np.concatenate([np.arange(h * HEAD_DIM, (h + 1) * HEAD_DIM) for h in _DQ_HEADS])
    triu = jnp.asarray(np.triu(np.ones((TM, TM), np.float32)), BF16)
    return dict(seq_of_tile=seq_of_tile, seq_of_out_step=seq_of_out_step, seq_of_in_step=seq_of_in_step,
                rope_of_tile=rope_of_tile, triu=triu,
                s32=s32, s64=s64,
                cosc=cosc, sinc=sinc, cosd=cosd, sind=sind, dq_cols=dq_cols)


def _prepare(params, consts):
    w = GROUP_WIDTH
    p = params
    dq = consts["dq_cols"]
    heads = lambda g, reps: jnp.tile(g.reshape(DEPTH, 1, -1), (1, 1, reps))
    w_in, w_out = p["w_in"], p["w_out"]
    lru_w = jnp.stack([p["lru_wa"][:, 0], p["lru_wx"][:, 0], p["lru_wa"][:, 1], p["lru_wx"][:, 1]], axis=1)
    lru_b = jnp.stack([p["lru_ba"][:, 0], p["lru_bx"][:, 0], p["lru_ba"][:, 1], p["lru_bx"][:, 1]], axis=1)
    cond = jnp.concatenate([p["c_ctx"][None], p["c"], jnp.zeros((N_COND - 1 - DEC_BATCH, D_MODEL), F32)], axis=0)
    return dict(
        mod3=_mod_call(cond, p["mod_w"], p["mod_b"]).reshape(DEPTH * N_COND, 1, 6 * D_MODEL),
        w_in=jnp.concatenate([w_in[..., :OFF_DQ], w_in[..., OFF_DQ:OFF_DK][..., dq], w_in[..., OFF_DK:]],
                             axis=-1).astype(BF16),
        w_out=jnp.concatenate([w_out[:, :3 * w], w_out[:, 3 * w:][:, dq]], axis=1).astype(BF16),
        gqc=heads(p["diff_qnorm_g"], N_GROUP_HEADS), gkc=heads(p["diff_knorm_g"], N_GROUP_HEADS),
        gqd=heads(p["swa_qnorm_g"], SWA_HEADS), gkd=heads(p["swa_knorm_g"], SWA_KV_HEADS),
        g_sub=heads(p["diff_subln_g"], N_GROUP_HEADS), g_mlp=p["mlp_vnorm_g"].reshape(DEPTH, 1, w),
        mlp_bias=jnp.repeat(jnp.swapaxes(p["mlp_bs"], 1, 2), HEAD_DIM, axis=2),
        mlp_ws=p["mlp_ws"].astype(BF16),
        wcat=jnp.swapaxes(_block_diag(lru_w), 1, 2).reshape(DEPTH, w, 4 * w).astype(BF16),
        bcat=lru_b.reshape(DEPTH, 1, 4 * w),
        conv_b=p["lru_conv_b"].reshape(DEPTH, 1, w),
        g1=p["norm1_g"].reshape(DEPTH, 1, D_MODEL), g2=p["norm2_g"].reshape(DEPTH, 1, D_MODEL),
        router_wt=jnp.swapaxes(p["router_w"], 1, 2), router_bt=p["router_b"].reshape(DEPTH, N_EXPERTS, 1),
        moe_w_gu=p["moe_w_gu"].astype(BF16), moe_w_down=p["moe_w_down"].astype(BF16),
        moe_b_gu=p["moe_b_gu"].reshape(DEPTH, N_EXPERTS, 1, 2 * D_FF),
        moe_b_down=p["moe_b_down"].reshape(DEPTH, N_EXPERTS, 1, D_MODEL),
        ctx_diff_k=p["cache_diff_k"].reshape(DEC_BATCH, DEPTH, PAST_LEN, w),
        ctx_diff_v=p["cache_diff_v"].reshape(DEC_BATCH, DEPTH, PAST_LEN, w),
        ctx_swa_k=p["cache_swa_k"].reshape(DEC_BATCH, DEPTH, PAST_LEN, SWA_KV_HEADS * HEAD_DIM),
        ctx_swa_v=p["cache_swa_v"].reshape(DEC_BATCH, DEPTH, PAST_LEN, SWA_KV_HEADS * HEAD_DIM),
    )


def _mixer_stage(x, params, prep, l, consts):
    w = GROUP_WIDTH
    s32, s64 = consts["s32"], consts["s64"]
    lam_init = 0.8 - 0.6 * math.exp(-0.3 * l)
    proj = _inproj_call(l, consts["seq_of_in_step"][l], consts["rope_of_tile"], x, prep["mod3"], prep["g1"][l],
                        prep["w_in"], s32, s64, prep["gqc"][l], prep["gkc"][l], prep["gqd"][l], prep["gkd"][l],
                        consts["cosc"], consts["sinc"], consts["cosd"], consts["sind"])
    ya = _gmlp_call(proj, s64, prep["g_mlp"][l], prep["mlp_ws"][l], prep["mlp_bias"][l])

    lru_args = (params["lru_conv_w"][l], prep["conv_b"][l], prep["wcat"][l], prep["bcat"][l], params["lru_lambda"][l])
    yb_p, st_p = _lru_call(proj, jnp.zeros((BATCH, 2, w), F32), *lru_args, seq_len=SEQ, n_seq=BATCH, row_block0=0)
    yb_s, _ = _lru_call(proj, params["state_lru"][:, l], *lru_args, seq_len=DEC_SEQ, n_seq=DEC_BATCH,
                        row_block0=N_PROMPT // DEC_SEQ)

    ld = params["diff_lambda"][l]
    yc_p = _diff_prompt_call(proj, ld, s64, prep["g_sub"][l], lam_init)
    yc_s = _diff_sample_call(l, proj, prep["ctx_diff_k"], prep["ctx_diff_v"], ld, s64, prep["g_sub"][l], lam_init)

    sink = params["swa_sink"][l]
    yd_p = _swa_prompt_call(sink, proj)
    yd_s = _swa_sample_call(l, sink, proj, prep["ctx_swa_k"], prep["ctx_swa_v"])
    return dict(proj=proj, ya=ya, yb=(yb_p, yb_s), yc=(yc_p, yc_s), yd=(yd_p, yd_s), st_p=st_p)


def _ffn_stage(x, st, params, prep, l, consts):
    seq_of_tile = consts["seq_of_tile"][l]
    x1, h2_rows, top_i, top_w, rank, counts = _outproj_call(
        l, consts["seq_of_out_step"][l], st["ya"], st["yb"], st["yc"], st["yd"], x, prep["mod3"], prep["w_out"],
        prep["g2"][l],
        prep["router_wt"][l], prep["router_bt"][l], consts["triu"])
    routing = _moe_routing(top_i[:TOP_K], top_w[:TOP_K], rank[:TOP_K], counts)
    moe = _moe_call(l, *routing, h2_rows, prep["moe_w_gu"], prep["moe_b_gu"], prep["moe_w_down"], prep["moe_b_down"])
    moe_rows = moe.reshape(N_TOK * ROW_PIECES, 128)
    return (_residual_call(seq_of_tile, x1, moe_rows, prep["mod3"], 0, PROMPT_TILES),
            _residual_call(seq_of_tile, x1, moe_rows, prep["mod3"], PROMPT_TILES, N_TILES - PROMPT_TILES))
```

```python
import functools
import math

import jax
import jax.numpy as jnp
import numpy as np
from jax import lax
from jax.experimental import pallas as pl
from jax.experimental.pallas import tpu as pltpu

F32 = jnp.float32
BF16 = jnp.bfloat16

D_MODEL = 1024
BATCH = 16
SEQ = 256
DEPTH = 2
DEC_BATCH = 4
DEC_SEQ = 2048
PAST_LEN = 512
GRID_W = 64
HEAD_DIM = 64
GROUP_WIDTH = 256
N_GROUP_HEADS = 4
CHUNK = 128
LRU_C = 8.0
CONV_W = 4
DIFF_QK_DIM = 32
SWA_KV_HEADS = 2
SWA_GROUPS = 2
SWA_HEADS = SWA_KV_HEADS * SWA_GROUPS
WINDOW = 128
N_EXPERTS = 32
TOP_K = 4
D_FF = 1024
SWIGLU_LIMIT = 7.0
SWIGLU_ALPHA = 1.702
ROPE_BASE = 10000.0
EPS = 1e-6
LOG2_E = math.log2(math.e)

N_PROMPT = BATCH * SEQ
N_SAMPLE = DEC_BATCH * DEC_SEQ
N_TOK = N_PROMPT + N_SAMPLE
N_COND = 8
TM = 256
N_TILES = N_TOK // TM
PROMPT_TILES = N_PROMPT // TM
SAMPLE_TILES_PER_SEQ = DEC_SEQ // TM
IN_WIDTH = 2304
OFF_AU, OFF_AV, OFF_BX, OFF_BG, OFF_CQ, OFF_CK, OFF_CV, OFF_DQ, OFF_DK, OFF_DV = (
    0, 256, 512, 768, 1024, 1280, 1536, 1792, 2048, 2176)
TOPK_PAD = 8
ROW_PIECES = D_MODEL // 128
MOE_TILE = 256
MOE_TB = 4096
MOE_MAX_TILES = (N_TOK // MOE_TB) * (MOE_TB * TOP_K // MOE_TILE + N_EXPERTS)
MOE_XT_STRIDE = MOE_TILE + 8
MOE_RMW_BATCH = 4
MOE_PLACE_UNROLL = 16
MOE_ROWS_PER_BLOCK = MOE_TB * TOP_K + N_EXPERTS * MOE_TILE
MOE_PAD_PAIR = MOE_TB * TOP_K
MOE_PAIR_SHIFT = TOP_K.bit_length() - 1
assert 1 << MOE_PAIR_SHIFT == TOP_K
MOE_KIND_SKIP, MOE_KIND_FIRST, MOE_KIND_MIDDLE, MOE_KIND_LAST = 0, 1, 2, 3
OUT_TILES = 4
IN_TILES = 4
assert PROMPT_TILES % OUT_TILES == 0 and SAMPLE_TILES_PER_SEQ % OUT_TILES == 0
assert PROMPT_TILES % IN_TILES == 0 and SAMPLE_TILES_PER_SEQ % IN_TILES == 0
LRU_CHUNK = 256
DIFF_HEADS_PER_PASS = 1
VMEM_LIMIT = 56 * 1024 * 1024


def _cparams(sem):
    return pltpu.CompilerParams(dimension_semantics=sem, vmem_limit_bytes=VMEM_LIMIT)


def _dot(a, b):
    return jnp.dot(a, b, preferred_element_type=F32)


def _dot_nt(a, b):
    return lax.dot_general(a, b, (((1,), (1,)), ((), ())), preferred_element_type=F32)


def _split_bf16(x):
    hi = x.astype(BF16)
    lo = (x - hi.astype(F32)).astype(BF16)
    return hi, lo


def _seg_rms_norm(x, seg_mat, g):
    hi, lo = _split_bf16(x * x)
    ms = _dot(hi, seg_mat) + _dot(lo, seg_mat)
    return x * lax.rsqrt(ms + EPS) * g


def _rope(x, cos_t, sin_t, nf):
    n = x.shape[-1]
    lane = lax.broadcasted_iota(jnp.int32, x.shape, 1)
    first = (lane & (2 * nf - 1)) < nf
    partner = jnp.where(first, pltpu.roll(x, n - nf, axis=1), pltpu.roll(x, nf, axis=1))
    return x * cos_t + partner * sin_t


def _softplus(x):
    return jnp.maximum(x, 0.0) + jnp.log1p(jnp.exp(-jnp.abs(x)))


def _mod_kernel(cond_ref, w_ref, b_ref, o_ref):
    c = cond_ref[...]
    s = c * jax.nn.sigmoid(c)
    o_ref[0] = _dot(s.astype(BF16), w_ref[0].astype(BF16)) + b_ref[0]


def _mod_call(cond, w, b):
    nb = 6
    return pl.pallas_call(
        _mod_kernel,
        grid=(DEPTH, nb),
        in_specs=[pl.BlockSpec((N_COND, D_MODEL), lambda l, j: (0, 0)),
                  pl.BlockSpec((1, D_MODEL, D_MODEL), lambda l, j: (l, 0, j)),
                  pl.BlockSpec((1, 1, D_MODEL), lambda l, j: (l, 0, j))],
        out_specs=pl.BlockSpec((1, N_COND, D_MODEL), lambda l, j: (l, 0, j)),
        out_shape=jax.ShapeDtypeStruct((DEPTH, N_COND, 6 * D_MODEL), F32),
        compiler_params=_cparams(("arbitrary", "arbitrary")),
        name="mod",
    )(cond, w, b.reshape(DEPTH, 1, 6 * D_MODEL))


def _inproj_kernel(seq_ref, rope_ref, xp_ref, xs_ref, mod_ref, g1_ref, w_ref, s32_ref, s64_ref,
                   gqc_ref, gkc_ref, gqd_ref, gkd_ref, cosc_ref, sinc_ref, cosd_ref, sind_ref, o_ref):
    del seq_ref, rope_ref
    x = jnp.where(pl.program_id(0) < PROMPT_TILES // IN_TILES, xp_ref[...], xs_ref[...])
    xn = x * lax.rsqrt(jnp.mean(x * x, axis=-1, keepdims=True) + EPS) * g1_ref[...]
    mod = mod_ref[0]
    sh1 = mod[:, 0:D_MODEL]
    sc1 = mod[:, D_MODEL:2 * D_MODEL]
    h = xn * (1.0 + sc1) + sh1
    p = _dot(h.astype(BF16), w_ref[0])
    o_ref[:, OFF_AU:OFF_BX] = jax.nn.gelu(p[:, OFF_AU:OFF_BX])
    o_ref[:, OFF_BX:OFF_BG] = p[:, OFF_BX:OFF_BG]
    o_ref[:, OFF_BG:OFF_CQ] = jax.nn.gelu(p[:, OFF_BG:OFF_CQ])
    s32 = s32_ref[...]
    cosc = cosc_ref[...]
    sinc = sinc_ref[...]
    cq = _seg_rms_norm(p[:, OFF_CQ:OFF_CK], s32, gqc_ref[...])
    ck = _seg_rms_norm(p[:, OFF_CK:OFF_CV], s32, gkc_ref[...])
    o_ref[:, OFF_CQ:OFF_CK] = _rope(cq, cosc, sinc, DIFF_QK_DIM // 4)
    o_ref[:, OFF_CK:OFF_CV] = _rope(ck, cosc, sinc, DIFF_QK_DIM // 4)
    o_ref[:, OFF_CV:OFF_DQ] = p[:, OFF_CV:OFF_DQ]
    s64 = s64_ref[...]
    cosd = cosd_ref[...]
    sind = sind_ref[...]
    dq = _seg_rms_norm(p[:, OFF_DQ:OFF_DK], s64, gqd_ref[...])
    dk = _seg_rms_norm(p[:, OFF_DK:OFF_DV], s64[0:128, 0:128], gkd_ref[...])
    o_ref[:, OFF_DQ:OFF_DK] = _rope(dq, cosd, sind, HEAD_DIM // 4)
    o_ref[:, OFF_DK:OFF_DV] = _rope(dk, cosd[:, 0:128], sind[:, 0:128], HEAD_DIM // 4)
    o_ref[:, OFF_DV:IN_WIDTH] = p[:, OFF_DV:IN_WIDTH]


def _inproj_call(layer, seq_of_tile, rope_of_tile, x, mod3, g1, w_in, s32, s64, gqc, gkc, gqd, gkd,
                 cosc, sinc, cosd, sind):
    full = lambda shape: pl.BlockSpec(shape, lambda i, s, r: (0,) * len(shape))
    rows = TM * IN_TILES
    prompt_steps = PROMPT_TILES // IN_TILES
    rope_spec = pl.BlockSpec((rows, GROUP_WIDTH), lambda i, s, r: (r[i], 0))
    grid_spec = pltpu.PrefetchScalarGridSpec(
        num_scalar_prefetch=2,
        grid=(N_TILES // IN_TILES,),
        in_specs=[pl.BlockSpec((rows, D_MODEL), lambda i, s, r: (jnp.minimum(i, prompt_steps - 1), 0)),
                  pl.BlockSpec((rows, D_MODEL), lambda i, s, r: (jnp.maximum(i - prompt_steps, 0), 0)),
                  pl.BlockSpec((1, 1, 6 * D_MODEL), lambda i, s, r: (s[i], 0, 0)),
                  full((1, D_MODEL)),
                  pl.BlockSpec((1, D_MODEL, IN_WIDTH), lambda i, s, r: (layer, 0, 0)),
                  full((GROUP_WIDTH, GROUP_WIDTH)),
                  full((GROUP_WIDTH, GROUP_WIDTH)),
                  full((1, GROUP_WIDTH)), full((1, GROUP_WIDTH)), full((1, GROUP_WIDTH)), full((1, 128)),
                  rope_spec, rope_spec, rope_spec, rope_spec],
        out_specs=pl.BlockSpec((rows, IN_WIDTH), lambda i, s, r: (i, 0)),
    )
    return pl.pallas_call(
        _inproj_kernel,
        grid_spec=grid_spec,
        out_shape=jax.ShapeDtypeStruct((N_TOK, IN_WIDTH), F32),
        compiler_params=_cparams(("arbitrary",)),
        name="inproj",
    )(seq_of_tile, rope_of_tile, *x, mod3, g1, w_in, s32, s64, gqc, gkc, gqd, gkd, cosc, sinc, cosd, sind)


def _gmlp_kernel(u_ref, v_ref, s64_ref, g_ref, ws_ref, bias_ref, o_ref):
    vh = _seg_rms_norm(v_ref[...], s64_ref[...], g_ref[...]).astype(BF16)
    head = lax.broadcasted_iota(jnp.int32, (CHUNK, GROUP_WIDTH), 1) // HEAD_DIM
    for ch in range(TM // CHUNK):
        rows = slice(ch * CHUNK, (ch + 1) * CHUNK)
        vc = vh[rows]
        mixed = bias_ref[...]
        for h in range(N_GROUP_HEADS):
            mixed = mixed + jnp.where(head == h, _dot(ws_ref[h], vc), 0.0)
        o_ref[rows, :] = u_ref[rows, :] * mixed


def _gmlp_call(proj, s64, g, ws, bias):
    return pl.pallas_call(
        _gmlp_kernel,
        grid=(N_TILES,),
        in_specs=[pl.BlockSpec((TM, GROUP_WIDTH), lambda i: (i, OFF_AU // GROUP_WIDTH)),
                  pl.BlockSpec((TM, GROUP_WIDTH), lambda i: (i, OFF_AV // GROUP_WIDTH)),
                  pl.BlockSpec((GROUP_WIDTH, GROUP_WIDTH), lambda i: (0, 0)),
                  pl.BlockSpec((1, GROUP_WIDTH), lambda i: (0, 0)),
                  pl.BlockSpec((N_GROUP_HEADS, CHUNK, CHUNK), lambda i: (0, 0, 0)),
                  pl.BlockSpec((CHUNK, GROUP_WIDTH), lambda i: (0, 0))],
        out_specs=pl.BlockSpec((TM, GROUP_WIDTH), lambda i: (i, 0)),
        out_shape=jax.ShapeDtypeStruct((N_TOK, GROUP_WIDTH), F32),
        compiler_params=_cparams(("arbitrary",)),
        name="gmlp",
    )(proj, proj, s64, g, ws, bias)


def _scan_chunk(a, b, reverse):
    n = a.shape[0]
    row = lax.broadcasted_iota(jnp.int32, a.shape, 0)
    s = 1
    while s < n:
        if reverse:
            keep = row < n - s
            shift = n - s
        else:
            keep = row >= s
            shift = s
        a_prev = jnp.where(keep, pltpu.roll(a, shift, axis=0), 1.0)
        b_prev = jnp.where(keep, pltpu.roll(b, shift, axis=0), 0.0)
        b = a * b_prev + b
        a = a * a_prev
        s *= 2
    return a, b


def _lru_kernel(x_ref, g_ref, h0_ref, cw_ref, cb_ref, wcat_ref, bcat_ref, lam_ref, y_ref, st_ref,
                xpad, a_f, b_f, a_b, b_b, h_f, *, seq_len):
    nc = seq_len // LRU_CHUNK
    w = GROUP_WIDTH
    zeros8 = jnp.zeros((8, w), F32)
    xpad[0:8, :] = zeros8
    xpad[seq_len + 8:seq_len + 16, :] = zeros8
    xpad[8:seq_len + 8, :] = x_ref[...]
    sp = _softplus(-lam_ref[...])
    cw = cw_ref[...]
    cb = cb_ref[...]
    win_rows = LRU_CHUNK + 16

    def gates(c, carry):
        r0 = pl.multiple_of(c * LRU_CHUNK, LRU_CHUNK)
        win = xpad[pl.ds(r0, win_rows), :]
        inner = slice(8, 8 + LRU_CHUNK)
        xc = cb + pltpu.roll(win, 2, axis=0)[inner] * cw[0:1]
        xc = xc + pltpu.roll(win, 1, axis=0)[inner] * cw[1:2]
        xc = xc + win[inner] * cw[2:3]
        xc = xc + pltpu.roll(win, win_rows - 1, axis=0)[inner] * cw[3:4]
        sg = jax.nn.sigmoid(_dot(xc.astype(BF16), wcat_ref[...]) + bcat_ref[...])
        for d, (a_ref, b_ref) in enumerate(((a_f, b_f), (a_b, b_b))):
            r = sg[:, (2 * d) * w:(2 * d + 1) * w]
            i = sg[:, (2 * d + 1) * w:(2 * d + 2) * w]
            log_a = (-LRU_C * r) * sp[d:d + 1]
            a = jnp.exp(log_a)
            a_ref[pl.ds(r0, LRU_CHUNK), :] = a
            b_ref[pl.ds(r0, LRU_CHUNK), :] = jnp.sqrt(-jnp.tanh(log_a) * (a * a + 1.0)) * (i * xc)
        return carry

    lax.fori_loop(0, nc, gates, 0)

    def fwd(c, carry):
        r0 = pl.multiple_of(c * LRU_CHUNK, LRU_CHUNK)
        a_cum, h_loc = _scan_chunk(a_f[pl.ds(r0, LRU_CHUNK), :], b_f[pl.ds(r0, LRU_CHUNK), :], False)
        h = h_loc + a_cum * carry
        h_f[pl.ds(r0, LRU_CHUNK), :] = h
        return h[LRU_CHUNK - 1:LRU_CHUNK, :]

    s_f = lax.fori_loop(0, nc, fwd, h0_ref[0, 0:1, :])

    def bwd(k, carry):
        c = nc - 1 - k
        r0 = pl.multiple_of(c * LRU_CHUNK, LRU_CHUNK)
        a_cum, h_loc = _scan_chunk(a_b[pl.ds(r0, LRU_CHUNK), :], b_b[pl.ds(r0, LRU_CHUNK), :], True)
        h = h_loc + a_cum * carry
        y_ref[pl.ds(r0, LRU_CHUNK), :] = (h_f[pl.ds(r0, LRU_CHUNK), :] + h) * g_ref[pl.ds(r0, LRU_CHUNK), :]
        return h[0:1, :]

    s_b = lax.fori_loop(0, nc, bwd, h0_ref[0, 1:2, :])
    st_ref[0, 0:1, :] = s_f
    st_ref[0, 1:2, :] = s_b


def _lru_call(proj, h0, cw, cb, wcat, bcat, lam, *, seq_len, n_seq, row_block0):
    w = GROUP_WIDTH
    full = lambda shape: pl.BlockSpec(shape, lambda b: (0,) * len(shape))
    seq_block = lambda col: pl.BlockSpec((seq_len, w), lambda b: (b + row_block0, col))
    return pl.pallas_call(
        functools.partial(_lru_kernel, seq_len=seq_len),
        grid=(n_seq,),
        in_specs=[seq_block(OFF_BX // w), seq_block(OFF_BG // w),
                  pl.BlockSpec((1, 2, w), lambda b: (b, 0, 0)),
                  full((CONV_W, w)), full((1, w)), full((w, 4 * w)), full((1, 4 * w)), full((2, w))],
        out_specs=[pl.BlockSpec((seq_len, w), lambda b: (b, 0)),
                   pl.BlockSpec((1, 2, w), lambda b: (b, 0, 0))],
        out_shape=[jax.ShapeDtypeStruct((n_seq * seq_len, w), F32),
                   jax.ShapeDtypeStruct((n_seq, 2, w), F32)],
        scratch_shapes=[pltpu.VMEM((seq_len + 16, w), F32)] + [pltpu.VMEM((seq_len, w), F32)] * 5,
        compiler_params=_cparams(("arbitrary",)),
        name="lru_%d" % seq_len,
    )(proj, proj, h0, cw, cb, wcat, bcat, lam)


def _diff_lambda(ld_ref, lam_init):
    ld = ld_ref[...]
    l1 = jnp.sum(ld[0:1] * ld[1:2], axis=-1, keepdims=True)
    l2 = jnp.sum(ld[2:3] * ld[3:4], axis=-1, keepdims=True)
    return jnp.exp(l1) - jnp.exp(l2) + lam_init


def _diff_attn_body(q, keys, values, lam, s64, g, lam_init):
    tq = q.shape[0]
    lane = lax.broadcasted_iota(jnp.int32, (1, GROUP_WIDTH), 1)
    qs = q * (DIFF_QK_DIM ** -0.5 * LOG2_E)
    o = jnp.zeros((tq, GROUP_WIDTH), F32)
    for h0 in range(0, N_GROUP_HEADS, DIFF_HEADS_PER_PASS):
        segs = range(h0 * 2, (h0 + DIFF_HEADS_PER_PASS) * 2)
        qn = jnp.concatenate([jnp.where(lane // DIFF_QK_DIM == seg, qs, 0.0).astype(BF16) for seg in segs], axis=0)
        s = [_dot_nt(qn, k) for k in keys]
        m = functools.reduce(jnp.maximum, [jnp.max(x, axis=-1, keepdims=True) for x in s])
        e = [jnp.exp2(x - m) for x in s]
        den = functools.reduce(jnp.add, [jnp.sum(x, axis=-1, keepdims=True) for x in e])
        att = None
        for x, v in zip(e, values):
            part = _dot(x.astype(BF16), v)
            att = part if att is None else att + part
        att = att * (1.0 / den)
        for j in range(DIFF_HEADS_PER_PASS):
            oh = att[2 * j * tq:(2 * j + 1) * tq] - lam * att[(2 * j + 1) * tq:(2 * j + 2) * tq]
            o = o + jnp.where(lane // HEAD_DIM == h0 + j, oh, 0.0)
    return _seg_rms_norm(o, s64, g) * (1.0 - lam_init)


def _diff_prompt_kernel(q_ref, k_ref, v_ref, ld_ref, s64_ref, g_ref, o_ref, *, lam_init):
    lam = _diff_lambda(ld_ref, lam_init)
    o_ref[...] = _diff_attn_body(q_ref[...], [k_ref[...].astype(BF16)], [v_ref[...].astype(BF16)],
                                 lam, s64_ref[...], g_ref[...], lam_init)


def _diff_sample_kernel(q_ref, k_ref, v_ref, ck_ref, cv_ref, ld_ref, s64_ref, g_ref, o_ref, *, lam_init):
    lam = _diff_lambda(ld_ref, lam_init)
    keys = [ck_ref[0, 0].astype(BF16), k_ref[...].astype(BF16)]
    values = [cv_ref[0, 0].astype(BF16), v_ref[...].astype(BF16)]
    o_ref[...] = _diff_attn_body(q_ref[...], keys, values, lam, s64_ref[...], g_ref[...], lam_init)


def _diff_prompt_call(proj, ld, s64, g, lam_init):
    w = GROUP_WIDTH
    full = lambda shape: pl.BlockSpec(shape, lambda b: (0,) * len(shape))
    return pl.pallas_call(
        functools.partial(_diff_prompt_kernel, lam_init=lam_init),
        grid=(BATCH,),
        in_specs=[pl.BlockSpec((SEQ, w), lambda b: (b, OFF_CQ // w)),
                  pl.BlockSpec((SEQ, w), lambda b: (b, OFF_CK // w)),
                  pl.BlockSpec((SEQ, w), lambda b: (b, OFF_CV // w)),
                  full((4, DIFF_QK_DIM)), full((w, w)), full((1, w))],
        out_specs=pl.BlockSpec((SEQ, w), lambda b: (b, 0)),
        out_shape=jax.ShapeDtypeStruct((N_PROMPT, w), F32),
        compiler_params=_cparams(("arbitrary",)),
        name="diff_prompt",
    )(proj, proj, proj, ld, s64, g)


def _diff_sample_call(layer, proj, ctx_k, ctx_v, ld, s64, g, lam_init):
    w = GROUP_WIDTH
    tq = 256
    nq = DEC_SEQ // tq
    full = lambda shape: pl.BlockSpec(shape, lambda b, i: (0,) * len(shape))
    seq_block0 = N_PROMPT // DEC_SEQ
    return pl.pallas_call(
        functools.partial(_diff_sample_kernel, lam_init=lam_init),
        grid=(DEC_BATCH, nq),
        in_specs=[pl.BlockSpec((tq, w), lambda b, i: (N_PROMPT // tq + b * nq + i, OFF_CQ // w)),
                  pl.BlockSpec((DEC_SEQ, w), lambda b, i: (seq_block0 + b, OFF_CK // w)),
                  pl.BlockSpec((DEC_SEQ, w), lambda b, i: (seq_block0 + b, OFF_CV // w)),
                  pl.BlockSpec((1, 1, PAST_LEN, w), lambda b, i: (b, layer, 0, 0)),
                  pl.BlockSpec((1, 1, PAST_LEN, w), lambda b, i: (b, layer, 0, 0)),
                  full((4, DIFF_QK_DIM)), full((w, w)), full((1, w))],
        out_specs=pl.BlockSpec((tq, w), lambda b, i: (b * nq + i, 0)),
        out_shape=jax.ShapeDtypeStruct((N_SAMPLE, w), F32),
        compiler_params=_cparams(("arbitrary", "arbitrary")),
        name="diff_sample",
    )(proj, proj, proj, ctx_k, ctx_v, ld, s64, g)


def _sink_attn_body(q, keys, values, masks, sink_ref):
    tq = q.shape[0]
    lane = lax.broadcasted_iota(jnp.int32, (1, 128), 1)
    qs = q * (HEAD_DIM ** -0.5 * LOG2_E)
    head_rows = lax.broadcasted_iota(jnp.int32, (SWA_HEADS * tq, 1), 0) // tq
    stacked, sink = [], jnp.zeros((SWA_HEADS * tq, 1), F32)
    for grp in range(SWA_GROUPS):
        for kh in range(SWA_KV_HEADS):
            stacked.append(jnp.where(lane // HEAD_DIM == kh, qs[:, grp * 128:(grp + 1) * 128], 0.0).astype(BF16))
            sink = jnp.where(head_rows == len(stacked) - 1, sink_ref[kh * SWA_GROUPS + grp] * LOG2_E, sink)
    q4 = jnp.concatenate(stacked, axis=0)
    s = []
    for k, msk in zip(keys, masks):
        x = _dot_nt(q4, k)
        s.append(x if msk is None else jnp.where(msk, x, -jnp.inf))
    m = functools.reduce(jnp.maximum, [jnp.max(x, axis=-1, keepdims=True) for x in s])
    m = jnp.maximum(m, sink)
    e = [jnp.exp2(x - m) for x in s]
    den = functools.reduce(jnp.add, [jnp.sum(x, axis=-1, keepdims=True) for x in e]) + jnp.exp2(sink - m)
    o4 = None
    for x, v in zip(e, values):
        part = _dot(x.astype(BF16), v)
        o4 = part if o4 is None else o4 + part
    o4 = o4 * (1.0 / den)
    outs = []
    for grp in range(SWA_GROUPS):
        r0 = grp * SWA_KV_HEADS * tq
        outs.append(jnp.where(lane // HEAD_DIM == 0, o4[r0:r0 + tq], o4[r0 + tq:r0 + 2 * tq]))
    return outs


def _swa_prompt_kernel(sink_ref, q_ref, k_ref, v_ref, o_ref):
    outs = _sink_attn_body(q_ref[...], [k_ref[...].astype(BF16)], [v_ref[...].astype(BF16)], [None], sink_ref)
    for grp in range(SWA_GROUPS):
        o_ref[:, grp * 128:(grp + 1) * 128] = outs[grp]


def _swa_sample_kernel(sink_ref, q_ref, kp_ref, kc_ref, kn_ref, vp_ref, vc_ref, vn_ref, ck_ref, cv_ref, o_ref):
    n = pl.program_id(1)
    nb = pl.num_programs(1)
    r = lax.broadcasted_iota(jnp.int32, (SWA_HEADS * WINDOW, WINDOW), 0) & (WINDOW - 1)
    c = lax.broadcasted_iota(jnp.int32, (SWA_HEADS * WINDOW, WINDOW), 1)
    mask_prev = c >= r + jnp.where(n > 0, 0, WINDOW)
    mask_next = c <= r - jnp.where(n < nb - 1, 0, WINDOW)
    keys = [ck_ref[0, 0].astype(BF16), kp_ref[...].astype(BF16), kc_ref[...].astype(BF16), kn_ref[...].astype(BF16)]
    values = [cv_ref[0, 0].astype(BF16), vp_ref[...].astype(BF16), vc_ref[...].astype(BF16), vn_ref[...].astype(BF16)]
    outs = _sink_attn_body(q_ref[...], keys, values, [None, mask_prev, None, mask_next], sink_ref)
    for grp in range(SWA_GROUPS):
        o_ref[:, grp * 128:(grp + 1) * 128] = outs[grp]


def _swa_prompt_call(sink, proj):
    w = GROUP_WIDTH
    return pl.pallas_call(
        _swa_prompt_kernel,
        grid=(BATCH,),
        in_specs=[pl.BlockSpec(memory_space=pltpu.SMEM),
                  pl.BlockSpec((SEQ, w), lambda b: (b, OFF_DQ // w)),
                  pl.BlockSpec((SEQ, 128), lambda b: (b, OFF_DK // 128)),
                  pl.BlockSpec((SEQ, 128), lambda b: (b, OFF_DV // 128))],
        out_specs=pl.BlockSpec((SEQ, w), lambda b: (b, 0)),
        out_shape=jax.ShapeDtypeStruct((N_PROMPT, w), F32),
        compiler_params=_cparams(("arbitrary",)),
        name="swa_prompt",
    )(sink, proj, proj, proj)


def _swa_sample_call(layer, sink, proj, ctx_k, ctx_v):
    w = GROUP_WIDTH
    tq = WINDOW
    nq = DEC_SEQ // tq
    row0 = N_PROMPT // tq

    def kv_spec(col, delta):
        def index(b, i):
            j = jnp.clip(i + delta, 0, nq - 1)
            return (row0 + b * nq + j, col)
        return pl.BlockSpec((tq, 128), index)

    ctx_spec = pl.BlockSpec((1, 1, PAST_LEN, 128), lambda b, i: (b, layer, 0, 0))
    return pl.pallas_call(
        _swa_sample_kernel,
        grid=(DEC_BATCH, nq),
        in_specs=[pl.BlockSpec(memory_space=pltpu.SMEM),
                  pl.BlockSpec((tq, w), lambda b, i: (row0 + b * nq + i, OFF_DQ // w)),
                  kv_spec(OFF_DK // 128, -1), kv_spec(OFF_DK // 128, 0), kv_spec(OFF_DK // 128, 1),
                  kv_spec(OFF_DV // 128, -1), kv_spec(OFF_DV // 128, 0), kv_spec(OFF_DV // 128, 1),
                  ctx_spec, ctx_spec],
        out_specs=pl.BlockSpec((tq, w), lambda b, i: (b * nq + i, 0)),
        out_shape=jax.ShapeDtypeStruct((N_SAMPLE, w), F32),
        compiler_params=_cparams(("arbitrary", "arbitrary")),
        name="swa_sample",
    )(sink, proj, proj, proj, proj, proj, proj, proj, ctx_k, ctx_v)


def _outproj_kernel(seq_ref, ya_ref, ybp_ref, ybs_ref, ycp_ref, ycs_ref, ydp_ref, yds_ref, xp_ref, xs_ref, mod_ref,
                    w_ref, g2_ref, rwt_ref, rbt_ref, triu_ref,
                    x1_ref, h2r_ref, topi_ref, topw_ref, rank_ref, count_ref, cnt_ref):
    del seq_ref
    w = GROUP_WIDTH
    is_prompt = pl.program_id(0) < PROMPT_TILES // OUT_TILES
    mod = mod_ref[0]
    g1 = mod[:, 2 * D_MODEL:3 * D_MODEL]
    sh2 = mod[:, 3 * D_MODEL:4 * D_MODEL]
    sc2 = mod[:, 4 * D_MODEL:5 * D_MODEL]

    @pl.when(pl.program_id(0) % (MOE_TB // (TM * OUT_TILES)) == 0)
    def _():
        cnt_ref[...] = jnp.zeros_like(cnt_ref)

    count = cnt_ref[...]
    for t in range(OUT_TILES):
        rows = slice(t * TM, (t + 1) * TM)
        yb = jnp.where(is_prompt, ybp_ref[rows, :], ybs_ref[rows, :])
        yc = jnp.where(is_prompt, ycp_ref[rows, :], ycs_ref[rows, :])
        yd = jnp.where(is_prompt, ydp_ref[rows, :], yds_ref[rows, :])
        mix = _dot(ya_ref[rows, :].astype(BF16), w_ref[0, 0:w, :])
        mix = mix + _dot(yb.astype(BF16), w_ref[0, w:2 * w, :])
        mix = mix + _dot(yc.astype(BF16), w_ref[0, 2 * w:3 * w, :])
        mix = mix + _dot(yd.astype(BF16), w_ref[0, 3 * w:4 * w, :])
        x1 = jnp.where(is_prompt, xp_ref[rows, :], xs_ref[rows, :]) + g1 * mix
        x1_ref[rows, :] = x1
        xn = x1 * lax.rsqrt(jnp.mean(x1 * x1, axis=-1, keepdims=True) + EPS) * g2_ref[...]
        h2 = xn * (1.0 + sc2) + sh2
        for j in range(ROW_PIECES):
            h2r_ref[pl.ds(t * TM * ROW_PIECES + j, TM, stride=ROW_PIECES), :] = h2[:, j * 128:(j + 1) * 128]
        h_hi, h_lo = _split_bf16(h2)
        r_hi, r_lo = _split_bf16(rwt_ref[...])
        logits = _dot_nt(r_hi, h_hi) + (_dot_nt(r_lo, h_hi) + _dot_nt(r_hi, h_lo)) + rbt_ref[...]
        expert = lax.broadcasted_iota(jnp.int32, logits.shape, 0)
        work = logits
        picks = []
        for _ in range(TOP_K):
            m = jnp.max(work, axis=0, keepdims=True)
            idx = jnp.min(jnp.where(work == m, expert, N_EXPERTS), axis=0, keepdims=True)
            picks.append((m, idx))
            work = jnp.where(expert == idx, -jnp.inf, work)
        top = picks[0][0]
        ex = [jnp.exp(m - top) for m, _ in picks]
        inv = 1.0 / functools.reduce(jnp.add, ex)
        sel = jnp.zeros(logits.shape, F32)
        for _, idx in picks:
            sel = sel + jnp.where(expert == idx, 1.0, 0.0)
        csum = _dot(sel.astype(BF16), triu_ref[...])
        before = count + csum - sel
        count = count + csum[:, TM - 1:TM]
        slot = lax.broadcasted_iota(jnp.int32, (TOPK_PAD, TM), 0)
        top_i = jnp.zeros((TOPK_PAD, TM), jnp.int32)
        top_w = jnp.zeros((TOPK_PAD, TM), F32)
        rank = jnp.zeros((TOPK_PAD, TM), jnp.int32)
        for k, (e, (_, idx)) in enumerate(zip(ex, picks)):
            rank_k = jnp.sum(jnp.where(expert == idx, before, 0.0), axis=0, keepdims=True).astype(jnp.int32)
            top_i = jnp.where(slot == k, idx, top_i)
            top_w = jnp.where(slot == k, e * inv, top_w)
            rank = jnp.where(slot == k, rank_k, rank)
        topi_ref[:, rows] = top_i
        topw_ref[:, rows] = top_w
        rank_ref[:, rows] = rank
    cnt_ref[...] = count
    count_ref[0] = count.astype(jnp.int32)


def _outproj_call(layer, seq_of_tile, ya, yb, yc, yd, x, mod3, w_out, g2, rwt, rbt, triu):
    w = GROUP_WIDTH
    rows = TM * OUT_TILES
    prompt_steps = PROMPT_TILES // OUT_TILES
    full = lambda shape: pl.BlockSpec(shape, lambda i, s: (0,) * len(shape))
    tile = lambda width: pl.BlockSpec((rows, width), lambda i, s: (i, 0))
    prompt_tile = lambda width: pl.BlockSpec((rows, width), lambda i, s: (jnp.minimum(i, prompt_steps - 1), 0))
    sample_tile = lambda width: pl.BlockSpec((rows, width), lambda i, s: (jnp.maximum(i - prompt_steps, 0), 0))
    slots = pl.BlockSpec((TOPK_PAD, rows), lambda i, s: (0, i))
    tiles_per_block = MOE_TB // rows
    return pl.pallas_call(
        _outproj_kernel,
        grid_spec=pltpu.PrefetchScalarGridSpec(
            num_scalar_prefetch=1,
            grid=(N_TILES // OUT_TILES,),
            in_specs=[tile(w), prompt_tile(w), sample_tile(w), prompt_tile(w), sample_tile(w),
                      prompt_tile(w), sample_tile(w), prompt_tile(D_MODEL), sample_tile(D_MODEL),
                      pl.BlockSpec((1, 1, 6 * D_MODEL), lambda i, s: (s[i], 0, 0)),
                      pl.BlockSpec((1, D_MODEL, D_MODEL), lambda i, s: (layer, 0, 0)), full((1, D_MODEL)),
                      full((N_EXPERTS, D_MODEL)), full((N_EXPERTS, 1)), full((TM, TM))],
            out_specs=[tile(D_MODEL), pl.BlockSpec((rows * ROW_PIECES, 128), lambda i, s: (i, 0)),
                       slots, slots, slots,
                       pl.BlockSpec((1, N_EXPERTS, 1), lambda i, s: (i // tiles_per_block, 0, 0))],
            scratch_shapes=[pltpu.VMEM((N_EXPERTS, 1), F32)],
        ),
        out_shape=[jax.ShapeDtypeStruct((N_TOK, D_MODEL), F32),
                   jax.ShapeDtypeStruct((N_TOK * ROW_PIECES, 128), F32),
                   jax.ShapeDtypeStruct((TOPK_PAD, N_TOK), jnp.int32),
                   jax.ShapeDtypeStruct((TOPK_PAD, N_TOK), F32),
                   jax.ShapeDtypeStruct((TOPK_PAD, N_TOK), jnp.int32),
                   jax.ShapeDtypeStruct((N_TOK // MOE_TB, N_EXPERTS, 1), jnp.int32)],
        compiler_params=_cparams(("arbitrary",)),
        name="outproj",
    )(seq_of_tile, ya, *yb, *yc, *yd, *x, mod3, w_out, g2, rwt, rbt, triu)


def _moe_kernel(expert_ref, block_ref, kind_ref, local_ref, dest_ref, topw_ref, gstart_ref, count_ref,
                src_ref, wgu_ref, bgu_ref, wd_ref, bd_ref, acc_ref, pair_ref, xt_even, xt_odd, y3_even, y3_odd):
    del expert_ref, block_ref
    i = pl.program_id(0)
    kind = kind_ref[i]

    def token_of(pair):
        return (pair >> MOE_PAIR_SHIFT) & (MOE_TB - 1)

    def gather(tile, xt_ref):
        row0 = local_ref[tile]
        for m in range(MOE_TILE):
            t = token_of(pair_ref[row0 + m])
            slab = src_ref[pl.ds(pl.multiple_of(t * ROW_PIECES, ROW_PIECES), ROW_PIECES), :]
            xt_ref[pl.ds(m, ROW_PIECES, stride=MOE_XT_STRIDE), :] = slab

    def scatter(tile, y3_ref):
        row0 = local_ref[tile]
        for m0 in range(0, MOE_TILE, MOE_RMW_BATCH):
            rows = range(m0, m0 + MOE_RMW_BATCH)
            pairs = [pair_ref[row0 + m] for m in rows]
            toks = [token_of(p) for p in pairs]
            vals = [acc_ref[t] + topw_ref[0, 0, p] * y3_ref[m // 8, pl.ds(m % 8, 8, stride=8), :]
                    for t, p, m in zip(toks, pairs, rows)]
            for t, v in reversed(list(zip(toks, vals))):
                acc_ref[t] = v

    def compute(xt_ref, y3_ref):
        x = jnp.concatenate([xt_ref[j * MOE_XT_STRIDE:j * MOE_XT_STRIDE + MOE_TILE, :] for j in range(ROW_PIECES)],
                            axis=1).astype(BF16)
        gu = _dot(x, wgu_ref[0, 0]) + bgu_ref[0, 0]
        gate = jnp.minimum(gu[:, :D_FF], SWIGLU_LIMIT)
        up = jnp.clip(gu[:, D_FF:], -SWIGLU_LIMIT, SWIGLU_LIMIT)
        act = (up + 1.0) * gate * jax.nn.sigmoid(SWIGLU_ALPHA * gate)
        y = _dot(act.astype(BF16), wd_ref[0, 0]) + bd_ref[0, 0]
        for j in range(ROW_PIECES):
            y3_ref[:, j * 8:(j + 1) * 8, :] = y[:, j * 128:(j + 1) * 128].reshape(MOE_TILE // 8, 8, 128)

    @pl.when(kind == MOE_KIND_FIRST)
    def _():
        def zero(c, carry):
            acc_ref[pl.ds(pl.multiple_of(c * MOE_TILE, MOE_TILE), MOE_TILE)] = jnp.zeros((MOE_TILE, 8, 128), F32)
            return carry
        lax.fori_loop(0, MOE_TB // MOE_TILE, zero, 0)

        def pad_group(e, carry):
            cnt = count_ref[0, 0, e]
            first_pad = gstart_ref[0, 0, e] + cnt
            n_pad = (-cnt) & (MOE_TILE - 1)

            def pad_row(r, c2):
                pair_ref[first_pad + r] = MOE_PAD_PAIR
                return c2
            lax.fori_loop(0, n_pad, pad_row, 0)
            return carry
        lax.fori_loop(0, N_EXPERTS, pad_group, 0)

        def place(c, carry):
            for k in range(MOE_PLACE_UNROLL):
                pair = c * MOE_PLACE_UNROLL + k
                pair_ref[dest_ref[0, 0, pair]] = pair
            return carry
        lax.fori_loop(0, MOE_TB * TOP_K // MOE_PLACE_UNROLL, place, 0)

    step = kind * 2 + (i & 1)
    for parity, (xt_cur, xt_nxt, y3_cur, y3_prv) in enumerate(((xt_even, xt_odd, y3_even, y3_odd),
                                                               (xt_odd, xt_even, y3_odd, y3_even))):
        @pl.when(step == MOE_KIND_FIRST * 2 + parity)
        def _():
            gather(i, xt_cur)
            gather(i + 1, xt_nxt)
            compute(xt_cur, y3_cur)

        @pl.when(step == MOE_KIND_MIDDLE * 2 + parity)
        def _():
            gather(i + 1, xt_nxt)
            compute(xt_cur, y3_cur)
            scatter(i - 1, y3_prv)

        @pl.when(step == MOE_KIND_LAST * 2 + parity)
        def _():
            compute(xt_cur, y3_cur)
            scatter(i - 1, y3_prv)
            scatter(i, y3_cur)


def _moe_call(layer, tile_expert, tile_block, tile_flags, tile_local, dest, topw, gstart, counts, h2_rows,
              wgu, bgu, wd, bd):
    pieces = ROW_PIECES
    once = pl.Buffered(1)
    n_pairs = MOE_TB * TOP_K

    def smem_block(width):
        return pl.BlockSpec((1, 1, width), lambda i, e, b, f, lo: (b[i], 0, 0), memory_space=pltpu.SMEM,
                            pipeline_mode=once)

    def expert_block(rows, cols):
        return pl.BlockSpec((1, 1, rows, cols), lambda i, e, b, f, lo: (layer, e[i], 0, 0))

    return pl.pallas_call(
        _moe_kernel,
        grid_spec=pltpu.PrefetchScalarGridSpec(
            num_scalar_prefetch=4,
            grid=(MOE_MAX_TILES,),
            in_specs=[smem_block(n_pairs), smem_block(n_pairs + 128), smem_block(N_EXPERTS), smem_block(N_EXPERTS),
                      pl.BlockSpec((MOE_TB * pieces, 128), lambda i, e, b, f, lo: (b[i], 0), pipeline_mode=once),
                      expert_block(D_MODEL, 2 * D_FF), expert_block(1, 2 * D_FF),
                      expert_block(D_FF, D_MODEL), expert_block(1, D_MODEL)],
            out_specs=pl.BlockSpec((MOE_TB, 8, 128), lambda i, e, b, f, lo: (b[i], 0, 0), pipeline_mode=once),
            scratch_shapes=[pltpu.SMEM((MOE_ROWS_PER_BLOCK,), jnp.int32),
                            pltpu.VMEM((pieces * MOE_XT_STRIDE, 128), F32),
                            pltpu.VMEM((pieces * MOE_XT_STRIDE, 128), F32),
                            pltpu.VMEM((MOE_TILE // 8, 8 * pieces, 128), F32),
                            pltpu.VMEM((MOE_TILE // 8, 8 * pieces, 128), F32)],
        ),
        out_shape=jax.ShapeDtypeStruct((N_TOK, 8, 128), F32),
        compiler_params=_cparams(("arbitrary",)),
        name="moe",
    )(tile_expert, tile_block, tile_flags, tile_local, dest, topw, gstart, counts, h2_rows, wgu, bgu, wd, bd)


def _moe_routing(top_i, top_w, rank, counts):
    nb = N_TOK // MOE_TB
    n_groups = nb * N_EXPERTS
    counts = counts.reshape(nb, N_EXPERTS)
    padded = ((counts + MOE_TILE - 1) // MOE_TILE) * MOE_TILE
    group_end = jnp.cumsum(padded.reshape(-1))
    group_start = (group_end - padded.reshape(-1)).reshape(nb, N_EXPERTS)
    block_row0 = group_start[:, 0]
    gstart_local = group_start - block_row0[:, None]
    experts = jnp.arange(N_EXPERTS, dtype=jnp.int32)
    pair_major = lambda a: jnp.transpose(a.reshape(TOP_K, nb, MOE_TB), (1, 2, 0)).reshape(nb, 1, MOE_TB * TOP_K)
    picked = top_i.reshape(TOP_K, nb, MOE_TB, 1) == experts
    dest = jnp.sum(jnp.where(picked, gstart_local[None, :, None, :], 0), axis=-1) + rank.reshape(TOP_K, nb, MOE_TB)
    dest = pair_major(dest).astype(jnp.int32)
    topw = jnp.concatenate([pair_major(top_w), jnp.zeros((nb, 1, 128), F32)], axis=-1)
    tile_start = jnp.arange(MOE_MAX_TILES, dtype=jnp.int32) * MOE_TILE
    tile_group = jnp.sum((group_end[None, :] <= tile_start[:, None]).astype(jnp.int32), axis=1)
    tile_group = jnp.minimum(tile_group, n_groups - 1)
    valid = tile_start < group_end[-1]
    tile_block = tile_group // N_EXPERTS
    new_block = tile_block[1:] != tile_block[:-1]
    first = jnp.concatenate([jnp.ones((1,), bool), new_block])
    last = jnp.concatenate([new_block | ~valid[1:], jnp.ones((1,), bool)])
    kinds = jnp.where(first, MOE_KIND_FIRST, jnp.where(last, MOE_KIND_LAST, MOE_KIND_MIDDLE))
    flags = jnp.where(valid, kinds, MOE_KIND_SKIP).astype(jnp.int32)
    blocks = jnp.arange(nb, dtype=jnp.int32)
    tile_row0 = jnp.sum(jnp.where(tile_block[:, None] == blocks[None, :], block_row0[None, :], 0), axis=1)
    tile_local = (tile_start - tile_row0).astype(jnp.int32)
    return (tile_group % N_EXPERTS, tile_block, flags, tile_local, dest, topw,
            gstart_local.reshape(nb, 1, N_EXPERTS).astype(jnp.int32), counts.reshape(nb, 1, N_EXPERTS))


def _residual_kernel(seq_ref, x1_ref, moe_ref, mod_ref, o_ref):
    del seq_ref
    g2 = mod_ref[0][:, 5 * D_MODEL:6 * D_MODEL]
    moe = jnp.concatenate([moe_ref[pl.ds(j, TM, stride=ROW_PIECES), :] for j in range(ROW_PIECES)], axis=1)
    o_ref[...] = x1_ref[...] + g2 * moe


def _residual_call(seq_of_tile, x1, moe_rows, mod3, first_tile, n_tiles):
    return pl.pallas_call(
        _residual_kernel,
        grid_spec=pltpu.PrefetchScalarGridSpec(
            num_scalar_prefetch=1,
            grid=(n_tiles,),
            in_specs=[pl.BlockSpec((TM, D_MODEL), lambda i, s: (i + first_tile, 0)),
                      pl.BlockSpec((TM * ROW_PIECES, 128), lambda i, s: (i + first_tile, 0)),
                      pl.BlockSpec((1, 1, 6 * D_MODEL), lambda i, s: (s[i + first_tile], 0, 0))],
            out_specs=pl.BlockSpec((TM, D_MODEL), lambda i, s: (i, 0)),
        ),
        out_shape=jax.ShapeDtypeStruct((n_tiles * TM, D_MODEL), F32),
        compiler_params=_cparams(("arbitrary",)),
        name="residual",
    )(seq_of_tile, x1, moe_rows, mod3)


def _segment_matrix(width, seg):
    idx = np.arange(width) // seg
    return jnp.asarray((idx[:, None] == idx[None, :]).astype(np.float32) / seg, BF16)


def _rope_tables(rot_dim, width):
    rows = DEC_SEQ // GRID_W
    nf = rot_dim // 4
    f32 = np.float32
    inv = (f32(1.0) / (f32(ROPE_BASE) ** (np.arange(nf, dtype=f32) / f32(nf)))).astype(f32)
    row = np.repeat(np.arange(rows, dtype=f32), GRID_W)
    col = np.tile(np.arange(GRID_W, dtype=f32), rows)
    ang = np.stack([row[:, None] * inv, col[:, None] * inv], axis=1).astype(f32)
    cos, sin = np.cos(ang).astype(f32), np.sin(ang).astype(f32)
    cos_r = np.concatenate([cos[:, 0], cos[:, 0], cos[:, 1], cos[:, 1]], axis=-1)
    sin_r = np.concatenate([-sin[:, 0], sin[:, 0], -sin[:, 1], sin[:, 1]], axis=-1)
    reps = width // rot_dim
    cos_t = np.concatenate([np.ones((TM * IN_TILES, width), f32), np.tile(cos_r, (1, reps))], axis=0)
    sin_t = np.concatenate([np.zeros((TM * IN_TILES, width), f32), np.tile(sin_r, (1, reps))], axis=0)
    return jnp.asarray(cos_t), jnp.asarray(sin_t)


def _block_diag(wb):
    nb, n = wb.shape[-3], wb.shape[-1]
    rows = wb.reshape(wb.shape[:-3] + (nb * n, n))
    tiled = jnp.tile(rows, (1,) * (wb.ndim - 2) + (nb,))
    blk = np.arange(nb * n) // n
    return jnp.where(jnp.asarray(blk[:, None] == blk[None, :]), tiled, 0.0)


_DQ_HEADS = [kh * SWA_GROUPS + g for g in range(SWA_GROUPS) for kh in range(SWA_KV_HEADS)]


def kernel(x_prompt, x_sample, c, cache_diff_k, cache_diff_v, cache_swa_k, cache_swa_v, state_lru, c_ctx, mod_w, mod_b, norm1_g, norm2_g, w_in, w_out, mlp_vnorm_g, mlp_ws, mlp_bs, lru_conv_w, lru_conv_b, lru_wa, lru_ba, lru_wx, lru_bx, lru_lambda, diff_qnorm_g, diff_knorm_g, diff_lambda, diff_subln_g, swa_qnorm_g, swa_knorm_g, swa_sink, router_w, router_b, moe_w_gu, moe_b_gu, moe_w_down, moe_b_down):
    params = dict(locals())
    consts = _constants()
    prep = _prepare(params, consts)
    x = (x_prompt.reshape(N_PROMPT, D_MODEL), x_sample.reshape(N_SAMPLE, D_MODEL))
    dk_l, dv_l, sk_l, sv_l, lru_l = [], [], [], [], []
    for l in range(DEPTH):
        st = _mixer_stage(x, params, prep, l, consts)
        x = _ffn_stage(x, st, params, prep, l, consts)
        pp = st["proj"][:N_PROMPT]
        dk_l.append(pp[:, OFF_CK:OFF_CV].reshape(BATCH, SEQ, N_GROUP_HEADS, 2, DIFF_QK_DIM))
        dv_l.append(pp[:, OFF_CV:OFF_DQ].reshape(BATCH, SEQ, N_GROUP_HEADS, HEAD_DIM))
        sk_l.append(pp[:, OFF_DK:OFF_DV].reshape(BATCH, SEQ, SWA_KV_HEADS, HEAD_DIM))
        sv_l.append(pp[:, OFF_DV:IN_WIDTH].reshape(BATCH, SEQ, SWA_KV_HEADS, HEAD_DIM))
        lru_l.append(st["st_p"])

    y_p = x[0].reshape(BATCH, SEQ, D_MODEL)
    y_s = x[1].reshape(DEC_BATCH, DEC_SEQ, D_MODEL)
    return (y_p, y_s, jnp.stack(dk_l, axis=1), jnp.stack(dv_l, axis=1), jnp.stack(sk_l, axis=1),
            jnp.stack(sv_l, axis=1), jnp.stack(lru_l, axis=1))


def _constants():
    w = GROUP_WIDTH
    tiles = np.arange(N_TILES)
    seq_np = np.where(tiles < PROMPT_TILES, 0, 1 + (tiles - PROMPT_TILES) // SAMPLE_TILES_PER_SEQ)
    seq_of_tile = [jnp.asarray(seq_np + l * N_COND, jnp.int32) for l in range(DEPTH)]
    seq_of_out_step = [jnp.asarray(seq_np[::OUT_TILES] + l * N_COND, jnp.int32) for l in range(DEPTH)]
    steps = np.arange(N_TILES // IN_TILES)
    in_prompt_steps, in_steps_per_seq = PROMPT_TILES // IN_TILES, SAMPLE_TILES_PER_SEQ // IN_TILES
    seq_of_in_step = [jnp.asarray(seq_np[::IN_TILES] + l * N_COND, jnp.int32) for l in range(DEPTH)]
    rope_of_tile = jnp.asarray(np.where(steps < in_prompt_steps, 0, 1 + (steps - in_prompt_steps) % in_steps_per_seq),
                               jnp.int32)

    s32 = _segment_matrix(w, DIFF_QK_DIM)
    s64 = _segment_matrix(w, HEAD_DIM)
    cosc, sinc = _rope_tables(DIFF_QK_DIM, w)
    cosd, sind = _rope_tables(HEAD_DIM, w)
    dq_cols = np.concatenate([np.arange(h * HEAD_DIM, (h + 1) * HEAD_DIM) for h in _DQ_HEADS])
    triu = jnp.asarray(np.triu(np.ones((TM, TM), np.float32)), BF16)
    return dict(seq_of_tile=seq_of_tile, seq_of_out_step=seq_of_out_step, seq_of_in_step=seq_of_in_step,
                rope_of_tile=rope_of_tile, triu=triu,
                s32=s32, s64=s64,
                cosc=cosc, sinc=sinc, cosd=cosd, sind=sind, dq_cols=dq_cols)


def _prepare(params, consts):
    w = GROUP_WIDTH
    p = params
    dq = consts["dq_cols"]
    heads = lambda g, reps: jnp.tile(g.reshape(DEPTH, 1, -1), (1, 1, reps))
    w_in, w_out = p["w_in"], p["w_out"]
    lru_w = jnp.stack([p["lru_wa"][:, 0], p["lru_wx"][:, 0], p["lru_wa"][:, 1], p["lru_wx"][:, 1]], axis=1)
    lru_b = jnp.stack([p["lru_ba"][:, 0], p["lru_bx"][:, 0], p["lru_ba"][:, 1], p["lru_bx"][:, 1]], axis=1)
    cond = jnp.concatenate([p["c_ctx"][None], p["c"], jnp.zeros((N_COND - 1 - DEC_BATCH, D_MODEL), F32)], axis=0)
    return dict(
        mod3=_mod_call(cond, p["mod_w"], p["mod_b"]).reshape(DEPTH * N_COND, 1, 6 * D_MODEL),
        w_in=jnp.concatenate([w_in[..., :OFF_DQ], w_in[..., OFF_DQ:OFF_DK][..., dq], w_in[..., OFF_DK:]],
                             axis=-1).astype(BF16),
        w_out=jnp.concatenate([w_out[:, :3 * w], w_out[:, 3 * w:][:, dq]], axis=1).astype(BF16),
        gqc=heads(p["diff_qnorm_g"], N_GROUP_HEADS), gkc=heads(p["diff_knorm_g"], N_GROUP_HEADS),
        gqd=heads(p["swa_qnorm_g"], SWA_HEADS), gkd=heads(p["swa_knorm_g"], SWA_KV_HEADS),
        g_sub=heads(p["diff_subln_g"], N_GROUP_HEADS), g_mlp=p["mlp_vnorm_g"].reshape(DEPTH, 1, w),
        mlp_bias=jnp.repeat(jnp.swapaxes(p["mlp_bs"], 1, 2), HEAD_DIM, axis=2),
        mlp_ws=p["mlp_ws"].astype(BF16),
        wcat=jnp.swapaxes(_block_diag(lru_w), 1, 2).reshape(DEPTH, w, 4 * w).astype(BF16),
        bcat=lru_b.reshape(DEPTH, 1, 4 * w),
        conv_b=p["lru_conv_b"].reshape(DEPTH, 1, w),
        g1=p["norm1_g"].reshape(DEPTH, 1, D_MODEL), g2=p["norm2_g"].reshape(DEPTH, 1, D_MODEL),
        router_wt=jnp.swapaxes(p["router_w"], 1, 2), router_bt=p["router_b"].reshape(DEPTH, N_EXPERTS, 1),
        moe_w_gu=p["moe_w_gu"].astype(BF16), moe_w_down=p["moe_w_down"].astype(BF16),
        moe_b_gu=p["moe_b_gu"].reshape(DEPTH, N_EXPERTS, 1, 2 * D_FF),
        moe_b_down=p["moe_b_down"].reshape(DEPTH, N_EXPERTS, 1, D_MODEL),
        ctx_diff_k=p["cache_diff_k"].reshape(DEC_BATCH, DEPTH, PAST_LEN, w),
        ctx_diff_v=p["cache_diff_v"].reshape(DEC_BATCH, DEPTH, PAST_LEN, w),
        ctx_swa_k=p["cache_swa_k"].reshape(DEC_BATCH, DEPTH, PAST_LEN, SWA_KV_HEADS * HEAD_DIM),
        ctx_swa_v=p["cache_swa_v"].reshape(DEC_BATCH, DEPTH, PAST_LEN, SWA_KV_HEADS * HEAD_DIM),
    )


def _mixer_stage(x, params, prep, l, consts):
    w = GROUP_WIDTH
    s32, s64 = consts["s32"], consts["s64"]
    lam_init = 0.8 - 0.6 * math.exp(-0.3 * l)
    proj = _inproj_call(l, consts["seq_of_in_step"][l], consts["rope_of_tile"], x, prep["mod3"], prep["g1"][l],
                        prep["w_in"], s32, s64, prep["gqc"][l], prep["gkc"][l], prep["gqd"][l], prep["gkd"][l],
                        consts["cosc"], consts["sinc"], consts["cosd"], consts["sind"])
    ya = _gmlp_call(proj, s64, prep["g_mlp"][l], prep["mlp_ws"][l], prep["mlp_bias"][l])

    lru_args = (params["lru_conv_w"][l], prep["conv_b"][l], prep["wcat"][l], prep["bcat"][l], params["lru_lambda"][l])
    yb_p, st_p = _lru_call(proj, jnp.zeros((BATCH, 2, w), F32), *lru_args, seq_len=SEQ, n_seq=BATCH, row_block0=0)
    yb_s, _ = _lru_call(proj, params["state_lru"][:, l], *lru_args, seq_len=DEC_SEQ, n_seq=DEC_BATCH,
                        row_block0=N_PROMPT // DEC_SEQ)

    ld = params["diff_lambda"][l]
    yc_p = _diff_prompt_call(proj, ld, s64, prep["g_sub"][l], lam_init)
    yc_s = _diff_sample_call(l, proj, prep["ctx_diff_k"], prep["ctx_diff_v"], ld, s64, prep["g_sub"][l], lam_init)

    sink = params["swa_sink"][l]
    yd_p = _swa_prompt_call(sink, proj)
    yd_s = _swa_sample_call(l, sink, proj, prep["ctx_swa_k"], prep["ctx_swa_v"])
    return dict(proj=proj, ya=ya, yb=(yb_p, yb_s), yc=(yc_p, yc_s), yd=(yd_p, yd_s), st_p=st_p)


def _ffn_stage(x, st, params, prep, l, consts):
    seq_of_tile = consts["seq_of_tile"][l]
    x1, h2_rows, top_i, top_w, rank, counts = _outproj_call(
        l, consts["seq_of_out_step"][l], st["ya"], st["yb"], st["yc"], st["yd"], x, prep["mod3"], prep["w_out"],
        prep["g2"][l],
        prep["router_wt"][l], prep["router_bt"][l], consts["triu"])
    routing = _moe_routing(top_i[:TOP_K], top_w[:TOP_K], rank[:TOP_K], counts)
    moe = _moe_call(l, *routing, h2_rows, prep["moe_w_gu"], prep["moe_b_gu"], prep["moe_w_down"], prep["moe_b_down"])
    moe_rows = moe.reshape(N_TOK * ROW_PIECES, 128)
    return (_residual_call(seq_of_tile, x1, moe_rows, prep["mod3"], 0, PROMPT_TILES),
            _residual_call(seq_of_tile, x1, moe_rows, prep["mod3"], PROMPT_TILES, N_TILES - PROMPT_TILES))
```
